```python
import math
import jax, jax.numpy as jnp
from jax import lax
import numpy as np

D_MODEL = 1024
BATCH = 2
SEQ = 8192
DEPTH = 2

N_MIXERS = 2
N_POOL_LAYERS = (DEPTH + 1) // 2
N_ATTN_LAYERS = DEPTH // 2
POOL_WINDOWS = (2, 4, 8, 16)
N_POOL_GROUPS = len(POOL_WINDOWS)
POOL_GROUP_CH = D_MODEL // N_POOL_GROUPS
HEAD_DIM = 64
N_HEADS = D_MODEL // HEAD_DIM
N_KV_HEADS = N_HEADS // 8
GQA_GROUP = N_HEADS // N_KV_HEADS
Q_DIM = N_HEADS * HEAD_DIM
KV_DIM = N_KV_HEADS * HEAD_DIM
QKV_DIM = Q_DIM + 2 * KV_DIM
WINDOW = 128
BLOCK = 128
N_BUCKETS = 32
MAX_DISTANCE = 128
N_EXPERTS = 32
TOP_K = 4
D_FF = D_MODEL
SWIGLU_ALPHA = 1.702
SWIGLU_LIMIT = 7.0
MOE_BLOCK = 128
RMS_EPS = 1e-5
NEG_INF = -1e30

kernel_name = "hybrid_pool_swa_moe"


def rmsnorm(x, g):
    xf = x.astype(jnp.float32)
    y = xf * lax.rsqrt(jnp.mean(xf * xf, axis=-1, keepdims=True) + RMS_EPS)
    return (y * g.astype(jnp.float32)).astype(x.dtype)


def pool_mixer(h, w, b, scale):
    bsz, s, d = h.shape
    hf = h.astype(jnp.float32).reshape(bsz, s, N_POOL_GROUPS, POOL_GROUP_CH)
    cs = jnp.cumsum(hf, axis=1)
    t = jnp.arange(s)
    outs = []
    for g, win in enumerate(POOL_WINDOWS):
        c = cs[:, :, g]
        lag = jnp.concatenate([jnp.zeros((bsz, win, POOL_GROUP_CH), jnp.float32), c[:, :-win]], axis=1)
        cnt = jnp.minimum(t + 1, win).astype(jnp.float32)[None, :, None]
        outs.append((c - lag) / cnt - hf[:, :, g])
    dlt = jnp.stack(outs, axis=2).astype(h.dtype)
    y = jnp.einsum('bsgc,gcd->bsgd', dlt, w).reshape(bsz, s, d) + b
    return y * scale


def t5_bucket(n):
    max_exact = N_BUCKETS // 2
    nf = jnp.maximum(n, max_exact).astype(jnp.float32)
    large = max_exact + (jnp.log(nf / max_exact) / math.log(MAX_DISTANCE / max_exact)
                         * (N_BUCKETS - max_exact)).astype(jnp.int32)
    large = jnp.minimum(large, N_BUCKETS - 1)
    return jnp.where(n < max_exact, n, large)


def swa_attention(h, w_qkv, b_qkv, sinks, w_o, b_o, rel_bias):
    bsz, s, _ = h.shape
    nb = s // BLOCK
    qkv = h @ w_qkv + b_qkv
    q = qkv[..., :Q_DIM].reshape(bsz, nb, BLOCK, N_KV_HEADS, GQA_GROUP, HEAD_DIM)
    k = qkv[..., Q_DIM:Q_DIM + KV_DIM].reshape(bsz, s, N_KV_HEADS, HEAD_DIM)
    v = qkv[..., Q_DIM + KV_DIM:].reshape(bsz, s, N_KV_HEADS, HEAD_DIM)

    def band(t):
        tp = jnp.pad(t, ((0, 0), (BLOCK, 0), (0, 0), (0, 0))).reshape(bsz, nb + 1, BLOCK, N_KV_HEADS, HEAD_DIM)
        return jnp.concatenate([tp[:, :-1], tp[:, 1:]], axis=2)

    kb, vb = band(k), band(v)
    sc = jnp.einsum('bnqkgd,bnjkd->bnkgqj', q, kb).astype(jnp.float32) * (HEAD_DIM ** -0.5)
    r = jnp.arange(BLOCK)[:, None]
    j = jnp.arange(2 * BLOCK)[None, :]
    rel = BLOCK + r - j
    bucket = t5_bucket(jnp.clip(rel, 0, WINDOW - 1))
    bias = rel_bias[bucket].astype(jnp.float32)
    bias = jnp.transpose(bias, (2, 0, 1)).reshape(N_KV_HEADS, GQA_GROUP, BLOCK, 2 * BLOCK)
    rel_ok = (rel >= 0) & (rel < WINDOW)
    key_ok = (jnp.arange(nb)[:, None] * BLOCK + jnp.arange(2 * BLOCK)[None, :] - BLOCK) >= 0
    mask = rel_ok[None] & key_ok[:, None, :]
    sc = jnp.where(mask[None, :, None, None], sc + bias, NEG_INF)
    sink = sinks.astype(jnp.float32).reshape(N_KV_HEADS, GQA_GROUP)[..., None, None]
    m = jnp.maximum(jnp.max(sc, axis=-1, keepdims=True), sink)
    p = jnp.exp(sc - m)
    wts = p / (jnp.sum(p, axis=-1, keepdims=True) + jnp.exp(sink - m))
    o = jnp.einsum('bnkgqj,bnjkd->bnqkgd', wts.astype(v.dtype), vb).reshape(bsz, s, Q_DIM)
    return o @ w_o + b_o


def clamped_swiglu(a):
    glu, lin = a[..., :D_FF], a[..., D_FF:]
    glu = jnp.minimum(glu, SWIGLU_LIMIT)
    lin = jnp.clip(lin, -SWIGLU_LIMIT, SWIGLU_LIMIT)
    return glu * jax.nn.sigmoid(SWIGLU_ALPHA * glu) * (lin + 1)


def moe(h, w_router, b_router, w1, b1, w2, b2):
    bsz, s, d = h.shape
    t = bsz * s
    hf = h.reshape(t, d)
    logits = (hf @ w_router + b_router).astype(jnp.float32)
    top_vals, top_ids = lax.top_k(logits, TOP_K)
    gates = jax.nn.softmax(top_vals, axis=-1)
    flat_e = top_ids.reshape(-1)
    flat_g = gates.reshape(-1)
    order = jnp.argsort(flat_e)
    sorted_e = flat_e[order]
    slot_token = (order // TOP_K).astype(jnp.int32)
    sizes = jnp.bincount(flat_e, length=N_EXPERTS)
    padded = ((sizes + MOE_BLOCK - 1) // MOE_BLOCK) * MOE_BLOCK
    start = jnp.cumsum(sizes) - sizes
    pend = jnp.cumsum(padded)
    pstart = pend - padded
    rank = jnp.arange(t * TOP_K) - start[sorted_e]
    dest = pstart[sorted_e] + rank
    cap = t * TOP_K + N_EXPERTS * MOE_BLOCK
    cap = ((cap + MOE_BLOCK - 1) // MOE_BLOCK) * MOE_BLOCK
    nblk = cap // MOE_BLOCK
    row_tok = jnp.full((cap,), t, jnp.int32).at[dest].set(slot_token)
    row_gate = jnp.zeros((cap,), jnp.float32).at[dest].set(flat_g[order])
    blk_e = jnp.minimum(jnp.searchsorted(pend, jnp.arange(nblk) * MOE_BLOCK, side='right'),
                        N_EXPERTS - 1)
    x_pad = jnp.concatenate([hf, jnp.zeros((1, d), hf.dtype)], axis=0)
    xs = x_pad[row_tok].reshape(nblk, MOE_BLOCK, d)

    def expert_block(args):
        xb, e = args
        a = xb @ w1[e] + b1[e]
        return clamped_swiglu(a) @ w2[e] + b2[e]

    ys = lax.map(expert_block, (xs, blk_e)).reshape(cap, d)
    out = jax.ops.segment_sum(ys.astype(jnp.float32) * row_gate[:, None], row_tok, num_segments=t + 1)[:t]
    return out.astype(h.dtype).reshape(bsz, s, d)


def setup_inputs(seed: int = 0) -> dict:
    key = jax.random.key(seed)
    ks = jax.random.split(key, 20)
    f32 = jnp.float32
    nrm = lambda k, shape, sc: jax.random.normal(k, shape, f32) * sc
    return {
        "x": nrm(ks[0], (BATCH, SEQ, D_MODEL), 1.0),
        "norm_mix": 1.0 + nrm(ks[1], (DEPTH, D_MODEL), 0.05),
        "norm_ffn": 1.0 + nrm(ks[2], (DEPTH, D_MODEL), 0.05),
        "norm_final": 1.0 + nrm(ks[3], (D_MODEL,), 0.05),
        "pool_w": nrm(ks[4], (N_POOL_LAYERS, N_POOL_GROUPS, POOL_GROUP_CH, POOL_GROUP_CH), POOL_GROUP_CH ** -0.5),
        "pool_b": nrm(ks[5], (N_POOL_LAYERS, D_MODEL), 0.02),
        "pool_scale": 1.0 + nrm(ks[6], (N_POOL_LAYERS, D_MODEL), 0.1),
        "w_qkv": nrm(ks[7], (N_ATTN_LAYERS, D_MODEL, QKV_DIM), D_MODEL ** -0.5),
        "b_qkv": nrm(ks[8], (N_ATTN_LAYERS, QKV_DIM), 0.02),
        "sinks": nrm(ks[9], (N_ATTN_LAYERS, N_HEADS), 0.5),
        "w_o": nrm(ks[10], (N_ATTN_LAYERS, Q_DIM, D_MODEL), Q_DIM ** -0.5),
        "b_o": nrm(ks[11], (N_ATTN_LAYERS, D_MODEL), 0.02),
        "rel_bias": nrm(ks[12], (N_BUCKETS, N_HEADS), 0.5),
        "w_router": nrm(ks[13], (DEPTH, D_MODEL, N_EXPERTS), D_MODEL ** -0.5),
        "b_router": nrm(ks[14], (DEPTH, N_EXPERTS), 0.01),
        "w1": nrm(ks[15], (DEPTH, N_EXPERTS, D_MODEL, 2 * D_FF), D_MODEL ** -0.5),
        "b1": nrm(ks[16], (DEPTH, N_EXPERTS, 2 * D_FF), 0.02),
        "w2": nrm(ks[17], (DEPTH, N_EXPERTS, D_FF, D_MODEL), D_FF ** -0.5),
        "b2": nrm(ks[18], (DEPTH, N_EXPERTS, D_MODEL), 0.02),
    }


def reference(x, norm_mix, norm_ffn, norm_final, pool_w, pool_b, pool_scale,
              w_qkv, b_qkv, sinks, w_o, b_o, rel_bias,
              w_router, b_router, w1, b1, w2, b2):
    for i in range(DEPTH):
        h = rmsnorm(x, norm_mix[i])
        j = i // N_MIXERS
        if i % N_MIXERS == 0:
            x = x + pool_mixer(h, pool_w[j], pool_b[j], pool_scale[j])
        else:
            x = x + swa_attention(h, w_qkv[j], b_qkv[j], sinks[j], w_o[j], b_o[j], rel_bias)
        h = rmsnorm(x, norm_ffn[i])
        x = x + moe(h, w_router[i], b_router[i], w1[i], b1[i], w2[i], b2[i])
    return rmsnorm(x, norm_final)
```

```python
import functools
import math

import jax
import jax.numpy as jnp
import numpy as np
from jax import lax
from jax.experimental import pallas as pl
from jax.experimental.pallas import tpu as pltpu

F32 = jnp.float32
BF16 = jnp.bfloat16
I32 = jnp.int32

D_MODEL = 1024
POOL_WINDOWS = (2, 4, 8, 16)
POOL_GROUP_CH = D_MODEL // len(POOL_WINDOWS)
POOL_HALO = 16
HEAD_DIM = 64
N_HEADS = 16
N_KV_HEADS = 2
GQA_GROUP = N_HEADS // N_KV_HEADS
HEADS_PER_SLAB = 128 // HEAD_DIM
SLABS_PER_GROUP = GQA_GROUP // HEADS_PER_SLAB
ATT_BLOCK = 128
WINDOW = 128
N_BUCKETS = 32
MAX_DISTANCE = 128
N_EXPERTS = 32
TOP_K = 4
D_FF = D_MODEL
SWIGLU_ALPHA = 1.702
SWIGLU_LIMIT = 7.0
RMS_EPS = 1e-5
NEG_INF = -1e30

ROW_TILE = 512
MOE_BLOCK = 512
DISPATCH_TILE = 1024
COMBINE_TILE = 256
VMEM_LIMIT_BYTES = 56 * 1024 * 1024


def _rms(x, g):
    return x * lax.rsqrt(jnp.mean(x * x, axis=-1, keepdims=True) + RMS_EPS) * g


def _pool_kernel(x_ref, xh_ref, g_ref, w_ref, b_ref, s_ref, o_ref, buf_ref, *, ts):
    i = pl.program_id(1)
    x = x_ref[...]
    g = g_ref[...]
    h = _rms(x, g)
    hh = _rms(xh_ref[...], g)
    hh = jnp.where(i == 0, 0.0, hh)
    buf_ref[0:POOL_HALO, :] = hh
    buf_ref[POOL_HALO:POOL_HALO + ts, :] = h
    t = i * ts + lax.broadcasted_iota(I32, (ts, 1), 0)
    outs = []
    for gi, win in enumerate(POOL_WINDOWS):
        c0 = gi * POOL_GROUP_CH
        hg = h[:, c0:c0 + POOL_GROUP_CH]
        acc = hg
        for d in range(1, win):
            acc = acc + buf_ref[POOL_HALO - d:POOL_HALO - d + ts, c0:c0 + POOL_GROUP_CH]
        cnt = jnp.minimum(t + 1, win).astype(F32)
        dlt = acc / cnt - hg
        outs.append(jnp.dot(dlt.astype(BF16), w_ref[gi], preferred_element_type=F32))
    y = jnp.concatenate(outs, axis=1)
    o_ref[...] = x + (y + b_ref[...]) * s_ref[...]


def _pool_layer(x, g, w, b, s):
    bsz, seq, d = x.shape
    ts = min(ROW_TILE, seq)
    kern = functools.partial(_pool_kernel, ts=ts)
    vec = lambda: pl.BlockSpec((1, d), lambda bi, i: (0, 0))
    return pl.pallas_call(
        kern,
        out_shape=jax.ShapeDtypeStruct(x.shape, F32),
        grid=(bsz, seq // ts),
        in_specs=[
            pl.BlockSpec((None, ts, d), lambda bi, i: (bi, i, 0)),
            pl.BlockSpec((None, POOL_HALO, d),
                         lambda bi, i: (bi, jnp.maximum(i * (ts // POOL_HALO) - 1, 0), 0)),
            vec(),
            pl.BlockSpec(w.shape, lambda bi, i: (0, 0, 0)),
            vec(),
            vec(),
        ],
        out_specs=pl.BlockSpec((None, ts, d), lambda bi, i: (bi, i, 0)),
        scratch_shapes=[pltpu.VMEM((POOL_HALO + ts, d), F32)],
        compiler_params=pltpu.CompilerParams(
            dimension_semantics=("arbitrary", "arbitrary"),
            vmem_limit_bytes=VMEM_LIMIT_BYTES),
        name="pool_layer",
    )(x, x, g.reshape(1, d), w.astype(BF16), b.reshape(1, d), s.reshape(1, d))


def _router_kernel(x_ref, g_ref, wr_ref, br_ref,
                   h_ref, ids_ref, gates_ref, rank_ref, sizes_ref, carry_ref, *, ts):
    i = pl.program_id(0)

    @pl.when(i == 0)
    def _():
        carry_ref[...] = jnp.zeros_like(carry_ref)

    h = _rms(x_ref[...], g_ref[...])
    h_ref[...] = h
    h_hi = h.astype(BF16)
    h_lo = (h - h_hi.astype(F32)).astype(BF16)
    w = wr_ref[...]
    w_hi = w.astype(BF16)
    w_lo = (w - w_hi.astype(F32)).astype(BF16)
    dn = (((1,), (1,)), ((), ()))
    lg = (lax.dot_general(w_hi, h_hi, dn, preferred_element_type=F32)
          + lax.dot_general(w_lo, h_hi, dn, preferred_element_type=F32)
          + lax.dot_general(w_hi, h_lo, dn, preferred_element_type=F32))
    lg = lg + br_ref[:, 0:1]

    iota_e = lax.broadcasted_iota(I32, (N_EXPERTS, ts), 0)
    vals, ids = [], []
    for _ in range(TOP_K):
        m = jnp.max(lg, axis=0, keepdims=True)
        idx = jnp.min(jnp.where(lg == m, iota_e, N_EXPERTS), axis=0, keepdims=True)
        vals.append(m)
        ids.append(idx)
        lg = jnp.where(iota_e == idx, -jnp.inf, lg)
    ex = [jnp.exp(v - vals[0]) for v in vals]
    den = ex[0] + ex[1] + ex[2] + ex[3]
    gates_ref[...] = jnp.concatenate([e / den for e in ex], axis=0)
    ids_ref[...] = jnp.concatenate(ids, axis=0)

    hot = [(iota_e == idx) for idx in ids]
    multi = (hot[0] | hot[1] | hot[2] | hot[3])
    multi_f = multi.astype(F32)
    r = lax.broadcasted_iota(I32, (ts, ts), 0)
    c = lax.broadcasted_iota(I32, (ts, ts), 1)
    upper = (r < c).astype(BF16)
    before = jnp.dot(multi_f.astype(BF16), upper, preferred_element_type=F32)
    before = before + carry_ref[:, 0:1]
    ranks = [jnp.sum(jnp.where(hk, before, 0.0), axis=0, keepdims=True) for hk in hot]
    rank_ref[...] = jnp.concatenate(ranks, axis=0).astype(I32)
    carry_ref[...] = carry_ref[...] + jnp.sum(multi_f, axis=1, keepdims=True)
    sizes_ref[...] = carry_ref[...]


def _route(x2d, g, w_router, b_router):
    t, d = x2d.shape
    ts = min(ROW_TILE, t)
    kern = functools.partial(_router_kernel, ts=ts)
    tok = lambda: pl.BlockSpec((TOP_K, ts), lambda i: (0, i))
    return pl.pallas_call(
        kern,
        out_shape=(
            jax.ShapeDtypeStruct((t, d), F32),
            jax.ShapeDtypeStruct((TOP_K, t), I32),
            jax.ShapeDtypeStruct((TOP_K, t), F32),
            jax.ShapeDtypeStruct((TOP_K, t), I32),
            jax.ShapeDtypeStruct((N_EXPERTS, 128), F32),
        ),
        grid=(t // ts,),
        in_specs=[
            pl.BlockSpec((ts, d), lambda i: (i, 0)),
            pl.BlockSpec((1, d), lambda i: (0, 0)),
            pl.BlockSpec((N_EXPERTS, d), lambda i: (0, 0)),
            pl.BlockSpec((N_EXPERTS, 128), lambda i: (0, 0)),
        ],
        out_specs=(
            pl.BlockSpec((ts, d), lambda i: (i, 0)),
            tok(), tok(), tok(),
            pl.BlockSpec((N_EXPERTS, 128), lambda i: (0, 0)),
        ),
        scratch_shapes=[pltpu.VMEM((N_EXPERTS, 128), F32)],
        compiler_params=pltpu.CompilerParams(
            dimension_semantics=("arbitrary",), vmem_limit_bytes=VMEM_LIMIT_BYTES),
        name="moe_route",
    )(x2d, g.reshape(1, d), w_router.T,
      jnp.broadcast_to(b_router.reshape(N_EXPERTS, 1), (N_EXPERTS, 128)))


def _dispatch_kernel(zrow_ref, n_used_ref, dest_ref, h_ref, xs_ref, zbuf, sem_z, sem,
                     *, td, blk, nblk):
    i = pl.program_id(0)

    def zero_block(row):
        row = pl.multiple_of(row, blk)
        return pltpu.make_async_copy(zbuf, xs_ref.at[pl.ds(row, blk), :], sem_z)

    @pl.when(i == 0)
    def _():
        zbuf[...] = jnp.zeros_like(zbuf)
        for e in range(N_EXPERTS):
            @pl.when(zrow_ref[e] >= 0)
            def _():
                zero_block(zrow_ref[e]).start()

        def start_tail(b, carry):
            zero_block(b * blk).start()
            return carry

        def wait_tail(b, carry):
            zero_block(b * blk).wait()
            return carry

        lax.fori_loop(n_used_ref[0], nblk, start_tail, 0)
        for e in range(N_EXPERTS):
            @pl.when(zrow_ref[e] >= 0)
            def _():
                zero_block(zrow_ref[e]).wait()
        lax.fori_loop(n_used_ref[0], nblk, wait_tail, 0)

    t0 = i * td

    def issue(t, carry):
        for k in range(TOP_K):
            d = dest_ref[0, 0, k * td + t]
            pltpu.make_async_copy(h_ref.at[pl.ds(t0 + t, 1), :],
                                  xs_ref.at[pl.ds(d, 1), :], sem).start()
        return carry

    lax.fori_loop(0, td, issue, 0)
    pltpu.make_async_copy(h_ref.at[pl.ds(0, TOP_K * td), :],
                          xs_ref.at[pl.ds(0, TOP_K * td), :], sem).wait()


def _dispatch(h, dest, zrow, n_used, n_rows):
    t, d = h.shape
    td = min(DISPATCH_TILE, t)
    nt = t // td
    dest_tiles = dest.reshape(TOP_K, nt, td).transpose(1, 0, 2).reshape(nt, 1, TOP_K * td)
    kern = functools.partial(_dispatch_kernel, td=td, blk=MOE_BLOCK, nblk=n_rows // MOE_BLOCK)
    return pl.pallas_call(
        kern,
        out_shape=jax.ShapeDtypeStruct((n_rows, d), F32),
        grid_spec=pltpu.PrefetchScalarGridSpec(
            num_scalar_prefetch=2,
            grid=(nt,),
            in_specs=[
                pl.BlockSpec((1, 1, TOP_K * td), lambda i, z, nu: (i, 0, 0),
                             memory_space=pltpu.SMEM),
                pl.BlockSpec(memory_space=pl.ANY),
            ],
            out_specs=pl.BlockSpec(memory_space=pl.ANY),
            scratch_shapes=[
                pltpu.VMEM((MOE_BLOCK, d), F32),
                pltpu.SemaphoreType.DMA(()),
                pltpu.SemaphoreType.DMA(()),
            ],
        ),
        compiler_params=pltpu.CompilerParams(
            dimension_semantics=("arbitrary",), vmem_limit_bytes=VMEM_LIMIT_BYTES),
        name="moe_dispatch",
    )(zrow, n_used, dest_tiles, h)


def _expert_kernel(blk_e_ref, n_used_ref, x_ref, w1_ref, b1_ref, w2_ref, b2_ref,
                   o_ref, w1b_ref, w2b_ref):
    i = pl.program_id(0)
    prev = blk_e_ref[jnp.maximum(i - 1, 0)]
    fresh = (i == 0) | (blk_e_ref[i] != prev)
    live = i < n_used_ref[0]

    @pl.when(live & fresh)
    def _():
        w1b_ref[...] = w1_ref[...].astype(BF16)
        w2b_ref[...] = w2_ref[...].astype(BF16)

    @pl.when(live)
    def _():
        x = x_ref[...].astype(BF16)
        a = jnp.dot(x, w1b_ref[...], preferred_element_type=F32) + b1_ref[...]
        glu = jnp.minimum(a[:, :D_FF], SWIGLU_LIMIT)
        lin = jnp.clip(a[:, D_FF:], -SWIGLU_LIMIT, SWIGLU_LIMIT)
        act = glu * jax.nn.sigmoid(SWIGLU_ALPHA * glu) * (lin + 1.0)
        o_ref[...] = jnp.dot(act.astype(BF16), w2b_ref[...],
                             preferred_element_type=F32) + b2_ref[...]

    @pl.when(jnp.logical_not(live))
    def _():
        o_ref[...] = jnp.zeros_like(o_ref)


def _experts(xs, blk_e, n_used, w1, b1, w2, b2):
    n_rows, d = xs.shape
    nblk = n_rows // MOE_BLOCK
    row = lambda i, be, nu: (jnp.minimum(i, nu[0] - 1), 0)
    out_row = lambda i, be, nu: (i, 0)
    exp3 = lambda i, be, nu: (be[jnp.minimum(i, nu[0] - 1)], 0, 0)
    return pl.pallas_call(
        _expert_kernel,
        out_shape=jax.ShapeDtypeStruct((n_rows, d), F32),
        grid_spec=pltpu.PrefetchScalarGridSpec(
            num_scalar_prefetch=2,
            grid=(nblk,),
            in_specs=[
                pl.BlockSpec((MOE_BLOCK, d), row),
                pl.BlockSpec((None, d, 2 * D_FF), exp3),
                pl.BlockSpec((None, 1, 2 * D_FF), exp3),
                pl.BlockSpec((None, D_FF, d), exp3),
                pl.BlockSpec((None, 1, d), exp3),
            ],
            out_specs=pl.BlockSpec((MOE_BLOCK, d), out_row),
            scratch_shapes=[
                pltpu.VMEM((d, 2 * D_FF), BF16),
                pltpu.VMEM((D_FF, d), BF16),
            ],
        ),
        compiler_params=pltpu.CompilerParams(
            dimension_semantics=("arbitrary",), vmem_limit_bytes=VMEM_LIMIT_BYTES),
        name="moe_experts",
    )(blk_e, n_used, xs, w1, b1.reshape(N_EXPERTS, 1, 2 * D_FF), w2,
      b2.reshape(N_EXPERTS, 1, d))


def _combine_kernel(dcur_ref, dnext_ref, x_ref, g_ref, ys_ref, gf_ref, o_ref, buf, sem,
                    *, tc, n_tiles, final_norm):
    i = pl.program_id(0)

    def issue(dref, slot):
        def body(t, carry):
            for k in range(TOP_K):
                d = dref[0, 0, k * tc + t]
                pltpu.make_async_copy(ys_ref.at[pl.ds(d, 1), :],
                                      buf.at[slot, k, pl.ds(t, 1), :], sem.at[slot]).start()
            return carry
        lax.fori_loop(0, tc, body, 0)

    @pl.when(i == 0)
    def _():
        issue(dcur_ref, 0)

    @pl.when(i + 1 < n_tiles)
    def _():
        issue(dnext_ref, (i + 1) % 2)

    slot = i % 2
    for k in range(TOP_K):
        pltpu.make_async_copy(ys_ref.at[pl.ds(0, tc), :], buf.at[slot, k], sem.at[slot]).wait()
    acc = x_ref[...]
    gates = g_ref[...]
    for k in range(TOP_K):
        acc = acc + gates[:, k:k + 1] * buf[slot, k]
    if final_norm:
        acc = _rms(acc, gf_ref[...])
    o_ref[...] = acc


def _combine(x2d, gates_t, dest, ys, g_final, final_norm):
    t, d = x2d.shape
    tc = min(COMBINE_TILE, t)
    nt = t // tc
    dest_tiles = dest.reshape(TOP_K, nt, tc).transpose(1, 0, 2).reshape(nt, 1, TOP_K * tc)
    kern = functools.partial(_combine_kernel, tc=tc, n_tiles=nt, final_norm=final_norm)
    return pl.pallas_call(
        kern,
        out_shape=jax.ShapeDtypeStruct((t, d), F32),
        grid=(nt,),
        in_specs=[
            pl.BlockSpec((1, 1, TOP_K * tc), lambda i: (i, 0, 0), memory_space=pltpu.SMEM),
            pl.BlockSpec((1, 1, TOP_K * tc), lambda i: (jnp.minimum(i + 1, nt - 1), 0, 0),
                         memory_space=pltpu.SMEM),
            pl.BlockSpec((tc, d), lambda i: (i, 0)),
            pl.BlockSpec((tc, TOP_K), lambda i: (i, 0)),
            pl.BlockSpec(memory_space=pl.ANY),
            pl.BlockSpec((1, d), lambda i: (0, 0)),
        ],
        out_specs=pl.BlockSpec((tc, d), lambda i: (i, 0)),
        scratch_shapes=[
            pltpu.VMEM((2, TOP_K, tc, d), F32),
            pltpu.SemaphoreType.DMA((2,)),
        ],
        compiler_params=pltpu.CompilerParams(
            dimension_semantics=("arbitrary",), vmem_limit_bytes=VMEM_LIMIT_BYTES),
        name="moe_combine_final" if final_norm else "moe_combine",
    )(dest_tiles, dest_tiles, x2d, gates_t.T, ys, g_final.reshape(1, d))


def _moe_layer(x2d, g_ffn, w_router, b_router, w1, b1, w2, b2, g_final, final_norm):
    t, d = x2d.shape
    h, ids, gates, rank, sizes = _route(x2d, g_ffn, w_router, b_router)
    sizes = sizes[:, 0].astype(I32)
    padded = ((sizes + MOE_BLOCK - 1) // MOE_BLOCK) * MOE_BLOCK
    pend = jnp.cumsum(padded)
    pstart = pend - padded
    onehot = ids[:, :, None] == jnp.arange(N_EXPERTS, dtype=I32)
    dest = rank + jnp.sum(jnp.where(onehot, pstart, 0), axis=-1)
    n_rows = t * TOP_K + N_EXPERTS * MOE_BLOCK
    nblk = n_rows // MOE_BLOCK
    blk_e = jnp.minimum(
        jnp.searchsorted(pend, jnp.arange(nblk, dtype=I32) * MOE_BLOCK, side="right"),
        N_EXPERTS - 1).astype(I32)
    n_used = (pend[-1:] // MOE_BLOCK).astype(I32)
    zrow = jnp.where(padded > 0, pend - MOE_BLOCK, -1).astype(I32)
    xs = _dispatch(h, dest, zrow, n_used, n_rows)
    ys = _experts(xs, blk_e, n_used, w1, b1, w2, b2)
    return _combine(x2d, gates, dest, ys, g_final, final_norm)


def _qkv_kernel(x_ref, g_ref, w_ref, b_ref, q_ref, k_ref, v_ref):
    h = _rms(x_ref[...], g_ref[...])
    qkv = jnp.dot(h.astype(BF16), w_ref[...], preferred_element_type=F32) + b_ref[...]
    nq = N_HEADS * HEAD_DIM
    nk = 2 * N_KV_HEADS * HEAD_DIM
    q_ref[...] = (qkv[:, :nq] * (HEAD_DIM ** -0.5)).astype(BF16)
    k_ref[...] = qkv[:, nq:nq + nk].astype(BF16)
    v_ref[...] = qkv[:, nq + nk:].astype(BF16)


def _qkv(x2d, g, w_qkv, b_qkv):
    t, d = x2d.shape
    ts = min(ROW_TILE, t)
    nq = N_HEADS * HEAD_DIM
    kv = N_KV_HEADS * HEAD_DIM

    def dup_heads(m):
        parts = []
        for hd in range(N_KV_HEADS):
            sl = m[..., hd * HEAD_DIM:(hd + 1) * HEAD_DIM]
            parts += [sl, sl]
        return jnp.concatenate(parts, axis=-1)

    w = jnp.concatenate([w_qkv[:, :nq], dup_heads(w_qkv[:, nq:nq + kv]),
                         dup_heads(w_qkv[:, nq + kv:])], axis=1).astype(BF16)
    b = jnp.concatenate([b_qkv[:nq], dup_heads(b_qkv[nq:nq + kv]),
                         dup_heads(b_qkv[nq + kv:])]).reshape(1, -1)
    n_out = w.shape[1]
    return pl.pallas_call(
        _qkv_kernel,
        out_shape=(
            jax.ShapeDtypeStruct((t, nq), BF16),
            jax.ShapeDtypeStruct((t, 2 * kv), BF16),
            jax.ShapeDtypeStruct((t, 2 * kv), BF16),
        ),
        grid=(t // ts,),
        in_specs=[
            pl.BlockSpec((ts, d), lambda i: (i, 0)),
            pl.BlockSpec((1, d), lambda i: (0, 0)),
            pl.BlockSpec((d, n_out), lambda i: (0, 0)),
            pl.BlockSpec((1, n_out), lambda i: (0, 0)),
        ],
        out_specs=(
            pl.BlockSpec((ts, nq), lambda i: (i, 0)),
            pl.BlockSpec((ts, 2 * kv), lambda i: (i, 0)),
            pl.BlockSpec((ts, 2 * kv), lambda i: (i, 0)),
        ),
        compiler_params=pltpu.CompilerParams(
            dimension_semantics=("arbitrary",), vmem_limit_bytes=VMEM_LIMIT_BYTES),
        name="attn_qkv",
    )(x2d, g.reshape(1, d), w, b)


def _attn_kernel(sinks_ref, q_ref, kp_ref, kc_ref, vp_ref, vc_ref, bias_ref, o_ref):
    n = pl.program_id(1)
    rows = SLABS_PER_GROUP * ATT_BLOCK
    q = q_ref[...]
    kcat = jnp.concatenate([kp_ref[...], kc_ref[...]], axis=0)
    vcat = jnp.concatenate([vp_ref[...], vc_ref[...]], axis=0)
    r = lax.broadcasted_iota(I32, (rows, 2 * ATT_BLOCK), 0) & (ATT_BLOCK - 1)
    j = lax.broadcasted_iota(I32, (rows, 2 * ATT_BLOCK), 1)
    lo = jnp.where(n == 0, jnp.maximum(r, ATT_BLOCK - 1), r)
    mask = (j > lo) & (j <= r + WINDOW)
    lane_k = lax.broadcasted_iota(I32, (2 * ATT_BLOCK, 128), 1)
    lane_o = lax.broadcasted_iota(I32, (rows, 128), 1)
    dn = (((1,), (1,)), ((), ()))
    zero = jnp.zeros((), BF16)
    for g in range(N_KV_HEADS):
        s0 = g * SLABS_PER_GROUP
        qg = jnp.concatenate([q[:, (s0 + s) * 128:(s0 + s + 1) * 128]
                              for s in range(SLABS_PER_GROUP)], axis=0)
        kg = kcat[:, g * 128:(g + 1) * 128]
        vg = vcat[:, g * 128:(g + 1) * 128]
        halves = [lane_k < HEAD_DIM, lane_k >= HEAD_DIM]
        ps, invs = [], []
        for p in range(HEADS_PER_SLAB):
            kp = jnp.where(halves[p], kg, zero)
            sc = lax.dot_general(qg, kp, dn, preferred_element_type=F32)
            sc = jnp.where(mask, sc + bias_ref[g, p], NEG_INF)
            sink = jnp.concatenate(
                [jnp.full((ATT_BLOCK, 1), sinks_ref[g * GQA_GROUP + s * HEADS_PER_SLAB + p], F32)
                 for s in range(SLABS_PER_GROUP)], axis=0)
            m = jnp.maximum(jnp.max(sc, axis=-1, keepdims=True), sink)
            pe = jnp.exp(sc - m)
            den = jnp.sum(pe, axis=-1, keepdims=True) + jnp.exp(sink - m)
            ps.append(pe.astype(BF16))
            invs.append(1.0 / den)
        pcat = jnp.concatenate(ps, axis=1)
        vblk = jnp.concatenate([jnp.where(halves[p], vg, zero)
                                for p in range(HEADS_PER_SLAB)], axis=0)
        og = jnp.dot(pcat, vblk, preferred_element_type=F32)
        og = og * jnp.where(lane_o < HEAD_DIM, invs[0], invs[1])
        for s in range(SLABS_PER_GROUP):
            o_ref[:, (s0 + s) * 128:(s0 + s + 1) * 128] = (
                og[s * ATT_BLOCK:(s + 1) * ATT_BLOCK].astype(BF16))


def _t5_bias_table(rel_bias):
    r = np.arange(ATT_BLOCK)[:, None]
    j = np.arange(2 * ATT_BLOCK)[None, :]
    rel = np.clip(ATT_BLOCK + r - j, 0, WINDOW - 1)
    max_exact = N_BUCKETS // 2
    nf = jnp.maximum(rel, max_exact).astype(F32)
    large = max_exact + (jnp.log(nf / max_exact) / math.log(MAX_DISTANCE / max_exact)
                         * (N_BUCKETS - max_exact)).astype(I32)
    large = jnp.minimum(large, N_BUCKETS - 1)
    bucket = jnp.where(rel < max_exact, rel, large)
    bias = jnp.transpose(rel_bias[bucket].astype(F32), (2, 0, 1))
    bias = bias.reshape(N_KV_HEADS, SLABS_PER_GROUP, HEADS_PER_SLAB, ATT_BLOCK, 2 * ATT_BLOCK)
    bias = jnp.transpose(bias, (0, 2, 1, 3, 4))
    return bias.reshape(N_KV_HEADS, HEADS_PER_SLAB, SLABS_PER_GROUP * ATT_BLOCK, 2 * ATT_BLOCK)


def _attention(q, kk, vv, sinks, rel_bias, bsz, seq):
    nq = N_HEADS * HEAD_DIM
    kvw = kk.shape[-1]
    nb = seq // ATT_BLOCK
    q3 = q.reshape(bsz, seq, nq)
    k3 = kk.reshape(bsz, seq, kvw)
    v3 = vv.reshape(bsz, seq, kvw)
    bias = _t5_bias_table(rel_bias)
    prev = lambda bi, n: (bi, jnp.maximum(n - 1, 0), 0)
    cur = lambda bi, n: (bi, n, 0)
    out = pl.pallas_call(
        _attn_kernel,
        out_shape=jax.ShapeDtypeStruct((bsz, seq, nq), BF16),
        grid=(bsz, nb),
        in_specs=[
            pl.BlockSpec(memory_space=pltpu.SMEM),
            pl.BlockSpec((None, ATT_BLOCK, nq), cur),
            pl.BlockSpec((None, ATT_BLOCK, kvw), prev),
            pl.BlockSpec((None, ATT_BLOCK, kvw), cur),
            pl.BlockSpec((None, ATT_BLOCK, kvw), prev),
            pl.BlockSpec((None, ATT_BLOCK, kvw), cur),
            pl.BlockSpec(bias.shape, lambda bi, n: (0, 0, 0, 0)),
        ],
        out_specs=pl.BlockSpec((None, ATT_BLOCK, nq), cur),
        compiler_params=pltpu.CompilerParams(
            dimension_semantics=("arbitrary", "arbitrary"),
            vmem_limit_bytes=VMEM_LIMIT_BYTES),
        name="attn_core",
    )(sinks.astype(F32), q3, k3, k3, v3, v3, bias)
    return out.reshape(bsz * seq, nq)


def _oproj_kernel(x_ref, o_ref, w_ref, b_ref, y_ref):
    y_ref[...] = x_ref[...] + jnp.dot(o_ref[...], w_ref[...],
                                      preferred_element_type=F32) + b_ref[...]


def _out_proj(x2d, o, w_o, b_o):
    t, d = x2d.shape
    ts = min(ROW_TILE, t)
    nq = o.shape[1]
    return pl.pallas_call(
        _oproj_kernel,
        out_shape=jax.ShapeDtypeStruct((t, d), F32),
        grid=(t // ts,),
        in_specs=[
            pl.BlockSpec((ts, d), lambda i: (i, 0)),
            pl.BlockSpec((ts, nq), lambda i: (i, 0)),
            pl.BlockSpec((nq, d), lambda i: (0, 0)),
            pl.BlockSpec((1, d), lambda i: (0, 0)),
        ],
        out_specs=pl.BlockSpec((ts, d), lambda i: (i, 0)),
        compiler_params=pltpu.CompilerParams(
            dimension_semantics=("arbitrary",), vmem_limit_bytes=VMEM_LIMIT_BYTES),
        name="attn_out_proj",
    )(x2d, o, w_o.astype(BF16), b_o.reshape(1, d))


def kernel(x, norm_mix, norm_ffn, norm_final, pool_w, pool_b, pool_scale, w_qkv, b_qkv,
           sinks, w_o, b_o, rel_bias, w_router, b_router, w1, b1, w2, b2):
    bsz, seq, d = x.shape
    t = bsz * seq
    x = _pool_layer(x, norm_mix[0], pool_w[0], pool_b[0], pool_scale[0]).reshape(t, d)
    x = _moe_layer(x, norm_ffn[0], w_router[0], b_router[0], w1[0], b1[0], w2[0], b2[0],
                   norm_final, final_norm=False)
    q, kk, vv = _qkv(x, norm_mix[1], w_qkv[0], b_qkv[0])
    o = _attention(q, kk, vv, sinks[0], rel_bias, bsz, seq)
    x = _out_proj(x, o, w_o[0], b_o[0])
    x = _moe_layer(x, norm_ffn[1], w_router[1], b_router[1], w1[1], b1[1], w2[1], b2[1],
                   norm_final, final_norm=True)
    return x.reshape(bsz, seq, d)
```

```python
import functools
import math

import jax
import jax.numpy as jnp
import numpy as np
from jax import lax
from jax.experimental import pallas as pl
from jax.experimental.pallas import tpu as pltpu

F32 = jnp.float32
BF16 = jnp.bfloat16
I32 = jnp.int32

D_MODEL = 1024
POOL_WINDOWS = (2, 4, 8, 16)
POOL_GROUP_CH = D_MODEL // len(POOL_WINDOWS)
POOL_HALO = 16
HEAD_DIM = 64
N_HEADS = 16
N_KV_HEADS = 2
GQA_GROUP = N_HEADS // N_KV_HEADS
HEADS_PER_SLAB = 128 // HEAD_DIM
SLABS_PER_GROUP = GQA_GROUP // HEADS_PER_SLAB
ATT_BLOCK = 128
WINDOW = 128
N_BUCKETS = 32
MAX_DISTANCE = 128
N_EXPERTS = 32
TOP_K = 4
D_FF = D_MODEL
SWIGLU_ALPHA = 1.702
SWIGLU_LIMIT = 7.0
RMS_EPS = 1e-5
NEG_INF = -1e30

ROW_TILE = 512
MOE_BLOCK = 512
DISPATCH_TILE = 1024
COMBINE_TILE = 256
VMEM_LIMIT_BYTES = 56 * 1024 * 1024


def _rms(x, g):
    return x * lax.rsqrt(jnp.mean(x * x, axis=-1, keepdims=True) + RMS_EPS) * g


def _pool_kernel(x_ref, xh_ref, g_ref, w_ref, b_ref, s_ref, o_ref, buf_ref, *, ts):
    i = pl.program_id(1)
    x = x_ref[...]
    g = g_ref[...]
    h = _rms(x, g)
    hh = _rms(xh_ref[...], g)
    hh = jnp.where(i == 0, 0.0, hh)
    buf_ref[0:POOL_HALO, :] = hh
    buf_ref[POOL_HALO:POOL_HALO + ts, :] = h
    t = i * ts + lax.broadcasted_iota(I32, (ts, 1), 0)
    outs = []
    for gi, win in enumerate(POOL_WINDOWS):
        c0 = gi * POOL_GROUP_CH
        hg = h[:, c0:c0 + POOL_GROUP_CH]
        acc = hg
        for d in range(1, win):
            acc = acc + buf_ref[POOL_HALO - d:POOL_HALO - d + ts, c0:c0 + POOL_GROUP_CH]
        cnt = jnp.minimum(t + 1, win).astype(F32)
        dlt = acc / cnt - hg
        outs.append(jnp.dot(dlt.astype(BF16), w_ref[gi], preferred_element_type=F32))
    y = jnp.concatenate(outs, axis=1)
    o_ref[...] = x + (y + b_ref[...]) * s_ref[...]


def _pool_layer(x, g, w, b, s):
    bsz, seq, d = x.shape
    ts = min(ROW_TILE, seq)
    kern = functools.partial(_pool_kernel, ts=ts)
    vec = lambda: pl.BlockSpec((1, d), lambda bi, i: (0, 0))
    return pl.pallas_call(
        kern,
        out_shape=jax.ShapeDtypeStruct(x.shape, F32),
        grid=(bsz, seq // ts),
        in_specs=[
            pl.BlockSpec((None, ts, d), lambda bi, i: (bi, i, 0)),
            pl.BlockSpec((None, POOL_HALO, d),
                         lambda bi, i: (bi, jnp.maximum(i * (ts // POOL_HALO) - 1, 0), 0)),
            vec(),
            pl.BlockSpec(w.shape, lambda bi, i: (0, 0, 0)),
            vec(),
            vec(),
        ],
        out_specs=pl.BlockSpec((None, ts, d), lambda bi, i: (bi, i, 0)),
        scratch_shapes=[pltpu.VMEM((POOL_HALO + ts, d), F32)],
        compiler_params=pltpu.CompilerParams(
            dimension_semantics=("arbitrary", "arbitrary"),
            vmem_limit_bytes=VMEM_LIMIT_BYTES),
        name="pool_layer",
    )(x, x, g.reshape(1, d), w.astype(BF16), b.reshape(1, d), s.reshape(1, d))


def _router_kernel(x_ref, g_ref, wr_ref, br_ref,
                   h_ref, ids_ref, gates_ref, rank_ref, sizes_ref, carry_ref, *, ts):
    i = pl.program_id(0)

    @pl.when(i == 0)
    def _():
        carry_ref[...] = jnp.zeros_like(carry_ref)

    h = _rms(x_ref[...], g_ref[...])
    h_ref[...] = h
    h_hi = h.astype(BF16)
    h_lo = (h - h_hi.astype(F32)).astype(BF16)
    w = wr_ref[...]
    w_hi = w.astype(BF16)
    w_lo = (w - w_hi.astype(F32)).astype(BF16)
    dn = (((1,), (1,)), ((), ()))
    lg = (lax.dot_general(w_hi, h_hi, dn, preferred_element_type=F32)
          + lax.dot_general(w_lo, h_hi, dn, preferred_element_type=F32)
          + lax.dot_general(w_hi, h_lo, dn, preferred_element_type=F32))
    lg = lg + br_ref[:, 0:1]

    iota_e = lax.broadcasted_iota(I32, (N_EXPERTS, ts), 0)
    vals, ids = [], []
    for _ in range(TOP_K):
        m = jnp.max(lg, axis=0, keepdims=True)
        idx = jnp.min(jnp.where(lg == m, iota_e, N_EXPERTS), axis=0, keepdims=True)
        vals.append(m)
        ids.append(idx)
        lg = jnp.where(iota_e == idx, -jnp.inf, lg)
    ex = [jnp.exp(v - vals[0]) for v in vals]
    den = ex[0] + ex[1] + ex[2] + ex[3]
    gates_ref[...] = jnp.concatenate([e / den for e in ex], axis=0)
    ids_ref[...] = jnp.concatenate(ids, axis=0)

    hot = [(iota_e == idx) for idx in ids]
    multi = (hot[0] | hot[1] | hot[2] | hot[3])
    multi_f = multi.astype(F32)
    r = lax.broadcasted_iota(I32, (ts, ts), 0)
    c = lax.broadcasted_iota(I32, (ts, ts), 1)
    upper = (r < c).astype(BF16)
    before = jnp.dot(multi_f.astype(BF16), upper, preferred_element_type=F32)
    before = before + carry_ref[:, 0:1]
    ranks = [jnp.sum(jnp.where(hk, before, 0.0), axis=0, keepdims=True) for hk in hot]
    rank_ref[...] = jnp.concatenate(ranks, axis=0).astype(I32)
    carry_ref[...] = carry_ref[...] + jnp.sum(multi_f, axis=1, keepdims=True)
    sizes_ref[...] = carry_ref[...]


def _route(x2d, g, w_router, b_router):
    t, d = x2d.shape
    ts = min(ROW_TILE, t)
    kern = functools.partial(_router_kernel, ts=ts)
    tok = lambda: pl.BlockSpec((TOP_K, ts), lambda i: (0, i))
    return pl.pallas_call(
        kern,
        out_shape=(
            jax.ShapeDtypeStruct((t, d), F32),
            jax.ShapeDtypeStruct((TOP_K, t), I32),
            jax.ShapeDtypeStruct((TOP_K, t), F32),
            jax.ShapeDtypeStruct((TOP_K, t), I32),
            jax.ShapeDtypeStruct((N_EXPERTS, 128), F32),
        ),
        grid=(t // ts,),
        in_specs=[
            pl.BlockSpec((ts, d), lambda i: (i, 0)),
            pl.BlockSpec((1, d), lambda i: (0, 0)),
            pl.BlockSpec((N_EXPERTS, d), lambda i: (0, 0)),
            pl.BlockSpec((N_EXPERTS, 128), lambda i: (0, 0)),
        ],
        out_specs=(
            pl.BlockSpec((ts, d), lambda i: (i, 0)),
            tok(), tok(), tok(),
            pl.BlockSpec((N_EXPERTS, 128), lambda i: (0, 0)),
        ),
        scratch_shapes=[pltpu.VMEM((N_EXPERTS, 128), F32)],
        compiler_params=pltpu.CompilerParams(
            dimension_semantics=("arbitrary",), vmem_limit_bytes=VMEM_LIMIT_BYTES),
        name="moe_route",
    )(x2d, g.reshape(1, d), w_router.T,
      jnp.broadcast_to(b_router.reshape(N_EXPERTS, 1), (N_EXPERTS, 128)))


def _dispatch_kernel(zrow_ref, n_used_ref, dest_ref, h_ref, xs_ref, zbuf, sem_z, sem,
                     *, td, blk, nblk):
    i = pl.program_id(0)

    def zero_block(row):
        row = pl.multiple_of(row, blk)
        return pltpu.make_async_copy(zbuf, xs_ref.at[pl.ds(row, blk), :], sem_z)

    @pl.when(i == 0)
    def _():
        zbuf[...] = jnp.zeros_like(zbuf)
        for e in range(N_EXPERTS):
            @pl.when(zrow_ref[e] >= 0)
            def _():
                zero_block(zrow_ref[e]).start()

        def start_tail(b, carry):
            zero_block(b * blk).start()
            return carry

        def wait_tail(b, carry):
            zero_block(b * blk).wait()
            return carry

        lax.fori_loop(n_used_ref[0], nblk, start_tail, 0)
        for e in range(N_EXPERTS):
            @pl.when(zrow_ref[e] >= 0)
            def _():
                zero_block(zrow_ref[e]).wait()
        lax.fori_loop(n_used_ref[0], nblk, wait_tail, 0)

    def issue(t, carry):
        for k in range(TOP_K):
            d = dest_ref[0, 0, k * td + t]
            pltpu.make_async_copy(h_ref.at[pl.ds(t, 1), :],
                                  xs_ref.at[pl.ds(d, 1), :], sem).start()
        return carry

    lax.fori_loop(0, td, issue, 0)
    for k in range(TOP_K):
        pltpu.make_async_copy(h_ref, xs_ref.at[pl.ds(0, td), :], sem).wait()


def _dispatch(h, dest, zrow, n_used, n_rows):
    t, d = h.shape
    td = min(DISPATCH_TILE, t)
    nt = t // td
    dest_tiles = dest.reshape(TOP_K, nt, td).transpose(1, 0, 2).reshape(nt, 1, TOP_K * td)
    kern = functools.partial(_dispatch_kernel, td=td, blk=MOE_BLOCK, nblk=n_rows // MOE_BLOCK)
    return pl.pallas_call(
        kern,
        out_shape=jax.ShapeDtypeStruct((n_rows, d), F32),
        grid_spec=pltpu.PrefetchScalarGridSpec(
            num_scalar_prefetch=2,
            grid=(nt,),
            in_specs=[
                pl.BlockSpec((1, 1, TOP_K * td), lambda i, z, nu: (i, 0, 0),
                             memory_space=pltpu.SMEM),
                pl.BlockSpec((td, d), lambda i, z, nu: (i, 0)),
            ],
            out_specs=pl.BlockSpec(memory_space=pl.ANY),
            scratch_shapes=[
                pltpu.VMEM((MOE_BLOCK, d), F32),
                pltpu.SemaphoreType.DMA(()),
                pltpu.SemaphoreType.DMA(()),
            ],
        ),
        compiler_params=pltpu.CompilerParams(
            dimension_semantics=("arbitrary",), vmem_limit_bytes=VMEM_LIMIT_BYTES),
        name="moe_dispatch",
    )(zrow, n_used, dest_tiles, h)


def _expert_kernel(blk_e_ref, n_used_ref, x_ref, w1_ref, b1_ref, w2_ref, b2_ref,
                   o_ref, w1b_ref, w2b_ref):
    i = pl.program_id(0)
    prev = blk_e_ref[jnp.maximum(i - 1, 0)]
    fresh = (i == 0) | (blk_e_ref[i] != prev)
    live = i < n_used_ref[0]

    @pl.when(live & fresh)
    def _():
        w1b_ref[...] = w1_ref[...].astype(BF16)
        w2b_ref[...] = w2_ref[...].astype(BF16)

    @pl.when(live)
    def _():
        x = x_ref[...].astype(BF16)
        a = jnp.dot(x, w1b_ref[...], preferred_element_type=F32) + b1_ref[...]
        glu = jnp.minimum(a[:, :D_FF], SWIGLU_LIMIT)
        lin = jnp.clip(a[:, D_FF:], -SWIGLU_LIMIT, SWIGLU_LIMIT)
        act = glu * jax.nn.sigmoid(SWIGLU_ALPHA * glu) * (lin + 1.0)
        o_ref[...] = jnp.dot(act.astype(BF16), w2b_ref[...],
                             preferred_element_type=F32) + b2_ref[...]

    @pl.when(jnp.logical_not(live))
    def _():
        o_ref[...] = jnp.zeros_like(o_ref)


def _experts(xs, blk_e, n_used, layer, w1, b1, w2, b2):
    n_rows, d = xs.shape
    nblk = n_rows // MOE_BLOCK
    depth = w1.shape[0]
    row = lambda i, be, nu: (jnp.minimum(i, nu[0] - 1), 0)
    out_row = lambda i, be, nu: (i, 0)
    exp3 = lambda i, be, nu: (layer, be[jnp.minimum(i, nu[0] - 1)], 0, 0)
    return pl.pallas_call(
        _expert_kernel,
        out_shape=jax.ShapeDtypeStruct((n_rows, d), F32),
        grid_spec=pltpu.PrefetchScalarGridSpec(
            num_scalar_prefetch=2,
            grid=(nblk,),
            in_specs=[
                pl.BlockSpec((MOE_BLOCK, d), row),
                pl.BlockSpec((None, None, d, 2 * D_FF), exp3),
                pl.BlockSpec((None, None, 1, 2 * D_FF), exp3),
                pl.BlockSpec((None, None, D_FF, d), exp3),
                pl.BlockSpec((None, None, 1, d), exp3),
            ],
            out_specs=pl.BlockSpec((MOE_BLOCK, d), out_row),
            scratch_shapes=[
                pltpu.VMEM((d, 2 * D_FF), BF16),
                pltpu.VMEM((D_FF, d), BF16),
            ],
        ),
        compiler_params=pltpu.CompilerParams(
            dimension_semantics=("arbitrary",), vmem_limit_bytes=VMEM_LIMIT_BYTES),
        name="moe_experts",
    )(blk_e, n_used, xs, w1, b1.reshape(depth, N_EXPERTS, 1, 2 * D_FF), w2,
      b2.reshape(depth, N_EXPERTS, 1, d))


def _combine_kernel(dcur_ref, dnext_ref, x_ref, g_ref, ys_ref, gf_ref, o_ref, buf, sem,
                    *, tc, n_tiles, final_norm):
    i = pl.program_id(0)

    def issue(dref, slot):
        def body(t, carry):
            for k in range(TOP_K):
                d = dref[0, 0, k * tc + t]
                pltpu.make_async_copy(ys_ref.at[pl.ds(d, 1), :],
                                      buf.at[slot, k, pl.ds(t, 1), :], sem.at[slot]).start()
            return carry
        lax.fori_loop(0, tc, body, 0)

    @pl.when(i == 0)
    def _():
        issue(dcur_ref, 0)

    @pl.when(i + 1 < n_tiles)
    def _():
        issue(dnext_ref, (i + 1) % 2)

    slot = i % 2
    for k in range(TOP_K):
        pltpu.make_async_copy(ys_ref.at[pl.ds(0, tc), :], buf.at[slot, k], sem.at[slot]).wait()
    acc = x_ref[...]
    gates = g_ref[...]
    for k in range(TOP_K):
        acc = acc + gates[:, k:k + 1] * buf[slot, k]
    if final_norm:
        acc = _rms(acc, gf_ref[...])
    o_ref[...] = acc


def _combine(x2d, gates_t, dest, ys, g_final, final_norm):
    t, d = x2d.shape
    tc = min(COMBINE_TILE, t)
    nt = t // tc
    dest_tiles = dest.reshape(TOP_K, nt, tc).transpose(1, 0, 2).reshape(nt, 1, TOP_K * tc)
    kern = functools.partial(_combine_kernel, tc=tc, n_tiles=nt, final_norm=final_norm)
    return pl.pallas_call(
        kern,
        out_shape=jax.ShapeDtypeStruct((t, d), F32),
        grid=(nt,),
        in_specs=[
            pl.BlockSpec((1, 1, TOP_K * tc), lambda i: (i, 0, 0), memory_space=pltpu.SMEM),
            pl.BlockSpec((1, 1, TOP_K * tc), lambda i: (jnp.minimum(i + 1, nt - 1), 0, 0),
                         memory_space=pltpu.SMEM),
            pl.BlockSpec((tc, d), lambda i: (i, 0)),
            pl.BlockSpec((tc, TOP_K), lambda i: (i, 0)),
            pl.BlockSpec(memory_space=pl.ANY),
            pl.BlockSpec((1, d), lambda i: (0, 0)),
        ],
        out_specs=pl.BlockSpec((tc, d), lambda i: (i, 0)),
        scratch_shapes=[
            pltpu.VMEM((2, TOP_K, tc, d), F32),
            pltpu.SemaphoreType.DMA((2,)),
        ],
        compiler_params=pltpu.CompilerParams(
            dimension_semantics=("arbitrary",), vmem_limit_bytes=VMEM_LIMIT_BYTES),
        name="moe_combine_final" if final_norm else "moe_combine",
    )(dest_tiles, dest_tiles, x2d, gates_t.T, ys, g_final.reshape(1, d))


def _moe_layer(x2d, g_ffn, w_router, b_router, layer, w1, b1, w2, b2, g_final, final_norm):
    t, d = x2d.shape
    h, ids, gates, rank, sizes = _route(x2d, g_ffn, w_router, b_router)
    sizes = sizes[:, 0].astype(I32)
    padded = ((sizes + MOE_BLOCK - 1) // MOE_BLOCK) * MOE_BLOCK
    pend = jnp.cumsum(padded)
    pstart = pend - padded
    onehot = ids[:, :, None] == jnp.arange(N_EXPERTS, dtype=I32)
    dest = rank + jnp.sum(jnp.where(onehot, pstart, 0), axis=-1)
    n_rows = t * TOP_K + N_EXPERTS * MOE_BLOCK
    nblk = n_rows // MOE_BLOCK
    blk_row = jnp.arange(nblk, dtype=I32) * MOE_BLOCK
    blk_e = jnp.minimum(jnp.sum((pend[None, :] <= blk_row[:, None]).astype(I32), axis=1),
                        N_EXPERTS - 1)
    n_used = (pend[-1:] // MOE_BLOCK).astype(I32)
    zrow = jnp.where(padded > 0, pend - MOE_BLOCK, -1).astype(I32)
    xs = _dispatch(h, dest, zrow, n_used, n_rows)
    ys = _experts(xs, blk_e, n_used, layer, w1, b1, w2, b2)
    return _combine(x2d, gates, dest, ys, g_final, final_norm)


def _qkv_kernel(x_ref, g_ref, w_ref, b_ref, q_ref, k_ref, v_ref):
    h = _rms(x_ref[...], g_ref[...])
    qkv = jnp.dot(h.astype(BF16), w_ref[...], preferred_element_type=F32) + b_ref[...]
    nq = N_HEADS * HEAD_DIM
    nk = 2 * N_KV_HEADS * HEAD_DIM
    q_ref[...] = (qkv[:, :nq] * (HEAD_DIM ** -0.5)).astype(BF16)
    k_ref[...] = qkv[:, nq:nq + nk].astype(BF16)
    v_ref[...] = qkv[:, nq + nk:].astype(BF16)


def _qkv(x2d, g, w_qkv, b_qkv):
    t, d = x2d.shape
    ts = min(ROW_TILE, t)
    nq = N_HEADS * HEAD_DIM
    kv = N_KV_HEADS * HEAD_DIM

    def dup_heads(m):
        parts = []
        for hd in range(N_KV_HEADS):
            sl = m[..., hd * HEAD_DIM:(hd + 1) * HEAD_DIM]
            parts += [sl, sl]
        return jnp.concatenate(parts, axis=-1)

    w = jnp.concatenate([w_qkv[:, :nq], dup_heads(w_qkv[:, nq:nq + kv]),
                         dup_heads(w_qkv[:, nq + kv:])], axis=1).astype(BF16)
    b = jnp.concatenate([b_qkv[:nq], dup_heads(b_qkv[nq:nq + kv]),
                         dup_heads(b_qkv[nq + kv:])]).reshape(1, -1)
    n_out = w.shape[1]
    return pl.pallas_call(
        _qkv_kernel,
        out_shape=(
            jax.ShapeDtypeStruct((t, nq), BF16),
            jax.ShapeDtypeStruct((t, 2 * kv), BF16),
            jax.ShapeDtypeStruct((t, 2 * kv), BF16),
        ),
        grid=(t // ts,),
        in_specs=[
            pl.BlockSpec((ts, d), lambda i: (i, 0)),
            pl.BlockSpec((1, d), lambda i: (0, 0)),
            pl.BlockSpec((d, n_out), lambda i: (0, 0)),
            pl.BlockSpec((1, n_out), lambda i: (0, 0)),
        ],
        out_specs=(
            pl.BlockSpec((ts, nq), lambda i: (i, 0)),
            pl.BlockSpec((ts, 2 * kv), lambda i: (i, 0)),
            pl.BlockSpec((ts, 2 * kv), lambda i: (i, 0)),
        ),
        compiler_params=pltpu.CompilerParams(
            dimension_semantics=("arbitrary",), vmem_limit_bytes=VMEM_LIMIT_BYTES),
        name="attn_qkv",
    )(x2d, g.reshape(1, d), w, b)


def _attn_kernel(sinks_ref, q_ref, kp_ref, kc_ref, vp_ref, vc_ref, bias_ref, o_ref):
    n = pl.program_id(1)
    rows = SLABS_PER_GROUP * ATT_BLOCK
    q = q_ref[...]
    kcat = jnp.concatenate([kp_ref[...], kc_ref[...]], axis=0)
    vcat = jnp.concatenate([vp_ref[...], vc_ref[...]], axis=0)
    r = lax.broadcasted_iota(I32, (rows, 2 * ATT_BLOCK), 0) & (ATT_BLOCK - 1)
    j = lax.broadcasted_iota(I32, (rows, 2 * ATT_BLOCK), 1)
    lo = jnp.where(n == 0, jnp.maximum(r, ATT_BLOCK - 1), r)
    mask = (j > lo) & (j <= r + WINDOW)
    lane_k = lax.broadcasted_iota(I32, (2 * ATT_BLOCK, 128), 1)
    lane_o = lax.broadcasted_iota(I32, (rows, 128), 1)
    dn = (((1,), (1,)), ((), ()))
    zero = jnp.zeros((), BF16)
    for g in range(N_KV_HEADS):
        s0 = g * SLABS_PER_GROUP
        qg = jnp.concatenate([q[:, (s0 + s) * 128:(s0 + s + 1) * 128]
                              for s in range(SLABS_PER_GROUP)], axis=0)
        kg = kcat[:, g * 128:(g + 1) * 128]
        vg = vcat[:, g * 128:(g + 1) * 128]
        halves = [lane_k < HEAD_DIM, lane_k >= HEAD_DIM]
        ps, invs = [], []
        for p in range(HEADS_PER_SLAB):
            kp = jnp.where(halves[p], kg, zero)
            sc = lax.dot_general(qg, kp, dn, preferred_element_type=F32)
            sc = jnp.where(mask, sc + bias_ref[g, p], NEG_INF)
            sink = jnp.concatenate(
                [jnp.full((ATT_BLOCK, 1), sinks_ref[g * GQA_GROUP + s * HEADS_PER_SLAB + p], F32)
                 for s in range(SLABS_PER_GROUP)], axis=0)
            m = jnp.maximum(jnp.max(sc, axis=-1, keepdims=True), sink)
            pe = jnp.exp(sc - m)
            den = jnp.sum(pe, axis=-1, keepdims=True) + jnp.exp(sink - m)
            ps.append(pe.astype(BF16))
            invs.append(1.0 / den)
        pcat = jnp.concatenate(ps, axis=1)
        vblk = jnp.concatenate([jnp.where(halves[p], vg, zero)
                                for p in range(HEADS_PER_SLAB)], axis=0)
        og = jnp.dot(pcat, vblk, preferred_element_type=F32)
        og = og * jnp.where(lane_o < HEAD_DIM, invs[0], invs[1])
        for s in range(SLABS_PER_GROUP):
            o_ref[:, (s0 + s) * 128:(s0 + s + 1) * 128] = (
                og[s * ATT_BLOCK:(s + 1) * ATT_BLOCK].astype(BF16))


def _t5_bias_table(rel_bias):
    r = np.arange(ATT_BLOCK)[:, None]
    j = np.arange(2 * ATT_BLOCK)[None, :]
    rel = np.clip(ATT_BLOCK + r - j, 0, WINDOW - 1)
    max_exact = N_BUCKETS // 2
    nf = jnp.maximum(rel, max_exact).astype(F32)
    large = max_exact + (jnp.log(nf / max_exact) / math.log(MAX_DISTANCE / max_exact)
                         * (N_BUCKETS - max_exact)).astype(I32)
    large = jnp.minimum(large, N_BUCKETS - 1)
    bucket = jnp.where(rel < max_exact, rel, large)
    pick = (bucket[None] == jnp.arange(N_BUCKETS, dtype=I32)[:, None, None])
    bias = jnp.sum(jnp.where(pick[:, None], rel_bias.astype(F32)[:, :, None, None], 0.0),
                   axis=0)
    bias = bias.reshape(N_KV_HEADS, SLABS_PER_GROUP, HEADS_PER_SLAB, ATT_BLOCK, 2 * ATT_BLOCK)
    bias = jnp.transpose(bias, (0, 2, 1, 3, 4))
    return bias.reshape(N_KV_HEADS, HEADS_PER_SLAB, SLABS_PER_GROUP * ATT_BLOCK, 2 * ATT_BLOCK)


def _attention(q, kk, vv, sinks, rel_bias, bsz, seq):
    nq = N_HEADS * HEAD_DIM
    kvw = kk.shape[-1]
    nb = seq // ATT_BLOCK
    q3 = q.reshape(bsz, seq, nq)
    k3 = kk.reshape(bsz, seq, kvw)
    v3 = vv.reshape(bsz, seq, kvw)
    bias = _t5_bias_table(rel_bias)
    prev = lambda bi, n: (bi, jnp.maximum(n - 1, 0), 0)
    cur = lambda bi, n: (bi, n, 0)
    out = pl.pallas_call(
        _attn_kernel,
        out_shape=jax.ShapeDtypeStruct((bsz, seq, nq), BF16),
        grid=(bsz, nb),
        in_specs=[
            pl.BlockSpec(memory_space=pltpu.SMEM),
            pl.BlockSpec((None, ATT_BLOCK, nq), cur),
            pl.BlockSpec((None, ATT_BLOCK, kvw), prev),
            pl.BlockSpec((None, ATT_BLOCK, kvw), cur),
            pl.BlockSpec((None, ATT_BLOCK, kvw), prev),
            pl.BlockSpec((None, ATT_BLOCK, kvw), cur),
            pl.BlockSpec(bias.shape, lambda bi, n: (0, 0, 0, 0)),
        ],
        out_specs=pl.BlockSpec((None, ATT_BLOCK, nq), cur),
        compiler_params=pltpu.CompilerParams(
            dimension_semantics=("arbitrary", "arbitrary"),
            vmem_limit_bytes=VMEM_LIMIT_BYTES),
        name="attn_core",
    )(sinks.astype(F32), q3, k3, k3, v3, v3, bias)
    return out.reshape(bsz * seq, nq)


def _oproj_kernel(x_ref, o_ref, w_ref, b_ref, y_ref):
    y_ref[...] = x_ref[...] + jnp.dot(o_ref[...], w_ref[...],
                                      preferred_element_type=F32) + b_ref[...]


def _out_proj(x2d, o, w_o, b_o):
    t, d = x2d.shape
    ts = min(ROW_TILE, t)
    nq = o.shape[1]
    return pl.pallas_call(
        _oproj_kernel,
        out_shape=jax.ShapeDtypeStruct((t, d), F32),
        grid=(t // ts,),
        in_specs=[
            pl.BlockSpec((ts, d), lambda i: (i, 0)),
            pl.BlockSpec((ts, nq), lambda i: (i, 0)),
            pl.BlockSpec((nq, d), lambda i: (0, 0)),
            pl.BlockSpec((1, d), lambda i: (0, 0)),
        ],
        out_specs=pl.BlockSpec((ts, d), lambda i: (i, 0)),
        compiler_params=pltpu.CompilerParams(
            dimension_semantics=("arbitrary",), vmem_limit_bytes=VMEM_LIMIT_BYTES),
        name="attn_out_proj",
    )(x2d, o, w_o.astype(BF16), b_o.reshape(1, d))


def kernel(x, norm_mix, norm_ffn, norm_final, pool_w, pool_b, pool_scale, w_qkv, b_qkv,
           sinks, w_o, b_o, rel_bias, w_router, b_router, w1, b1, w2, b2):
    bsz, seq, d = x.shape
    t = bsz * seq
    x = _pool_layer(x, norm_mix[0], pool_w[0], pool_b[0], pool_scale[0]).reshape(t, d)
    x = _moe_layer(x, norm_ffn[0], w_router[0], b_router[0], 0, w1, b1, w2, b2,
                   norm_final, final_norm=False)
    q, kk, vv = _qkv(x, norm_mix[1], w_qkv[0], b_qkv[0])
    o = _attention(q, kk, vv, sinks[0], rel_bias, bsz, seq)
    x = _out_proj(x, o, w_o[0], b_o[0])
    x = _moe_layer(x, norm_ffn[1], w_router[1], b_router[1], 1, w1, b1, w2, b2,
                   norm_final, final_norm=True)
    return x.reshape(bsz, seq, d)
```

```python
import functools
import math

import jax
import jax.numpy as jnp
import numpy as np
from jax import lax
from jax.experimental import pallas as pl
from jax.experimental.pallas import tpu as pltpu

F32 = jnp.float32
BF16 = jnp.bfloat16
I32 = jnp.int32

D_MODEL = 1024
POOL_WINDOWS = (2, 4, 8, 16)
POOL_GROUP_CH = D_MODEL // len(POOL_WINDOWS)
POOL_HALO = 16
HEAD_DIM = 64
N_HEADS = 16
N_KV_HEADS = 2
GQA_GROUP = N_HEADS // N_KV_HEADS
HEADS_PER_SLAB = 128 // HEAD_DIM
SLABS_PER_GROUP = GQA_GROUP // HEADS_PER_SLAB
ATT_BLOCK = 128
WINDOW = 128
N_BUCKETS = 32
MAX_DISTANCE = 128
N_EXPERTS = 32
TOP_K = 4
D_FF = D_MODEL
SWIGLU_ALPHA = 1.702
SWIGLU_LIMIT = 7.0
RMS_EPS = 1e-5
NEG_INF = -1e30

ROW_TILE = 512
MOE_BLOCK = 512
DISPATCH_TILE = 1024
COMBINE_TILE = 256
ISSUE_UNROLL = 4
VMEM_LIMIT_BYTES = 56 * 1024 * 1024


def _rms(x, g):
    return x * lax.rsqrt(jnp.mean(x * x, axis=-1, keepdims=True) + RMS_EPS) * g


ROW_TILE_SUB = D_MODEL // 128


def _load_row_tiles(ref, n, lead=()):
    return jnp.concatenate(
        [ref[lead + (pl.ds(j, n, stride=ROW_TILE_SUB), slice(None))]
         for j in range(ROW_TILE_SUB)], axis=1)


def _store_row_tiles(ref, val, n):
    for j in range(ROW_TILE_SUB):
        ref[pl.ds(j, n, stride=ROW_TILE_SUB), :] = val[:, j * 128:(j + 1) * 128]


def _pool_kernel(x_ref, xh_ref, g_ref, w_ref, b_ref, s_ref, o_ref, buf_ref, *, ts):
    i = pl.program_id(1)
    x = x_ref[...]
    g = g_ref[...]
    h = _rms(x, g)
    hh = _rms(xh_ref[...], g)
    hh = jnp.where(i == 0, 0.0, hh)
    buf_ref[0:POOL_HALO, :] = hh
    buf_ref[POOL_HALO:POOL_HALO + ts, :] = h
    t = i * ts + lax.broadcasted_iota(I32, (ts, 1), 0)
    outs = []
    for gi, win in enumerate(POOL_WINDOWS):
        c0 = gi * POOL_GROUP_CH
        hg = h[:, c0:c0 + POOL_GROUP_CH]
        acc = hg
        for d in range(1, win):
            acc = acc + buf_ref[POOL_HALO - d:POOL_HALO - d + ts, c0:c0 + POOL_GROUP_CH]
        cnt = jnp.minimum(t + 1, win).astype(F32)
        dlt = acc / cnt - hg
        outs.append(jnp.dot(dlt.astype(BF16), w_ref[gi], preferred_element_type=F32))
    y = jnp.concatenate(outs, axis=1)
    o_ref[...] = x + (y + b_ref[...]) * s_ref[...]


def _pool_layer(x, g, w, b, s):
    bsz, seq, d = x.shape
    ts = min(ROW_TILE, seq)
    kern = functools.partial(_pool_kernel, ts=ts)
    vec = lambda: pl.BlockSpec((1, d), lambda bi, i: (0, 0))
    return pl.pallas_call(
        kern,
        out_shape=jax.ShapeDtypeStruct(x.shape, F32),
        grid=(bsz, seq // ts),
        in_specs=[
            pl.BlockSpec((None, ts, d), lambda bi, i: (bi, i, 0)),
            pl.BlockSpec((None, POOL_HALO, d),
                         lambda bi, i: (bi, jnp.maximum(i * (ts // POOL_HALO) - 1, 0), 0)),
            vec(),
            pl.BlockSpec(w.shape, lambda bi, i: (0, 0, 0)),
            vec(),
            vec(),
        ],
        out_specs=pl.BlockSpec((None, ts, d), lambda bi, i: (bi, i, 0)),
        scratch_shapes=[pltpu.VMEM((POOL_HALO + ts, d), F32)],
        compiler_params=pltpu.CompilerParams(
            dimension_semantics=("arbitrary", "arbitrary"),
            vmem_limit_bytes=VMEM_LIMIT_BYTES),
        name="pool_layer",
    )(x, x, g.reshape(1, d), w.astype(BF16), b.reshape(1, d), s.reshape(1, d))


def _router_kernel(x_ref, g_ref, wr_ref, br_ref,
                   h_ref, ids_ref, gates_ref, rank_ref, sizes_ref, carry_ref, *, ts):
    i = pl.program_id(0)

    @pl.when(i == 0)
    def _():
        carry_ref[...] = jnp.zeros_like(carry_ref)

    h = _rms(x_ref[...], g_ref[...])
    _store_row_tiles(h_ref, h, ts)
    h_hi = h.astype(BF16)
    h_lo = (h - h_hi.astype(F32)).astype(BF16)
    w = wr_ref[...]
    w_hi = w.astype(BF16)
    w_lo = (w - w_hi.astype(F32)).astype(BF16)
    dn = (((1,), (1,)), ((), ()))
    lg = (lax.dot_general(w_hi, h_hi, dn, preferred_element_type=F32)
          + lax.dot_general(w_lo, h_hi, dn, preferred_element_type=F32)
          + lax.dot_general(w_hi, h_lo, dn, preferred_element_type=F32))
    lg = lg + br_ref[:, 0:1]

    iota_e = lax.broadcasted_iota(I32, (N_EXPERTS, ts), 0)
    vals, ids = [], []
    for _ in range(TOP_K):
        m = jnp.max(lg, axis=0, keepdims=True)
        idx = jnp.min(jnp.where(lg == m, iota_e, N_EXPERTS), axis=0, keepdims=True)
        vals.append(m)
        ids.append(idx)
        lg = jnp.where(iota_e == idx, -jnp.inf, lg)
    ex = [jnp.exp(v - vals[0]) for v in vals]
    den = ex[0] + ex[1] + ex[2] + ex[3]
    gates_ref[...] = jnp.concatenate([e / den for e in ex], axis=0)
    ids_ref[...] = jnp.concatenate(ids, axis=0)

    hot = [(iota_e == idx) for idx in ids]
    multi = (hot[0] | hot[1] | hot[2] | hot[3])
    multi_f = multi.astype(F32)
    r = lax.broadcasted_iota(I32, (ts, ts), 0)
    c = lax.broadcasted_iota(I32, (ts, ts), 1)
    upper = (r < c).astype(BF16)
    before = jnp.dot(multi_f.astype(BF16), upper, preferred_element_type=F32)
    before = before + carry_ref[:, 0:1]
    ranks = [jnp.sum(jnp.where(hk, before, 0.0), axis=0, keepdims=True) for hk in hot]
    rank_ref[...] = jnp.concatenate(ranks, axis=0).astype(I32)
    carry_ref[...] = carry_ref[...] + jnp.sum(multi_f, axis=1, keepdims=True)
    sizes_ref[...] = carry_ref[...]


def _route(x2d, g, w_router, b_router):
    t, d = x2d.shape
    ts = min(ROW_TILE, t)
    kern = functools.partial(_router_kernel, ts=ts)
    tok = lambda: pl.BlockSpec((TOP_K, ts), lambda i: (0, i))
    return pl.pallas_call(
        kern,
        out_shape=(
            jax.ShapeDtypeStruct((t * ROW_TILE_SUB, 128), F32),
            jax.ShapeDtypeStruct((TOP_K, t), I32),
            jax.ShapeDtypeStruct((TOP_K, t), F32),
            jax.ShapeDtypeStruct((TOP_K, t), I32),
            jax.ShapeDtypeStruct((N_EXPERTS, 128), F32),
        ),
        grid=(t // ts,),
        in_specs=[
            pl.BlockSpec((ts, d), lambda i: (i, 0)),
            pl.BlockSpec((1, d), lambda i: (0, 0)),
            pl.BlockSpec((N_EXPERTS, d), lambda i: (0, 0)),
            pl.BlockSpec((N_EXPERTS, 128), lambda i: (0, 0)),
        ],
        out_specs=(
            pl.BlockSpec((ts * ROW_TILE_SUB, 128), lambda i: (i, 0)),
            tok(), tok(), tok(),
            pl.BlockSpec((N_EXPERTS, 128), lambda i: (0, 0)),
        ),
        scratch_shapes=[pltpu.VMEM((N_EXPERTS, 128), F32)],
        compiler_params=pltpu.CompilerParams(
            dimension_semantics=("arbitrary",), vmem_limit_bytes=VMEM_LIMIT_BYTES),
        name="moe_route",
    )(x2d, g.reshape(1, d), w_router.T,
      jnp.broadcast_to(b_router.reshape(N_EXPERTS, 1), (N_EXPERTS, 128)))


def _dispatch_kernel(zrow_ref, n_used_ref, dest_ref, h_ref, xs_ref, zbuf, sem_z, sem,
                     *, td, blk, nblk):
    i = pl.program_id(0)
    sub = ROW_TILE_SUB

    def zero_block(row):
        row = pl.multiple_of(row * sub, blk * sub)
        return pltpu.make_async_copy(zbuf, xs_ref.at[pl.ds(row, blk * sub), :], sem_z)

    @pl.when(i == 0)
    def _():
        zbuf[...] = jnp.zeros_like(zbuf)
        for e in range(N_EXPERTS):
            @pl.when(zrow_ref[e] >= 0)
            def _():
                zero_block(zrow_ref[e]).start()

        def start_tail(b, carry):
            zero_block(b * blk).start()
            return carry

        def wait_tail(b, carry):
            zero_block(b * blk).wait()
            return carry

        lax.fori_loop(n_used_ref[0], nblk, start_tail, 0)
        for e in range(N_EXPERTS):
            @pl.when(zrow_ref[e] >= 0)
            def _():
                zero_block(zrow_ref[e]).wait()
        lax.fori_loop(n_used_ref[0], nblk, wait_tail, 0)

    def issue(t, carry):
        src = h_ref.at[pl.ds(pl.multiple_of(t * sub, sub), sub), :]
        for k in range(TOP_K):
            d = dest_ref[0, 0, k * td + t]
            pltpu.make_async_copy(src, xs_ref.at[pl.ds(pl.multiple_of(d * sub, sub), sub), :],
                                  sem).start(priority=k % 2)
        return carry

    lax.fori_loop(0, td, issue, 0, unroll=ISSUE_UNROLL)
    for k in range(TOP_K):
        pltpu.make_async_copy(h_ref, xs_ref.at[pl.ds(0, td * sub), :], sem).wait()


def _dispatch(h_tiles, dest, zrow, n_used, n_rows):
    sub = ROW_TILE_SUB
    t = h_tiles.shape[0] // sub
    td = min(DISPATCH_TILE, t)
    nt = t // td
    dest_tiles = dest.reshape(TOP_K, nt, td).transpose(1, 0, 2).reshape(nt, 1, TOP_K * td)
    kern = functools.partial(_dispatch_kernel, td=td, blk=MOE_BLOCK, nblk=n_rows // MOE_BLOCK)
    return pl.pallas_call(
        kern,
        out_shape=jax.ShapeDtypeStruct((n_rows * sub, 128), F32),
        grid_spec=pltpu.PrefetchScalarGridSpec(
            num_scalar_prefetch=2,
            grid=(nt,),
            in_specs=[
                pl.BlockSpec((1, 1, TOP_K * td), lambda i, z, nu: (i, 0, 0),
                             memory_space=pltpu.SMEM),
                pl.BlockSpec((td * sub, 128), lambda i, z, nu: (i, 0)),
            ],
            out_specs=pl.BlockSpec(memory_space=pl.ANY),
            scratch_shapes=[
                pltpu.VMEM((MOE_BLOCK * sub, 128), F32),
                pltpu.SemaphoreType.DMA(()),
                pltpu.SemaphoreType.DMA(()),
            ],
        ),
        compiler_params=pltpu.CompilerParams(
            dimension_semantics=("arbitrary",), vmem_limit_bytes=VMEM_LIMIT_BYTES),
        name="moe_dispatch",
    )(zrow, n_used, dest_tiles, h_tiles)


def _expert_kernel(blk_e_ref, n_used_ref, x_ref, w1_ref, b1_ref, w2_ref, b2_ref,
                   o_ref, w1b_ref, w2b_ref):
    i = pl.program_id(0)
    prev = blk_e_ref[jnp.maximum(i - 1, 0)]
    fresh = (i == 0) | (blk_e_ref[i] != prev)
    live = i < n_used_ref[0]

    @pl.when(live & fresh)
    def _():
        w1b_ref[...] = w1_ref[...].astype(BF16)
        w2b_ref[...] = w2_ref[...].astype(BF16)

    @pl.when(live)
    def _():
        x = _load_row_tiles(x_ref, MOE_BLOCK).astype(BF16)
        a = jnp.dot(x, w1b_ref[...], preferred_element_type=F32) + b1_ref[...]
        glu = jnp.minimum(a[:, :D_FF], SWIGLU_LIMIT)
        lin = jnp.clip(a[:, D_FF:], -SWIGLU_LIMIT, SWIGLU_LIMIT)
        act = glu * jax.nn.sigmoid(SWIGLU_ALPHA * glu) * (lin + 1.0)
        y = jnp.dot(act.astype(BF16), w2b_ref[...], preferred_element_type=F32) + b2_ref[...]
        _store_row_tiles(o_ref, y, MOE_BLOCK)

    @pl.when(jnp.logical_not(live))
    def _():
        o_ref[...] = jnp.zeros_like(o_ref)


def _experts(xs, blk_e, n_used, layer, w1, b1, w2, b2):
    sub = ROW_TILE_SUB
    n_rows = xs.shape[0] // sub
    d = w2.shape[-1]
    nblk = n_rows // MOE_BLOCK
    depth = w1.shape[0]
    row = lambda i, be, nu: (jnp.minimum(i, nu[0] - 1), 0)
    out_row = lambda i, be, nu: (i, 0)
    exp3 = lambda i, be, nu: (layer, be[jnp.minimum(i, nu[0] - 1)], 0, 0)
    return pl.pallas_call(
        _expert_kernel,
        out_shape=jax.ShapeDtypeStruct((n_rows * sub, 128), F32),
        grid_spec=pltpu.PrefetchScalarGridSpec(
            num_scalar_prefetch=2,
            grid=(nblk,),
            in_specs=[
                pl.BlockSpec((MOE_BLOCK * sub, 128), row),
                pl.BlockSpec((None, None, d, 2 * D_FF), exp3),
                pl.BlockSpec((None, None, 1, 2 * D_FF), exp3),
                pl.BlockSpec((None, None, D_FF, d), exp3),
                pl.BlockSpec((None, None, 1, d), exp3),
            ],
            out_specs=pl.BlockSpec((MOE_BLOCK * sub, 128), out_row),
            scratch_shapes=[
                pltpu.VMEM((d, 2 * D_FF), BF16),
                pltpu.VMEM((D_FF, d), BF16),
            ],
        ),
        compiler_params=pltpu.CompilerParams(
            dimension_semantics=("arbitrary",), vmem_limit_bytes=VMEM_LIMIT_BYTES),
        name="moe_experts",
    )(blk_e, n_used, xs, w1, b1.reshape(depth, N_EXPERTS, 1, 2 * D_FF), w2,
      b2.reshape(depth, N_EXPERTS, 1, d))


def _combine_kernel(dcur_ref, dnext_ref, x_ref, g_ref, ys_ref, gf_ref, o_ref, buf, sem,
                    *, tc, n_tiles, final_norm):
    i = pl.program_id(0)
    sub = ROW_TILE_SUB

    def issue(dref, slot):
        def body(t, carry):
            for k in range(TOP_K):
                d = dref[0, 0, k * tc + t]
                pltpu.make_async_copy(
                    ys_ref.at[pl.ds(pl.multiple_of(d * sub, sub), sub), :],
                    buf.at[slot, k, pl.ds(pl.multiple_of(t * sub, sub), sub), :],
                    sem.at[slot]).start(priority=k % 2)
            return carry
        lax.fori_loop(0, tc, body, 0, unroll=ISSUE_UNROLL)

    @pl.when(i == 0)
    def _():
        issue(dcur_ref, 0)

    @pl.when(i + 1 < n_tiles)
    def _():
        issue(dnext_ref, (i + 1) % 2)

    slot = i % 2
    for k in range(TOP_K):
        pltpu.make_async_copy(ys_ref.at[pl.ds(0, tc * sub), :], buf.at[slot, k],
                              sem.at[slot]).wait()
    acc = x_ref[...]
    gates = g_ref[...]
    for k in range(TOP_K):
        acc = acc + gates[:, k:k + 1] * _load_row_tiles(buf, tc, lead=(slot, k))
    if final_norm:
        acc = _rms(acc, gf_ref[...])
    o_ref[...] = acc


def _combine(x2d, gates_t, dest, ys, g_final, final_norm):
    t, d = x2d.shape
    tc = min(COMBINE_TILE, t)
    nt = t // tc
    dest_tiles = dest.reshape(TOP_K, nt, tc).transpose(1, 0, 2).reshape(nt, 1, TOP_K * tc)
    kern = functools.partial(_combine_kernel, tc=tc, n_tiles=nt, final_norm=final_norm)
    return pl.pallas_call(
        kern,
        out_shape=jax.ShapeDtypeStruct((t, d), F32),
        grid=(nt,),
        in_specs=[
            pl.BlockSpec((1, 1, TOP_K * tc), lambda i: (i, 0, 0), memory_space=pltpu.SMEM),
            pl.BlockSpec((1, 1, TOP_K * tc), lambda i: (jnp.minimum(i + 1, nt - 1), 0, 0),
                         memory_space=pltpu.SMEM),
            pl.BlockSpec((tc, d), lambda i: (i, 0)),
            pl.BlockSpec((tc, TOP_K), lambda i: (i, 0)),
            pl.BlockSpec(memory_space=pl.ANY),
            pl.BlockSpec((1, d), lambda i: (0, 0)),
        ],
        out_specs=pl.BlockSpec((tc, d), lambda i: (i, 0)),
        scratch_shapes=[
            pltpu.VMEM((2, TOP_K, tc * ROW_TILE_SUB, 128), F32),
            pltpu.SemaphoreType.DMA((2,)),
        ],
        compiler_params=pltpu.CompilerParams(
            dimension_semantics=("arbitrary",), vmem_limit_bytes=VMEM_LIMIT_BYTES),
        name="moe_combine_final" if final_norm else "moe_combine",
    )(dest_tiles, dest_tiles, x2d, gates_t.T, ys, g_final.reshape(1, d))


def _moe_layer(x2d, g_ffn, w_router, b_router, layer, w1, b1, w2, b2, g_final, final_norm):
    t, d = x2d.shape
    h, ids, gates, rank, sizes = _route(x2d, g_ffn, w_router, b_router)
    sizes = sizes[:, 0].astype(I32)
    padded = ((sizes + MOE_BLOCK - 1) // MOE_BLOCK) * MOE_BLOCK
    pend = jnp.cumsum(padded)
    pstart = pend - padded
    onehot = ids[:, :, None] == jnp.arange(N_EXPERTS, dtype=I32)
    dest = rank + jnp.sum(jnp.where(onehot, pstart, 0), axis=-1)
    n_rows = t * TOP_K + N_EXPERTS * MOE_BLOCK
    nblk = n_rows // MOE_BLOCK
    blk_row = jnp.arange(nblk, dtype=I32) * MOE_BLOCK
    blk_e = jnp.minimum(jnp.sum((pend[None, :] <= blk_row[:, None]).astype(I32), axis=1),
                        N_EXPERTS - 1)
    n_used = (pend[-1:] // MOE_BLOCK).astype(I32)
    zrow = jnp.where(padded > 0, pend - MOE_BLOCK, -1).astype(I32)
    xs = _dispatch(h, dest, zrow, n_used, n_rows)
    ys = _experts(xs, blk_e, n_used, layer, w1, b1, w2, b2)
    return _combine(x2d, gates, dest, ys, g_final, final_norm)


def _qkv_kernel(x_ref, g_ref, w_ref, b_ref, q_ref, k_ref, v_ref):
    h = _rms(x_ref[...], g_ref[...])
    qkv = jnp.dot(h.astype(BF16), w_ref[...], preferred_element_type=F32) + b_ref[...]
    nq = N_HEADS * HEAD_DIM
    nk = 2 * N_KV_HEADS * HEAD_DIM
    q_ref[...] = (qkv[:, :nq] * (HEAD_DIM ** -0.5)).astype(BF16)
    k_ref[...] = qkv[:, nq:nq + nk].astype(BF16)
    v_ref[...] = qkv[:, nq + nk:].astype(BF16)


def _qkv(x2d, g, w_qkv, b_qkv):
    t, d = x2d.shape
    ts = min(ROW_TILE, t)
    nq = N_HEADS * HEAD_DIM
    kv = N_KV_HEADS * HEAD_DIM

    def dup_heads(m):
        parts = []
        for hd in range(N_KV_HEADS):
            sl = m[..., hd * HEAD_DIM:(hd + 1) * HEAD_DIM]
            parts += [sl, sl]
        return jnp.concatenate(parts, axis=-1)

    w = jnp.concatenate([w_qkv[:, :nq], dup_heads(w_qkv[:, nq:nq + kv]),
                         dup_heads(w_qkv[:, nq + kv:])], axis=1).astype(BF16)
    b = jnp.concatenate([b_qkv[:nq], dup_heads(b_qkv[nq:nq + kv]),
                         dup_heads(b_qkv[nq + kv:])]).reshape(1, -1)
    n_out = w.shape[1]
    return pl.pallas_call(
        _qkv_kernel,
        out_shape=(
            jax.ShapeDtypeStruct((t, nq), BF16),
            jax.ShapeDtypeStruct((t, 2 * kv), BF16),
            jax.ShapeDtypeStruct((t, 2 * kv), BF16),
        ),
        grid=(t // ts,),
        in_specs=[
            pl.BlockSpec((ts, d), lambda i: (i, 0)),
            pl.BlockSpec((1, d), lambda i: (0, 0)),
            pl.BlockSpec((d, n_out), lambda i: (0, 0)),
            pl.BlockSpec((1, n_out), lambda i: (0, 0)),
        ],
        out_specs=(
            pl.BlockSpec((ts, nq), lambda i: (i, 0)),
            pl.BlockSpec((ts, 2 * kv), lambda i: (i, 0)),
            pl.BlockSpec((ts, 2 * kv), lambda i: (i, 0)),
        ),
        compiler_params=pltpu.CompilerParams(
            dimension_semantics=("arbitrary",), vmem_limit_bytes=VMEM_LIMIT_BYTES),
        name="attn_qkv",
    )(x2d, g.reshape(1, d), w, b)


def _attn_kernel(sinks_ref, q_ref, kp_ref, kc_ref, vp_ref, vc_ref, bias_ref, o_ref):
    n = pl.program_id(1)
    rows = SLABS_PER_GROUP * ATT_BLOCK
    q = q_ref[...]
    kcat = jnp.concatenate([kp_ref[...], kc_ref[...]], axis=0)
    vcat = jnp.concatenate([vp_ref[...], vc_ref[...]], axis=0)
    r = lax.broadcasted_iota(I32, (rows, 2 * ATT_BLOCK), 0) & (ATT_BLOCK - 1)
    j = lax.broadcasted_iota(I32, (rows, 2 * ATT_BLOCK), 1)
    lo = jnp.where(n == 0, jnp.maximum(r, ATT_BLOCK - 1), r)
    mask = (j > lo) & (j <= r + WINDOW)
    lane_k = lax.broadcasted_iota(I32, (2 * ATT_BLOCK, 128), 1)
    lane_o = lax.broadcasted_iota(I32, (rows, 128), 1)
    dn = (((1,), (1,)), ((), ()))
    zero = jnp.zeros((), BF16)
    for g in range(N_KV_HEADS):
        s0 = g * SLABS_PER_GROUP
        qg = jnp.concatenate([q[:, (s0 + s) * 128:(s0 + s + 1) * 128]
                              for s in range(SLABS_PER_GROUP)], axis=0)
        kg = kcat[:, g * 128:(g + 1) * 128]
        vg = vcat[:, g * 128:(g + 1) * 128]
        halves = [lane_k < HEAD_DIM, lane_k >= HEAD_DIM]
        ps, invs = [], []
        for p in range(HEADS_PER_SLAB):
            kp = jnp.where(halves[p], kg, zero)
            sc = lax.dot_general(qg, kp, dn, preferred_element_type=F32)
            sc = jnp.where(mask, sc + bias_ref[g, p], NEG_INF)
            sink = jnp.concatenate(
                [jnp.full((ATT_BLOCK, 1), sinks_ref[g * GQA_GROUP + s * HEADS_PER_SLAB + p], F32)
                 for s in range(SLABS_PER_GROUP)], axis=0)
            m = jnp.maximum(jnp.max(sc, axis=-1, keepdims=True), sink)
            pe = jnp.exp(sc - m)
            den = jnp.sum(pe, axis=-1, keepdims=True) + jnp.exp(sink - m)
            ps.append(pe.astype(BF16))
            invs.append(1.0 / den)
        pcat = jnp.concatenate(ps, axis=1)
        vblk = jnp.concatenate([jnp.where(halves[p], vg, zero)
                                for p in range(HEADS_PER_SLAB)], axis=0)
        og = jnp.dot(pcat, vblk, preferred_element_type=F32)
        og = og * jnp.where(lane_o < HEAD_DIM, invs[0], invs[1])
        for s in range(SLABS_PER_GROUP):
            o_ref[:, (s0 + s) * 128:(s0 + s + 1) * 128] = (
                og[s * ATT_BLOCK:(s + 1) * ATT_BLOCK].astype(BF16))


def _t5_bias_table(rel_bias):
    r = np.arange(ATT_BLOCK)[:, None]
    j = np.arange(2 * ATT_BLOCK)[None, :]
    rel = np.clip(ATT_BLOCK + r - j, 0, WINDOW - 1)
    max_exact = N_BUCKETS // 2
    nf = jnp.maximum(rel, max_exact).astype(F32)
    large = max_exact + (jnp.log(nf / max_exact) / math.log(MAX_DISTANCE / max_exact)
                         * (N_BUCKETS - max_exact)).astype(I32)
    large = jnp.minimum(large, N_BUCKETS - 1)
    bucket = jnp.where(rel < max_exact, rel, large)
    pick = (bucket[None] == jnp.arange(N_BUCKETS, dtype=I32)[:, None, None])
    bias = jnp.sum(jnp.where(pick[:, None], rel_bias.astype(F32)[:, :, None, None], 0.0),
                   axis=0)
    bias = bias.reshape(N_KV_HEADS, SLABS_PER_GROUP, HEADS_PER_SLAB, ATT_BLOCK, 2 * ATT_BLOCK)
    bias = jnp.transpose(bias, (0, 2, 1, 3, 4))
    return bias.reshape(N_KV_HEADS, HEADS_PER_SLAB, SLABS_PER_GROUP * ATT_BLOCK, 2 * ATT_BLOCK)


def _attention(q, kk, vv, sinks, rel_bias, bsz, seq):
    nq = N_HEADS * HEAD_DIM
    kvw = kk.shape[-1]
    nb = seq // ATT_BLOCK
    q3 = q.reshape(bsz, seq, nq)
    k3 = kk.reshape(bsz, seq, kvw)
    v3 = vv.reshape(bsz, seq, kvw)
    bias = _t5_bias_table(rel_bias)
    prev = lambda bi, n: (bi, jnp.maximum(n - 1, 0), 0)
    cur = lambda bi, n: (bi, n, 0)
    out = pl.pallas_call(
        _attn_kernel,
        out_shape=jax.ShapeDtypeStruct((bsz, seq, nq), BF16),
        grid=(bsz, nb),
        in_specs=[
            pl.BlockSpec(memory_space=pltpu.SMEM),
            pl.BlockSpec((None, ATT_BLOCK, nq), cur),
            pl.BlockSpec((None, ATT_BLOCK, kvw), prev),
            pl.BlockSpec((None, ATT_BLOCK, kvw), cur),
            pl.BlockSpec((None, ATT_BLOCK, kvw), prev),
            pl.BlockSpec((None, ATT_BLOCK, kvw), cur),
            pl.BlockSpec(bias.shape, lambda bi, n: (0, 0, 0, 0)),
        ],
        out_specs=pl.BlockSpec((None, ATT_BLOCK, nq), cur),
        compiler_params=pltpu.CompilerParams(
            dimension_semantics=("arbitrary", "arbitrary"),
            vmem_limit_bytes=VMEM_LIMIT_BYTES),
        name="attn_core",
    )(sinks.astype(F32), q3, k3, k3, v3, v3, bias)
    return out.reshape(bsz * seq, nq)


def _oproj_kernel(x_ref, o_ref, w_ref, b_ref, y_ref):
    y_ref[...] = x_ref[...] + jnp.dot(o_ref[...], w_ref[...],
                                      preferred_element_type=F32) + b_ref[...]


def _out_proj(x2d, o, w_o, b_o):
    t, d = x2d.shape
    ts = min(ROW_TILE, t)
    nq = o.shape[1]
    return pl.pallas_call(
        _oproj_kernel,
        out_shape=jax.ShapeDtypeStruct((t, d), F32),
        grid=(t // ts,),
        in_specs=[
            pl.BlockSpec((ts, d), lambda i: (i, 0)),
            pl.BlockSpec((ts, nq), lambda i: (i, 0)),
            pl.BlockSpec((nq, d), lambda i: (0, 0)),
            pl.BlockSpec((1, d), lambda i: (0, 0)),
        ],
        out_specs=pl.BlockSpec((ts, d), lambda i: (i, 0)),
        compiler_params=pltpu.CompilerParams(
            dimension_semantics=("arbitrary",), vmem_limit_bytes=VMEM_LIMIT_BYTES),
        name="attn_out_proj",
    )(x2d, o, w_o.astype(BF16), b_o.reshape(1, d))


def kernel(x, norm_mix, norm_ffn, norm_final, pool_w, pool_b, pool_scale, w_qkv, b_qkv,
           sinks, w_o, b_o, rel_bias, w_router, b_router, w1, b1, w2, b2):
    bsz, seq, d = x.shape
    t = bsz * seq
    x = _pool_layer(x, norm_mix[0], pool_w[0], pool_b[0], pool_scale[0]).reshape(t, d)
    x = _moe_layer(x, norm_ffn[0], w_router[0], b_router[0], 0, w1, b1, w2, b2,
                   norm_final, final_norm=False)
    q, kk, vv = _qkv(x, norm_mix[1], w_qkv[0], b_qkv[0])
    o = _attention(q, kk, vv, sinks[0], rel_bias, bsz, seq)
    x = _out_proj(x, o, w_o[0], b_o[0])
    x = _moe_layer(x, norm_ffn[1], w_router[1], b_router[1], 1, w1, b1, w2, b2,
                   norm_final, final_norm=True)
    return x.reshape(bsz, seq, d)
```

```python
import functools
import math

import jax
import jax.numpy as jnp
import numpy as np
from jax import lax
from jax.experimental import pallas as pl
from jax.experimental.pallas import tpu as pltpu

F32 = jnp.float32
BF16 = jnp.bfloat16
I32 = jnp.int32

D_MODEL = 1024
POOL_WINDOWS = (2, 4, 8, 16)
POOL_GROUP_CH = D_MODEL // len(POOL_WINDOWS)
POOL_HALO = 16
HEAD_DIM = 64
N_HEADS = 16
N_KV_HEADS = 2
GQA_GROUP = N_HEADS // N_KV_HEADS
HEADS_PER_SLAB = 128 // HEAD_DIM
SLABS_PER_GROUP = GQA_GROUP // HEADS_PER_SLAB
ATT_BLOCK = 128
WINDOW = 128
N_BUCKETS = 32
MAX_DISTANCE = 128
N_EXPERTS = 32
TOP_K = 4
D_FF = D_MODEL
SWIGLU_ALPHA = 1.702
SWIGLU_LIMIT = 7.0
RMS_EPS = 1e-5
NEG_INF = -1e30

ROW_TILE = 512
MOE_BLOCK = 512
DISPATCH_TILE = 1024
COMBINE_TILE = 256
ISSUE_UNROLL = 4
VMEM_LIMIT_BYTES = 56 * 1024 * 1024


def _rms(x, g):
    return x * lax.rsqrt(jnp.mean(x * x, axis=-1, keepdims=True) + RMS_EPS) * g


ROW_TILE_SUB = D_MODEL // 128


def _load_row_tiles(ref, n, lead=()):
    return jnp.concatenate(
        [ref[lead + (pl.ds(j, n, stride=ROW_TILE_SUB), slice(None))]
         for j in range(ROW_TILE_SUB)], axis=1)


def _store_row_tiles(ref, val, n):
    for j in range(ROW_TILE_SUB):
        ref[pl.ds(j, n, stride=ROW_TILE_SUB), :] = val[:, j * 128:(j + 1) * 128]


def _pool_kernel(x_ref, xh_ref, g_ref, w_ref, b_ref, s_ref, o_ref, buf_ref, *, ts):
    i = pl.program_id(1)
    x = x_ref[...]
    g = g_ref[...]
    h = _rms(x, g)
    hh = _rms(xh_ref[...], g)
    hh = jnp.where(i == 0, 0.0, hh)
    buf_ref[0:POOL_HALO, :] = hh
    buf_ref[POOL_HALO:POOL_HALO + ts, :] = h
    t = i * ts + lax.broadcasted_iota(I32, (ts, 1), 0)
    outs = []
    for gi, win in enumerate(POOL_WINDOWS):
        c0 = gi * POOL_GROUP_CH
        hg = h[:, c0:c0 + POOL_GROUP_CH]
        acc = hg
        for d in range(1, win):
            acc = acc + buf_ref[POOL_HALO - d:POOL_HALO - d + ts, c0:c0 + POOL_GROUP_CH]
        cnt = jnp.minimum(t + 1, win).astype(F32)
        dlt = acc / cnt - hg
        outs.append(jnp.dot(dlt.astype(BF16), w_ref[gi], preferred_element_type=F32))
    y = jnp.concatenate(outs, axis=1)
    o_ref[...] = x + (y + b_ref[...]) * s_ref[...]


def _pool_layer(x, g, w, b, s):
    bsz, seq, d = x.shape
    ts = min(ROW_TILE, seq)
    kern = functools.partial(_pool_kernel, ts=ts)
    vec = lambda: pl.BlockSpec((1, d), lambda bi, i: (0, 0))
    return pl.pallas_call(
        kern,
        out_shape=jax.ShapeDtypeStruct(x.shape, F32),
        grid=(bsz, seq // ts),
        in_specs=[
            pl.BlockSpec((None, ts, d), lambda bi, i: (bi, i, 0)),
            pl.BlockSpec((None, POOL_HALO, d),
                         lambda bi, i: (bi, jnp.maximum(i * (ts // POOL_HALO) - 1, 0), 0)),
            vec(),
            pl.BlockSpec(w.shape, lambda bi, i: (0, 0, 0)),
            vec(),
            vec(),
        ],
        out_specs=pl.BlockSpec((None, ts, d), lambda bi, i: (bi, i, 0)),
        scratch_shapes=[pltpu.VMEM((POOL_HALO + ts, d), F32)],
        compiler_params=pltpu.CompilerParams(
            dimension_semantics=("arbitrary", "arbitrary"),
            vmem_limit_bytes=VMEM_LIMIT_BYTES),
        name="pool_layer",
    )(x, x, g.reshape(1, d), w.astype(BF16), b.reshape(1, d), s.reshape(1, d))


def _router_kernel(x_ref, g_ref, wr_ref, br_ref,
                   h_ref, ids_ref, gates_ref, rank_ref, sizes_ref, carry_ref, *, ts):
    i = pl.program_id(0)

    @pl.when(i == 0)
    def _():
        carry_ref[...] = jnp.zeros_like(carry_ref)

    h = _rms(x_ref[...], g_ref[...])
    _store_row_tiles(h_ref, h, ts)
    h_hi = h.astype(BF16)
    h_lo = (h - h_hi.astype(F32)).astype(BF16)
    w = wr_ref[...]
    w_hi = w.astype(BF16)
    w_lo = (w - w_hi.astype(F32)).astype(BF16)
    dn = (((1,), (1,)), ((), ()))
    lg = (lax.dot_general(w_hi, h_hi, dn, preferred_element_type=F32)
          + lax.dot_general(w_lo, h_hi, dn, preferred_element_type=F32)
          + lax.dot_general(w_hi, h_lo, dn, preferred_element_type=F32))
    lg = lg + br_ref[:, 0:1]

    iota_e = lax.broadcasted_iota(I32, (N_EXPERTS, ts), 0)
    vals, ids = [], []
    for _ in range(TOP_K):
        m = jnp.max(lg, axis=0, keepdims=True)
        idx = jnp.min(jnp.where(lg == m, iota_e, N_EXPERTS), axis=0, keepdims=True)
        vals.append(m)
        ids.append(idx)
        lg = jnp.where(iota_e == idx, -jnp.inf, lg)
    ex = [jnp.exp(v - vals[0]) for v in vals]
    den = ex[0] + ex[1] + ex[2] + ex[3]
    gates_ref[...] = jnp.concatenate([e / den for e in ex], axis=0)
    ids_ref[...] = jnp.concatenate(ids, axis=0)

    hot = [(iota_e == idx) for idx in ids]
    multi = (hot[0] | hot[1] | hot[2] | hot[3])
    multi_f = multi.astype(F32)
    r = lax.broadcasted_iota(I32, (ts, ts), 0)
    c = lax.broadcasted_iota(I32, (ts, ts), 1)
    upper = (r < c).astype(BF16)
    before = jnp.dot(multi_f.astype(BF16), upper, preferred_element_type=F32)
    before = before + carry_ref[:, 0:1]
    ranks = [jnp.sum(jnp.where(hk, before, 0.0), axis=0, keepdims=True) for hk in hot]
    rank_ref[...] = jnp.concatenate(ranks, axis=0).astype(I32)
    carry_ref[...] = carry_ref[...] + jnp.sum(multi_f, axis=1, keepdims=True)
    sizes_ref[...] = carry_ref[...]


def _route(x2d, g, w_router, b_router):
    t, d = x2d.shape
    ts = min(ROW_TILE, t)
    kern = functools.partial(_router_kernel, ts=ts)
    tok = lambda: pl.BlockSpec((TOP_K, ts), lambda i: (0, i))
    return pl.pallas_call(
        kern,
        out_shape=(
            jax.ShapeDtypeStruct((t * ROW_TILE_SUB, 128), F32),
            jax.ShapeDtypeStruct((TOP_K, t), I32),
            jax.ShapeDtypeStruct((TOP_K, t), F32),
            jax.ShapeDtypeStruct((TOP_K, t), I32),
            jax.ShapeDtypeStruct((N_EXPERTS, 128), F32),
        ),
        grid=(t // ts,),
        in_specs=[
            pl.BlockSpec((ts, d), lambda i: (i, 0)),
            pl.BlockSpec((1, d), lambda i: (0, 0)),
            pl.BlockSpec((N_EXPERTS, d), lambda i: (0, 0)),
            pl.BlockSpec((N_EXPERTS, 128), lambda i: (0, 0)),
        ],
        out_specs=(
            pl.BlockSpec((ts * ROW_TILE_SUB, 128), lambda i: (i, 0)),
            tok(), tok(), tok(),
            pl.BlockSpec((N_EXPERTS, 128), lambda i: (0, 0)),
        ),
        scratch_shapes=[pltpu.VMEM((N_EXPERTS, 128), F32)],
        compiler_params=pltpu.CompilerParams(
            dimension_semantics=("arbitrary",), vmem_limit_bytes=VMEM_LIMIT_BYTES),
        name="moe_route",
    )(x2d, g.reshape(1, d), w_router.T,
      jnp.broadcast_to(b_router.reshape(N_EXPERTS, 1), (N_EXPERTS, 128)))


def _dispatch_kernel(zrow_ref, n_used_ref, dest_ref, h_ref, xs_ref, zbuf, sem_z, sem,
                     *, td, blk, nblk):
    i = pl.program_id(0)
    sub = ROW_TILE_SUB

    def zero_block(row):
        row = pl.multiple_of(row * sub, blk * sub)
        return pltpu.make_async_copy(zbuf, xs_ref.at[pl.ds(row, blk * sub), :], sem_z)

    @pl.when(i == 0)
    def _():
        zbuf[...] = jnp.zeros_like(zbuf)
        for e in range(N_EXPERTS):
            @pl.when(zrow_ref[e] >= 0)
            def _():
                zero_block(zrow_ref[e]).start()

        def start_tail(b, carry):
            zero_block(b * blk).start()
            return carry

        def wait_tail(b, carry):
            zero_block(b * blk).wait()
            return carry

        lax.fori_loop(n_used_ref[0], nblk, start_tail, 0)
        for e in range(N_EXPERTS):
            @pl.when(zrow_ref[e] >= 0)
            def _():
                zero_block(zrow_ref[e]).wait()
        lax.fori_loop(n_used_ref[0], nblk, wait_tail, 0)

    def issue(t, carry):
        src = h_ref.at[pl.ds(pl.multiple_of(t * sub, sub), sub), :]
        for k in range(TOP_K):
            d = dest_ref[0, 0, k * td + t]
            pltpu.make_async_copy(src, xs_ref.at[pl.ds(pl.multiple_of(d * sub, sub), sub), :],
                                  sem).start(priority=k % 2)
        return carry

    lax.fori_loop(0, td, issue, 0, unroll=ISSUE_UNROLL)
    for k in range(TOP_K):
        pltpu.make_async_copy(h_ref, xs_ref.at[pl.ds(0, td * sub), :], sem).wait()


def _dispatch(h_tiles, dest, zrow, n_used, n_rows):
    sub = ROW_TILE_SUB
    t = h_tiles.shape[0] // sub
    td = min(DISPATCH_TILE, t)
    nt = t // td
    dest_tiles = dest.reshape(TOP_K, nt, td).transpose(1, 0, 2).reshape(nt, 1, TOP_K * td)
    kern = functools.partial(_dispatch_kernel, td=td, blk=MOE_BLOCK, nblk=n_rows // MOE_BLOCK)
    return pl.pallas_call(
        kern,
        out_shape=jax.ShapeDtypeStruct((n_rows * sub, 128), F32),
        grid_spec=pltpu.PrefetchScalarGridSpec(
            num_scalar_prefetch=2,
            grid=(nt,),
            in_specs=[
                pl.BlockSpec((1, 1, TOP_K * td), lambda i, z, nu: (i, 0, 0),
                             memory_space=pltpu.SMEM),
                pl.BlockSpec((td * sub, 128), lambda i, z, nu: (i, 0)),
            ],
            out_specs=pl.BlockSpec(memory_space=pl.ANY),
            scratch_shapes=[
                pltpu.VMEM((MOE_BLOCK * sub, 128), F32),
                pltpu.SemaphoreType.DMA(()),
                pltpu.SemaphoreType.DMA(()),
            ],
        ),
        compiler_params=pltpu.CompilerParams(
            dimension_semantics=("arbitrary",), vmem_limit_bytes=VMEM_LIMIT_BYTES),
        name="moe_dispatch",
    )(zrow, n_used, dest_tiles, h_tiles)


def _expert_kernel(blk_e_ref, n_used_ref, x_ref, w1_ref, b1_ref, w2_ref, b2_ref,
                   o_ref, w1b_ref, w2b_ref):
    i = pl.program_id(0)
    prev = blk_e_ref[jnp.maximum(i - 1, 0)]
    fresh = (i == 0) | (blk_e_ref[i] != prev)
    live = i < n_used_ref[0]

    @pl.when(live & fresh)
    def _():
        w1b_ref[...] = w1_ref[...].astype(BF16)
        w2b_ref[...] = w2_ref[...].astype(BF16)

    @pl.when(live)
    def _():
        x = _load_row_tiles(x_ref, MOE_BLOCK).astype(BF16)
        a = jnp.dot(x, w1b_ref[...], preferred_element_type=F32) + b1_ref[...]
        glu = jnp.minimum(a[:, :D_FF], SWIGLU_LIMIT)
        lin = jnp.clip(a[:, D_FF:], -SWIGLU_LIMIT, SWIGLU_LIMIT)
        act = glu * jax.nn.sigmoid(SWIGLU_ALPHA * glu) * (lin + 1.0)
        y = jnp.dot(act.astype(BF16), w2b_ref[...], preferred_element_type=F32) + b2_ref[...]
        _store_row_tiles(o_ref, y, MOE_BLOCK)

    @pl.when(jnp.logical_not(live))
    def _():
        o_ref[...] = jnp.zeros_like(o_ref)


def _experts(xs, blk_e, n_used, layer, w1, b1, w2, b2):
    sub = ROW_TILE_SUB
    n_rows = xs.shape[0] // sub
    d = w2.shape[-1]
    nblk = n_rows // MOE_BLOCK
    depth = w1.shape[0]
    row = lambda i, be, nu: (jnp.minimum(i, nu[0] - 1), 0)
    out_row = lambda i, be, nu: (i, 0)
    exp3 = lambda i, be, nu: (layer, be[jnp.minimum(i, nu[0] - 1)], 0, 0)
    return pl.pallas_call(
        _expert_kernel,
        out_shape=jax.ShapeDtypeStruct((n_rows * sub, 128), F32),
        grid_spec=pltpu.PrefetchScalarGridSpec(
            num_scalar_prefetch=2,
            grid=(nblk,),
            in_specs=[
                pl.BlockSpec((MOE_BLOCK * sub, 128), row),
                pl.BlockSpec((None, None, d, 2 * D_FF), exp3),
                pl.BlockSpec((None, None, 1, 2 * D_FF), exp3),
                pl.BlockSpec((None, None, D_FF, d), exp3),
                pl.BlockSpec((None, None, 1, d), exp3),
            ],
            out_specs=pl.BlockSpec((MOE_BLOCK * sub, 128), out_row),
            scratch_shapes=[
                pltpu.VMEM((d, 2 * D_FF), BF16),
                pltpu.VMEM((D_FF, d), BF16),
            ],
        ),
        compiler_params=pltpu.CompilerParams(
            dimension_semantics=("arbitrary",), vmem_limit_bytes=VMEM_LIMIT_BYTES),
        name="moe_experts",
    )(blk_e, n_used, xs, w1, b1.reshape(depth, N_EXPERTS, 1, 2 * D_FF), w2,
      b2.reshape(depth, N_EXPERTS, 1, d))


def _combine_kernel(dcur_ref, dnext_ref, x_ref, g_ref, ys_ref, gf_ref, o_ref, buf, sem,
                    *, tc, n_tiles, final_norm):
    i = pl.program_id(0)
    sub = ROW_TILE_SUB

    def issue(dref, slot):
        def body(t, carry):
            for k in range(TOP_K):
                d = dref[0, 0, k * tc + t]
                pltpu.make_async_copy(
                    ys_ref.at[pl.ds(pl.multiple_of(d * sub, sub), sub), :],
                    buf.at[slot, k, pl.ds(pl.multiple_of(t * sub, sub), sub), :],
                    sem.at[slot]).start(priority=k % 2)
            return carry
        lax.fori_loop(0, tc, body, 0, unroll=ISSUE_UNROLL)

    @pl.when(i == 0)
    def _():
        issue(dcur_ref, 0)

    @pl.when(i + 1 < n_tiles)
    def _():
        issue(dnext_ref, (i + 1) % 2)

    slot = i % 2
    for k in range(TOP_K):
        pltpu.make_async_copy(ys_ref.at[pl.ds(0, tc * sub), :], buf.at[slot, k],
                              sem.at[slot]).wait()
    acc = x_ref[...]
    gates = g_ref[...]
    for k in range(TOP_K):
        acc = acc + gates[:, k:k + 1] * _load_row_tiles(buf, tc, lead=(slot, k))
    if final_norm:
        acc = _rms(acc, gf_ref[...])
    o_ref[...] = acc


def _combine(x2d, gates_t, dest, ys, g_final, final_norm):
    t, d = x2d.shape
    tc = min(COMBINE_TILE, t)
    nt = t // tc
    dest_tiles = dest.reshape(TOP_K, nt, tc).transpose(1, 0, 2).reshape(nt, 1, TOP_K * tc)
    kern = functools.partial(_combine_kernel, tc=tc, n_tiles=nt, final_norm=final_norm)
    return pl.pallas_call(
        kern,
        out_shape=jax.ShapeDtypeStruct((t, d), F32),
        grid=(nt,),
        in_specs=[
            pl.BlockSpec((1, 1, TOP_K * tc), lambda i: (i, 0, 0), memory_space=pltpu.SMEM),
            pl.BlockSpec((1, 1, TOP_K * tc), lambda i: (jnp.minimum(i + 1, nt - 1), 0, 0),
                         memory_space=pltpu.SMEM),
            pl.BlockSpec((tc, d), lambda i: (i, 0)),
            pl.BlockSpec((tc, TOP_K), lambda i: (i, 0)),
            pl.BlockSpec(memory_space=pl.ANY),
            pl.BlockSpec((1, d), lambda i: (0, 0)),
        ],
        out_specs=pl.BlockSpec((tc, d), lambda i: (i, 0)),
        scratch_shapes=[
            pltpu.VMEM((2, TOP_K, tc * ROW_TILE_SUB, 128), F32),
            pltpu.SemaphoreType.DMA((2,)),
        ],
        compiler_params=pltpu.CompilerParams(
            dimension_semantics=("arbitrary",), vmem_limit_bytes=VMEM_LIMIT_BYTES),
        name="moe_combine_final" if final_norm else "moe_combine",
    )(dest_tiles, dest_tiles, x2d, gates_t.T, ys, g_final.reshape(1, d))


def _moe_layer(x2d, g_ffn, w_router, b_router, layer, w1, b1, w2, b2, g_final, final_norm):
    t, d = x2d.shape
    h, ids, gates, rank, sizes = _route(x2d, g_ffn, w_router, b_router)
    sizes = sizes[:, 0].astype(I32)
    padded = ((sizes + MOE_BLOCK - 1) // MOE_BLOCK) * MOE_BLOCK
    pend = jnp.cumsum(padded)
    pstart = pend - padded
    onehot = ids[:, :, None] == jnp.arange(N_EXPERTS, dtype=I32)
    dest = rank + jnp.sum(jnp.where(onehot, pstart, 0), axis=-1)
    n_rows = t * TOP_K + N_EXPERTS * MOE_BLOCK
    nblk = n_rows // MOE_BLOCK
    blk_row = jnp.arange(nblk, dtype=I32) * MOE_BLOCK
    blk_e = jnp.minimum(jnp.sum((pend[None, :] <= blk_row[:, None]).astype(I32), axis=1),
                        N_EXPERTS - 1)
    n_used = (pend[-1:] // MOE_BLOCK).astype(I32)
    zrow = jnp.where(padded > 0, pend - MOE_BLOCK, -1).astype(I32)
    xs = _dispatch(h, dest, zrow, n_used, n_rows)
    ys = _experts(xs, blk_e, n_used, layer, w1, b1, w2, b2)
    return _combine(x2d, gates, dest, ys, g_final, final_norm)


def _qkv_kernel(x_ref, g_ref, w_ref, b_ref, q_ref, k_ref, v_ref):
    h = _rms(x_ref[...], g_ref[...])
    qkv = jnp.dot(h.astype(BF16), w_ref[...], preferred_element_type=F32) + b_ref[...]
    nq = N_HEADS * HEAD_DIM
    nk = 2 * N_KV_HEADS * HEAD_DIM
    q_ref[...] = (qkv[:, :nq] * (HEAD_DIM ** -0.5)).astype(BF16)
    k_ref[...] = qkv[:, nq:nq + nk].astype(BF16)
    v_ref[...] = qkv[:, nq + nk:].astype(BF16)


def _qkv(x2d, g, w_qkv, b_qkv):
    t, d = x2d.shape
    ts = min(ROW_TILE, t)
    nq = N_HEADS * HEAD_DIM
    kv = N_KV_HEADS * HEAD_DIM

    def dup_heads(m):
        parts = []
        for hd in range(N_KV_HEADS):
            sl = m[..., hd * HEAD_DIM:(hd + 1) * HEAD_DIM]
            parts += [sl, sl]
        return jnp.concatenate(parts, axis=-1)

    w = jnp.concatenate([w_qkv[:, :nq], dup_heads(w_qkv[:, nq:nq + kv]),
                         dup_heads(w_qkv[:, nq + kv:])], axis=1).astype(BF16)
    b = jnp.concatenate([b_qkv[:nq], dup_heads(b_qkv[nq:nq + kv]),
                         dup_heads(b_qkv[nq + kv:])]).reshape(1, -1)
    n_out = w.shape[1]
    return pl.pallas_call(
        _qkv_kernel,
        out_shape=(
            jax.ShapeDtypeStruct((t, nq), BF16),
            jax.ShapeDtypeStruct((t, 2 * kv), BF16),
            jax.ShapeDtypeStruct((t, 2 * kv), BF16),
        ),
        grid=(t // ts,),
        in_specs=[
            pl.BlockSpec((ts, d), lambda i: (i, 0)),
            pl.BlockSpec((1, d), lambda i: (0, 0)),
            pl.BlockSpec((d, n_out), lambda i: (0, 0)),
            pl.BlockSpec((1, n_out), lambda i: (0, 0)),
        ],
        out_specs=(
            pl.BlockSpec((ts, nq), lambda i: (i, 0)),
            pl.BlockSpec((ts, 2 * kv), lambda i: (i, 0)),
            pl.BlockSpec((ts, 2 * kv), lambda i: (i, 0)),
        ),
        compiler_params=pltpu.CompilerParams(
            dimension_semantics=("arbitrary",), vmem_limit_bytes=VMEM_LIMIT_BYTES),
        name="attn_qkv",
    )(x2d, g.reshape(1, d), w, b)


def _attn_kernel(sinks_ref, q_ref, kp_ref, kc_ref, vp_ref, vc_ref, bias_ref, o_ref):
    n = pl.program_id(1)
    rows = SLABS_PER_GROUP * ATT_BLOCK
    keys = 2 * ATT_BLOCK
    q = q_ref[...]
    kcat = jnp.concatenate([kp_ref[...], kc_ref[...]], axis=0)
    vcat = jnp.concatenate([vp_ref[...], vc_ref[...]], axis=0)
    j = lax.broadcasted_iota(I32, (keys, rows), 0)
    r = lax.broadcasted_iota(I32, (keys, rows), 1) & (ATT_BLOCK - 1)
    lo = jnp.where(n == 0, jnp.maximum(r, ATT_BLOCK - 1), r)
    mask = (j > lo) & (j <= r + WINDOW)
    lane_k = lax.broadcasted_iota(I32, (keys, 128), 1)
    halves = [lane_k < HEAD_DIM, lane_k >= HEAD_DIM]
    sub_o = lax.broadcasted_iota(I32, (128, rows), 0)
    zero = jnp.zeros((), BF16)
    for g in range(N_KV_HEADS):
        s0 = g * SLABS_PER_GROUP
        qg = jnp.concatenate([q[:, (s0 + s) * 128:(s0 + s + 1) * 128]
                              for s in range(SLABS_PER_GROUP)], axis=0)
        kg = kcat[:, g * 128:(g + 1) * 128]
        vg = vcat[:, g * 128:(g + 1) * 128]
        pts, invs = [], []
        for p in range(HEADS_PER_SLAB):
            kp = jnp.where(halves[p], kg, zero)
            sc = lax.dot_general(kp, qg, (((1,), (1,)), ((), ())),
                                 preferred_element_type=F32)
            sc = jnp.where(mask, sc + bias_ref[g, p], NEG_INF)
            sink = jnp.concatenate(
                [jnp.full((1, ATT_BLOCK), sinks_ref[g * GQA_GROUP + s * HEADS_PER_SLAB + p], F32)
                 for s in range(SLABS_PER_GROUP)], axis=1)
            m = jnp.maximum(jnp.max(sc, axis=0, keepdims=True), sink)
            pe = jnp.exp(sc - m)
            den = jnp.sum(pe, axis=0, keepdims=True) + jnp.exp(sink - m)
            pts.append(pe.astype(BF16))
            invs.append(1.0 / den)
        pcat = jnp.concatenate(pts, axis=0)
        vblk = jnp.concatenate([jnp.where(halves[p], vg, zero)
                                for p in range(HEADS_PER_SLAB)], axis=0)
        og = lax.dot_general(vblk, pcat, (((0,), (0,)), ((), ())),
                             preferred_element_type=F32)
        og = og * jnp.where(sub_o < HEAD_DIM, invs[0], invs[1])
        og = og.T
        for s in range(SLABS_PER_GROUP):
            o_ref[:, (s0 + s) * 128:(s0 + s + 1) * 128] = (
                og[s * ATT_BLOCK:(s + 1) * ATT_BLOCK].astype(BF16))


def _t5_bias_table(rel_bias):
    r = np.arange(ATT_BLOCK)[:, None]
    j = np.arange(2 * ATT_BLOCK)[None, :]
    rel = np.clip(ATT_BLOCK + r - j, 0, WINDOW - 1)
    max_exact = N_BUCKETS // 2
    nf = jnp.maximum(rel, max_exact).astype(F32)
    large = max_exact + (jnp.log(nf / max_exact) / math.log(MAX_DISTANCE / max_exact)
                         * (N_BUCKETS - max_exact)).astype(I32)
    large = jnp.minimum(large, N_BUCKETS - 1)
    bucket = jnp.where(rel < max_exact, rel, large)
    pick = (bucket[None] == jnp.arange(N_BUCKETS, dtype=I32)[:, None, None])
    bias = jnp.sum(jnp.where(pick[:, None], rel_bias.astype(F32)[:, :, None, None], 0.0),
                   axis=0)
    bias = bias.reshape(N_KV_HEADS, SLABS_PER_GROUP, HEADS_PER_SLAB, ATT_BLOCK, 2 * ATT_BLOCK)
    bias = jnp.transpose(bias, (0, 2, 4, 1, 3))
    return bias.reshape(N_KV_HEADS, HEADS_PER_SLAB, 2 * ATT_BLOCK, SLABS_PER_GROUP * ATT_BLOCK)


def _attention(q, kk, vv, sinks, rel_bias, bsz, seq):
    nq = N_HEADS * HEAD_DIM
    kvw = kk.shape[-1]
    nb = seq // ATT_BLOCK
    q3 = q.reshape(bsz, seq, nq)
    k3 = kk.reshape(bsz, seq, kvw)
    v3 = vv.reshape(bsz, seq, kvw)
    bias = _t5_bias_table(rel_bias)
    prev = lambda bi, n: (bi, jnp.maximum(n - 1, 0), 0)
    cur = lambda bi, n: (bi, n, 0)
    out = pl.pallas_call(
        _attn_kernel,
        out_shape=jax.ShapeDtypeStruct((bsz, seq, nq), BF16),
        grid=(bsz, nb),
        in_specs=[
            pl.BlockSpec(memory_space=pltpu.SMEM),
            pl.BlockSpec((None, ATT_BLOCK, nq), cur),
            pl.BlockSpec((None, ATT_BLOCK, kvw), prev),
            pl.BlockSpec((None, ATT_BLOCK, kvw), cur),
            pl.BlockSpec((None, ATT_BLOCK, kvw), prev),
            pl.BlockSpec((None, ATT_BLOCK, kvw), cur),
            pl.BlockSpec(bias.shape, lambda bi, n: (0, 0, 0, 0)),
        ],
        out_specs=pl.BlockSpec((None, ATT_BLOCK, nq), cur),
        compiler_params=pltpu.CompilerParams(
            dimension_semantics=("arbitrary", "arbitrary"),
            vmem_limit_bytes=VMEM_LIMIT_BYTES),
        name="attn_core",
    )(sinks.astype(F32), q3, k3, k3, v3, v3, bias)
    return out.reshape(bsz * seq, nq)


def _oproj_kernel(x_ref, o_ref, w_ref, b_ref, y_ref):
    y_ref[...] = x_ref[...] + jnp.dot(o_ref[...], w_ref[...],
                                      preferred_element_type=F32) + b_ref[...]


def _out_proj(x2d, o, w_o, b_o):
    t, d = x2d.shape
    ts = min(ROW_TILE, t)
    nq = o.shape[1]
    return pl.pallas_call(
        _oproj_kernel,
        out_shape=jax.ShapeDtypeStruct((t, d), F32),
        grid=(t // ts,),
        in_specs=[
            pl.BlockSpec((ts, d), lambda i: (i, 0)),
            pl.BlockSpec((ts, nq), lambda i: (i, 0)),
            pl.BlockSpec((nq, d), lambda i: (0, 0)),
            pl.BlockSpec((1, d), lambda i: (0, 0)),
        ],
        out_specs=pl.BlockSpec((ts, d), lambda i: (i, 0)),
        compiler_params=pltpu.CompilerParams(
            dimension_semantics=("arbitrary",), vmem_limit_bytes=VMEM_LIMIT_BYTES),
        name="attn_out_proj",
    )(x2d, o, w_o.astype(BF16), b_o.reshape(1, d))


def kernel(x, norm_mix, norm_ffn, norm_final, pool_w, pool_b, pool_scale, w_qkv, b_qkv,
           sinks, w_o, b_o, rel_bias, w_router, b_router, w1, b1, w2, b2):
    bsz, seq, d = x.shape
    t = bsz * seq
    x = _pool_layer(x, norm_mix[0], pool_w[0], pool_b[0], pool_scale[0]).reshape(t, d)
    x = _moe_layer(x, norm_ffn[0], w_router[0], b_router[0], 0, w1, b1, w2, b2,
                   norm_final, final_norm=False)
    q, kk, vv = _qkv(x, norm_mix[1], w_qkv[0], b_qkv[0])
    o = _attention(q, kk, vv, sinks[0], rel_bias, bsz, seq)
    x = _out_proj(x, o, w_o[0], b_o[0])
    x = _moe_layer(x, norm_ffn[1], w_router[1], b_router[1], 1, w1, b1, w2, b2,
                   norm_final, final_norm=True)
    return x.reshape(bsz, seq, d)
```

```python
import functools
import math

import jax
import jax.numpy as jnp
import numpy as np
from jax import lax
from jax.experimental import pallas as pl
from jax.experimental.pallas import tpu as pltpu

F32 = jnp.float32
BF16 = jnp.bfloat16
I32 = jnp.int32

D_MODEL = 1024
POOL_WINDOWS = (2, 4, 8, 16)
POOL_GROUP_CH = D_MODEL // len(POOL_WINDOWS)
POOL_HALO = 16
HEAD_DIM = 64
N_HEADS = 16
N_KV_HEADS = 2
GQA_GROUP = N_HEADS // N_KV_HEADS
HEADS_PER_SLAB = 128 // HEAD_DIM
SLABS_PER_GROUP = GQA_GROUP // HEADS_PER_SLAB
ATT_BLOCK = 128
WINDOW = 128
N_BUCKETS = 32
MAX_DISTANCE = 128
N_EXPERTS = 32
TOP_K = 4
D_FF = D_MODEL
SWIGLU_ALPHA = 1.702
SWIGLU_LIMIT = 7.0
RMS_EPS = 1e-5
NEG_INF = -1e30

ROW_TILE = 512
MOE_BLOCK = 512
DISPATCH_TILE = 1024
COMBINE_TILE = 256
ISSUE_UNROLL = 4
VMEM_LIMIT_BYTES = 56 * 1024 * 1024


def _rms(x, g):
    return x * lax.rsqrt(jnp.mean(x * x, axis=-1, keepdims=True) + RMS_EPS) * g


ROW_TILE_SUB = D_MODEL // 128


def _load_row_tiles(ref, n, lead=()):
    return jnp.concatenate(
        [ref[lead + (pl.ds(j, n, stride=ROW_TILE_SUB), slice(None))]
         for j in range(ROW_TILE_SUB)], axis=1)


def _store_row_tiles(ref, val, n):
    for j in range(ROW_TILE_SUB):
        ref[pl.ds(j, n, stride=ROW_TILE_SUB), :] = val[:, j * 128:(j + 1) * 128]


def _pool_kernel(x_ref, xh_ref, g_ref, w_ref, b_ref, s_ref, o_ref, buf_ref, *, ts):
    i = pl.program_id(1)
    x = x_ref[...]
    g = g_ref[...]
    h = _rms(x, g)
    hh = _rms(xh_ref[...], g)
    hh = jnp.where(i == 0, 0.0, hh)
    buf_ref[0:POOL_HALO, :] = hh
    buf_ref[POOL_HALO:POOL_HALO + ts, :] = h
    t = i * ts + lax.broadcasted_iota(I32, (ts, 1), 0)
    outs = []
    for gi, win in enumerate(POOL_WINDOWS):
        c0 = gi * POOL_GROUP_CH
        hg = h[:, c0:c0 + POOL_GROUP_CH]
        acc = hg
        for d in range(1, win):
            acc = acc + buf_ref[POOL_HALO - d:POOL_HALO - d + ts, c0:c0 + POOL_GROUP_CH]
        cnt = jnp.minimum(t + 1, win).astype(F32)
        dlt = acc / cnt - hg
        outs.append(jnp.dot(dlt.astype(BF16), w_ref[gi], preferred_element_type=F32))
    y = jnp.concatenate(outs, axis=1)
    o_ref[...] = x + (y + b_ref[...]) * s_ref[...]


def _pool_layer(x, g, w, b, s):
    bsz, seq, d = x.shape
    ts = min(ROW_TILE, seq)
    kern = functools.partial(_pool_kernel, ts=ts)
    vec = lambda: pl.BlockSpec((1, d), lambda bi, i: (0, 0))
    return pl.pallas_call(
        kern,
        out_shape=jax.ShapeDtypeStruct(x.shape, F32),
        grid=(bsz, seq // ts),
        in_specs=[
            pl.BlockSpec((None, ts, d), lambda bi, i: (bi, i, 0)),
            pl.BlockSpec((None, POOL_HALO, d),
                         lambda bi, i: (bi, jnp.maximum(i * (ts // POOL_HALO) - 1, 0), 0)),
            vec(),
            pl.BlockSpec(w.shape, lambda bi, i: (0, 0, 0)),
            vec(),
            vec(),
        ],
        out_specs=pl.BlockSpec((None, ts, d), lambda bi, i: (bi, i, 0)),
        scratch_shapes=[pltpu.VMEM((POOL_HALO + ts, d), F32)],
        compiler_params=pltpu.CompilerParams(
            dimension_semantics=("arbitrary", "arbitrary"),
            vmem_limit_bytes=VMEM_LIMIT_BYTES),
        name="pool_layer",
    )(x, x, g.reshape(1, d), w.astype(BF16), b.reshape(1, d), s.reshape(1, d))


def _router_kernel(x_ref, g_ref, wr_ref, br_ref,
                   h_ref, ids_ref, gates_ref, rank_ref, sizes_ref, carry_ref, *, ts):
    i = pl.program_id(0)

    @pl.when(i == 0)
    def _():
        carry_ref[...] = jnp.zeros_like(carry_ref)

    h = _rms(x_ref[...], g_ref[...])
    _store_row_tiles(h_ref, h, ts)
    h_hi = h.astype(BF16)
    h_lo = (h - h_hi.astype(F32)).astype(BF16)
    w = wr_ref[...]
    w_hi = w.astype(BF16)
    w_lo = (w - w_hi.astype(F32)).astype(BF16)
    dn = (((1,), (1,)), ((), ()))
    lg = (lax.dot_general(w_hi, h_hi, dn, preferred_element_type=F32)
          + lax.dot_general(w_lo, h_hi, dn, preferred_element_type=F32)
          + lax.dot_general(w_hi, h_lo, dn, preferred_element_type=F32))
    lg = lg + br_ref[:, 0:1]

    iota_e = lax.broadcasted_iota(I32, (N_EXPERTS, ts), 0)
    vals, ids = [], []
    for _ in range(TOP_K):
        m = jnp.max(lg, axis=0, keepdims=True)
        idx = jnp.min(jnp.where(lg == m, iota_e, N_EXPERTS), axis=0, keepdims=True)
        vals.append(m)
        ids.append(idx)
        lg = jnp.where(iota_e == idx, -jnp.inf, lg)
    ex = [jnp.exp(v - vals[0]) for v in vals]
    den = ex[0] + ex[1] + ex[2] + ex[3]
    gates_ref[...] = jnp.concatenate([e / den for e in ex], axis=0)
    ids_ref[...] = jnp.concatenate(ids, axis=0)

    hot = [(iota_e == idx) for idx in ids]
    multi = (hot[0] | hot[1] | hot[2] | hot[3])
    multi_f = multi.astype(F32)
    r = lax.broadcasted_iota(I32, (ts, ts), 0)
    c = lax.broadcasted_iota(I32, (ts, ts), 1)
    upper = (r < c).astype(BF16)
    before = jnp.dot(multi_f.astype(BF16), upper, preferred_element_type=F32)
    before = before + carry_ref[:, 0:1]
    ranks = [jnp.sum(jnp.where(hk, before, 0.0), axis=0, keepdims=True) for hk in hot]
    rank_ref[...] = jnp.concatenate(ranks, axis=0).astype(I32)
    carry_ref[...] = carry_ref[...] + jnp.sum(multi_f, axis=1, keepdims=True)
    sizes_ref[...] = carry_ref[...]


def _route(x2d, g, w_router, b_router):
    t, d = x2d.shape
    ts = min(ROW_TILE, t)
    kern = functools.partial(_router_kernel, ts=ts)
    tok = lambda: pl.BlockSpec((TOP_K, ts), lambda i: (0, i))
    return pl.pallas_call(
        kern,
        out_shape=(
            jax.ShapeDtypeStruct((t * ROW_TILE_SUB, 128), F32),
            jax.ShapeDtypeStruct((TOP_K, t), I32),
            jax.ShapeDtypeStruct((TOP_K, t), F32),
            jax.ShapeDtypeStruct((TOP_K, t), I32),
            jax.ShapeDtypeStruct((N_EXPERTS, 128), F32),
        ),
        grid=(t // ts,),
        in_specs=[
            pl.BlockSpec((ts, d), lambda i: (i, 0)),
            pl.BlockSpec((1, d), lambda i: (0, 0)),
            pl.BlockSpec((N_EXPERTS, d), lambda i: (0, 0)),
            pl.BlockSpec((N_EXPERTS, 128), lambda i: (0, 0)),
        ],
        out_specs=(
            pl.BlockSpec((ts * ROW_TILE_SUB, 128), lambda i: (i, 0)),
            tok(), tok(), tok(),
            pl.BlockSpec((N_EXPERTS, 128), lambda i: (0, 0)),
        ),
        scratch_shapes=[pltpu.VMEM((N_EXPERTS, 128), F32)],
        compiler_params=pltpu.CompilerParams(
            dimension_semantics=("arbitrary",), vmem_limit_bytes=VMEM_LIMIT_BYTES),
        name="moe_route",
    )(x2d, g.reshape(1, d), w_router.T,
      jnp.broadcast_to(b_router.reshape(N_EXPERTS, 1), (N_EXPERTS, 128)))


def _dispatch_kernel(zrow_ref, n_used_ref, dest_ref, h_ref, xs_ref, zbuf, sem_z, sem,
                     *, td, blk, nblk):
    i = pl.program_id(0)
    sub = ROW_TILE_SUB

    def zero_block(row):
        row = pl.multiple_of(row * sub, blk * sub)
        return pltpu.make_async_copy(zbuf, xs_ref.at[pl.ds(row, blk * sub), :], sem_z)

    @pl.when(i == 0)
    def _():
        zbuf[...] = jnp.zeros_like(zbuf)
        for e in range(N_EXPERTS):
            @pl.when(zrow_ref[e] >= 0)
            def _():
                zero_block(zrow_ref[e]).start()

        def start_tail(b, carry):
            zero_block(b * blk).start()
            return carry

        def wait_tail(b, carry):
            zero_block(b * blk).wait()
            return carry

        lax.fori_loop(n_used_ref[0], nblk, start_tail, 0)
        for e in range(N_EXPERTS):
            @pl.when(zrow_ref[e] >= 0)
            def _():
                zero_block(zrow_ref[e]).wait()
        lax.fori_loop(n_used_ref[0], nblk, wait_tail, 0)

    def issue(t, carry):
        src = h_ref.at[pl.ds(pl.multiple_of(t * sub, sub), sub), :]
        for k in range(TOP_K):
            d = dest_ref[0, 0, k * td + t]
            pltpu.make_async_copy(src, xs_ref.at[pl.ds(pl.multiple_of(d * sub, sub), sub), :],
                                  sem).start(priority=k % 2)
        return carry

    lax.fori_loop(0, td, issue, 0, unroll=ISSUE_UNROLL)
    for k in range(TOP_K):
        pltpu.make_async_copy(h_ref, xs_ref.at[pl.ds(0, td * sub), :], sem).wait()


def _dispatch(h_tiles, dest, zrow, n_used, n_rows):
    sub = ROW_TILE_SUB
    t = h_tiles.shape[0] // sub
    td = min(DISPATCH_TILE, t)
    nt = t // td
    dest_tiles = dest.reshape(TOP_K, nt, td).transpose(1, 0, 2).reshape(nt, 1, TOP_K * td)
    kern = functools.partial(_dispatch_kernel, td=td, blk=MOE_BLOCK, nblk=n_rows // MOE_BLOCK)
    return pl.pallas_call(
        kern,
        out_shape=jax.ShapeDtypeStruct((n_rows * sub, 128), F32),
        grid_spec=pltpu.PrefetchScalarGridSpec(
            num_scalar_prefetch=2,
            grid=(nt,),
            in_specs=[
                pl.BlockSpec((1, 1, TOP_K * td), lambda i, z, nu: (i, 0, 0),
                             memory_space=pltpu.SMEM),
                pl.BlockSpec((td * sub, 128), lambda i, z, nu: (i, 0)),
            ],
            out_specs=pl.BlockSpec(memory_space=pl.ANY),
            scratch_shapes=[
                pltpu.VMEM((MOE_BLOCK * sub, 128), F32),
                pltpu.SemaphoreType.DMA(()),
                pltpu.SemaphoreType.DMA(()),
            ],
        ),
        compiler_params=pltpu.CompilerParams(
            dimension_semantics=("arbitrary",), vmem_limit_bytes=VMEM_LIMIT_BYTES),
        name="moe_dispatch",
    )(zrow, n_used, dest_tiles, h_tiles)


def _expert_kernel(blk_e_ref, n_used_ref, nxt_e_ref, x_ref, w1_hbm, b1_ref, w2_hbm, b2_ref,
                   o_ref, w1s_ref, w2s_ref, w1b_ref, w2b_ref, wsem, *, layer):
    i = pl.program_id(0)
    e = blk_e_ref[i]
    prev = blk_e_ref[jnp.maximum(i - 1, 0)]
    fresh = (i == 0) | (e != prev)
    live = i < n_used_ref[0]

    def weight_copies(expert):
        return (pltpu.make_async_copy(w1_hbm.at[layer, expert], w1s_ref, wsem.at[0]),
                pltpu.make_async_copy(w2_hbm.at[layer, expert], w2s_ref, wsem.at[1]))

    @pl.when(i == 0)
    def _():
        for cp in weight_copies(e):
            cp.start()

    @pl.when(live & fresh)
    def _():
        for cp in weight_copies(e):
            cp.wait()
        w1b_ref[...] = w1s_ref[...].astype(BF16)
        w2b_ref[...] = w2s_ref[...].astype(BF16)
        nxt = nxt_e_ref[i]

        @pl.when(nxt >= 0)
        def _():
            for cp in weight_copies(nxt):
                cp.start()

    @pl.when(live)
    def _():
        x = _load_row_tiles(x_ref, MOE_BLOCK).astype(BF16)
        a = jnp.dot(x, w1b_ref[...], preferred_element_type=F32) + b1_ref[...]
        glu = jnp.minimum(a[:, :D_FF], SWIGLU_LIMIT)
        lin = jnp.clip(a[:, D_FF:], -SWIGLU_LIMIT, SWIGLU_LIMIT)
        act = glu * jax.nn.sigmoid(SWIGLU_ALPHA * glu) * (lin + 1.0)
        y = jnp.dot(act.astype(BF16), w2b_ref[...], preferred_element_type=F32) + b2_ref[...]
        _store_row_tiles(o_ref, y, MOE_BLOCK)

    @pl.when(jnp.logical_not(live))
    def _():
        o_ref[...] = jnp.zeros_like(o_ref)


def _experts(xs, blk_e, n_used, nxt_e, layer, w1, b1, w2, b2):
    sub = ROW_TILE_SUB
    n_rows = xs.shape[0] // sub
    d = w2.shape[-1]
    nblk = n_rows // MOE_BLOCK
    depth = w1.shape[0]
    row = lambda i, be, nu, nx: (jnp.minimum(i, nu[0] - 1), 0)
    out_row = lambda i, be, nu, nx: (i, 0)
    exp3 = lambda i, be, nu, nx: (layer, be[jnp.minimum(i, nu[0] - 1)], 0, 0)
    return pl.pallas_call(
        functools.partial(_expert_kernel, layer=layer),
        out_shape=jax.ShapeDtypeStruct((n_rows * sub, 128), F32),
        grid_spec=pltpu.PrefetchScalarGridSpec(
            num_scalar_prefetch=3,
            grid=(nblk,),
            in_specs=[
                pl.BlockSpec((MOE_BLOCK * sub, 128), row),
                pl.BlockSpec(memory_space=pl.ANY),
                pl.BlockSpec((None, None, 1, 2 * D_FF), exp3),
                pl.BlockSpec(memory_space=pl.ANY),
                pl.BlockSpec((None, None, 1, d), exp3),
            ],
            out_specs=pl.BlockSpec((MOE_BLOCK * sub, 128), out_row),
            scratch_shapes=[
                pltpu.VMEM((d, 2 * D_FF), F32),
                pltpu.VMEM((D_FF, d), F32),
                pltpu.VMEM((d, 2 * D_FF), BF16),
                pltpu.VMEM((D_FF, d), BF16),
                pltpu.SemaphoreType.DMA((2,)),
            ],
        ),
        compiler_params=pltpu.CompilerParams(
            dimension_semantics=("arbitrary",), vmem_limit_bytes=VMEM_LIMIT_BYTES),
        name="moe_experts",
    )(blk_e, n_used, nxt_e, xs, w1, b1.reshape(depth, N_EXPERTS, 1, 2 * D_FF), w2,
      b2.reshape(depth, N_EXPERTS, 1, d))


def _combine_kernel(dcur_ref, dnext_ref, x_ref, g_ref, ys_ref, gf_ref, o_ref, buf, sem,
                    *, tc, n_tiles, final_norm):
    i = pl.program_id(0)
    sub = ROW_TILE_SUB

    def issue(dref, slot):
        def body(t, carry):
            for k in range(TOP_K):
                d = dref[0, 0, k * tc + t]
                pltpu.make_async_copy(
                    ys_ref.at[pl.ds(pl.multiple_of(d * sub, sub), sub), :],
                    buf.at[slot, k, pl.ds(pl.multiple_of(t * sub, sub), sub), :],
                    sem.at[slot]).start(priority=k % 2)
            return carry
        lax.fori_loop(0, tc, body, 0, unroll=ISSUE_UNROLL)

    @pl.when(i == 0)
    def _():
        issue(dcur_ref, 0)

    @pl.when(i + 1 < n_tiles)
    def _():
        issue(dnext_ref, (i + 1) % 2)

    slot = i % 2
    for k in range(TOP_K):
        pltpu.make_async_copy(ys_ref.at[pl.ds(0, tc * sub), :], buf.at[slot, k],
                              sem.at[slot]).wait()
    acc = x_ref[...]
    gates = g_ref[...]
    for k in range(TOP_K):
        acc = acc + gates[:, k:k + 1] * _load_row_tiles(buf, tc, lead=(slot, k))
    if final_norm:
        acc = _rms(acc, gf_ref[...])
    o_ref[...] = acc


def _combine(x2d, gates_t, dest, ys, g_final, final_norm):
    t, d = x2d.shape
    tc = min(COMBINE_TILE, t)
    nt = t // tc
    dest_tiles = dest.reshape(TOP_K, nt, tc).transpose(1, 0, 2).reshape(nt, 1, TOP_K * tc)
    kern = functools.partial(_combine_kernel, tc=tc, n_tiles=nt, final_norm=final_norm)
    return pl.pallas_call(
        kern,
        out_shape=jax.ShapeDtypeStruct((t, d), F32),
        grid=(nt,),
        in_specs=[
            pl.BlockSpec((1, 1, TOP_K * tc), lambda i: (i, 0, 0), memory_space=pltpu.SMEM),
            pl.BlockSpec((1, 1, TOP_K * tc), lambda i: (jnp.minimum(i + 1, nt - 1), 0, 0),
                         memory_space=pltpu.SMEM),
            pl.BlockSpec((tc, d), lambda i: (i, 0)),
            pl.BlockSpec((tc, TOP_K), lambda i: (i, 0)),
            pl.BlockSpec(memory_space=pl.ANY),
            pl.BlockSpec((1, d), lambda i: (0, 0)),
        ],
        out_specs=pl.BlockSpec((tc, d), lambda i: (i, 0)),
        scratch_shapes=[
            pltpu.VMEM((2, TOP_K, tc * ROW_TILE_SUB, 128), F32),
            pltpu.SemaphoreType.DMA((2,)),
        ],
        compiler_params=pltpu.CompilerParams(
            dimension_semantics=("arbitrary",), vmem_limit_bytes=VMEM_LIMIT_BYTES),
        name="moe_combine_final" if final_norm else "moe_combine",
    )(dest_tiles, dest_tiles, x2d, gates_t.T, ys, g_final.reshape(1, d))


def _moe_layer(x2d, g_ffn, w_router, b_router, layer, w1, b1, w2, b2, g_final, final_norm):
    t, d = x2d.shape
    h, ids, gates, rank, sizes = _route(x2d, g_ffn, w_router, b_router)
    sizes = sizes[:, 0].astype(I32)
    padded = ((sizes + MOE_BLOCK - 1) // MOE_BLOCK) * MOE_BLOCK
    pend = jnp.cumsum(padded)
    pstart = pend - padded
    onehot = ids[:, :, None] == jnp.arange(N_EXPERTS, dtype=I32)
    dest = rank + jnp.sum(jnp.where(onehot, pstart, 0), axis=-1)
    n_rows = t * TOP_K + N_EXPERTS * MOE_BLOCK
    nblk = n_rows // MOE_BLOCK
    blk_row = jnp.arange(nblk, dtype=I32) * MOE_BLOCK
    blk_e = jnp.minimum(jnp.sum((pend[None, :] <= blk_row[:, None]).astype(I32), axis=1),
                        N_EXPERTS - 1)
    n_used = (pend[-1:] // MOE_BLOCK).astype(I32)
    zrow = jnp.where(padded > 0, pend - MOE_BLOCK, -1).astype(I32)
    xs = _dispatch(h, dest, zrow, n_used, n_rows)
    blk_id = jnp.arange(nblk, dtype=I32)
    later = ((blk_id[None, :] > blk_id[:, None]) & (blk_e[None, :] != blk_e[:, None])
             & (blk_id[None, :] < n_used[0]))
    nxt_blk = jnp.min(jnp.where(later, blk_id[None, :], nblk), axis=1)
    nxt_e = jnp.sum(jnp.where(blk_id[None, :] == nxt_blk[:, None], blk_e[None, :] + 1, 0),
                    axis=1).astype(I32) - 1
    ys = _experts(xs, blk_e, n_used, nxt_e, layer, w1, b1, w2, b2)
    return _combine(x2d, gates, dest, ys, g_final, final_norm)


def _qkv_kernel(x_ref, g_ref, w_ref, b_ref, q_ref, k_ref, v_ref):
    h = _rms(x_ref[...], g_ref[...])
    qkv = jnp.dot(h.astype(BF16), w_ref[...], preferred_element_type=F32) + b_ref[...]
    nq = N_HEADS * HEAD_DIM
    nk = 2 * N_KV_HEADS * HEAD_DIM
    q_ref[...] = (qkv[:, :nq] * (HEAD_DIM ** -0.5)).astype(BF16)
    k_ref[...] = qkv[:, nq:nq + nk].astype(BF16)
    v_ref[...] = qkv[:, nq + nk:].astype(BF16)


def _qkv(x2d, g, w_qkv, b_qkv):
    t, d = x2d.shape
    ts = min(ROW_TILE, t)
    nq = N_HEADS * HEAD_DIM
    kv = N_KV_HEADS * HEAD_DIM

    def dup_heads(m):
        parts = []
        for hd in range(N_KV_HEADS):
            sl = m[..., hd * HEAD_DIM:(hd + 1) * HEAD_DIM]
            parts += [sl, sl]
        return jnp.concatenate(parts, axis=-1)

    w = jnp.concatenate([w_qkv[:, :nq], dup_heads(w_qkv[:, nq:nq + kv]),
                         dup_heads(w_qkv[:, nq + kv:])], axis=1).astype(BF16)
    b = jnp.concatenate([b_qkv[:nq], dup_heads(b_qkv[nq:nq + kv]),
                         dup_heads(b_qkv[nq + kv:])]).reshape(1, -1)
    n_out = w.shape[1]
    return pl.pallas_call(
        _qkv_kernel,
        out_shape=(
            jax.ShapeDtypeStruct((t, nq), BF16),
            jax.ShapeDtypeStruct((t, 2 * kv), BF16),
            jax.ShapeDtypeStruct((t, 2 * kv), BF16),
        ),
        grid=(t // ts,),
        in_specs=[
            pl.BlockSpec((ts, d), lambda i: (i, 0)),
            pl.BlockSpec((1, d), lambda i: (0, 0)),
            pl.BlockSpec((d, n_out), lambda i: (0, 0)),
            pl.BlockSpec((1, n_out), lambda i: (0, 0)),
        ],
        out_specs=(
            pl.BlockSpec((ts, nq), lambda i: (i, 0)),
            pl.BlockSpec((ts, 2 * kv), lambda i: (i, 0)),
            pl.BlockSpec((ts, 2 * kv), lambda i: (i, 0)),
        ),
        compiler_params=pltpu.CompilerParams(
            dimension_semantics=("arbitrary",), vmem_limit_bytes=VMEM_LIMIT_BYTES),
        name="attn_qkv",
    )(x2d, g.reshape(1, d), w, b)


def _attn_kernel(sinks_ref, q_ref, kp_ref, kc_ref, vp_ref, vc_ref, bias_ref, o_ref):
    n = pl.program_id(1)
    rows = SLABS_PER_GROUP * ATT_BLOCK
    keys = 2 * ATT_BLOCK
    q = q_ref[...]
    kcat = jnp.concatenate([kp_ref[...], kc_ref[...]], axis=0)
    vcat = jnp.concatenate([vp_ref[...], vc_ref[...]], axis=0)
    j = lax.broadcasted_iota(I32, (keys, rows), 0)
    r = lax.broadcasted_iota(I32, (keys, rows), 1) & (ATT_BLOCK - 1)
    lo = jnp.where(n == 0, jnp.maximum(r, ATT_BLOCK - 1), r)
    mask = (j > lo) & (j <= r + WINDOW)
    lane_k = lax.broadcasted_iota(I32, (keys, 128), 1)
    halves = [lane_k < HEAD_DIM, lane_k >= HEAD_DIM]
    sub_o = lax.broadcasted_iota(I32, (128, rows), 0)
    zero = jnp.zeros((), BF16)
    for g in range(N_KV_HEADS):
        s0 = g * SLABS_PER_GROUP
        qg = jnp.concatenate([q[:, (s0 + s) * 128:(s0 + s + 1) * 128]
                              for s in range(SLABS_PER_GROUP)], axis=0)
        kg = kcat[:, g * 128:(g + 1) * 128]
        vg = vcat[:, g * 128:(g + 1) * 128]
        pts, invs = [], []
        for p in range(HEADS_PER_SLAB):
            kp = jnp.where(halves[p], kg, zero)
            sc = lax.dot_general(kp, qg, (((1,), (1,)), ((), ())),
                                 preferred_element_type=F32)
            sc = jnp.where(mask, sc + bias_ref[g, p], NEG_INF)
            sink = jnp.concatenate(
                [jnp.full((1, ATT_BLOCK), sinks_ref[g * GQA_GROUP + s * HEADS_PER_SLAB + p], F32)
                 for s in range(SLABS_PER_GROUP)], axis=1)
            m = jnp.maximum(jnp.max(sc, axis=0, keepdims=True), sink)
            pe = jnp.exp(sc - m)
            den = jnp.sum(pe, axis=0, keepdims=True) + jnp.exp(sink - m)
            pts.append(pe.astype(BF16))
            invs.append(1.0 / den)
        pcat = jnp.concatenate(pts, axis=0)
        vblk = jnp.concatenate([jnp.where(halves[p], vg, zero)
                                for p in range(HEADS_PER_SLAB)], axis=0)
        og = lax.dot_general(vblk, pcat, (((0,), (0,)), ((), ())),
                             preferred_element_type=F32)
        og = og * jnp.where(sub_o < HEAD_DIM, invs[0], invs[1])
        og = og.T
        for s in range(SLABS_PER_GROUP):
            o_ref[:, (s0 + s) * 128:(s0 + s + 1) * 128] = (
                og[s * ATT_BLOCK:(s + 1) * ATT_BLOCK].astype(BF16))


def _t5_bias_table(rel_bias):
    r = np.arange(ATT_BLOCK)[:, None]
    j = np.arange(2 * ATT_BLOCK)[None, :]
    rel = np.clip(ATT_BLOCK + r - j, 0, WINDOW - 1)
    max_exact = N_BUCKETS // 2
    nf = jnp.maximum(rel, max_exact).astype(F32)
    large = max_exact + (jnp.log(nf / max_exact) / math.log(MAX_DISTANCE / max_exact)
                         * (N_BUCKETS - max_exact)).astype(I32)
    large = jnp.minimum(large, N_BUCKETS - 1)
    bucket = jnp.where(rel < max_exact, rel, large)
    pick = (bucket[None] == jnp.arange(N_BUCKETS, dtype=I32)[:, None, None])
    bias = jnp.sum(jnp.where(pick[:, None], rel_bias.astype(F32)[:, :, None, None], 0.0),
                   axis=0)
    bias = bias.reshape(N_KV_HEADS, SLABS_PER_GROUP, HEADS_PER_SLAB, ATT_BLOCK, 2 * ATT_BLOCK)
    bias = jnp.transpose(bias, (0, 2, 4, 1, 3))
    return bias.reshape(N_KV_HEADS, HEADS_PER_SLAB, 2 * ATT_BLOCK, SLABS_PER_GROUP * ATT_BLOCK)


def _attention(q, kk, vv, sinks, rel_bias, bsz, seq):
    nq = N_HEADS * HEAD_DIM
    kvw = kk.shape[-1]
    nb = seq // ATT_BLOCK
    q3 = q.reshape(bsz, seq, nq)
    k3 = kk.reshape(bsz, seq, kvw)
    v3 = vv.reshape(bsz, seq, kvw)
    bias = _t5_bias_table(rel_bias)
    prev = lambda bi, n: (bi, jnp.maximum(n - 1, 0), 0)
    cur = lambda bi, n: (bi, n, 0)
    out = pl.pallas_call(
        _attn_kernel,
        out_shape=jax.ShapeDtypeStruct((bsz, seq, nq), BF16),
        grid=(bsz, nb),
        in_specs=[
            pl.BlockSpec(memory_space=pltpu.SMEM),
            pl.BlockSpec((None, ATT_BLOCK, nq), cur),
            pl.BlockSpec((None, ATT_BLOCK, kvw), prev),
            pl.BlockSpec((None, ATT_BLOCK, kvw), cur),
            pl.BlockSpec((None, ATT_BLOCK, kvw), prev),
            pl.BlockSpec((None, ATT_BLOCK, kvw), cur),
            pl.BlockSpec(bias.shape, lambda bi, n: (0, 0, 0, 0)),
        ],
        out_specs=pl.BlockSpec((None, ATT_BLOCK, nq), cur),
        compiler_params=pltpu.CompilerParams(
            dimension_semantics=("arbitrary", "arbitrary"),
            vmem_limit_bytes=VMEM_LIMIT_BYTES),
        name="attn_core",
    )(sinks.astype(F32), q3, k3, k3, v3, v3, bias)
    return out.reshape(bsz * seq, nq)


def _oproj_kernel(x_ref, o_ref, w_ref, b_ref, y_ref):
    y_ref[...] = x_ref[...] + jnp.dot(o_ref[...], w_ref[...],
                                      preferred_element_type=F32) + b_ref[...]


def _out_proj(x2d, o, w_o, b_o):
    t, d = x2d.shape
    ts = min(ROW_TILE, t)
    nq = o.shape[1]
    return pl.pallas_call(
        _oproj_kernel,
        out_shape=jax.ShapeDtypeStruct((t, d), F32),
        grid=(t // ts,),
        in_specs=[
            pl.BlockSpec((ts, d), lambda i: (i, 0)),
            pl.BlockSpec((ts, nq), lambda i: (i, 0)),
            pl.BlockSpec((nq, d), lambda i: (0, 0)),
            pl.BlockSpec((1, d), lambda i: (0, 0)),
        ],
        out_specs=pl.BlockSpec((ts, d), lambda i: (i, 0)),
        compiler_params=pltpu.CompilerParams(
            dimension_semantics=("arbitrary",), vmem_limit_bytes=VMEM_LIMIT_BYTES),
        name="attn_out_proj",
    )(x2d, o, w_o.astype(BF16), b_o.reshape(1, d))


def kernel(x, norm_mix, norm_ffn, norm_final, pool_w, pool_b, pool_scale, w_qkv, b_qkv,
           sinks, w_o, b_o, rel_bias, w_router, b_router, w1, b1, w2, b2):
    bsz, seq, d = x.shape
    t = bsz * seq
    x = _pool_layer(x, norm_mix[0], pool_w[0], pool_b[0], pool_scale[0]).reshape(t, d)
    x = _moe_layer(x, norm_ffn[0], w_router[0], b_router[0], 0, w1, b1, w2, b2,
                   norm_final, final_norm=False)
    q, kk, vv = _qkv(x, norm_mix[1], w_qkv[0], b_qkv[0])
    o = _attention(q, kk, vv, sinks[0], rel_bias, bsz, seq)
    x = _out_proj(x, o, w_o[0], b_o[0])
    x = _moe_layer(x, norm_ffn[1], w_router[1], b_router[1], 1, w1, b1, w2, b2,
                   norm_final, final_norm=True)
    return x.reshape(bsz, seq, d)
```

```python
import functools
import math

import jax
import jax.numpy as jnp
import numpy as np
from jax import lax
from jax.experimental import pallas as pl
from jax.experimental.pallas import tpu as pltpu

F32 = jnp.float32
BF16 = jnp.bfloat16
I32 = jnp.int32

D_MODEL = 1024
POOL_WINDOWS = (2, 4, 8, 16)
POOL_GROUP_CH = D_MODEL // len(POOL_WINDOWS)
POOL_HALO = 16
HEAD_DIM = 64
N_HEADS = 16
N_KV_HEADS = 2
GQA_GROUP = N_HEADS // N_KV_HEADS
HEADS_PER_SLAB = 128 // HEAD_DIM
SLABS_PER_GROUP = GQA_GROUP // HEADS_PER_SLAB
ATT_BLOCK = 128
WINDOW = 128
N_BUCKETS = 32
MAX_DISTANCE = 128
N_EXPERTS = 32
TOP_K = 4
D_FF = D_MODEL
SWIGLU_ALPHA = 1.702
SWIGLU_LIMIT = 7.0
RMS_EPS = 1e-5
NEG_INF = -1e30

ROW_TILE = 512
MOE_BLOCK = 512
DISPATCH_TILE = 2048
COMBINE_TILE = 512
ISSUE_UNROLL = 4
VMEM_LIMIT_BYTES = 56 * 1024 * 1024


def _rms(x, g):
    return x * lax.rsqrt(jnp.mean(x * x, axis=-1, keepdims=True) + RMS_EPS) * g


ROW_TILE_SUB = D_MODEL // 128


def _load_row_tiles(ref, n, lead=()):
    return jnp.concatenate(
        [ref[lead + (pl.ds(j, n, stride=ROW_TILE_SUB), slice(None))]
         for j in range(ROW_TILE_SUB)], axis=1)


def _store_row_tiles(ref, val, n):
    for j in range(ROW_TILE_SUB):
        ref[pl.ds(j, n, stride=ROW_TILE_SUB), :] = val[:, j * 128:(j + 1) * 128]


def _pool_kernel(x_ref, xh_ref, g_ref, w_ref, b_ref, s_ref, o_ref, buf_ref, *, ts):
    i = pl.program_id(1)
    x = x_ref[...]
    g = g_ref[...]
    h = _rms(x, g)
    hh = _rms(xh_ref[...], g)
    hh = jnp.where(i == 0, 0.0, hh)
    buf_ref[0:POOL_HALO, :] = hh
    buf_ref[POOL_HALO:POOL_HALO + ts, :] = h
    t = i * ts + lax.broadcasted_iota(I32, (ts, 1), 0)
    outs = []
    for gi, win in enumerate(POOL_WINDOWS):
        c0 = gi * POOL_GROUP_CH
        hg = h[:, c0:c0 + POOL_GROUP_CH]
        acc = hg
        for d in range(1, win):
            acc = acc + buf_ref[POOL_HALO - d:POOL_HALO - d + ts, c0:c0 + POOL_GROUP_CH]
        cnt = jnp.minimum(t + 1, win).astype(F32)
        dlt = acc / cnt - hg
        outs.append(jnp.dot(dlt.astype(BF16), w_ref[gi], preferred_element_type=F32))
    y = jnp.concatenate(outs, axis=1)
    o_ref[...] = x + (y + b_ref[...]) * s_ref[...]


def _pool_layer(x, g, w, b, s):
    bsz, seq, d = x.shape
    ts = min(ROW_TILE, seq)
    kern = functools.partial(_pool_kernel, ts=ts)
    vec = lambda: pl.BlockSpec((1, d), lambda bi, i: (0, 0))
    return pl.pallas_call(
        kern,
        out_shape=jax.ShapeDtypeStruct(x.shape, F32),
        grid=(bsz, seq // ts),
        in_specs=[
            pl.BlockSpec((None, ts, d), lambda bi, i: (bi, i, 0)),
            pl.BlockSpec((None, POOL_HALO, d),
                         lambda bi, i: (bi, jnp.maximum(i * (ts // POOL_HALO) - 1, 0), 0)),
            vec(),
            pl.BlockSpec(w.shape, lambda bi, i: (0, 0, 0)),
            vec(),
            vec(),
        ],
        out_specs=pl.BlockSpec((None, ts, d), lambda bi, i: (bi, i, 0)),
        scratch_shapes=[pltpu.VMEM((POOL_HALO + ts, d), F32)],
        compiler_params=pltpu.CompilerParams(
            dimension_semantics=("arbitrary", "arbitrary"),
            vmem_limit_bytes=VMEM_LIMIT_BYTES),
        name="pool_layer",
    )(x, x, g.reshape(1, d), w.astype(BF16), b.reshape(1, d), s.reshape(1, d))


def _router_kernel(x_ref, g_ref, wr_ref, br_ref,
                   h_ref, ids_ref, gates_ref, rank_ref, sizes_ref, carry_ref, *, ts):
    i = pl.program_id(0)

    @pl.when(i == 0)
    def _():
        carry_ref[...] = jnp.zeros_like(carry_ref)

    h = _rms(x_ref[...], g_ref[...])
    _store_row_tiles(h_ref, h, ts)
    h_hi = h.astype(BF16)
    h_lo = (h - h_hi.astype(F32)).astype(BF16)
    w = wr_ref[...]
    w_hi = w.astype(BF16)
    w_lo = (w - w_hi.astype(F32)).astype(BF16)
    dn = (((1,), (1,)), ((), ()))
    lg = (lax.dot_general(w_hi, h_hi, dn, preferred_element_type=F32)
          + lax.dot_general(w_lo, h_hi, dn, preferred_element_type=F32)
          + lax.dot_general(w_hi, h_lo, dn, preferred_element_type=F32))
    lg = lg + br_ref[:, 0:1]

    iota_e = lax.broadcasted_iota(I32, (N_EXPERTS, ts), 0)
    vals, ids = [], []
    for _ in range(TOP_K):
        m = jnp.max(lg, axis=0, keepdims=True)
        idx = jnp.min(jnp.where(lg == m, iota_e, N_EXPERTS), axis=0, keepdims=True)
        vals.append(m)
        ids.append(idx)
        lg = jnp.where(iota_e == idx, -jnp.inf, lg)
    ex = [jnp.exp(v - vals[0]) for v in vals]
    den = ex[0] + ex[1] + ex[2] + ex[3]
    gates_ref[...] = jnp.concatenate([e / den for e in ex], axis=0)
    ids_ref[...] = jnp.concatenate(ids, axis=0)

    hot = [(iota_e == idx) for idx in ids]
    multi = (hot[0] | hot[1] | hot[2] | hot[3])
    multi_f = multi.astype(F32)
    r = lax.broadcasted_iota(I32, (ts, ts), 0)
    c = lax.broadcasted_iota(I32, (ts, ts), 1)
    upper = (r < c).astype(BF16)
    before = jnp.dot(multi_f.astype(BF16), upper, preferred_element_type=F32)
    before = before + carry_ref[:, 0:1]
    ranks = [jnp.sum(jnp.where(hk, before, 0.0), axis=0, keepdims=True) for hk in hot]
    rank_ref[...] = jnp.concatenate(ranks, axis=0).astype(I32)
    carry_ref[...] = carry_ref[...] + jnp.sum(multi_f, axis=1, keepdims=True)
    sizes_ref[...] = carry_ref[...]


def _route(x2d, g, w_router, b_router):
    t, d = x2d.shape
    ts = min(ROW_TILE, t)
    kern = functools.partial(_router_kernel, ts=ts)
    tok = lambda: pl.BlockSpec((TOP_K, ts), lambda i: (0, i))
    return pl.pallas_call(
        kern,
        out_shape=(
            jax.ShapeDtypeStruct((t * ROW_TILE_SUB, 128), F32),
            jax.ShapeDtypeStruct((TOP_K, t), I32),
            jax.ShapeDtypeStruct((TOP_K, t), F32),
            jax.ShapeDtypeStruct((TOP_K, t), I32),
            jax.ShapeDtypeStruct((N_EXPERTS, 128), F32),
        ),
        grid=(t // ts,),
        in_specs=[
            pl.BlockSpec((ts, d), lambda i: (i, 0)),
            pl.BlockSpec((1, d), lambda i: (0, 0)),
            pl.BlockSpec((N_EXPERTS, d), lambda i: (0, 0)),
            pl.BlockSpec((N_EXPERTS, 128), lambda i: (0, 0)),
        ],
        out_specs=(
            pl.BlockSpec((ts * ROW_TILE_SUB, 128), lambda i: (i, 0)),
            tok(), tok(), tok(),
            pl.BlockSpec((N_EXPERTS, 128), lambda i: (0, 0)),
        ),
        scratch_shapes=[pltpu.VMEM((N_EXPERTS, 128), F32)],
        compiler_params=pltpu.CompilerParams(
            dimension_semantics=("arbitrary",), vmem_limit_bytes=VMEM_LIMIT_BYTES),
        name="moe_route",
    )(x2d, g.reshape(1, d), w_router.T,
      jnp.broadcast_to(b_router.reshape(N_EXPERTS, 1), (N_EXPERTS, 128)))


def _dispatch_kernel(zrow_ref, n_used_ref, dest_ref, h_ref, xs_ref, zbuf, sem_z, sem,
                     *, td, blk, nblk):
    i = pl.program_id(0)
    sub = ROW_TILE_SUB

    def zero_block(row):
        row = pl.multiple_of(row * sub, blk * sub)
        return pltpu.make_async_copy(zbuf, xs_ref.at[pl.ds(row, blk * sub), :], sem_z)

    @pl.when(i == 0)
    def _():
        zbuf[...] = jnp.zeros_like(zbuf)
        for e in range(N_EXPERTS):
            @pl.when(zrow_ref[e] >= 0)
            def _():
                zero_block(zrow_ref[e]).start()

        def start_tail(b, carry):
            zero_block(b * blk).start()
            return carry

        def wait_tail(b, carry):
            zero_block(b * blk).wait()
            return carry

        lax.fori_loop(n_used_ref[0], nblk, start_tail, 0)
        for e in range(N_EXPERTS):
            @pl.when(zrow_ref[e] >= 0)
            def _():
                zero_block(zrow_ref[e]).wait()
        lax.fori_loop(n_used_ref[0], nblk, wait_tail, 0)

    def issue(t, carry):
        src = h_ref.at[pl.ds(pl.multiple_of(t * sub, sub), sub), :]
        for k in range(TOP_K):
            d = dest_ref[0, 0, k * td + t]
            pltpu.make_async_copy(src, xs_ref.at[pl.ds(pl.multiple_of(d * sub, sub), sub), :],
                                  sem).start(priority=k % 2)
        return carry

    lax.fori_loop(0, td, issue, 0, unroll=ISSUE_UNROLL)
    for k in range(TOP_K):
        pltpu.make_async_copy(h_ref, xs_ref.at[pl.ds(0, td * sub), :], sem).wait()


def _dispatch(h_tiles, dest, zrow, n_used, n_rows):
    sub = ROW_TILE_SUB
    t = h_tiles.shape[0] // sub
    td = min(DISPATCH_TILE, t)
    nt = t // td
    dest_tiles = dest.reshape(TOP_K, nt, td).transpose(1, 0, 2).reshape(nt, 1, TOP_K * td)
    kern = functools.partial(_dispatch_kernel, td=td, blk=MOE_BLOCK, nblk=n_rows // MOE_BLOCK)
    return pl.pallas_call(
        kern,
        out_shape=jax.ShapeDtypeStruct((n_rows * sub, 128), F32),
        grid_spec=pltpu.PrefetchScalarGridSpec(
            num_scalar_prefetch=2,
            grid=(nt,),
            in_specs=[
                pl.BlockSpec((1, 1, TOP_K * td), lambda i, z, nu: (i, 0, 0),
                             memory_space=pltpu.SMEM),
                pl.BlockSpec((td * sub, 128), lambda i, z, nu: (i, 0)),
            ],
            out_specs=pl.BlockSpec(memory_space=pl.ANY),
            scratch_shapes=[
                pltpu.VMEM((MOE_BLOCK * sub, 128), F32),
                pltpu.SemaphoreType.DMA(()),
                pltpu.SemaphoreType.DMA(()),
            ],
        ),
        compiler_params=pltpu.CompilerParams(
            dimension_semantics=("arbitrary",), vmem_limit_bytes=VMEM_LIMIT_BYTES),
        name="moe_dispatch",
    )(zrow, n_used, dest_tiles, h_tiles)


def _expert_kernel(blk_e_ref, n_used_ref, nxt_e_ref, n_valid_ref, x_ref, w1_hbm, b1_ref, w2_hbm,
                   b2_ref, o_ref, w1s_ref, w2s_ref, w1b_ref, w2b_ref, wsem, *, layer):
    i = pl.program_id(0)
    e = blk_e_ref[i]
    prev = blk_e_ref[jnp.maximum(i - 1, 0)]
    fresh = (i == 0) | (e != prev)
    live = i < n_used_ref[0]

    def weight_copies(expert):
        return (pltpu.make_async_copy(w1_hbm.at[layer, expert], w1s_ref, wsem.at[0]),
                pltpu.make_async_copy(w2_hbm.at[layer, expert], w2s_ref, wsem.at[1]))

    @pl.when(i == 0)
    def _():
        for cp in weight_copies(e):
            cp.start()

    @pl.when(live & fresh)
    def _():
        for cp in weight_copies(e):
            cp.wait()
        w1b_ref[...] = w1s_ref[...].astype(BF16)
        w2b_ref[...] = w2s_ref[...].astype(BF16)
        nxt = nxt_e_ref[i]

        @pl.when(nxt >= 0)
        def _():
            for cp in weight_copies(nxt):
                cp.start()

    def mlp(rows):
        x = _load_row_tiles(x_ref, rows).astype(BF16)
        a = jnp.dot(x, w1b_ref[...], preferred_element_type=F32) + b1_ref[...]
        glu = jnp.minimum(a[:, :D_FF], SWIGLU_LIMIT)
        lin = jnp.clip(a[:, D_FF:], -SWIGLU_LIMIT, SWIGLU_LIMIT)
        act = glu * jax.nn.sigmoid(SWIGLU_ALPHA * glu) * (lin + 1.0)
        y = jnp.dot(act.astype(BF16), w2b_ref[...], preferred_element_type=F32) + b2_ref[...]
        _store_row_tiles(o_ref, y, rows)
        if rows < MOE_BLOCK:
            o_ref[rows * ROW_TILE_SUB:, :] = jnp.zeros(
                ((MOE_BLOCK - rows) * ROW_TILE_SUB, 128), F32)

    half = MOE_BLOCK // 2
    short = n_valid_ref[i] <= half

    @pl.when(live & jnp.logical_not(short))
    def _():
        mlp(MOE_BLOCK)

    @pl.when(live & short)
    def _():
        mlp(half)

    @pl.when(jnp.logical_not(live))
    def _():
        o_ref[...] = jnp.zeros_like(o_ref)


def _experts(xs, blk_e, n_used, nxt_e, n_valid, layer, w1, b1, w2, b2):
    sub = ROW_TILE_SUB
    n_rows = xs.shape[0] // sub
    d = w2.shape[-1]
    nblk = n_rows // MOE_BLOCK
    depth = w1.shape[0]
    row = lambda i, be, nu, nx, nv: (jnp.minimum(i, nu[0] - 1), 0)
    out_row = lambda i, be, nu, nx, nv: (i, 0)
    exp3 = lambda i, be, nu, nx, nv: (layer, be[jnp.minimum(i, nu[0] - 1)], 0, 0)
    return pl.pallas_call(
        functools.partial(_expert_kernel, layer=layer),
        out_shape=jax.ShapeDtypeStruct((n_rows * sub, 128), F32),
        grid_spec=pltpu.PrefetchScalarGridSpec(
            num_scalar_prefetch=4,
            grid=(nblk,),
            in_specs=[
                pl.BlockSpec((MOE_BLOCK * sub, 128), row),
                pl.BlockSpec(memory_space=pl.ANY),
                pl.BlockSpec((None, None, 1, 2 * D_FF), exp3),
                pl.BlockSpec(memory_space=pl.ANY),
                pl.BlockSpec((None, None, 1, d), exp3),
            ],
            out_specs=pl.BlockSpec((MOE_BLOCK * sub, 128), out_row),
            scratch_shapes=[
                pltpu.VMEM((d, 2 * D_FF), F32),
                pltpu.VMEM((D_FF, d), F32),
                pltpu.VMEM((d, 2 * D_FF), BF16),
                pltpu.VMEM((D_FF, d), BF16),
                pltpu.SemaphoreType.DMA((2,)),
            ],
        ),
        compiler_params=pltpu.CompilerParams(
            dimension_semantics=("arbitrary",), vmem_limit_bytes=VMEM_LIMIT_BYTES),
        name="moe_experts",
    )(blk_e, n_used, nxt_e, n_valid, xs, w1, b1.reshape(depth, N_EXPERTS, 1, 2 * D_FF), w2,
      b2.reshape(depth, N_EXPERTS, 1, d))


def _combine_kernel(dcur_ref, dnext_ref, x_ref, g_ref, ys_ref, gf_ref, o_ref, buf, sem,
                    *, tc, n_tiles, final_norm):
    i = pl.program_id(0)
    sub = ROW_TILE_SUB

    def issue(dref, slot):
        def body(t, carry):
            for k in range(TOP_K):
                d = dref[0, 0, k * tc + t]
                pltpu.make_async_copy(
                    ys_ref.at[pl.ds(pl.multiple_of(d * sub, sub), sub), :],
                    buf.at[slot, k, pl.ds(pl.multiple_of(t * sub, sub), sub), :],
                    sem.at[slot]).start(priority=k % 2)
            return carry
        lax.fori_loop(0, tc, body, 0, unroll=ISSUE_UNROLL)

    @pl.when(i == 0)
    def _():
        issue(dcur_ref, 0)

    @pl.when(i + 1 < n_tiles)
    def _():
        issue(dnext_ref, (i + 1) % 2)

    slot = i % 2
    for k in range(TOP_K):
        pltpu.make_async_copy(ys_ref.at[pl.ds(0, tc * sub), :], buf.at[slot, k],
                              sem.at[slot]).wait()
    acc = x_ref[...]
    gates = g_ref[...]
    for k in range(TOP_K):
        acc = acc + gates[:, k:k + 1] * _load_row_tiles(buf, tc, lead=(slot, k))
    if final_norm:
        acc = _rms(acc, gf_ref[...])
    o_ref[...] = acc


def _combine(x2d, gates_t, dest, ys, g_final, final_norm):
    t, d = x2d.shape
    tc = min(COMBINE_TILE, t)
    nt = t // tc
    dest_tiles = dest.reshape(TOP_K, nt, tc).transpose(1, 0, 2).reshape(nt, 1, TOP_K * tc)
    kern = functools.partial(_combine_kernel, tc=tc, n_tiles=nt, final_norm=final_norm)
    return pl.pallas_call(
        kern,
        out_shape=jax.ShapeDtypeStruct((t, d), F32),
        grid=(nt,),
        in_specs=[
            pl.BlockSpec((1, 1, TOP_K * tc), lambda i: (i, 0, 0), memory_space=pltpu.SMEM),
            pl.BlockSpec((1, 1, TOP_K * tc), lambda i: (jnp.minimum(i + 1, nt - 1), 0, 0),
                         memory_space=pltpu.SMEM),
            pl.BlockSpec((tc, d), lambda i: (i, 0)),
            pl.BlockSpec((tc, TOP_K), lambda i: (i, 0)),
            pl.BlockSpec(memory_space=pl.ANY),
            pl.BlockSpec((1, d), lambda i: (0, 0)),
        ],
        out_specs=pl.BlockSpec((tc, d), lambda i: (i, 0)),
        scratch_shapes=[
            pltpu.VMEM((2, TOP_K, tc * ROW_TILE_SUB, 128), F32),
            pltpu.SemaphoreType.DMA((2,)),
        ],
        compiler_params=pltpu.CompilerParams(
            dimension_semantics=("arbitrary",), vmem_limit_bytes=VMEM_LIMIT_BYTES),
        name="moe_combine_final" if final_norm else "moe_combine",
    )(dest_tiles, dest_tiles, x2d, gates_t.T, ys, g_final.reshape(1, d))


def _moe_layer(x2d, g_ffn, w_router, b_router, layer, w1, b1, w2, b2, g_final, final_norm):
    t, d = x2d.shape
    h, ids, gates, rank, sizes = _route(x2d, g_ffn, w_router, b_router)
    sizes = sizes[:, 0].astype(I32)
    padded = ((sizes + MOE_BLOCK - 1) // MOE_BLOCK) * MOE_BLOCK
    pend = jnp.cumsum(padded)
    pstart = pend - padded
    onehot = ids[:, :, None] == jnp.arange(N_EXPERTS, dtype=I32)
    dest = rank + jnp.sum(jnp.where(onehot, pstart, 0), axis=-1)
    n_rows = t * TOP_K + N_EXPERTS * MOE_BLOCK
    nblk = n_rows // MOE_BLOCK
    blk_row = jnp.arange(nblk, dtype=I32) * MOE_BLOCK
    blk_e = jnp.minimum(jnp.sum((pend[None, :] <= blk_row[:, None]).astype(I32), axis=1),
                        N_EXPERTS - 1)
    n_used = (pend[-1:] // MOE_BLOCK).astype(I32)
    zrow = jnp.where(padded > 0, pend - MOE_BLOCK, -1).astype(I32)
    xs = _dispatch(h, dest, zrow, n_used, n_rows)
    blk_id = jnp.arange(nblk, dtype=I32)
    later = ((blk_id[None, :] > blk_id[:, None]) & (blk_e[None, :] != blk_e[:, None])
             & (blk_id[None, :] < n_used[0]))
    nxt_blk = jnp.min(jnp.where(later, blk_id[None, :], nblk), axis=1)
    nxt_e = jnp.sum(jnp.where(blk_id[None, :] == nxt_blk[:, None], blk_e[None, :] + 1, 0),
                    axis=1).astype(I32) - 1
    own = blk_e[:, None] == jnp.arange(N_EXPERTS, dtype=I32)[None, :]
    row_end = jnp.sum(jnp.where(own, (pstart + sizes)[None, :], 0), axis=1)
    n_valid = jnp.clip(row_end - blk_row, 0, MOE_BLOCK).astype(I32)
    ys = _experts(xs, blk_e, n_used, nxt_e, n_valid, layer, w1, b1, w2, b2)
    return _combine(x2d, gates, dest, ys, g_final, final_norm)


def _qkv_kernel(x_ref, g_ref, w_ref, b_ref, q_ref, k_ref, v_ref):
    h = _rms(x_ref[...], g_ref[...])
    qkv = jnp.dot(h.astype(BF16), w_ref[...], preferred_element_type=F32) + b_ref[...]
    nq = N_HEADS * HEAD_DIM
    nk = 2 * N_KV_HEADS * HEAD_DIM
    q_ref[...] = (qkv[:, :nq] * (HEAD_DIM ** -0.5)).astype(BF16)
    k_ref[...] = qkv[:, nq:nq + nk].astype(BF16)
    v_ref[...] = qkv[:, nq + nk:].astype(BF16)


def _qkv(x2d, g, w_qkv, b_qkv):
    t, d = x2d.shape
    ts = min(ROW_TILE, t)
    nq = N_HEADS * HEAD_DIM
    kv = N_KV_HEADS * HEAD_DIM

    def dup_heads(m):
        parts = []
        for hd in range(N_KV_HEADS):
            sl = m[..., hd * HEAD_DIM:(hd + 1) * HEAD_DIM]
            parts += [sl, sl]
        return jnp.concatenate(parts, axis=-1)

    w = jnp.concatenate([w_qkv[:, :nq], dup_heads(w_qkv[:, nq:nq + kv]),
                         dup_heads(w_qkv[:, nq + kv:])], axis=1).astype(BF16)
    b = jnp.concatenate([b_qkv[:nq], dup_heads(b_qkv[nq:nq + kv]),
                         dup_heads(b_qkv[nq + kv:])]).reshape(1, -1)
    n_out = w.shape[1]
    return pl.pallas_call(
        _qkv_kernel,
        out_shape=(
            jax.ShapeDtypeStruct((t, nq), BF16),
            jax.ShapeDtypeStruct((t, 2 * kv), BF16),
            jax.ShapeDtypeStruct((t, 2 * kv), BF16),
        ),
        grid=(t // ts,),
        in_specs=[
            pl.BlockSpec((ts, d), lambda i: (i, 0)),
            pl.BlockSpec((1, d), lambda i: (0, 0)),
            pl.BlockSpec((d, n_out), lambda i: (0, 0)),
            pl.BlockSpec((1, n_out), lambda i: (0, 0)),
        ],
        out_specs=(
            pl.BlockSpec((ts, nq), lambda i: (i, 0)),
            pl.BlockSpec((ts, 2 * kv), lambda i: (i, 0)),
            pl.BlockSpec((ts, 2 * kv), lambda i: (i, 0)),
        ),
        compiler_params=pltpu.CompilerParams(
            dimension_semantics=("arbitrary",), vmem_limit_bytes=VMEM_LIMIT_BYTES),
        name="attn_qkv",
    )(x2d, g.reshape(1, d), w, b)


def _attn_kernel(sinks_ref, q_ref, kp_ref, kc_ref, vp_ref, vc_ref, bias_ref, o_ref):
    n = pl.program_id(1)
    rows = SLABS_PER_GROUP * ATT_BLOCK
    keys = 2 * ATT_BLOCK
    q = q_ref[...]
    kcat = jnp.concatenate([kp_ref[...], kc_ref[...]], axis=0)
    vcat = jnp.concatenate([vp_ref[...], vc_ref[...]], axis=0)
    j = lax.broadcasted_iota(I32, (keys, rows), 0)
    r = lax.broadcasted_iota(I32, (keys, rows), 1) & (ATT_BLOCK - 1)
    lo = jnp.where(n == 0, jnp.maximum(r, ATT_BLOCK - 1), r)
    mask = (j > lo) & (j <= r + WINDOW)
    lane_k = lax.broadcasted_iota(I32, (keys, 128), 1)
    halves = [lane_k < HEAD_DIM, lane_k >= HEAD_DIM]
    sub_o = lax.broadcasted_iota(I32, (128, rows), 0)
    zero = jnp.zeros((), BF16)
    for g in range(N_KV_HEADS):
        s0 = g * SLABS_PER_GROUP
        qg = jnp.concatenate([q[:, (s0 + s) * 128:(s0 + s + 1) * 128]
                              for s in range(SLABS_PER_GROUP)], axis=0)
        kg = kcat[:, g * 128:(g + 1) * 128]
        vg = vcat[:, g * 128:(g + 1) * 128]
        pts, invs = [], []
        for p in range(HEADS_PER_SLAB):
            kp = jnp.where(halves[p], kg, zero)
            sc = lax.dot_general(kp, qg, (((1,), (1,)), ((), ())),
                                 preferred_element_type=F32)
            sc = jnp.where(mask, sc + bias_ref[g, p], NEG_INF)
            sink = jnp.concatenate(
                [jnp.full((1, ATT_BLOCK), sinks_ref[g * GQA_GROUP + s * HEADS_PER_SLAB + p], F32)
                 for s in range(SLABS_PER_GROUP)], axis=1)
            m = jnp.maximum(jnp.max(sc, axis=0, keepdims=True), sink)
            pe = jnp.exp(sc - m)
            den = jnp.sum(pe, axis=0, keepdims=True) + jnp.exp(sink - m)
            pts.append(pe.astype(BF16))
            invs.append(1.0 / den)
        pcat = jnp.concatenate(pts, axis=0)
        vblk = jnp.concatenate([jnp.where(halves[p], vg, zero)
                                for p in range(HEADS_PER_SLAB)], axis=0)
        og = lax.dot_general(vblk, pcat, (((0,), (0,)), ((), ())),
                             preferred_element_type=F32)
        og = og * jnp.where(sub_o < HEAD_DIM, invs[0], invs[1])
        og = og.T
        for s in range(SLABS_PER_GROUP):
            o_ref[:, (s0 + s) * 128:(s0 + s + 1) * 128] = (
                og[s * ATT_BLOCK:(s + 1) * ATT_BLOCK].astype(BF16))


def _t5_bias_table(rel_bias):
    r = np.arange(ATT_BLOCK)[:, None]
    j = np.arange(2 * ATT_BLOCK)[None, :]
    rel = np.clip(ATT_BLOCK + r - j, 0, WINDOW - 1)
    max_exact = N_BUCKETS // 2
    nf = jnp.maximum(rel, max_exact).astype(F32)
    large = max_exact + (jnp.log(nf / max_exact) / math.log(MAX_DISTANCE / max_exact)
                         * (N_BUCKETS - max_exact)).astype(I32)
    large = jnp.minimum(large, N_BUCKETS - 1)
    bucket = jnp.where(rel < max_exact, rel, large)
    pick = (bucket[None] == jnp.arange(N_BUCKETS, dtype=I32)[:, None, None])
    bias = jnp.sum(jnp.where(pick[:, None], rel_bias.astype(F32)[:, :, None, None], 0.0),
                   axis=0)
    bias = bias.reshape(N_KV_HEADS, SLABS_PER_GROUP, HEADS_PER_SLAB, ATT_BLOCK, 2 * ATT_BLOCK)
    bias = jnp.transpose(bias, (0, 2, 4, 1, 3))
    return bias.reshape(N_KV_HEADS, HEADS_PER_SLAB, 2 * ATT_BLOCK, SLABS_PER_GROUP * ATT_BLOCK)


def _attention(q, kk, vv, sinks, rel_bias, bsz, seq):
    nq = N_HEADS * HEAD_DIM
    kvw = kk.shape[-1]
    nb = seq // ATT_BLOCK
    q3 = q.reshape(bsz, seq, nq)
    k3 = kk.reshape(bsz, seq, kvw)
    v3 = vv.reshape(bsz, seq, kvw)
    bias = _t5_bias_table(rel_bias)
    prev = lambda bi, n: (bi, jnp.maximum(n - 1, 0), 0)
    cur = lambda bi, n: (bi, n, 0)
    out = pl.pallas_call(
        _attn_kernel,
        out_shape=jax.ShapeDtypeStruct((bsz, seq, nq), BF16),
        grid=(bsz, nb),
        in_specs=[
            pl.BlockSpec(memory_space=pltpu.SMEM),
            pl.BlockSpec((None, ATT_BLOCK, nq), cur),
            pl.BlockSpec((None, ATT_BLOCK, kvw), prev),
            pl.BlockSpec((None, ATT_BLOCK, kvw), cur),
            pl.BlockSpec((None, ATT_BLOCK, kvw), prev),
            pl.BlockSpec((None, ATT_BLOCK, kvw), cur),
            pl.BlockSpec(bias.shape, lambda bi, n: (0, 0, 0, 0)),
        ],
        out_specs=pl.BlockSpec((None, ATT_BLOCK, nq), cur),
        compiler_params=pltpu.CompilerParams(
            dimension_semantics=("arbitrary", "arbitrary"),
            vmem_limit_bytes=VMEM_LIMIT_BYTES),
        name="attn_core",
    )(sinks.astype(F32), q3, k3, k3, v3, v3, bias)
    return out.reshape(bsz * seq, nq)


def _oproj_kernel(x_ref, o_ref, w_ref, b_ref, y_ref):
    y_ref[...] = x_ref[...] + jnp.dot(o_ref[...], w_ref[...],
                                      preferred_element_type=F32) + b_ref[...]


def _out_proj(x2d, o, w_o, b_o):
    t, d = x2d.shape
    ts = min(ROW_TILE, t)
    nq = o.shape[1]
    return pl.pallas_call(
        _oproj_kernel,
        out_shape=jax.ShapeDtypeStruct((t, d), F32),
        grid=(t // ts,),
        in_specs=[
            pl.BlockSpec((ts, d), lambda i: (i, 0)),
            pl.BlockSpec((ts, nq), lambda i: (i, 0)),
            pl.BlockSpec((nq, d), lambda i: (0, 0)),
            pl.BlockSpec((1, d), lambda i: (0, 0)),
        ],
        out_specs=pl.BlockSpec((ts, d), lambda i: (i, 0)),
        compiler_params=pltpu.CompilerParams(
            dimension_semantics=("arbitrary",), vmem_limit_bytes=VMEM_LIMIT_BYTES),
        name="attn_out_proj",
    )(x2d, o, w_o.astype(BF16), b_o.reshape(1, d))


def kernel(x, norm_mix, norm_ffn, norm_final, pool_w, pool_b, pool_scale, w_qkv, b_qkv,
           sinks, w_o, b_o, rel_bias, w_router, b_router, w1, b1, w2, b2):
    bsz, seq, d = x.shape
    t = bsz * seq
    x = _pool_layer(x, norm_mix[0], pool_w[0], pool_b[0], pool_scale[0]).reshape(t, d)
    x = _moe_layer(x, norm_ffn[0], w_router[0], b_router[0], 0, w1, b1, w2, b2,
                   norm_final, final_norm=False)
    q, kk, vv = _qkv(x, norm_mix[1], w_qkv[0], b_qkv[0])
    o = _attention(q, kk, vv, sinks[0], rel_bias, bsz, seq)
    x = _out_proj(x, o, w_o[0], b_o[0])
    x = _moe_layer(x, norm_ffn[1], w_router[1], b_router[1], 1, w1, b1, w2, b2,
                   norm_final, final_norm=True)
    return x.reshape(bsz, seq, d)
```

```python
import functools
import math

import jax
import jax.numpy as jnp
import numpy as np
from jax import lax
from jax.experimental import pallas as pl
from jax.experimental.pallas import tpu as pltpu

F32 = jnp.float32
BF16 = jnp.bfloat16
I32 = jnp.int32

D_MODEL = 1024
POOL_WINDOWS = (2, 4, 8, 16)
POOL_GROUP_CH = D_MODEL // len(POOL_WINDOWS)
POOL_HALO = 16
HEAD_DIM = 64
N_HEADS = 16
N_KV_HEADS = 2
GQA_GROUP = N_HEADS // N_KV_HEADS
HEADS_PER_SLAB = 128 // HEAD_DIM
SLABS_PER_GROUP = GQA_GROUP // HEADS_PER_SLAB
ATT_BLOCK = 128
ATT_STEP_BLOCKS = 2
WINDOW = 128
N_BUCKETS = 32
MAX_DISTANCE = 128
N_EXPERTS = 32
TOP_K = 4
D_FF = D_MODEL
SWIGLU_ALPHA = 1.702
SWIGLU_LIMIT = 7.0
RMS_EPS = 1e-5
NEG_INF = -1e30

ROW_TILE = 512
MOE_BLOCK = 512
DISPATCH_TILE = 2048
COMBINE_TILE = 256
ISSUE_UNROLL = 4
VMEM_LIMIT_BYTES = 56 * 1024 * 1024


def _rms(x, g):
    return x * lax.rsqrt(jnp.mean(x * x, axis=-1, keepdims=True) + RMS_EPS) * g


ROW_TILE_SUB = D_MODEL // 128


def _load_row_tiles(ref, n, lead=()):
    return jnp.concatenate(
        [ref[lead + (pl.ds(j, n, stride=ROW_TILE_SUB), slice(None))]
         for j in range(ROW_TILE_SUB)], axis=1)


def _store_row_tiles(ref, val, n):
    for j in range(ROW_TILE_SUB):
        ref[pl.ds(j, n, stride=ROW_TILE_SUB), :] = val[:, j * 128:(j + 1) * 128]


POOL_PAD = 8


def _pool_kernel(x_ref, xh_ref, g_ref, w_ref, b_ref, s_ref, o_ref, buf_ref, s1_ref, s2_ref,
                 s3_ref, *, ts):
    i = pl.program_id(1)
    c = POOL_GROUP_CH
    h0 = POOL_PAD
    t0 = POOL_PAD + POOL_HALO
    r1 = t0 + ts
    x = x_ref[...]
    g = g_ref[...]
    h = _rms(x, g)
    hh = _rms(xh_ref[...], g)
    hh = jnp.where(i == 0, 0.0, hh)
    buf_ref[0:h0, :] = jnp.zeros((h0, D_MODEL), F32)
    buf_ref[h0:t0, :] = hh
    buf_ref[t0:r1, :] = h
    sum0 = h[:, 0:c] + buf_ref[t0 - 1:r1 - 1, 0:c]
    l1 = buf_ref[h0:r1, c:] + buf_ref[h0 - 1:r1 - 1, c:]
    s1_ref[0:h0, :] = jnp.zeros((h0, 3 * c), F32)
    s1_ref[h0:r1, :] = l1
    sum1 = s1_ref[t0:r1, 0:c] + s1_ref[t0 - 2:r1 - 2, 0:c]
    l2 = s1_ref[h0:r1, c:] + s1_ref[h0 - 2:r1 - 2, c:]
    s2_ref[0:h0, :] = jnp.zeros((h0, 2 * c), F32)
    s2_ref[h0:r1, :] = l2
    sum2 = s2_ref[t0:r1, 0:c] + s2_ref[t0 - 4:r1 - 4, 0:c]
    s3_ref[h0:r1, :] = s2_ref[h0:r1, c:] + s2_ref[h0 - 4:r1 - 4, c:]
    sum3 = s3_ref[t0:r1, :] + s3_ref[t0 - 8:r1 - 8, :]
    t = i * ts + lax.broadcasted_iota(I32, (ts, 1), 0)
    outs = []
    for gi, (win, acc) in enumerate(zip(POOL_WINDOWS, (sum0, sum1, sum2, sum3))):
        cnt = jnp.minimum(t + 1, win).astype(F32)
        dlt = acc / cnt - h[:, gi * c:(gi + 1) * c]
        outs.append(jnp.dot(dlt.astype(BF16), w_ref[gi], preferred_element_type=F32))
    y = jnp.concatenate(outs, axis=1)
    o_ref[...] = x + (y + b_ref[...]) * s_ref[...]


def _pool_layer(x, g, w, b, s):
    bsz, seq, d = x.shape
    ts = min(ROW_TILE, seq)
    kern = functools.partial(_pool_kernel, ts=ts)
    rows = POOL_PAD + POOL_HALO + ts
    vec = lambda: pl.BlockSpec((1, d), lambda bi, i: (0, 0))
    return pl.pallas_call(
        kern,
        out_shape=jax.ShapeDtypeStruct(x.shape, F32),
        grid=(bsz, seq // ts),
        in_specs=[
            pl.BlockSpec((None, ts, d), lambda bi, i: (bi, i, 0)),
            pl.BlockSpec((None, POOL_HALO, d),
                         lambda bi, i: (bi, jnp.maximum(i * (ts // POOL_HALO) - 1, 0), 0)),
            vec(),
            pl.BlockSpec(w.shape, lambda bi, i: (0, 0, 0)),
            vec(),
            vec(),
        ],
        out_specs=pl.BlockSpec((None, ts, d), lambda bi, i: (bi, i, 0)),
        scratch_shapes=[pltpu.VMEM((rows, d), F32),
                        pltpu.VMEM((rows, 3 * POOL_GROUP_CH), F32),
                        pltpu.VMEM((rows, 2 * POOL_GROUP_CH), F32),
                        pltpu.VMEM((rows, POOL_GROUP_CH), F32)],
        compiler_params=pltpu.CompilerParams(
            dimension_semantics=("arbitrary", "arbitrary"),
            vmem_limit_bytes=VMEM_LIMIT_BYTES),
        name="pool_layer",
    )(x, x, g.reshape(1, d), w.astype(BF16), b.reshape(1, d), s.reshape(1, d))


def _router_kernel(x_ref, g_ref, wr_ref, br_ref,
                   h_ref, ids_ref, gates_ref, rank_ref, sizes_ref, carry_ref, *, ts):
    i = pl.program_id(0)

    @pl.when(i == 0)
    def _():
        carry_ref[...] = jnp.zeros_like(carry_ref)

    h = _rms(x_ref[...], g_ref[...])
    _store_row_tiles(h_ref, h, ts)
    h_hi = h.astype(BF16)
    h_lo = (h - h_hi.astype(F32)).astype(BF16)
    w = wr_ref[...]
    w_hi = w.astype(BF16)
    w_lo = (w - w_hi.astype(F32)).astype(BF16)
    dn = (((1,), (1,)), ((), ()))
    lg = (lax.dot_general(w_hi, h_hi, dn, preferred_element_type=F32)
          + lax.dot_general(w_lo, h_hi, dn, preferred_element_type=F32)
          + lax.dot_general(w_hi, h_lo, dn, preferred_element_type=F32))
    lg = lg + br_ref[:, 0:1]

    iota_e = lax.broadcasted_iota(I32, (N_EXPERTS, ts), 0)
    vals, ids = [], []
    for _ in range(TOP_K):
        m = jnp.max(lg, axis=0, keepdims=True)
        idx = jnp.min(jnp.where(lg == m, iota_e, N_EXPERTS), axis=0, keepdims=True)
        vals.append(m)
        ids.append(idx)
        lg = jnp.where(iota_e == idx, -jnp.inf, lg)
    ex = [jnp.exp(v - vals[0]) for v in vals]
    den = ex[0] + ex[1] + ex[2] + ex[3]
    gates_ref[...] = jnp.concatenate([e / den for e in ex], axis=0)
    ids_ref[...] = jnp.concatenate(ids, axis=0)

    hot = [(iota_e == idx) for idx in ids]
    multi = (hot[0] | hot[1] | hot[2] | hot[3])
    multi_f = multi.astype(F32)
    r = lax.broadcasted_iota(I32, (ts, ts), 0)
    c = lax.broadcasted_iota(I32, (ts, ts), 1)
    upper = (r < c).astype(BF16)
    before = jnp.dot(multi_f.astype(BF16), upper, preferred_element_type=F32)
    before = before + carry_ref[:, 0:1]
    ranks = [jnp.sum(jnp.where(hk, before, 0.0), axis=0, keepdims=True) for hk in hot]
    rank_ref[...] = jnp.concatenate(ranks, axis=0).astype(I32)
    carry_ref[...] = carry_ref[...] + jnp.sum(multi_f, axis=1, keepdims=True)
    sizes_ref[...] = carry_ref[...]


def _route(x2d, g, w_router, b_router):
    t, d = x2d.shape
    ts = min(ROW_TILE, t)
    kern = functools.partial(_router_kernel, ts=ts)
    tok = lambda: pl.BlockSpec((TOP_K, ts), lambda i: (0, i))
    return pl.pallas_call(
        kern,
        out_shape=(
            jax.ShapeDtypeStruct((t * ROW_TILE_SUB, 128), F32),
            jax.ShapeDtypeStruct((TOP_K, t), I32),
            jax.ShapeDtypeStruct((TOP_K, t), F32),
            jax.ShapeDtypeStruct((TOP_K, t), I32),
            jax.ShapeDtypeStruct((N_EXPERTS, 128), F32),
        ),
        grid=(t // ts,),
        in_specs=[
            pl.BlockSpec((ts, d), lambda i: (i, 0)),
            pl.BlockSpec((1, d), lambda i: (0, 0)),
            pl.BlockSpec((N_EXPERTS, d), lambda i: (0, 0)),
            pl.BlockSpec((N_EXPERTS, 128), lambda i: (0, 0)),
        ],
        out_specs=(
            pl.BlockSpec((ts * ROW_TILE_SUB, 128), lambda i: (i, 0)),
            tok(), tok(), tok(),
            pl.BlockSpec((N_EXPERTS, 128), lambda i: (0, 0)),
        ),
        scratch_shapes=[pltpu.VMEM((N_EXPERTS, 128), F32)],
        compiler_params=pltpu.CompilerParams(
            dimension_semantics=("arbitrary",), vmem_limit_bytes=VMEM_LIMIT_BYTES),
        name="moe_route",
    )(x2d, g.reshape(1, d), w_router.T,
      jnp.broadcast_to(b_router.reshape(N_EXPERTS, 1), (N_EXPERTS, 128)))


def _dispatch_kernel(zrow_ref, n_used_ref, dest_ref, h_ref, xs_ref, zbuf, sem_z, sem,
                     *, td, blk, nblk):
    i = pl.program_id(0)
    sub = ROW_TILE_SUB

    def zero_block(row):
        row = pl.multiple_of(row * sub, blk * sub)
        return pltpu.make_async_copy(zbuf, xs_ref.at[pl.ds(row, blk * sub), :], sem_z)

    @pl.when(i == 0)
    def _():
        zbuf[...] = jnp.zeros_like(zbuf)
        for e in range(N_EXPERTS):
            @pl.when(zrow_ref[e] >= 0)
            def _():
                zero_block(zrow_ref[e]).start()

        def start_tail(b, carry):
            zero_block(b * blk).start()
            return carry

        def wait_tail(b, carry):
            zero_block(b * blk).wait()
            return carry

        lax.fori_loop(n_used_ref[0], nblk, start_tail, 0)
        for e in range(N_EXPERTS):
            @pl.when(zrow_ref[e] >= 0)
            def _():
                zero_block(zrow_ref[e]).wait()
        lax.fori_loop(n_used_ref[0], nblk, wait_tail, 0)

    def issue(t, carry):
        src = h_ref.at[pl.ds(pl.multiple_of(t * sub, sub), sub), :]
        for k in range(TOP_K):
            d = dest_ref[0, 0, k * td + t]
            pltpu.make_async_copy(src, xs_ref.at[pl.ds(pl.multiple_of(d * sub, sub), sub), :],
                                  sem).start(priority=k % 2)
        return carry

    lax.fori_loop(0, td, issue, 0, unroll=ISSUE_UNROLL)
    for k in range(TOP_K):
        pltpu.make_async_copy(h_ref, xs_ref.at[pl.ds(0, td * sub), :], sem).wait()


def _dispatch(h_tiles, dest, zrow, n_used, n_rows):
    sub = ROW_TILE_SUB
    t = h_tiles.shape[0] // sub
    td = min(DISPATCH_TILE, t)
    nt = t // td
    dest_tiles = dest.reshape(TOP_K, nt, td).transpose(1, 0, 2).reshape(nt, 1, TOP_K * td)
    kern = functools.partial(_dispatch_kernel, td=td, blk=MOE_BLOCK, nblk=n_rows // MOE_BLOCK)
    return pl.pallas_call(
        kern,
        out_shape=jax.ShapeDtypeStruct((n_rows * sub, 128), F32),
        grid_spec=pltpu.PrefetchScalarGridSpec(
            num_scalar_prefetch=2,
            grid=(nt,),
            in_specs=[
                pl.BlockSpec((1, 1, TOP_K * td), lambda i, z, nu: (i, 0, 0),
                             memory_space=pltpu.SMEM),
                pl.BlockSpec((td * sub, 128), lambda i, z, nu: (i, 0)),
            ],
            out_specs=pl.BlockSpec(memory_space=pl.ANY),
            scratch_shapes=[
                pltpu.VMEM((MOE_BLOCK * sub, 128), F32),
                pltpu.SemaphoreType.DMA(()),
                pltpu.SemaphoreType.DMA(()),
            ],
        ),
        compiler_params=pltpu.CompilerParams(
            dimension_semantics=("arbitrary",), vmem_limit_bytes=VMEM_LIMIT_BYTES),
        name="moe_dispatch",
    )(zrow, n_used, dest_tiles, h_tiles)


def _expert_kernel(blk_e_ref, n_used_ref, nxt_e_ref, n_valid_ref, x_ref, w1_hbm, b1_ref, w2_hbm,
                   b2_ref, o_ref, w1s_ref, w2s_ref, w1b_ref, w2b_ref, wsem, *, layer):
    i = pl.program_id(0)
    e = blk_e_ref[i]
    prev = blk_e_ref[jnp.maximum(i - 1, 0)]
    fresh = (i == 0) | (e != prev)
    live = i < n_used_ref[0]

    def weight_copies(expert):
        return (pltpu.make_async_copy(w1_hbm.at[layer, expert], w1s_ref, wsem.at[0]),
                pltpu.make_async_copy(w2_hbm.at[layer, expert], w2s_ref, wsem.at[1]))

    @pl.when(i == 0)
    def _():
        for cp in weight_copies(e):
            cp.start()

    @pl.when(live & fresh)
    def _():
        for cp in weight_copies(e):
            cp.wait()
        w1b_ref[...] = w1s_ref[...].astype(BF16)
        w2b_ref[...] = w2s_ref[...].astype(BF16)
        nxt = nxt_e_ref[i]

        @pl.when(nxt >= 0)
        def _():
            for cp in weight_copies(nxt):
                cp.start()

    def mlp(rows):
        x = _load_row_tiles(x_ref, rows).astype(BF16)
        a = jnp.dot(x, w1b_ref[...], preferred_element_type=F32) + b1_ref[...]
        glu = jnp.minimum(a[:, :D_FF], SWIGLU_LIMIT)
        lin = jnp.clip(a[:, D_FF:], -SWIGLU_LIMIT, SWIGLU_LIMIT)
        act = glu * jax.nn.sigmoid(SWIGLU_ALPHA * glu) * (lin + 1.0)
        y = jnp.dot(act.astype(BF16), w2b_ref[...], preferred_element_type=F32) + b2_ref[...]
        _store_row_tiles(o_ref, y, rows)
        if rows < MOE_BLOCK:
            o_ref[rows * ROW_TILE_SUB:, :] = jnp.zeros(
                ((MOE_BLOCK - rows) * ROW_TILE_SUB, 128), F32)

    half = MOE_BLOCK // 2
    short = n_valid_ref[i] <= half

    @pl.when(live & jnp.logical_not(short))
    def _():
        mlp(MOE_BLOCK)

    @pl.when(live & short)
    def _():
        mlp(half)

    @pl.when(jnp.logical_not(live))
    def _():
        o_ref[...] = jnp.zeros_like(o_ref)


def _experts(xs, blk_e, n_used, nxt_e, n_valid, layer, w1, b1, w2, b2):
    sub = ROW_TILE_SUB
    n_rows = xs.shape[0] // sub
    d = w2.shape[-1]
    nblk = n_rows // MOE_BLOCK
    depth = w1.shape[0]
    row = lambda i, be, nu, nx, nv: (jnp.minimum(i, nu[0] - 1), 0)
    out_row = lambda i, be, nu, nx, nv: (i, 0)
    exp3 = lambda i, be, nu, nx, nv: (layer, be[jnp.minimum(i, nu[0] - 1)], 0, 0)
    return pl.pallas_call(
        functools.partial(_expert_kernel, layer=layer),
        out_shape=jax.ShapeDtypeStruct((n_rows * sub, 128), F32),
        grid_spec=pltpu.PrefetchScalarGridSpec(
            num_scalar_prefetch=4,
            grid=(nblk,),
            in_specs=[
                pl.BlockSpec((MOE_BLOCK * sub, 128), row),
                pl.BlockSpec(memory_space=pl.ANY),
                pl.BlockSpec((None, None, 1, 2 * D_FF), exp3),
                pl.BlockSpec(memory_space=pl.ANY),
                pl.BlockSpec((None, None, 1, d), exp3),
            ],
            out_specs=pl.BlockSpec((MOE_BLOCK * sub, 128), out_row),
            scratch_shapes=[
                pltpu.VMEM((d, 2 * D_FF), F32),
                pltpu.VMEM((D_FF, d), F32),
                pltpu.VMEM((d, 2 * D_FF), BF16),
                pltpu.VMEM((D_FF, d), BF16),
                pltpu.SemaphoreType.DMA((2,)),
            ],
        ),
        compiler_params=pltpu.CompilerParams(
            dimension_semantics=("arbitrary",), vmem_limit_bytes=VMEM_LIMIT_BYTES),
        name="moe_experts",
    )(blk_e, n_used, nxt_e, n_valid, xs, w1, b1.reshape(depth, N_EXPERTS, 1, 2 * D_FF), w2,
      b2.reshape(depth, N_EXPERTS, 1, d))


def _combine_kernel(dcur_ref, dnext_ref, x_ref, g_ref, ys_ref, gf_ref, o_ref, buf, sem,
                    *, tc, n_tiles, final_norm):
    i = pl.program_id(0)
    sub = ROW_TILE_SUB

    def issue(dref, slot):
        def body(t, carry):
            for k in range(TOP_K):
                d = dref[0, 0, k * tc + t]
                pltpu.make_async_copy(
                    ys_ref.at[pl.ds(pl.multiple_of(d * sub, sub), sub), :],
                    buf.at[slot, k, pl.ds(pl.multiple_of(t * sub, sub), sub), :],
                    sem.at[slot]).start(priority=k % 2)
            return carry
        lax.fori_loop(0, tc, body, 0, unroll=ISSUE_UNROLL)

    @pl.when(i == 0)
    def _():
        issue(dcur_ref, 0)

    @pl.when(i + 1 < n_tiles)
    def _():
        issue(dnext_ref, (i + 1) % 2)

    slot = i % 2
    for k in range(TOP_K):
        pltpu.make_async_copy(ys_ref.at[pl.ds(0, tc * sub), :], buf.at[slot, k],
                              sem.at[slot]).wait()
    acc = x_ref[...]
    gates = g_ref[...]
    for k in range(TOP_K):
        acc = acc + gates[:, k:k + 1] * _load_row_tiles(buf, tc, lead=(slot, k))
    if final_norm:
        acc = _rms(acc, gf_ref[...])
    o_ref[...] = acc


def _combine(x2d, gates_t, dest, ys, g_final, final_norm):
    t, d = x2d.shape
    tc = min(COMBINE_TILE, t)
    nt = t // tc
    dest_tiles = dest.reshape(TOP_K, nt, tc).transpose(1, 0, 2).reshape(nt, 1, TOP_K * tc)
    kern = functools.partial(_combine_kernel, tc=tc, n_tiles=nt, final_norm=final_norm)
    return pl.pallas_call(
        kern,
        out_shape=jax.ShapeDtypeStruct((t, d), F32),
        grid=(nt,),
        in_specs=[
            pl.BlockSpec((1, 1, TOP_K * tc), lambda i: (i, 0, 0), memory_space=pltpu.SMEM),
            pl.BlockSpec((1, 1, TOP_K * tc), lambda i: (jnp.minimum(i + 1, nt - 1), 0, 0),
                         memory_space=pltpu.SMEM),
            pl.BlockSpec((tc, d), lambda i: (i, 0)),
            pl.BlockSpec((tc, TOP_K), lambda i: (i, 0)),
            pl.BlockSpec(memory_space=pl.ANY),
            pl.BlockSpec((1, d), lambda i: (0, 0)),
        ],
        out_specs=pl.BlockSpec((tc, d), lambda i: (i, 0)),
        scratch_shapes=[
            pltpu.VMEM((2, TOP_K, tc * ROW_TILE_SUB, 128), F32),
            pltpu.SemaphoreType.DMA((2,)),
        ],
        compiler_params=pltpu.CompilerParams(
            dimension_semantics=("arbitrary",), vmem_limit_bytes=VMEM_LIMIT_BYTES),
        name="moe_combine_final" if final_norm else "moe_combine",
    )(dest_tiles, dest_tiles, x2d, gates_t.T, ys, g_final.reshape(1, d))


def _moe_layer(x2d, g_ffn, w_router, b_router, layer, w1, b1, w2, b2, g_final, final_norm):
    t, d = x2d.shape
    h, ids, gates, rank, sizes = _route(x2d, g_ffn, w_router, b_router)
    sizes = sizes[:, 0].astype(I32)
    padded = ((sizes + MOE_BLOCK - 1) // MOE_BLOCK) * MOE_BLOCK
    pend = jnp.cumsum(padded)
    pstart = pend - padded
    onehot = ids[:, :, None] == jnp.arange(N_EXPERTS, dtype=I32)
    dest = rank + jnp.sum(jnp.where(onehot, pstart, 0), axis=-1)
    n_rows = t * TOP_K + N_EXPERTS * MOE_BLOCK
    nblk = n_rows // MOE_BLOCK
    blk_row = jnp.arange(nblk, dtype=I32) * MOE_BLOCK
    blk_e = jnp.minimum(jnp.sum((pend[None, :] <= blk_row[:, None]).astype(I32), axis=1),
                        N_EXPERTS - 1)
    n_used = (pend[-1:] // MOE_BLOCK).astype(I32)
    zrow = jnp.where(padded > 0, pend - MOE_BLOCK, -1).astype(I32)
    xs = _dispatch(h, dest, zrow, n_used, n_rows)
    blk_id = jnp.arange(nblk, dtype=I32)
    later = ((blk_id[None, :] > blk_id[:, None]) & (blk_e[None, :] != blk_e[:, None])
             & (blk_id[None, :] < n_used[0]))
    nxt_blk = jnp.min(jnp.where(later, blk_id[None, :], nblk), axis=1)
    nxt_e = jnp.sum(jnp.where(blk_id[None, :] == nxt_blk[:, None], blk_e[None, :] + 1, 0),
                    axis=1).astype(I32) - 1
    own = blk_e[:, None] == jnp.arange(N_EXPERTS, dtype=I32)[None, :]
    row_end = jnp.sum(jnp.where(own, (pstart + sizes)[None, :], 0), axis=1)
    n_valid = jnp.clip(row_end - blk_row, 0, MOE_BLOCK).astype(I32)
    ys = _experts(xs, blk_e, n_used, nxt_e, n_valid, layer, w1, b1, w2, b2)
    return _combine(x2d, gates, dest, ys, g_final, final_norm)


def _qkv_kernel(x_ref, g_ref, w_ref, b_ref, q_ref, k_ref, v_ref):
    h = _rms(x_ref[...], g_ref[...])
    qkv = jnp.dot(h.astype(BF16), w_ref[...], preferred_element_type=F32) + b_ref[...]
    nq = N_HEADS * HEAD_DIM
    nk = 2 * N_KV_HEADS * HEAD_DIM
    q_ref[...] = (qkv[:, :nq] * (HEAD_DIM ** -0.5)).astype(BF16)
    k_ref[...] = qkv[:, nq:nq + nk].astype(BF16)
    v_ref[...] = qkv[:, nq + nk:].astype(BF16)


def _qkv(x2d, g, w_qkv, b_qkv):
    t, d = x2d.shape
    ts = min(ROW_TILE, t)
    nq = N_HEADS * HEAD_DIM
    kv = N_KV_HEADS * HEAD_DIM

    def dup_heads(m):
        parts = []
        for hd in range(N_KV_HEADS):
            sl = m[..., hd * HEAD_DIM:(hd + 1) * HEAD_DIM]
            parts += [sl, sl]
        return jnp.concatenate(parts, axis=-1)

    w = jnp.concatenate([w_qkv[:, :nq], dup_heads(w_qkv[:, nq:nq + kv]),
                         dup_heads(w_qkv[:, nq + kv:])], axis=1).astype(BF16)
    b = jnp.concatenate([b_qkv[:nq], dup_heads(b_qkv[nq:nq + kv]),
                         dup_heads(b_qkv[nq + kv:])]).reshape(1, -1)
    n_out = w.shape[1]
    return pl.pallas_call(
        _qkv_kernel,
        out_shape=(
            jax.ShapeDtypeStruct((t, nq), BF16),
            jax.ShapeDtypeStruct((t, 2 * kv), BF16),
            jax.ShapeDtypeStruct((t, 2 * kv), BF16),
        ),
        grid=(t // ts,),
        in_specs=[
            pl.BlockSpec((ts, d), lambda i: (i, 0)),
            pl.BlockSpec((1, d), lambda i: (0, 0)),
            pl.BlockSpec((d, n_out), lambda i: (0, 0)),
            pl.BlockSpec((1, n_out), lambda i: (0, 0)),
        ],
        out_specs=(
            pl.BlockSpec((ts, nq), lambda i: (i, 0)),
            pl.BlockSpec((ts, 2 * kv), lambda i: (i, 0)),
            pl.BlockSpec((ts, 2 * kv), lambda i: (i, 0)),
        ),
        compiler_params=pltpu.CompilerParams(
            dimension_semantics=("arbitrary",), vmem_limit_bytes=VMEM_LIMIT_BYTES),
        name="attn_qkv",
    )(x2d, g.reshape(1, d), w, b)


def _attn_kernel(sinks_ref, q_ref, kp_ref, kc_ref, vp_ref, vc_ref, bias_ref, o_ref):
    n = pl.program_id(1)
    rows = SLABS_PER_GROUP * ATT_BLOCK
    keys = 2 * ATT_BLOCK
    kall = jnp.concatenate([kp_ref[...], kc_ref[...]], axis=0)
    vall = jnp.concatenate([vp_ref[...], vc_ref[...]], axis=0)
    j = lax.broadcasted_iota(I32, (keys, rows), 0)
    r = lax.broadcasted_iota(I32, (keys, rows), 1) & (ATT_BLOCK - 1)
    in_window = (j > r) & (j <= r + WINDOW)
    lane_k = lax.broadcasted_iota(I32, (keys, 128), 1)
    halves = [lane_k < HEAD_DIM, lane_k >= HEAD_DIM]
    sub_o = lax.broadcasted_iota(I32, (128, rows), 0)
    zero = jnp.zeros((), BF16)
    for blk in range(ATT_STEP_BLOCKS):
        q = q_ref[blk * ATT_BLOCK:(blk + 1) * ATT_BLOCK, :]
        kcat = kall[blk * ATT_BLOCK:blk * ATT_BLOCK + keys]
        vcat = vall[blk * ATT_BLOCK:blk * ATT_BLOCK + keys]
        if blk == 0:
            mask = in_window & ((n > 0) | (j >= ATT_BLOCK))
        else:
            mask = in_window
        for g in range(N_KV_HEADS):
            s0 = g * SLABS_PER_GROUP
            qg = jnp.concatenate([q[:, (s0 + s) * 128:(s0 + s + 1) * 128]
                                  for s in range(SLABS_PER_GROUP)], axis=0)
            kg = kcat[:, g * 128:(g + 1) * 128]
            vg = vcat[:, g * 128:(g + 1) * 128]
            pts, invs = [], []
            for p in range(HEADS_PER_SLAB):
                kp = jnp.where(halves[p], kg, zero)
                sc = lax.dot_general(kp, qg, (((1,), (1,)), ((), ())),
                                     preferred_element_type=F32)
                sc = jnp.where(mask, sc + bias_ref[g, p], NEG_INF)
                sink = jnp.concatenate(
                    [jnp.full((1, ATT_BLOCK),
                              sinks_ref[g * GQA_GROUP + s * HEADS_PER_SLAB + p], F32)
                     for s in range(SLABS_PER_GROUP)], axis=1)
                m = jnp.maximum(jnp.max(sc, axis=0, keepdims=True), sink)
                pe = jnp.exp(sc - m)
                den = jnp.sum(pe, axis=0, keepdims=True) + jnp.exp(sink - m)
                pts.append(pe.astype(BF16))
                invs.append(1.0 / den)
            pcat = jnp.concatenate(pts, axis=0)
            vblk = jnp.concatenate([jnp.where(halves[p], vg, zero)
                                    for p in range(HEADS_PER_SLAB)], axis=0)
            og = lax.dot_general(vblk, pcat, (((0,), (0,)), ((), ())),
                                 preferred_element_type=F32)
            og = og * jnp.where(sub_o < HEAD_DIM, invs[0], invs[1])
            og = og.T
            for s in range(SLABS_PER_GROUP):
                o_ref[blk * ATT_BLOCK:(blk + 1) * ATT_BLOCK, (s0 + s) * 128:(s0 + s + 1) * 128] = (
                    og[s * ATT_BLOCK:(s + 1) * ATT_BLOCK].astype(BF16))


def _t5_bias_table(rel_bias):
    r = np.arange(ATT_BLOCK)[:, None]
    j = np.arange(2 * ATT_BLOCK)[None, :]
    rel = np.clip(ATT_BLOCK + r - j, 0, WINDOW - 1)
    max_exact = N_BUCKETS // 2
    nf = jnp.maximum(rel, max_exact).astype(F32)
    large = max_exact + (jnp.log(nf / max_exact) / math.log(MAX_DISTANCE / max_exact)
                         * (N_BUCKETS - max_exact)).astype(I32)
    large = jnp.minimum(large, N_BUCKETS - 1)
    bucket = jnp.where(rel < max_exact, rel, large)
    pick = (bucket[None] == jnp.arange(N_BUCKETS, dtype=I32)[:, None, None])
    bias = jnp.sum(jnp.where(pick[:, None], rel_bias.astype(F32)[:, :, None, None], 0.0),
                   axis=0)
    bias = bias.reshape(N_KV_HEADS, SLABS_PER_GROUP, HEADS_PER_SLAB, ATT_BLOCK, 2 * ATT_BLOCK)
    bias = jnp.transpose(bias, (0, 2, 4, 1, 3))
    return bias.reshape(N_KV_HEADS, HEADS_PER_SLAB, 2 * ATT_BLOCK, SLABS_PER_GROUP * ATT_BLOCK)


def _attention(q, kk, vv, sinks, rel_bias, bsz, seq):
    nq = N_HEADS * HEAD_DIM
    kvw = kk.shape[-1]
    step = ATT_STEP_BLOCKS * ATT_BLOCK
    q3 = q.reshape(bsz, seq, nq)
    k3 = kk.reshape(bsz, seq, kvw)
    v3 = vv.reshape(bsz, seq, kvw)
    bias = _t5_bias_table(rel_bias)
    prev = lambda bi, n: (bi, jnp.maximum(n * ATT_STEP_BLOCKS - 1, 0), 0)
    cur = lambda bi, n: (bi, n, 0)
    out = pl.pallas_call(
        _attn_kernel,
        out_shape=jax.ShapeDtypeStruct((bsz, seq, nq), BF16),
        grid=(bsz, seq // step),
        in_specs=[
            pl.BlockSpec(memory_space=pltpu.SMEM),
            pl.BlockSpec((None, step, nq), cur),
            pl.BlockSpec((None, ATT_BLOCK, kvw), prev),
            pl.BlockSpec((None, step, kvw), cur),
            pl.BlockSpec((None, ATT_BLOCK, kvw), prev),
            pl.BlockSpec((None, step, kvw), cur),
            pl.BlockSpec(bias.shape, lambda bi, n: (0, 0, 0, 0)),
        ],
        out_specs=pl.BlockSpec((None, step, nq), cur),
        compiler_params=pltpu.CompilerParams(
            dimension_semantics=("arbitrary", "arbitrary"),
            vmem_limit_bytes=VMEM_LIMIT_BYTES),
        name="attn_core",
    )(sinks.astype(F32), q3, k3, k3, v3, v3, bias)
    return out.reshape(bsz * seq, nq)


def _oproj_kernel(x_ref, o_ref, w_ref, b_ref, y_ref):
    y_ref[...] = x_ref[...] + jnp.dot(o_ref[...], w_ref[...],
                                      preferred_element_type=F32) + b_ref[...]


def _out_proj(x2d, o, w_o, b_o):
    t, d = x2d.shape
    ts = min(ROW_TILE, t)
    nq = o.shape[1]
    return pl.pallas_call(
        _oproj_kernel,
        out_shape=jax.ShapeDtypeStruct((t, d), F32),
        grid=(t // ts,),
        in_specs=[
            pl.BlockSpec((ts, d), lambda i: (i, 0)),
            pl.BlockSpec((ts, nq), lambda i: (i, 0)),
            pl.BlockSpec((nq, d), lambda i: (0, 0)),
            pl.BlockSpec((1, d), lambda i: (0, 0)),
        ],
        out_specs=pl.BlockSpec((ts, d), lambda i: (i, 0)),
        compiler_params=pltpu.CompilerParams(
            dimension_semantics=("arbitrary",), vmem_limit_bytes=VMEM_LIMIT_BYTES),
        name="attn_out_proj",
    )(x2d, o, w_o.astype(BF16), b_o.reshape(1, d))


def kernel(x, norm_mix, norm_ffn, norm_final, pool_w, pool_b, pool_scale, w_qkv, b_qkv,
           sinks, w_o, b_o, rel_bias, w_router, b_router, w1, b1, w2, b2):
    bsz, seq, d = x.shape
    t = bsz * seq
    x = _pool_layer(x, norm_mix[0], pool_w[0], pool_b[0], pool_scale[0]).reshape(t, d)
    x = _moe_layer(x, norm_ffn[0], w_router[0], b_router[0], 0, w1, b1, w2, b2,
                   norm_final, final_norm=False)
    q, kk, vv = _qkv(x, norm_mix[1], w_qkv[0], b_qkv[0])
    o = _attention(q, kk, vv, sinks[0], rel_bias, bsz, seq)
    x = _out_proj(x, o, w_o[0], b_o[0])
    x = _moe_layer(x, norm_ffn[1], w_router[1], b_router[1], 1, w1, b1, w2, b2,
                   norm_final, final_norm=True)
    return x.reshape(bsz, seq, d)
```

```python
import functools
import math

import jax
import jax.numpy as jnp
import numpy as np
from jax import lax
from jax.experimental import pallas as pl
from jax.experimental.pallas import tpu as pltpu

F32 = jnp.float32
BF16 = jnp.bfloat16
I32 = jnp.int32

D_MODEL = 1024
POOL_WINDOWS = (2, 4, 8, 16)
POOL_GROUP_CH = D_MODEL // len(POOL_WINDOWS)
POOL_HALO = 16
HEAD_DIM = 64
N_HEADS = 16
N_KV_HEADS = 2
GQA_GROUP = N_HEADS // N_KV_HEADS
HEADS_PER_SLAB = 128 // HEAD_DIM
SLABS_PER_GROUP = GQA_GROUP // HEADS_PER_SLAB
ATT_BLOCK = 128
ATT_STEP_BLOCKS = 2
WINDOW = 128
N_BUCKETS = 32
MAX_DISTANCE = 128
N_EXPERTS = 32
TOP_K = 4
D_FF = D_MODEL
SWIGLU_ALPHA = 1.702
SWIGLU_LIMIT = 7.0
RMS_EPS = 1e-5
NEG_INF = -1e30

ROW_TILE = 512
MOE_BLOCK = 512
DISPATCH_TILE = 2048
COMBINE_TILE = 256
ISSUE_UNROLL = 4
VMEM_LIMIT_BYTES = 56 * 1024 * 1024


def _rms(x, g):
    return x * lax.rsqrt(jnp.mean(x * x, axis=-1, keepdims=True) + RMS_EPS) * g


ROW_TILE_SUB = D_MODEL // 128


def _load_row_tiles(ref, n, lead=()):
    return jnp.concatenate(
        [ref[lead + (pl.ds(j, n, stride=ROW_TILE_SUB), slice(None))]
         for j in range(ROW_TILE_SUB)], axis=1)


def _store_row_tiles(ref, val, n):
    for j in range(ROW_TILE_SUB):
        ref[pl.ds(j, n, stride=ROW_TILE_SUB), :] = val[:, j * 128:(j + 1) * 128]


POOL_PAD = 8


def _pool_kernel(x_ref, xh_ref, g_ref, w_ref, b_ref, s_ref, gr_ref, wr_ref, br_ref,
                 o_ref, h_ref, ids_ref, gates_ref, rank_ref, sizes_ref,
                 buf_ref, s1_ref, s2_ref, s3_ref, carry_ref, *, ts):
    i = pl.program_id(1)
    c = POOL_GROUP_CH
    h0 = POOL_PAD
    t0 = POOL_PAD + POOL_HALO
    r1 = t0 + ts
    x = x_ref[...]
    g = g_ref[...]
    h = _rms(x, g)
    hh = _rms(xh_ref[...], g)
    hh = jnp.where(i == 0, 0.0, hh)
    buf_ref[0:h0, :] = jnp.zeros((h0, D_MODEL), F32)
    buf_ref[h0:t0, :] = hh
    buf_ref[t0:r1, :] = h
    sum0 = h[:, 0:c] + buf_ref[t0 - 1:r1 - 1, 0:c]
    l1 = buf_ref[h0:r1, c:] + buf_ref[h0 - 1:r1 - 1, c:]
    s1_ref[0:h0, :] = jnp.zeros((h0, 3 * c), F32)
    s1_ref[h0:r1, :] = l1
    sum1 = s1_ref[t0:r1, 0:c] + s1_ref[t0 - 2:r1 - 2, 0:c]
    l2 = s1_ref[h0:r1, c:] + s1_ref[h0 - 2:r1 - 2, c:]
    s2_ref[0:h0, :] = jnp.zeros((h0, 2 * c), F32)
    s2_ref[h0:r1, :] = l2
    sum2 = s2_ref[t0:r1, 0:c] + s2_ref[t0 - 4:r1 - 4, 0:c]
    s3_ref[h0:r1, :] = s2_ref[h0:r1, c:] + s2_ref[h0 - 4:r1 - 4, c:]
    sum3 = s3_ref[t0:r1, :] + s3_ref[t0 - 8:r1 - 8, :]
    t = i * ts + lax.broadcasted_iota(I32, (ts, 1), 0)
    outs = []
    for gi, (win, acc) in enumerate(zip(POOL_WINDOWS, (sum0, sum1, sum2, sum3))):
        cnt = jnp.minimum(t + 1, win).astype(F32)
        dlt = acc / cnt - h[:, gi * c:(gi + 1) * c]
        outs.append(jnp.dot(dlt.astype(BF16), w_ref[gi], preferred_element_type=F32))
    y = jnp.concatenate(outs, axis=1)
    x_new = x + (y + b_ref[...]) * s_ref[...]
    o_ref[...] = x_new
    first = (pl.program_id(0) == 0) & (i == 0)
    _route_tile(x_new, first, gr_ref, wr_ref, br_ref,
                h_ref, ids_ref, gates_ref, rank_ref, sizes_ref, carry_ref, ts)


def _pool_layer(x, g, w, b, s, g_ffn, w_router, b_router):
    bsz, seq, d = x.shape
    ts = min(ROW_TILE, seq)
    tiles = seq // ts
    kern = functools.partial(_pool_kernel, ts=ts)
    rows = POOL_PAD + POOL_HALO + ts
    vec = lambda: pl.BlockSpec((1, d), lambda bi, i: (0, 0))
    r_in, r_shape, r_out = _route_specs(bsz * seq, ts, d, lambda bi, i: bi * tiles + i)
    outs = pl.pallas_call(
        kern,
        out_shape=(jax.ShapeDtypeStruct(x.shape, F32),) + r_shape,
        grid=(bsz, tiles),
        in_specs=[
            pl.BlockSpec((None, ts, d), lambda bi, i: (bi, i, 0)),
            pl.BlockSpec((None, POOL_HALO, d),
                         lambda bi, i: (bi, jnp.maximum(i * (ts // POOL_HALO) - 1, 0), 0)),
            vec(),
            pl.BlockSpec(w.shape, lambda bi, i: (0, 0, 0)),
            vec(),
            vec(),
        ] + r_in,
        out_specs=(pl.BlockSpec((None, ts, d), lambda bi, i: (bi, i, 0)),) + r_out,
        scratch_shapes=[pltpu.VMEM((rows, d), F32),
                        pltpu.VMEM((rows, 3 * POOL_GROUP_CH), F32),
                        pltpu.VMEM((rows, 2 * POOL_GROUP_CH), F32),
                        pltpu.VMEM((rows, POOL_GROUP_CH), F32),
                        pltpu.VMEM((N_EXPERTS, 128), F32)],
        compiler_params=pltpu.CompilerParams(
            dimension_semantics=("arbitrary", "arbitrary"),
            vmem_limit_bytes=VMEM_LIMIT_BYTES),
        name="pool_route",
    )(x, x, g.reshape(1, d), w.astype(BF16), b.reshape(1, d), s.reshape(1, d),
      *_route_operands(g_ffn, w_router, b_router))
    return outs[0], outs[1:]


def _route_tile(x, first, g_ref, wr_ref, br_ref,
                h_ref, ids_ref, gates_ref, rank_ref, sizes_ref, carry_ref, ts):
    @pl.when(first)
    def _():
        carry_ref[...] = jnp.zeros_like(carry_ref)

    h = _rms(x, g_ref[...])
    _store_row_tiles(h_ref, h, ts)
    h_hi = h.astype(BF16)
    h_lo = (h - h_hi.astype(F32)).astype(BF16)
    w = wr_ref[...]
    w_hi = w.astype(BF16)
    w_lo = (w - w_hi.astype(F32)).astype(BF16)
    dn = (((1,), (1,)), ((), ()))
    lg = (lax.dot_general(w_hi, h_hi, dn, preferred_element_type=F32)
          + lax.dot_general(w_lo, h_hi, dn, preferred_element_type=F32)
          + lax.dot_general(w_hi, h_lo, dn, preferred_element_type=F32))
    lg = lg + br_ref[:, 0:1]

    iota_e = lax.broadcasted_iota(I32, (N_EXPERTS, ts), 0)
    vals, ids = [], []
    for _ in range(TOP_K):
        m = jnp.max(lg, axis=0, keepdims=True)
        idx = jnp.min(jnp.where(lg == m, iota_e, N_EXPERTS), axis=0, keepdims=True)
        vals.append(m)
        ids.append(idx)
        lg = jnp.where(iota_e == idx, -jnp.inf, lg)
    ex = [jnp.exp(v - vals[0]) for v in vals]
    den = ex[0] + ex[1] + ex[2] + ex[3]
    gates_ref[...] = jnp.concatenate([e / den for e in ex], axis=0)
    ids_ref[...] = jnp.concatenate(ids, axis=0)

    hot = [(iota_e == idx) for idx in ids]
    multi = (hot[0] | hot[1] | hot[2] | hot[3])
    multi_f = multi.astype(F32)
    r = lax.broadcasted_iota(I32, (ts, ts), 0)
    c = lax.broadcasted_iota(I32, (ts, ts), 1)
    upper = (r < c).astype(BF16)
    before = jnp.dot(multi_f.astype(BF16), upper, preferred_element_type=F32)
    before = before + carry_ref[:, 0:1]
    ranks = [jnp.sum(jnp.where(hk, before, 0.0), axis=0, keepdims=True) for hk in hot]
    rank_ref[...] = jnp.concatenate(ranks, axis=0).astype(I32)
    carry_ref[...] = carry_ref[...] + jnp.sum(multi_f, axis=1, keepdims=True)
    sizes_ref[...] = carry_ref[...]


def _route_operands(g, w_router, b_router):
    d = g.shape[0]
    return (g.reshape(1, d), w_router.T,
            jnp.broadcast_to(b_router.reshape(N_EXPERTS, 1), (N_EXPERTS, 128)))


def _route_specs(t, ts, d, tile_of):
    const = lambda *idx: (0, 0)
    in_specs = [pl.BlockSpec((1, d), const), pl.BlockSpec((N_EXPERTS, d), const),
                pl.BlockSpec((N_EXPERTS, 128), const)]
    out_shape = (
        jax.ShapeDtypeStruct((t * ROW_TILE_SUB, 128), F32),
        jax.ShapeDtypeStruct((TOP_K, t), I32),
        jax.ShapeDtypeStruct((TOP_K, t), F32),
        jax.ShapeDtypeStruct((TOP_K, t), I32),
        jax.ShapeDtypeStruct((N_EXPERTS, 128), F32),
    )
    tok = lambda: pl.BlockSpec((TOP_K, ts), lambda *idx: (0, tile_of(*idx)))
    out_specs = (
        pl.BlockSpec((ts * ROW_TILE_SUB, 128), lambda *idx: (tile_of(*idx), 0)),
        tok(), tok(), tok(),
        pl.BlockSpec((N_EXPERTS, 128), const),
    )
    return in_specs, out_shape, out_specs


def _dispatch_kernel(zrow_ref, n_used_ref, dest_ref, h_ref, xs_ref, zbuf, sem_z, sem,
                     *, td, blk, nblk):
    i = pl.program_id(0)
    sub = ROW_TILE_SUB

    def zero_block(row):
        row = pl.multiple_of(row * sub, blk * sub)
        return pltpu.make_async_copy(zbuf, xs_ref.at[pl.ds(row, blk * sub), :], sem_z)

    @pl.when(i == 0)
    def _():
        zbuf[...] = jnp.zeros_like(zbuf)
        for e in range(N_EXPERTS):
            @pl.when(zrow_ref[e] >= 0)
            def _():
                zero_block(zrow_ref[e]).start()

        def start_tail(b, carry):
            zero_block(b * blk).start()
            return carry

        def wait_tail(b, carry):
            zero_block(b * blk).wait()
            return carry

        lax.fori_loop(n_used_ref[0], nblk, start_tail, 0)
        for e in range(N_EXPERTS):
            @pl.when(zrow_ref[e] >= 0)
            def _():
                zero_block(zrow_ref[e]).wait()
        lax.fori_loop(n_used_ref[0], nblk, wait_tail, 0)

    def issue(t, carry):
        src = h_ref.at[pl.ds(pl.multiple_of(t * sub, sub), sub), :]
        for k in range(TOP_K):
            d = dest_ref[0, 0, k * td + t]
            pltpu.make_async_copy(src, xs_ref.at[pl.ds(pl.multiple_of(d * sub, sub), sub), :],
                                  sem).start(priority=k % 2)
        return carry

    lax.fori_loop(0, td, issue, 0, unroll=ISSUE_UNROLL)
    for k in range(TOP_K):
        pltpu.make_async_copy(h_ref, xs_ref.at[pl.ds(0, td * sub), :], sem).wait()


def _dispatch(h_tiles, dest, zrow, n_used, n_rows):
    sub = ROW_TILE_SUB
    t = h_tiles.shape[0] // sub
    td = min(DISPATCH_TILE, t)
    nt = t // td
    dest_tiles = dest.reshape(TOP_K, nt, td).transpose(1, 0, 2).reshape(nt, 1, TOP_K * td)
    kern = functools.partial(_dispatch_kernel, td=td, blk=MOE_BLOCK, nblk=n_rows // MOE_BLOCK)
    return pl.pallas_call(
        kern,
        out_shape=jax.ShapeDtypeStruct((n_rows * sub, 128), F32),
        grid_spec=pltpu.PrefetchScalarGridSpec(
            num_scalar_prefetch=2,
            grid=(nt,),
            in_specs=[
                pl.BlockSpec((1, 1, TOP_K * td), lambda i, z, nu: (i, 0, 0),
                             memory_space=pltpu.SMEM),
                pl.BlockSpec((td * sub, 128), lambda i, z, nu: (i, 0)),
            ],
            out_specs=pl.BlockSpec(memory_space=pl.ANY),
            scratch_shapes=[
                pltpu.VMEM((MOE_BLOCK * sub, 128), F32),
                pltpu.SemaphoreType.DMA(()),
                pltpu.SemaphoreType.DMA(()),
            ],
        ),
        compiler_params=pltpu.CompilerParams(
            dimension_semantics=("arbitrary",), vmem_limit_bytes=VMEM_LIMIT_BYTES),
        name="moe_dispatch",
    )(zrow, n_used, dest_tiles, h_tiles)


def _expert_kernel(blk_e_ref, n_used_ref, nxt_e_ref, n_valid_ref, x_ref, w1_hbm, b1_ref, w2_hbm,
                   b2_ref, o_ref, w1s_ref, w2s_ref, w1b_ref, w2b_ref, wsem, *, layer):
    i = pl.program_id(0)
    e = blk_e_ref[i]
    prev = blk_e_ref[jnp.maximum(i - 1, 0)]
    fresh = (i == 0) | (e != prev)
    live = i < n_used_ref[0]

    def weight_copies(expert):
        return (pltpu.make_async_copy(w1_hbm.at[layer, expert], w1s_ref, wsem.at[0]),
                pltpu.make_async_copy(w2_hbm.at[layer, expert], w2s_ref, wsem.at[1]))

    @pl.when(i == 0)
    def _():
        for cp in weight_copies(e):
            cp.start()

    @pl.when(live & fresh)
    def _():
        for cp in weight_copies(e):
            cp.wait()
        w1b_ref[...] = w1s_ref[...].astype(BF16)
        w2b_ref[...] = w2s_ref[...].astype(BF16)
        nxt = nxt_e_ref[i]

        @pl.when(nxt >= 0)
        def _():
            for cp in weight_copies(nxt):
                cp.start()

    def mlp(rows):
        x = _load_row_tiles(x_ref, rows).astype(BF16)
        a = jnp.dot(x, w1b_ref[...], preferred_element_type=F32) + b1_ref[...]
        glu = jnp.minimum(a[:, :D_FF], SWIGLU_LIMIT)
        lin = jnp.clip(a[:, D_FF:], -SWIGLU_LIMIT, SWIGLU_LIMIT)
        act = glu * jax.nn.sigmoid(SWIGLU_ALPHA * glu) * (lin + 1.0)
        y = jnp.dot(act.astype(BF16), w2b_ref[...], preferred_element_type=F32) + b2_ref[...]
        _store_row_tiles(o_ref, y, rows)
        if rows < MOE_BLOCK:
            o_ref[rows * ROW_TILE_SUB:, :] = jnp.zeros(
                ((MOE_BLOCK - rows) * ROW_TILE_SUB, 128), F32)

    half = MOE_BLOCK // 2
    short = n_valid_ref[i] <= half

    @pl.when(live & jnp.logical_not(short))
    def _():
        mlp(MOE_BLOCK)

    @pl.when(live & short)
    def _():
        mlp(half)

    @pl.when(jnp.logical_not(live))
    def _():
        o_ref[...] = jnp.zeros_like(o_ref)


def _experts(xs, blk_e, n_used, nxt_e, n_valid, layer, w1, b1, w2, b2):
    sub = ROW_TILE_SUB
    n_rows = xs.shape[0] // sub
    d = w2.shape[-1]
    nblk = n_rows // MOE_BLOCK
    depth = w1.shape[0]
    row = lambda i, be, nu, nx, nv: (jnp.minimum(i, nu[0] - 1), 0)
    out_row = lambda i, be, nu, nx, nv: (i, 0)
    exp3 = lambda i, be, nu, nx, nv: (layer, be[jnp.minimum(i, nu[0] - 1)], 0, 0)
    return pl.pallas_call(
        functools.partial(_expert_kernel, layer=layer),
        out_shape=jax.ShapeDtypeStruct((n_rows * sub, 128), F32),
        grid_spec=pltpu.PrefetchScalarGridSpec(
            num_scalar_prefetch=4,
            grid=(nblk,),
            in_specs=[
                pl.BlockSpec((MOE_BLOCK * sub, 128), row),
                pl.BlockSpec(memory_space=pl.ANY),
                pl.BlockSpec((None, None, 1, 2 * D_FF), exp3),
                pl.BlockSpec(memory_space=pl.ANY),
                pl.BlockSpec((None, None, 1, d), exp3),
            ],
            out_specs=pl.BlockSpec((MOE_BLOCK * sub, 128), out_row),
            scratch_shapes=[
                pltpu.VMEM((d, 2 * D_FF), F32),
                pltpu.VMEM((D_FF, d), F32),
                pltpu.VMEM((d, 2 * D_FF), BF16),
                pltpu.VMEM((D_FF, d), BF16),
                pltpu.SemaphoreType.DMA((2,)),
            ],
        ),
        compiler_params=pltpu.CompilerParams(
            dimension_semantics=("arbitrary",), vmem_limit_bytes=VMEM_LIMIT_BYTES),
        name="moe_experts",
    )(blk_e, n_used, nxt_e, n_valid, xs, w1, b1.reshape(depth, N_EXPERTS, 1, 2 * D_FF), w2,
      b2.reshape(depth, N_EXPERTS, 1, d))


def _combine_kernel(dcur_ref, dnext_ref, x_ref, g_ref, ys_ref, gn_ref, *rest,
                    tc, n_tiles, final_norm):
    if final_norm:
        o_ref, buf, sem = rest
    else:
        wq_ref, bq_ref, o_ref, q_ref, k_ref, v_ref, buf, sem = rest
    i = pl.program_id(0)
    sub = ROW_TILE_SUB

    def issue(dref, slot):
        def body(t, carry):
            for k in range(TOP_K):
                d = dref[0, 0, k * tc + t]
                pltpu.make_async_copy(
                    ys_ref.at[pl.ds(pl.multiple_of(d * sub, sub), sub), :],
                    buf.at[slot, k, pl.ds(pl.multiple_of(t * sub, sub), sub), :],
                    sem.at[slot]).start(priority=k % 2)
            return carry
        lax.fori_loop(0, tc, body, 0, unroll=ISSUE_UNROLL)

    @pl.when(i == 0)
    def _():
        issue(dcur_ref, 0)

    @pl.when(i + 1 < n_tiles)
    def _():
        issue(dnext_ref, (i + 1) % 2)

    slot = i % 2
    for k in range(TOP_K):
        pltpu.make_async_copy(ys_ref.at[pl.ds(0, tc * sub), :], buf.at[slot, k],
                              sem.at[slot]).wait()
    acc = x_ref[...]
    gates = g_ref[...]
    for k in range(TOP_K):
        acc = acc + gates[:, k:k + 1] * _load_row_tiles(buf, tc, lead=(slot, k))
    if final_norm:
        o_ref[...] = _rms(acc, gn_ref[...])
    else:
        o_ref[...] = acc
        _qkv_project(_rms(acc, gn_ref[...]), wq_ref, bq_ref, q_ref, k_ref, v_ref)


def _combine(x2d, gates_t, dest, ys, g_norm, qkv_params=None):
    t, d = x2d.shape
    tc = min(COMBINE_TILE, t)
    nt = t // tc
    final_norm = qkv_params is None
    dest_tiles = dest.reshape(TOP_K, nt, tc).transpose(1, 0, 2).reshape(nt, 1, TOP_K * tc)
    kern = functools.partial(_combine_kernel, tc=tc, n_tiles=nt, final_norm=final_norm)
    row = lambda width: pl.BlockSpec((tc, width), lambda i: (i, 0))
    in_specs = [
        pl.BlockSpec((1, 1, TOP_K * tc), lambda i: (i, 0, 0), memory_space=pltpu.SMEM),
        pl.BlockSpec((1, 1, TOP_K * tc), lambda i: (jnp.minimum(i + 1, nt - 1), 0, 0),
                     memory_space=pltpu.SMEM),
        row(d),
        row(TOP_K),
        pl.BlockSpec(memory_space=pl.ANY),
        pl.BlockSpec((1, d), lambda i: (0, 0)),
    ]
    operands = [dest_tiles, dest_tiles, x2d, gates_t.T, ys, g_norm.reshape(1, d)]
    out_shape = jax.ShapeDtypeStruct((t, d), F32)
    out_specs = row(d)
    if not final_norm:
        w, b = _qkv_operands(*qkv_params)
        n_out = w.shape[1]
        nq = N_HEADS * HEAD_DIM
        nkv = (n_out - nq) // 2
        in_specs += [pl.BlockSpec((d, n_out), lambda i: (0, 0)),
                     pl.BlockSpec((1, n_out), lambda i: (0, 0))]
        operands += [w, b]
        out_shape = (out_shape, jax.ShapeDtypeStruct((t, nq), BF16),
                     jax.ShapeDtypeStruct((t, nkv), BF16), jax.ShapeDtypeStruct((t, nkv), BF16))
        out_specs = (out_specs, row(nq), row(nkv), row(nkv))
    return pl.pallas_call(
        kern,
        out_shape=out_shape,
        grid=(nt,),
        in_specs=in_specs,
        out_specs=out_specs,
        scratch_shapes=[
            pltpu.VMEM((2, TOP_K, tc * ROW_TILE_SUB, 128), F32),
            pltpu.SemaphoreType.DMA((2,)),
        ],
        compiler_params=pltpu.CompilerParams(
            dimension_semantics=("arbitrary",), vmem_limit_bytes=VMEM_LIMIT_BYTES),
        name="moe_combine_final" if final_norm else "moe_combine_qkv",
    )(*operands)


def _moe_experts(routed, layer, w1, b1, w2, b2):
    h, ids, gates, rank, sizes = routed
    t = ids.shape[1]
    sizes = sizes[:, 0].astype(I32)
    padded = ((sizes + MOE_BLOCK - 1) // MOE_BLOCK) * MOE_BLOCK
    pend = jnp.cumsum(padded)
    pstart = pend - padded
    onehot = ids[:, :, None] == jnp.arange(N_EXPERTS, dtype=I32)
    dest = rank + jnp.sum(jnp.where(onehot, pstart, 0), axis=-1)
    n_rows = t * TOP_K + N_EXPERTS * MOE_BLOCK
    nblk = n_rows // MOE_BLOCK
    blk_row = jnp.arange(nblk, dtype=I32) * MOE_BLOCK
    blk_e = jnp.minimum(jnp.sum((pend[None, :] <= blk_row[:, None]).astype(I32), axis=1),
                        N_EXPERTS - 1)
    n_used = (pend[-1:] // MOE_BLOCK).astype(I32)
    zrow = jnp.where(padded > 0, pend - MOE_BLOCK, -1).astype(I32)
    xs = _dispatch(h, dest, zrow, n_used, n_rows)
    blk_id = jnp.arange(nblk, dtype=I32)
    later = ((blk_id[None, :] > blk_id[:, None]) & (blk_e[None, :] != blk_e[:, None])
             & (blk_id[None, :] < n_used[0]))
    nxt_blk = jnp.min(jnp.where(later, blk_id[None, :], nblk), axis=1)
    nxt_e = jnp.sum(jnp.where(blk_id[None, :] == nxt_blk[:, None], blk_e[None, :] + 1, 0),
                    axis=1).astype(I32) - 1
    own = blk_e[:, None] == jnp.arange(N_EXPERTS, dtype=I32)[None, :]
    row_end = jnp.sum(jnp.where(own, (pstart + sizes)[None, :], 0), axis=1)
    n_valid = jnp.clip(row_end - blk_row, 0, MOE_BLOCK).astype(I32)
    ys = _experts(xs, blk_e, n_used, nxt_e, n_valid, layer, w1, b1, w2, b2)
    return ys, dest


def _qkv_project(h, w_ref, b_ref, q_ref, k_ref, v_ref):
    qkv = jnp.dot(h.astype(BF16), w_ref[...], preferred_element_type=F32) + b_ref[...]
    nq = N_HEADS * HEAD_DIM
    nk = 2 * N_KV_HEADS * HEAD_DIM
    q_ref[...] = (qkv[:, :nq] * (HEAD_DIM ** -0.5)).astype(BF16)
    k_ref[...] = qkv[:, nq:nq + nk].astype(BF16)
    v_ref[...] = qkv[:, nq + nk:].astype(BF16)


def _qkv_operands(w_qkv, b_qkv):
    nq = N_HEADS * HEAD_DIM
    kv = N_KV_HEADS * HEAD_DIM

    def dup_heads(m):
        parts = []
        for hd in range(N_KV_HEADS):
            sl = m[..., hd * HEAD_DIM:(hd + 1) * HEAD_DIM]
            parts += [sl, sl]
        return jnp.concatenate(parts, axis=-1)

    w = jnp.concatenate([w_qkv[:, :nq], dup_heads(w_qkv[:, nq:nq + kv]),
                         dup_heads(w_qkv[:, nq + kv:])], axis=1).astype(BF16)
    b = jnp.concatenate([b_qkv[:nq], dup_heads(b_qkv[nq:nq + kv]),
                         dup_heads(b_qkv[nq + kv:])]).reshape(1, -1)
    return w, b


def _attn_kernel(sinks_ref, q_ref, kp_ref, kc_ref, vp_ref, vc_ref, bias_ref, o_ref):
    n = pl.program_id(1)
    rows = SLABS_PER_GROUP * ATT_BLOCK
    keys = 2 * ATT_BLOCK
    kall = jnp.concatenate([kp_ref[...], kc_ref[...]], axis=0)
    vall = jnp.concatenate([vp_ref[...], vc_ref[...]], axis=0)
    j = lax.broadcasted_iota(I32, (keys, rows), 0)
    r = lax.broadcasted_iota(I32, (keys, rows), 1) & (ATT_BLOCK - 1)
    in_window = (j > r) & (j <= r + WINDOW)
    lane_k = lax.broadcasted_iota(I32, (keys, 128), 1)
    halves = [lane_k < HEAD_DIM, lane_k >= HEAD_DIM]
    sub_o = lax.broadcasted_iota(I32, (128, rows), 0)
    zero = jnp.zeros((), BF16)
    for blk in range(ATT_STEP_BLOCKS):
        q = q_ref[blk * ATT_BLOCK:(blk + 1) * ATT_BLOCK, :]
        kcat = kall[blk * ATT_BLOCK:blk * ATT_BLOCK + keys]
        vcat = vall[blk * ATT_BLOCK:blk * ATT_BLOCK + keys]
        if blk == 0:
            mask = in_window & ((n > 0) | (j >= ATT_BLOCK))
        else:
            mask = in_window
        for g in range(N_KV_HEADS):
            s0 = g * SLABS_PER_GROUP
            qg = jnp.concatenate([q[:, (s0 + s) * 128:(s0 + s + 1) * 128]
                                  for s in range(SLABS_PER_GROUP)], axis=0)
            kg = kcat[:, g * 128:(g + 1) * 128]
            vg = vcat[:, g * 128:(g + 1) * 128]
            pts, invs = [], []
            for p in range(HEADS_PER_SLAB):
                kp = jnp.where(halves[p], kg, zero)
                sc = lax.dot_general(kp, qg, (((1,), (1,)), ((), ())),
                                     preferred_element_type=F32)
                sc = jnp.where(mask, sc + bias_ref[g, p], NEG_INF)
                sink = jnp.concatenate(
                    [jnp.full((1, ATT_BLOCK),
                              sinks_ref[g * GQA_GROUP + s * HEADS_PER_SLAB + p], F32)
                     for s in range(SLABS_PER_GROUP)], axis=1)
                m = jnp.maximum(jnp.max(sc, axis=0, keepdims=True), sink)
                pe = jnp.exp(sc - m)
                den = jnp.sum(pe, axis=0, keepdims=True) + jnp.exp(sink - m)
                pts.append(pe.astype(BF16))
                invs.append(1.0 / den)
            pcat = jnp.concatenate(pts, axis=0)
            vblk = jnp.concatenate([jnp.where(halves[p], vg, zero)
                                    for p in range(HEADS_PER_SLAB)], axis=0)
            og = lax.dot_general(vblk, pcat, (((0,), (0,)), ((), ())),
                                 preferred_element_type=F32)
            og = og * jnp.where(sub_o < HEAD_DIM, invs[0], invs[1])
            og = og.T
            for s in range(SLABS_PER_GROUP):
                o_ref[blk * ATT_BLOCK:(blk + 1) * ATT_BLOCK, (s0 + s) * 128:(s0 + s + 1) * 128] = (
                    og[s * ATT_BLOCK:(s + 1) * ATT_BLOCK].astype(BF16))


def _t5_bias_table(rel_bias):
    r = np.arange(ATT_BLOCK)[:, None]
    j = np.arange(2 * ATT_BLOCK)[None, :]
    rel = np.clip(ATT_BLOCK + r - j, 0, WINDOW - 1)
    max_exact = N_BUCKETS // 2
    nf = jnp.maximum(rel, max_exact).astype(F32)
    large = max_exact + (jnp.log(nf / max_exact) / math.log(MAX_DISTANCE / max_exact)
                         * (N_BUCKETS - max_exact)).astype(I32)
    large = jnp.minimum(large, N_BUCKETS - 1)
    bucket = jnp.where(rel < max_exact, rel, large)
    pick = (bucket[None] == jnp.arange(N_BUCKETS, dtype=I32)[:, None, None])
    bias = jnp.sum(jnp.where(pick[:, None], rel_bias.astype(F32)[:, :, None, None], 0.0),
                   axis=0)
    bias = bias.reshape(N_KV_HEADS, SLABS_PER_GROUP, HEADS_PER_SLAB, ATT_BLOCK, 2 * ATT_BLOCK)
    bias = jnp.transpose(bias, (0, 2, 4, 1, 3))
    return bias.reshape(N_KV_HEADS, HEADS_PER_SLAB, 2 * ATT_BLOCK, SLABS_PER_GROUP * ATT_BLOCK)


def _attention(q, kk, vv, sinks, rel_bias, bsz, seq):
    nq = N_HEADS * HEAD_DIM
    kvw = kk.shape[-1]
    step = ATT_STEP_BLOCKS * ATT_BLOCK
    q3 = q.reshape(bsz, seq, nq)
    k3 = kk.reshape(bsz, seq, kvw)
    v3 = vv.reshape(bsz, seq, kvw)
    bias = _t5_bias_table(rel_bias)
    prev = lambda bi, n: (bi, jnp.maximum(n * ATT_STEP_BLOCKS - 1, 0), 0)
    cur = lambda bi, n: (bi, n, 0)
    out = pl.pallas_call(
        _attn_kernel,
        out_shape=jax.ShapeDtypeStruct((bsz, seq, nq), BF16),
        grid=(bsz, seq // step),
        in_specs=[
            pl.BlockSpec(memory_space=pltpu.SMEM),
            pl.BlockSpec((None, step, nq), cur),
            pl.BlockSpec((None, ATT_BLOCK, kvw), prev),
            pl.BlockSpec((None, step, kvw), cur),
            pl.BlockSpec((None, ATT_BLOCK, kvw), prev),
            pl.BlockSpec((None, step, kvw), cur),
            pl.BlockSpec(bias.shape, lambda bi, n: (0, 0, 0, 0)),
        ],
        out_specs=pl.BlockSpec((None, step, nq), cur),
        compiler_params=pltpu.CompilerParams(
            dimension_semantics=("arbitrary", "arbitrary"),
            vmem_limit_bytes=VMEM_LIMIT_BYTES),
        name="attn_core",
    )(sinks.astype(F32), q3, k3, k3, v3, v3, bias)
    return out.reshape(bsz * seq, nq)


def _oproj_kernel(x_ref, o_ref, w_ref, b_ref, gr_ref, wr_ref, br_ref,
                  y_ref, h_ref, ids_ref, gates_ref, rank_ref, sizes_ref, carry_ref, *, ts):
    x_new = x_ref[...] + jnp.dot(o_ref[...], w_ref[...],
                                 preferred_element_type=F32) + b_ref[...]
    y_ref[...] = x_new
    _route_tile(x_new, pl.program_id(0) == 0, gr_ref, wr_ref, br_ref,
                h_ref, ids_ref, gates_ref, rank_ref, sizes_ref, carry_ref, ts)


def _out_proj(x2d, o, w_o, b_o, g_ffn, w_router, b_router):
    t, d = x2d.shape
    ts = min(ROW_TILE, t)
    nq = o.shape[1]
    r_in, r_shape, r_out = _route_specs(t, ts, d, lambda i: i)
    outs = pl.pallas_call(
        functools.partial(_oproj_kernel, ts=ts),
        out_shape=(jax.ShapeDtypeStruct((t, d), F32),) + r_shape,
        grid=(t // ts,),
        in_specs=[
            pl.BlockSpec((ts, d), lambda i: (i, 0)),
            pl.BlockSpec((ts, nq), lambda i: (i, 0)),
            pl.BlockSpec((nq, d), lambda i: (0, 0)),
            pl.BlockSpec((1, d), lambda i: (0, 0)),
        ] + r_in,
        out_specs=(pl.BlockSpec((ts, d), lambda i: (i, 0)),) + r_out,
        scratch_shapes=[pltpu.VMEM((N_EXPERTS, 128), F32)],
        compiler_params=pltpu.CompilerParams(
            dimension_semantics=("arbitrary",), vmem_limit_bytes=VMEM_LIMIT_BYTES),
        name="attn_out_proj_route",
    )(x2d, o, w_o.astype(BF16), b_o.reshape(1, d),
      *_route_operands(g_ffn, w_router, b_router))
    return outs[0], outs[1:]


def kernel(x, norm_mix, norm_ffn, norm_final, pool_w, pool_b, pool_scale, w_qkv, b_qkv,
           sinks, w_o, b_o, rel_bias, w_router, b_router, w1, b1, w2, b2):
    bsz, seq, d = x.shape
    t = bsz * seq
    x, routed = _pool_layer(x, norm_mix[0], pool_w[0], pool_b[0], pool_scale[0],
                            norm_ffn[0], w_router[0], b_router[0])
    x = x.reshape(t, d)
    ys, dest = _moe_experts(routed, 0, w1, b1, w2, b2)
    x, q, kk, vv = _combine(x, routed[2], dest, ys, norm_mix[1], (w_qkv[0], b_qkv[0]))
    o = _attention(q, kk, vv, sinks[0], rel_bias, bsz, seq)
    x, routed = _out_proj(x, o, w_o[0], b_o[0], norm_ffn[1], w_router[1], b_router[1])
    ys, dest = _moe_experts(routed, 1, w1, b1, w2, b2)
    x = _combine(x, routed[2], dest, ys, norm_final)
    return x.reshape(bsz, seq, d)
```

```python
import functools
import math

import jax
import jax.numpy as jnp
import numpy as np
from jax import lax
from jax.experimental import pallas as pl
from jax.experimental.pallas import tpu as pltpu

F32 = jnp.float32
BF16 = jnp.bfloat16
I32 = jnp.int32

D_MODEL = 1024
POOL_WINDOWS = (2, 4, 8, 16)
POOL_GROUP_CH = D_MODEL // len(POOL_WINDOWS)
POOL_HALO = 16
HEAD_DIM = 64
N_HEADS = 16
N_KV_HEADS = 2
GQA_GROUP = N_HEADS // N_KV_HEADS
HEADS_PER_SLAB = 128 // HEAD_DIM
SLABS_PER_GROUP = GQA_GROUP // HEADS_PER_SLAB
ATT_BLOCK = 128
ATT_STEP_BLOCKS = 2
WINDOW = 128
N_BUCKETS = 32
MAX_DISTANCE = 128
N_EXPERTS = 32
TOP_K = 4
D_FF = D_MODEL
SWIGLU_ALPHA = 1.702
SWIGLU_LIMIT = 7.0
RMS_EPS = 1e-5
NEG_INF = -1e30

ROW_TILE = 512
MOE_BLOCK = 512
DISPATCH_TILE = 2048
COMBINE_TILE = 256
ISSUE_UNROLL = 4
VMEM_LIMIT_BYTES = 56 * 1024 * 1024


def _rms(x, g):
    return x * lax.rsqrt(jnp.mean(x * x, axis=-1, keepdims=True) + RMS_EPS) * g


ROW_TILE_SUB = D_MODEL // 128


def _load_row_tiles(ref, n, lead=()):
    return jnp.concatenate(
        [ref[lead + (pl.ds(j, n, stride=ROW_TILE_SUB), slice(None))]
         for j in range(ROW_TILE_SUB)], axis=1)


def _store_row_tiles(ref, val, n):
    for j in range(ROW_TILE_SUB):
        ref[pl.ds(j, n, stride=ROW_TILE_SUB), :] = val[:, j * 128:(j + 1) * 128]


POOL_PAD = 8


def _pool_kernel(x_ref, xh_ref, g_ref, w_ref, b_ref, s_ref, gr_ref, wr_ref, br_ref,
                 o_ref, h_ref, ids_ref, gates_ref, rank_ref, sizes_ref,
                 buf_ref, s1_ref, s2_ref, s3_ref, carry_ref, *, ts):
    i = pl.program_id(1)
    c = POOL_GROUP_CH
    h0 = POOL_PAD
    t0 = POOL_PAD + POOL_HALO
    r1 = t0 + ts
    x = x_ref[...]
    g = g_ref[...]
    h = _rms(x, g)
    hh = _rms(xh_ref[...], g)
    hh = jnp.where(i == 0, 0.0, hh)
    buf_ref[0:h0, :] = jnp.zeros((h0, D_MODEL), F32)
    buf_ref[h0:t0, :] = hh
    buf_ref[t0:r1, :] = h
    sum0 = h[:, 0:c] + buf_ref[t0 - 1:r1 - 1, 0:c]
    l1 = buf_ref[h0:r1, c:] + buf_ref[h0 - 1:r1 - 1, c:]
    s1_ref[0:h0, :] = jnp.zeros((h0, 3 * c), F32)
    s1_ref[h0:r1, :] = l1
    sum1 = s1_ref[t0:r1, 0:c] + s1_ref[t0 - 2:r1 - 2, 0:c]
    l2 = s1_ref[h0:r1, c:] + s1_ref[h0 - 2:r1 - 2, c:]
    s2_ref[0:h0, :] = jnp.zeros((h0, 2 * c), F32)
    s2_ref[h0:r1, :] = l2
    sum2 = s2_ref[t0:r1, 0:c] + s2_ref[t0 - 4:r1 - 4, 0:c]
    s3_ref[h0:r1, :] = s2_ref[h0:r1, c:] + s2_ref[h0 - 4:r1 - 4, c:]
    sum3 = s3_ref[t0:r1, :] + s3_ref[t0 - 8:r1 - 8, :]
    t = i * ts + lax.broadcasted_iota(I32, (ts, 1), 0)
    outs = []
    for gi, (win, acc) in enumerate(zip(POOL_WINDOWS, (sum0, sum1, sum2, sum3))):
        cnt = jnp.minimum(t + 1, win).astype(F32)
        dlt = acc / cnt - h[:, gi * c:(gi + 1) * c]
        outs.append(jnp.dot(dlt.astype(BF16), w_ref[gi], preferred_element_type=F32))
    y = jnp.concatenate(outs, axis=1)
    x_new = x + (y + b_ref[...]) * s_ref[...]
    o_ref[...] = x_new
    first = (pl.program_id(0) == 0) & (i == 0)
    _route_tile(x_new, first, gr_ref, wr_ref, br_ref,
                h_ref, ids_ref, gates_ref, rank_ref, sizes_ref, carry_ref, ts)


def _pool_layer(x, g, w, b, s, g_ffn, w_router, b_router):
    bsz, seq, d = x.shape
    ts = min(ROW_TILE, seq)
    tiles = seq // ts
    kern = functools.partial(_pool_kernel, ts=ts)
    rows = POOL_PAD + POOL_HALO + ts
    vec = lambda: pl.BlockSpec((1, d), lambda bi, i: (0, 0))
    r_in, r_shape, r_out = _route_specs(bsz * seq, ts, d, lambda bi, i: bi * tiles + i)
    outs = pl.pallas_call(
        kern,
        out_shape=(jax.ShapeDtypeStruct(x.shape, F32),) + r_shape,
        grid=(bsz, tiles),
        in_specs=[
            pl.BlockSpec((None, ts, d), lambda bi, i: (bi, i, 0)),
            pl.BlockSpec((None, POOL_HALO, d),
                         lambda bi, i: (bi, jnp.maximum(i * (ts // POOL_HALO) - 1, 0), 0)),
            vec(),
            pl.BlockSpec(w.shape, lambda bi, i: (0, 0, 0)),
            vec(),
            vec(),
        ] + r_in,
        out_specs=(pl.BlockSpec((None, ts, d), lambda bi, i: (bi, i, 0)),) + r_out,
        scratch_shapes=[pltpu.VMEM((rows, d), F32),
                        pltpu.VMEM((rows, 3 * POOL_GROUP_CH), F32),
                        pltpu.VMEM((rows, 2 * POOL_GROUP_CH), F32),
                        pltpu.VMEM((rows, POOL_GROUP_CH), F32),
                        pltpu.VMEM((N_EXPERTS, 128), F32)],
        compiler_params=pltpu.CompilerParams(
            dimension_semantics=("arbitrary", "arbitrary"),
            vmem_limit_bytes=VMEM_LIMIT_BYTES),
        name="pool_route",
    )(x, x, g.reshape(1, d), w.astype(BF16), b.reshape(1, d), s.reshape(1, d),
      *_route_operands(g_ffn, w_router, b_router))
    return outs[0], outs[1:]


def _route_tile(x, first, g_ref, wr_ref, br_ref,
                h_ref, ids_ref, gates_ref, rank_ref, sizes_ref, carry_ref, ts):
    @pl.when(first)
    def _():
        carry_ref[...] = jnp.zeros_like(carry_ref)

    h = _rms(x, g_ref[...])
    _store_row_tiles(h_ref, h, ts)
    h_hi = h.astype(BF16)
    h_lo = (h - h_hi.astype(F32)).astype(BF16)
    w = wr_ref[...]
    w_hi = w.astype(BF16)
    w_lo = (w - w_hi.astype(F32)).astype(BF16)
    dn = (((1,), (1,)), ((), ()))
    lg = (lax.dot_general(w_hi, h_hi, dn, preferred_element_type=F32)
          + lax.dot_general(w_lo, h_hi, dn, preferred_element_type=F32)
          + lax.dot_general(w_hi, h_lo, dn, preferred_element_type=F32))
    lg = lg + br_ref[:, 0:1]

    iota_e = lax.broadcasted_iota(I32, (N_EXPERTS, ts), 0)
    vals, ids = [], []
    for _ in range(TOP_K):
        m = jnp.max(lg, axis=0, keepdims=True)
        idx = jnp.min(jnp.where(lg == m, iota_e, N_EXPERTS), axis=0, keepdims=True)
        vals.append(m)
        ids.append(idx)
        lg = jnp.where(iota_e == idx, -jnp.inf, lg)
    ex = [jnp.exp(v - vals[0]) for v in vals]
    den = ex[0] + ex[1] + ex[2] + ex[3]
    gates_ref[...] = jnp.concatenate([e / den for e in ex], axis=0)
    ids_ref[...] = jnp.concatenate(ids, axis=0)

    hot = [(iota_e == idx) for idx in ids]
    multi = (hot[0] | hot[1] | hot[2] | hot[3])
    multi_f = multi.astype(F32)
    r = lax.broadcasted_iota(I32, (ts, ts), 0)
    c = lax.broadcasted_iota(I32, (ts, ts), 1)
    upper = (r < c).astype(BF16)
    before = jnp.dot(multi_f.astype(BF16), upper, preferred_element_type=F32)
    before = before + carry_ref[:, 0:1]
    ranks = [jnp.sum(jnp.where(hk, before, 0.0), axis=0, keepdims=True) for hk in hot]
    rank_ref[...] = jnp.concatenate(ranks, axis=0).astype(I32)
    carry_ref[...] = carry_ref[...] + jnp.sum(multi_f, axis=1, keepdims=True)
    sizes_ref[...] = carry_ref[...]


def _route_operands(g, w_router, b_router):
    d = g.shape[0]
    return (g.reshape(1, d), w_router.T,
            jnp.broadcast_to(b_router.reshape(N_EXPERTS, 1), (N_EXPERTS, 128)))


def _route_specs(t, ts, d, tile_of):
    const = lambda *idx: (0, 0)
    in_specs = [pl.BlockSpec((1, d), const), pl.BlockSpec((N_EXPERTS, d), const),
                pl.BlockSpec((N_EXPERTS, 128), const)]
    out_shape = (
        jax.ShapeDtypeStruct((t * ROW_TILE_SUB, 128), F32),
        jax.ShapeDtypeStruct((TOP_K, t), I32),
        jax.ShapeDtypeStruct((TOP_K, t), F32),
        jax.ShapeDtypeStruct((TOP_K, t), I32),
        jax.ShapeDtypeStruct((N_EXPERTS, 128), F32),
    )
    tok = lambda: pl.BlockSpec((TOP_K, ts), lambda *idx: (0, tile_of(*idx)))
    out_specs = (
        pl.BlockSpec((ts * ROW_TILE_SUB, 128), lambda *idx: (tile_of(*idx), 0)),
        tok(), tok(), tok(),
        pl.BlockSpec((N_EXPERTS, 128), const),
    )
    return in_specs, out_shape, out_specs


def _dispatch_kernel(zrow_ref, n_used_ref, dest_ref, h_ref, xs_ref, zbuf, sem_z, sem,
                     *, td, blk, nblk):
    i = pl.program_id(0)
    sub = ROW_TILE_SUB

    def zero_block(row):
        row = pl.multiple_of(row * sub, blk * sub)
        return pltpu.make_async_copy(zbuf, xs_ref.at[pl.ds(row, blk * sub), :], sem_z)

    @pl.when(i == 0)
    def _():
        zbuf[...] = jnp.zeros_like(zbuf)
        for e in range(N_EXPERTS):
            @pl.when(zrow_ref[e] >= 0)
            def _():
                zero_block(zrow_ref[e]).start()

        def start_tail(b, carry):
            zero_block(b * blk).start()
            return carry

        def wait_tail(b, carry):
            zero_block(b * blk).wait()
            return carry

        lax.fori_loop(n_used_ref[0], nblk, start_tail, 0)
        for e in range(N_EXPERTS):
            @pl.when(zrow_ref[e] >= 0)
            def _():
                zero_block(zrow_ref[e]).wait()
        lax.fori_loop(n_used_ref[0], nblk, wait_tail, 0)

    def issue(t, carry):
        src = h_ref.at[pl.ds(pl.multiple_of(t * sub, sub), sub), :]
        for k in range(TOP_K):
            d = dest_ref[0, 0, k * td + t]
            pltpu.make_async_copy(src, xs_ref.at[pl.ds(pl.multiple_of(d * sub, sub), sub), :],
                                  sem).start(priority=k % 2)
        return carry

    lax.fori_loop(0, td, issue, 0, unroll=ISSUE_UNROLL)
    for k in range(TOP_K):
        pltpu.make_async_copy(h_ref, xs_ref.at[pl.ds(0, td * sub), :], sem).wait()


def _dispatch(h_tiles, dest, zrow, n_used, n_rows):
    sub = ROW_TILE_SUB
    t = h_tiles.shape[0] // sub
    td = min(DISPATCH_TILE, t)
    nt = t // td
    dest_tiles = dest.reshape(TOP_K, nt, td).transpose(1, 0, 2).reshape(nt, 1, TOP_K * td)
    kern = functools.partial(_dispatch_kernel, td=td, blk=MOE_BLOCK, nblk=n_rows // MOE_BLOCK)
    return pl.pallas_call(
        kern,
        out_shape=jax.ShapeDtypeStruct((n_rows * sub, 128), F32),
        grid_spec=pltpu.PrefetchScalarGridSpec(
            num_scalar_prefetch=2,
            grid=(nt,),
            in_specs=[
                pl.BlockSpec((1, 1, TOP_K * td), lambda i, z, nu: (i, 0, 0),
                             memory_space=pltpu.SMEM),
                pl.BlockSpec((td * sub, 128), lambda i, z, nu: (i, 0)),
            ],
            out_specs=pl.BlockSpec(memory_space=pl.ANY),
            scratch_shapes=[
                pltpu.VMEM((MOE_BLOCK * sub, 128), F32),
                pltpu.SemaphoreType.DMA(()),
                pltpu.SemaphoreType.DMA(()),
            ],
        ),
        compiler_params=pltpu.CompilerParams(
            dimension_semantics=("arbitrary",), vmem_limit_bytes=VMEM_LIMIT_BYTES),
        name="moe_dispatch",
    )(zrow, n_used, dest_tiles, h_tiles)


def _expert_kernel(blk_e_ref, n_used_ref, nxt_e_ref, n_valid_ref, x_ref, w1_hbm, b1_ref, w2_hbm,
                   b2_ref, o_ref, w1s_ref, w2s_ref, w1b_ref, w2b_ref, wsem, *, layer):
    i = pl.program_id(0)
    e = blk_e_ref[i]
    prev = blk_e_ref[jnp.maximum(i - 1, 0)]
    fresh = (i == 0) | (e != prev)
    live = i < n_used_ref[0]

    def weight_copies(expert):
        return (pltpu.make_async_copy(w1_hbm.at[layer, expert], w1s_ref, wsem.at[0]),
                pltpu.make_async_copy(w2_hbm.at[layer, expert], w2s_ref, wsem.at[1]))

    @pl.when(i == 0)
    def _():
        for cp in weight_copies(e):
            cp.start()

    @pl.when(live & fresh)
    def _():
        for cp in weight_copies(e):
            cp.wait()
        w1b_ref[...] = w1s_ref[...].astype(BF16)
        w2b_ref[...] = w2s_ref[...].astype(BF16)
        nxt = nxt_e_ref[i]

        @pl.when(nxt >= 0)
        def _():
            for cp in weight_copies(nxt):
                cp.start()

    def mlp(rows):
        x = _load_row_tiles(x_ref, rows).astype(BF16)
        a = jnp.dot(x, w1b_ref[...], preferred_element_type=F32) + b1_ref[...]
        glu = jnp.minimum(a[:, :D_FF], SWIGLU_LIMIT)
        lin = jnp.clip(a[:, D_FF:], -SWIGLU_LIMIT, SWIGLU_LIMIT)
        act = glu * jax.nn.sigmoid(SWIGLU_ALPHA * glu) * (lin + 1.0)
        y = jnp.dot(act.astype(BF16), w2b_ref[...], preferred_element_type=F32) + b2_ref[...]
        _store_row_tiles(o_ref, y, rows)
        if rows < MOE_BLOCK:
            o_ref[rows * ROW_TILE_SUB:, :] = jnp.zeros(
                ((MOE_BLOCK - rows) * ROW_TILE_SUB, 128), F32)

    half = MOE_BLOCK // 2
    short = n_valid_ref[i] <= half

    @pl.when(live & jnp.logical_not(short))
    def _():
        mlp(MOE_BLOCK)

    @pl.when(live & short)
    def _():
        mlp(half)

    @pl.when(jnp.logical_not(live))
    def _():
        o_ref[...] = jnp.zeros_like(o_ref)


def _experts(xs, blk_e, n_used, nxt_e, n_valid, layer, w1, b1, w2, b2):
    sub = ROW_TILE_SUB
    n_rows = xs.shape[0] // sub
    d = w2.shape[-1]
    nblk = n_rows // MOE_BLOCK
    depth = w1.shape[0]
    row = lambda i, be, nu, nx, nv: (jnp.minimum(i, nu[0] - 1), 0)
    out_row = lambda i, be, nu, nx, nv: (i, 0)
    exp3 = lambda i, be, nu, nx, nv: (layer, be[jnp.minimum(i, nu[0] - 1)], 0, 0)
    return pl.pallas_call(
        functools.partial(_expert_kernel, layer=layer),
        out_shape=jax.ShapeDtypeStruct((n_rows * sub, 128), F32),
        grid_spec=pltpu.PrefetchScalarGridSpec(
            num_scalar_prefetch=4,
            grid=(nblk,),
            in_specs=[
                pl.BlockSpec((MOE_BLOCK * sub, 128), row),
                pl.BlockSpec(memory_space=pl.ANY),
                pl.BlockSpec((None, None, 1, 2 * D_FF), exp3),
                pl.BlockSpec(memory_space=pl.ANY),
                pl.BlockSpec((None, None, 1, d), exp3),
            ],
            out_specs=pl.BlockSpec((MOE_BLOCK * sub, 128), out_row),
            scratch_shapes=[
                pltpu.VMEM((d, 2 * D_FF), F32),
                pltpu.VMEM((D_FF, d), F32),
                pltpu.VMEM((d, 2 * D_FF), BF16),
                pltpu.VMEM((D_FF, d), BF16),
                pltpu.SemaphoreType.DMA((2,)),
            ],
        ),
        compiler_params=pltpu.CompilerParams(
            dimension_semantics=("arbitrary",), vmem_limit_bytes=VMEM_LIMIT_BYTES),
        name="moe_experts",
    )(blk_e, n_used, nxt_e, n_valid, xs, w1, b1.reshape(depth, N_EXPERTS, 1, 2 * D_FF), w2,
      b2.reshape(depth, N_EXPERTS, 1, d))


def _combine_kernel(dcur_ref, dnext_ref, x_ref, g_ref, ys_ref, gn_ref, o_ref, buf, sem,
                    *, tc, n_tiles, final_norm):
    i = pl.program_id(0)
    sub = ROW_TILE_SUB

    def issue(dref, slot):
        def body(t, carry):
            for k in range(TOP_K):
                d = dref[0, 0, k * tc + t]
                pltpu.make_async_copy(
                    ys_ref.at[pl.ds(pl.multiple_of(d * sub, sub), sub), :],
                    buf.at[slot, k, pl.ds(pl.multiple_of(t * sub, sub), sub), :],
                    sem.at[slot]).start(priority=k % 2)
            return carry
        lax.fori_loop(0, tc, body, 0, unroll=ISSUE_UNROLL)

    @pl.when(i == 0)
    def _():
        issue(dcur_ref, 0)

    @pl.when(i + 1 < n_tiles)
    def _():
        issue(dnext_ref, (i + 1) % 2)

    slot = i % 2
    for k in range(TOP_K):
        pltpu.make_async_copy(ys_ref.at[pl.ds(0, tc * sub), :], buf.at[slot, k],
                              sem.at[slot]).wait()
    acc = x_ref[...]
    gates = g_ref[...]
    for k in range(TOP_K):
        acc = acc + gates[:, k:k + 1] * _load_row_tiles(buf, tc, lead=(slot, k))
    if final_norm:
        acc = _rms(acc, gn_ref[...])
    o_ref[...] = acc


def _combine(x2d, gates_t, dest, ys, g_norm, final_norm):
    t, d = x2d.shape
    tc = min(COMBINE_TILE, t)
    nt = t // tc
    dest_tiles = dest.reshape(TOP_K, nt, tc).transpose(1, 0, 2).reshape(nt, 1, TOP_K * tc)
    kern = functools.partial(_combine_kernel, tc=tc, n_tiles=nt, final_norm=final_norm)
    return pl.pallas_call(
        kern,
        out_shape=jax.ShapeDtypeStruct((t, d), F32),
        grid=(nt,),
        in_specs=[
            pl.BlockSpec((1, 1, TOP_K * tc), lambda i: (i, 0, 0), memory_space=pltpu.SMEM),
            pl.BlockSpec((1, 1, TOP_K * tc), lambda i: (jnp.minimum(i + 1, nt - 1), 0, 0),
                         memory_space=pltpu.SMEM),
            pl.BlockSpec((tc, d), lambda i: (i, 0)),
            pl.BlockSpec((tc, TOP_K), lambda i: (i, 0)),
            pl.BlockSpec(memory_space=pl.ANY),
            pl.BlockSpec((1, d), lambda i: (0, 0)),
        ],
        out_specs=pl.BlockSpec((tc, d), lambda i: (i, 0)),
        scratch_shapes=[
            pltpu.VMEM((2, TOP_K, tc * ROW_TILE_SUB, 128), F32),
            pltpu.SemaphoreType.DMA((2,)),
        ],
        compiler_params=pltpu.CompilerParams(
            dimension_semantics=("arbitrary",), vmem_limit_bytes=VMEM_LIMIT_BYTES),
        name="moe_combine_final" if final_norm else "moe_combine",
    )(dest_tiles, dest_tiles, x2d, gates_t.T, ys, g_norm.reshape(1, d))


def _moe_experts(routed, layer, w1, b1, w2, b2):
    h, ids, gates, rank, sizes = routed
    t = ids.shape[1]
    sizes = sizes[:, 0].astype(I32)
    padded = ((sizes + MOE_BLOCK - 1) // MOE_BLOCK) * MOE_BLOCK
    pend = jnp.cumsum(padded)
    pstart = pend - padded
    onehot = ids[:, :, None] == jnp.arange(N_EXPERTS, dtype=I32)
    dest = rank + jnp.sum(jnp.where(onehot, pstart, 0), axis=-1)
    n_rows = t * TOP_K + N_EXPERTS * MOE_BLOCK
    nblk = n_rows // MOE_BLOCK
    blk_row = jnp.arange(nblk, dtype=I32) * MOE_BLOCK
    blk_e = jnp.minimum(jnp.sum((pend[None, :] <= blk_row[:, None]).astype(I32), axis=1),
                        N_EXPERTS - 1)
    n_used = (pend[-1:] // MOE_BLOCK).astype(I32)
    zrow = jnp.where(padded > 0, pend - MOE_BLOCK, -1).astype(I32)
    xs = _dispatch(h, dest, zrow, n_used, n_rows)
    blk_id = jnp.arange(nblk, dtype=I32)
    later = ((blk_id[None, :] > blk_id[:, None]) & (blk_e[None, :] != blk_e[:, None])
             & (blk_id[None, :] < n_used[0]))
    nxt_blk = jnp.min(jnp.where(later, blk_id[None, :], nblk), axis=1)
    nxt_e = jnp.sum(jnp.where(blk_id[None, :] == nxt_blk[:, None], blk_e[None, :] + 1, 0),
                    axis=1).astype(I32) - 1
    own = blk_e[:, None] == jnp.arange(N_EXPERTS, dtype=I32)[None, :]
    row_end = jnp.sum(jnp.where(own, (pstart + sizes)[None, :], 0), axis=1)
    n_valid = jnp.clip(row_end - blk_row, 0, MOE_BLOCK).astype(I32)
    ys = _experts(xs, blk_e, n_used, nxt_e, n_valid, layer, w1, b1, w2, b2)
    return ys, dest


def _qkv_kernel(x_ref, g_ref, w_ref, b_ref, q_ref, k_ref, v_ref):
    h = _rms(x_ref[...], g_ref[...])
    qkv = jnp.dot(h.astype(BF16), w_ref[...], preferred_element_type=F32) + b_ref[...]
    nq = N_HEADS * HEAD_DIM
    nk = 2 * N_KV_HEADS * HEAD_DIM
    q_ref[...] = (qkv[:, :nq] * (HEAD_DIM ** -0.5)).astype(BF16)
    k_ref[...] = qkv[:, nq:nq + nk].astype(BF16)
    v_ref[...] = qkv[:, nq + nk:].astype(BF16)


def _qkv(x2d, g, w_qkv, b_qkv):
    t, d = x2d.shape
    ts = min(ROW_TILE, t)
    nq = N_HEADS * HEAD_DIM
    kv = N_KV_HEADS * HEAD_DIM

    def dup_heads(m):
        parts = []
        for hd in range(N_KV_HEADS):
            sl = m[..., hd * HEAD_DIM:(hd + 1) * HEAD_DIM]
            parts += [sl, sl]
        return jnp.concatenate(parts, axis=-1)

    w = jnp.concatenate([w_qkv[:, :nq], dup_heads(w_qkv[:, nq:nq + kv]),
                         dup_heads(w_qkv[:, nq + kv:])], axis=1).astype(BF16)
    b = jnp.concatenate([b_qkv[:nq], dup_heads(b_qkv[nq:nq + kv]),
                         dup_heads(b_qkv[nq + kv:])]).reshape(1, -1)
    n_out = w.shape[1]
    row = lambda width: pl.BlockSpec((ts, width), lambda i: (i, 0))
    return pl.pallas_call(
        _qkv_kernel,
        out_shape=(
            jax.ShapeDtypeStruct((t, nq), BF16),
            jax.ShapeDtypeStruct((t, 2 * kv), BF16),
            jax.ShapeDtypeStruct((t, 2 * kv), BF16),
        ),
        grid=(t // ts,),
        in_specs=[
            row(d),
            pl.BlockSpec((1, d), lambda i: (0, 0)),
            pl.BlockSpec((d, n_out), lambda i: (0, 0)),
            pl.BlockSpec((1, n_out), lambda i: (0, 0)),
        ],
        out_specs=(row(nq), row(2 * kv), row(2 * kv)),
        compiler_params=pltpu.CompilerParams(
            dimension_semantics=("arbitrary",), vmem_limit_bytes=VMEM_LIMIT_BYTES),
        name="attn_qkv",
    )(x2d, g.reshape(1, d), w, b)


def _attn_kernel(sinks_ref, q_ref, kp_ref, kc_ref, vp_ref, vc_ref, bias_ref, o_ref):
    n = pl.program_id(1)
    rows = SLABS_PER_GROUP * ATT_BLOCK
    keys = 2 * ATT_BLOCK
    kall = jnp.concatenate([kp_ref[...], kc_ref[...]], axis=0)
    vall = jnp.concatenate([vp_ref[...], vc_ref[...]], axis=0)
    j = lax.broadcasted_iota(I32, (keys, rows), 0)
    r = lax.broadcasted_iota(I32, (keys, rows), 1) & (ATT_BLOCK - 1)
    in_window = (j > r) & (j <= r + WINDOW)
    lane_k = lax.broadcasted_iota(I32, (keys, 128), 1)
    halves = [lane_k < HEAD_DIM, lane_k >= HEAD_DIM]
    sub_o = lax.broadcasted_iota(I32, (128, rows), 0)
    zero = jnp.zeros((), BF16)
    for blk in range(ATT_STEP_BLOCKS):
        q = q_ref[blk * ATT_BLOCK:(blk + 1) * ATT_BLOCK, :]
        kcat = kall[blk * ATT_BLOCK:blk * ATT_BLOCK + keys]
        vcat = vall[blk * ATT_BLOCK:blk * ATT_BLOCK + keys]
        if blk == 0:
            mask = in_window & ((n > 0) | (j >= ATT_BLOCK))
        else:
            mask = in_window
        for g in range(N_KV_HEADS):
            s0 = g * SLABS_PER_GROUP
            qg = jnp.concatenate([q[:, (s0 + s) * 128:(s0 + s + 1) * 128]
                                  for s in range(SLABS_PER_GROUP)], axis=0)
            kg = kcat[:, g * 128:(g + 1) * 128]
            vg = vcat[:, g * 128:(g + 1) * 128]
            pts, invs = [], []
            for p in range(HEADS_PER_SLAB):
                kp = jnp.where(halves[p], kg, zero)
                sc = lax.dot_general(kp, qg, (((1,), (1,)), ((), ())),
                                     preferred_element_type=F32)
                sc = jnp.where(mask, sc + bias_ref[g, p], NEG_INF)
                sink = jnp.concatenate(
                    [jnp.full((1, ATT_BLOCK),
                              sinks_ref[g * GQA_GROUP + s * HEADS_PER_SLAB + p], F32)
                     for s in range(SLABS_PER_GROUP)], axis=1)
                m = jnp.maximum(jnp.max(sc, axis=0, keepdims=True), sink)
                pe = jnp.exp(sc - m)
                den = jnp.sum(pe, axis=0, keepdims=True) + jnp.exp(sink - m)
                pts.append(pe.astype(BF16))
                invs.append(1.0 / den)
            pcat = jnp.concatenate(pts, axis=0)
            vblk = jnp.concatenate([jnp.where(halves[p], vg, zero)
                                    for p in range(HEADS_PER_SLAB)], axis=0)
            og = lax.dot_general(vblk, pcat, (((0,), (0,)), ((), ())),
                                 preferred_element_type=F32)
            og = og * jnp.where(sub_o < HEAD_DIM, invs[0], invs[1])
            og = og.T
            for s in range(SLABS_PER_GROUP):
                o_ref[blk * ATT_BLOCK:(blk + 1) * ATT_BLOCK, (s0 + s) * 128:(s0 + s + 1) * 128] = (
                    og[s * ATT_BLOCK:(s + 1) * ATT_BLOCK].astype(BF16))


def _t5_bias_table(rel_bias):
    r = np.arange(ATT_BLOCK)[:, None]
    j = np.arange(2 * ATT_BLOCK)[None, :]
    rel = np.clip(ATT_BLOCK + r - j, 0, WINDOW - 1)
    max_exact = N_BUCKETS // 2
    nf = jnp.maximum(rel, max_exact).astype(F32)
    large = max_exact + (jnp.log(nf / max_exact) / math.log(MAX_DISTANCE / max_exact)
                         * (N_BUCKETS - max_exact)).astype(I32)
    large = jnp.minimum(large, N_BUCKETS - 1)
    bucket = jnp.where(rel < max_exact, rel, large)
    pick = (bucket[None] == jnp.arange(N_BUCKETS, dtype=I32)[:, None, None])
    bias = jnp.sum(jnp.where(pick[:, None], rel_bias.astype(F32)[:, :, None, None], 0.0),
                   axis=0)
    bias = bias.reshape(N_KV_HEADS, SLABS_PER_GROUP, HEADS_PER_SLAB, ATT_BLOCK, 2 * ATT_BLOCK)
    bias = jnp.transpose(bias, (0, 2, 4, 1, 3))
    return bias.reshape(N_KV_HEADS, HEADS_PER_SLAB, 2 * ATT_BLOCK, SLABS_PER_GROUP * ATT_BLOCK)


def _attention(q, kk, vv, sinks, rel_bias, bsz, seq):
    nq = N_HEADS * HEAD_DIM
    kvw = kk.shape[-1]
    step = ATT_STEP_BLOCKS * ATT_BLOCK
    q3 = q.reshape(bsz, seq, nq)
    k3 = kk.reshape(bsz, seq, kvw)
    v3 = vv.reshape(bsz, seq, kvw)
    bias = _t5_bias_table(rel_bias)
    prev = lambda bi, n: (bi, jnp.maximum(n * ATT_STEP_BLOCKS - 1, 0), 0)
    cur = lambda bi, n: (bi, n, 0)
    out = pl.pallas_call(
        _attn_kernel,
        out_shape=jax.ShapeDtypeStruct((bsz, seq, nq), BF16),
        grid=(bsz, seq // step),
        in_specs=[
            pl.BlockSpec(memory_space=pltpu.SMEM),
            pl.BlockSpec((None, step, nq), cur),
            pl.BlockSpec((None, ATT_BLOCK, kvw), prev),
            pl.BlockSpec((None, step, kvw), cur),
            pl.BlockSpec((None, ATT_BLOCK, kvw), prev),
            pl.BlockSpec((None, step, kvw), cur),
            pl.BlockSpec(bias.shape, lambda bi, n: (0, 0, 0, 0)),
        ],
        out_specs=pl.BlockSpec((None, step, nq), cur),
        compiler_params=pltpu.CompilerParams(
            dimension_semantics=("arbitrary", "arbitrary"),
            vmem_limit_bytes=VMEM_LIMIT_BYTES),
        name="attn_core",
    )(sinks.astype(F32), q3, k3, k3, v3, v3, bias)
    return out.reshape(bsz * seq, nq)


def _oproj_kernel(x_ref, o_ref, w_ref, b_ref, gr_ref, wr_ref, br_ref,
                  y_ref, h_ref, ids_ref, gates_ref, rank_ref, sizes_ref, carry_ref, *, ts):
    x_new = x_ref[...] + jnp.dot(o_ref[...], w_ref[...],
                                 preferred_element_type=F32) + b_ref[...]
    y_ref[...] = x_new
    _route_tile(x_new, pl.program_id(0) == 0, gr_ref, wr_ref, br_ref,
                h_ref, ids_ref, gates_ref, rank_ref, sizes_ref, carry_ref, ts)


def _out_proj(x2d, o, w_o, b_o, g_ffn, w_router, b_router):
    t, d = x2d.shape
    ts = min(ROW_TILE, t)
    nq = o.shape[1]
    r_in, r_shape, r_out = _route_specs(t, ts, d, lambda i: i)
    outs = pl.pallas_call(
        functools.partial(_oproj_kernel, ts=ts),
        out_shape=(jax.ShapeDtypeStruct((t, d), F32),) + r_shape,
        grid=(t // ts,),
        in_specs=[
            pl.BlockSpec((ts, d), lambda i: (i, 0)),
            pl.BlockSpec((ts, nq), lambda i: (i, 0)),
            pl.BlockSpec((nq, d), lambda i: (0, 0)),
            pl.BlockSpec((1, d), lambda i: (0, 0)),
        ] + r_in,
        out_specs=(pl.BlockSpec((ts, d), lambda i: (i, 0)),) + r_out,
        scratch_shapes=[pltpu.VMEM((N_EXPERTS, 128), F32)],
        compiler_params=pltpu.CompilerParams(
            dimension_semantics=("arbitrary",), vmem_limit_bytes=VMEM_LIMIT_BYTES),
        name="attn_out_proj_route",
    )(x2d, o, w_o.astype(BF16), b_o.reshape(1, d),
      *_route_operands(g_ffn, w_router, b_router))
    return outs[0], outs[1:]


def kernel(x, norm_mix, norm_ffn, norm_final, pool_w, pool_b, pool_scale, w_qkv, b_qkv,
           sinks, w_o, b_o, rel_bias, w_router, b_router, w1, b1, w2, b2):
    bsz, seq, d = x.shape
    t = bsz * seq
    x, routed = _pool_layer(x, norm_mix[0], pool_w[0], pool_b[0], pool_scale[0],
                            norm_ffn[0], w_router[0], b_router[0])
    x = x.reshape(t, d)
    ys, dest = _moe_experts(routed, 0, w1, b1, w2, b2)
    x = _combine(x, routed[2], dest, ys, norm_final, final_norm=False)
    q, kk, vv = _qkv(x, norm_mix[1], w_qkv[0], b_qkv[0])
    o = _attention(q, kk, vv, sinks[0], rel_bias, bsz, seq)
    x, routed = _out_proj(x, o, w_o[0], b_o[0], norm_ffn[1], w_router[1], b_router[1])
    ys, dest = _moe_experts(routed, 1, w1, b1, w2, b2)
    x = _combine(x, routed[2], dest, ys, norm_final, final_norm=True)
    return x.reshape(bsz, seq, d)
```

```python
import functools
import math

import jax
import jax.numpy as jnp
import numpy as np
from jax import lax
from jax.experimental import pallas as pl
from jax.experimental.pallas import tpu as pltpu

F32 = jnp.float32
BF16 = jnp.bfloat16
I32 = jnp.int32

D_MODEL = 1024
POOL_WINDOWS = (2, 4, 8, 16)
POOL_GROUP_CH = D_MODEL // len(POOL_WINDOWS)
POOL_HALO = 16
HEAD_DIM = 64
N_HEADS = 16
N_KV_HEADS = 2
GQA_GROUP = N_HEADS // N_KV_HEADS
HEADS_PER_SLAB = 128 // HEAD_DIM
SLABS_PER_GROUP = GQA_GROUP // HEADS_PER_SLAB
ATT_BLOCK = 128
ATT_STEP_BLOCKS = 2
WINDOW = 128
N_BUCKETS = 32
MAX_DISTANCE = 128
N_EXPERTS = 32
TOP_K = 4
D_FF = D_MODEL
SWIGLU_ALPHA = 1.702
SWIGLU_LIMIT = 7.0
RMS_EPS = 1e-5
NEG_INF = -1e30

ROW_TILE = 512
MOE_BLOCK = 512
EXPERT_STEP_BLOCKS = 2
DISPATCH_TILE = 2048
COMBINE_TILE = 256
ISSUE_UNROLL = 4
VMEM_LIMIT_BYTES = 56 * 1024 * 1024


def _rms(x, g):
    return x * lax.rsqrt(jnp.mean(x * x, axis=-1, keepdims=True) + RMS_EPS) * g


ROW_TILE_SUB = D_MODEL // 128


def _load_row_tiles(ref, n, lead=(), start=0):
    return jnp.concatenate(
        [ref[lead + (pl.ds(start * ROW_TILE_SUB + j, n, stride=ROW_TILE_SUB), slice(None))]
         for j in range(ROW_TILE_SUB)], axis=1)


def _store_row_tiles(ref, val, n, start=0):
    for j in range(ROW_TILE_SUB):
        ref[pl.ds(start * ROW_TILE_SUB + j, n, stride=ROW_TILE_SUB), :] = (
            val[:, j * 128:(j + 1) * 128])


POOL_PAD = 8


def _pool_kernel(x_ref, xh_ref, g_ref, w_ref, b_ref, s_ref, gr_ref, wr_ref, br_ref,
                 o_ref, h_ref, ids_ref, gates_ref, rank_ref, sizes_ref,
                 buf_ref, s1_ref, s2_ref, s3_ref, carry_ref, *, ts):
    i = pl.program_id(1)
    c = POOL_GROUP_CH
    h0 = POOL_PAD
    t0 = POOL_PAD + POOL_HALO
    r1 = t0 + ts
    x = x_ref[...]
    g = g_ref[...]
    h = _rms(x, g)
    hh = _rms(xh_ref[...], g)
    hh = jnp.where(i == 0, 0.0, hh)
    buf_ref[0:h0, :] = jnp.zeros((h0, D_MODEL), F32)
    buf_ref[h0:t0, :] = hh
    buf_ref[t0:r1, :] = h
    sum0 = h[:, 0:c] + buf_ref[t0 - 1:r1 - 1, 0:c]
    l1 = buf_ref[h0:r1, c:] + buf_ref[h0 - 1:r1 - 1, c:]
    s1_ref[0:h0, :] = jnp.zeros((h0, 3 * c), F32)
    s1_ref[h0:r1, :] = l1
    sum1 = s1_ref[t0:r1, 0:c] + s1_ref[t0 - 2:r1 - 2, 0:c]
    l2 = s1_ref[h0:r1, c:] + s1_ref[h0 - 2:r1 - 2, c:]
    s2_ref[0:h0, :] = jnp.zeros((h0, 2 * c), F32)
    s2_ref[h0:r1, :] = l2
    sum2 = s2_ref[t0:r1, 0:c] + s2_ref[t0 - 4:r1 - 4, 0:c]
    s3_ref[h0:r1, :] = s2_ref[h0:r1, c:] + s2_ref[h0 - 4:r1 - 4, c:]
    sum3 = s3_ref[t0:r1, :] + s3_ref[t0 - 8:r1 - 8, :]
    t = i * ts + lax.broadcasted_iota(I32, (ts, 1), 0)
    outs = []
    for gi, (win, acc) in enumerate(zip(POOL_WINDOWS, (sum0, sum1, sum2, sum3))):
        cnt = jnp.minimum(t + 1, win).astype(F32)
        dlt = acc / cnt - h[:, gi * c:(gi + 1) * c]
        outs.append(jnp.dot(dlt.astype(BF16), w_ref[gi], preferred_element_type=F32))
    y = jnp.concatenate(outs, axis=1)
    x_new = x + (y + b_ref[...]) * s_ref[...]
    o_ref[...] = x_new
    first = (pl.program_id(0) == 0) & (i == 0)
    _route_tile(x_new, first, gr_ref, wr_ref, br_ref,
                h_ref, ids_ref, gates_ref, rank_ref, sizes_ref, carry_ref, ts)


def _pool_layer(x, g, w, b, s, g_ffn, w_router, b_router):
    bsz, seq, d = x.shape
    ts = min(ROW_TILE, seq)
    tiles = seq // ts
    kern = functools.partial(_pool_kernel, ts=ts)
    rows = POOL_PAD + POOL_HALO + ts
    vec = lambda: pl.BlockSpec((1, d), lambda bi, i: (0, 0))
    r_in, r_shape, r_out = _route_specs(bsz * seq, ts, d, lambda bi, i: bi * tiles + i)
    outs = pl.pallas_call(
        kern,
        out_shape=(jax.ShapeDtypeStruct(x.shape, F32),) + r_shape,
        grid=(bsz, tiles),
        in_specs=[
            pl.BlockSpec((None, ts, d), lambda bi, i: (bi, i, 0)),
            pl.BlockSpec((None, POOL_HALO, d),
                         lambda bi, i: (bi, jnp.maximum(i * (ts // POOL_HALO) - 1, 0), 0)),
            vec(),
            pl.BlockSpec(w.shape, lambda bi, i: (0, 0, 0)),
            vec(),
            vec(),
        ] + r_in,
        out_specs=(pl.BlockSpec((None, ts, d), lambda bi, i: (bi, i, 0)),) + r_out,
        scratch_shapes=[pltpu.VMEM((rows, d), F32),
                        pltpu.VMEM((rows, 3 * POOL_GROUP_CH), F32),
                        pltpu.VMEM((rows, 2 * POOL_GROUP_CH), F32),
                        pltpu.VMEM((rows, POOL_GROUP_CH), F32),
                        pltpu.VMEM((N_EXPERTS, 128), F32)],
        compiler_params=pltpu.CompilerParams(
            dimension_semantics=("arbitrary", "arbitrary"),
            vmem_limit_bytes=VMEM_LIMIT_BYTES),
        name="pool_route",
    )(x, x, g.reshape(1, d), w.astype(BF16), b.reshape(1, d), s.reshape(1, d),
      *_route_operands(g_ffn, w_router, b_router))
    return outs[0], outs[1:]


def _route_tile(x, first, g_ref, wr_ref, br_ref,
                h_ref, ids_ref, gates_ref, rank_ref, sizes_ref, carry_ref, ts):
    @pl.when(first)
    def _():
        carry_ref[...] = jnp.zeros_like(carry_ref)

    h = _rms(x, g_ref[...])
    _store_row_tiles(h_ref, h, ts)
    h_hi = h.astype(BF16)
    h_lo = (h - h_hi.astype(F32)).astype(BF16)
    w = wr_ref[...]
    w_hi = w.astype(BF16)
    w_lo = (w - w_hi.astype(F32)).astype(BF16)
    dn = (((1,), (1,)), ((), ()))
    lg = (lax.dot_general(w_hi, h_hi, dn, preferred_element_type=F32)
          + lax.dot_general(w_lo, h_hi, dn, preferred_element_type=F32)
          + lax.dot_general(w_hi, h_lo, dn, preferred_element_type=F32))
    lg = lg + br_ref[:, 0:1]

    iota_e = lax.broadcasted_iota(I32, (N_EXPERTS, ts), 0)
    vals, ids = [], []
    for _ in range(TOP_K):
        m = jnp.max(lg, axis=0, keepdims=True)
        idx = jnp.min(jnp.where(lg == m, iota_e, N_EXPERTS), axis=0, keepdims=True)
        vals.append(m)
        ids.append(idx)
        lg = jnp.where(iota_e == idx, -jnp.inf, lg)
    ex = [jnp.exp(v - vals[0]) for v in vals]
    den = ex[0] + ex[1] + ex[2] + ex[3]
    gates_ref[...] = jnp.concatenate([e / den for e in ex], axis=0)
    ids_ref[...] = jnp.concatenate(ids, axis=0)

    hot = [(iota_e == idx) for idx in ids]
    multi = (hot[0] | hot[1] | hot[2] | hot[3])
    multi_f = multi.astype(F32)
    r = lax.broadcasted_iota(I32, (ts, ts), 0)
    c = lax.broadcasted_iota(I32, (ts, ts), 1)
    upper = (r < c).astype(BF16)
    before = jnp.dot(multi_f.astype(BF16), upper, preferred_element_type=F32)
    before = before + carry_ref[:, 0:1]
    ranks = [jnp.sum(jnp.where(hk, before, 0.0), axis=0, keepdims=True) for hk in hot]
    rank_ref[...] = jnp.concatenate(ranks, axis=0).astype(I32)
    carry_ref[...] = carry_ref[...] + jnp.sum(multi_f, axis=1, keepdims=True)
    sizes_ref[...] = carry_ref[...]


def _route_operands(g, w_router, b_router):
    d = g.shape[0]
    return (g.reshape(1, d), w_router.T,
            jnp.broadcast_to(b_router.reshape(N_EXPERTS, 1), (N_EXPERTS, 128)))


def _route_specs(t, ts, d, tile_of):
    const = lambda *idx: (0, 0)
    in_specs = [pl.BlockSpec((1, d), const), pl.BlockSpec((N_EXPERTS, d), const),
                pl.BlockSpec((N_EXPERTS, 128), const)]
    out_shape = (
        jax.ShapeDtypeStruct((t * ROW_TILE_SUB, 128), F32),
        jax.ShapeDtypeStruct((TOP_K, t), I32),
        jax.ShapeDtypeStruct((TOP_K, t), F32),
        jax.ShapeDtypeStruct((TOP_K, t), I32),
        jax.ShapeDtypeStruct((N_EXPERTS, 128), F32),
    )
    tok = lambda: pl.BlockSpec((TOP_K, ts), lambda *idx: (0, tile_of(*idx)))
    out_specs = (
        pl.BlockSpec((ts * ROW_TILE_SUB, 128), lambda *idx: (tile_of(*idx), 0)),
        tok(), tok(), tok(),
        pl.BlockSpec((N_EXPERTS, 128), const),
    )
    return in_specs, out_shape, out_specs


def _dispatch_kernel(zrow_ref, n_used_ref, dest_ref, h_ref, xs_ref, zbuf, sem_z, sem,
                     *, td, blk, nblk):
    i = pl.program_id(0)
    sub = ROW_TILE_SUB

    def zero_block(row):
        row = pl.multiple_of(row * sub, blk * sub)
        return pltpu.make_async_copy(zbuf, xs_ref.at[pl.ds(row, blk * sub), :], sem_z)

    @pl.when(i == 0)
    def _():
        zbuf[...] = jnp.zeros_like(zbuf)
        for e in range(N_EXPERTS):
            @pl.when(zrow_ref[e] >= 0)
            def _():
                zero_block(zrow_ref[e]).start()

        def start_tail(b, carry):
            zero_block(b * blk).start()
            return carry

        def wait_tail(b, carry):
            zero_block(b * blk).wait()
            return carry

        lax.fori_loop(n_used_ref[0], nblk, start_tail, 0)
        for e in range(N_EXPERTS):
            @pl.when(zrow_ref[e] >= 0)
            def _():
                zero_block(zrow_ref[e]).wait()
        lax.fori_loop(n_used_ref[0], nblk, wait_tail, 0)

    def issue(t, carry):
        src = h_ref.at[pl.ds(pl.multiple_of(t * sub, sub), sub), :]
        for k in range(TOP_K):
            d = dest_ref[0, 0, k * td + t]
            pltpu.make_async_copy(src, xs_ref.at[pl.ds(pl.multiple_of(d * sub, sub), sub), :],
                                  sem).start(priority=k % 2)
        return carry

    lax.fori_loop(0, td, issue, 0, unroll=ISSUE_UNROLL)
    for k in range(TOP_K):
        pltpu.make_async_copy(h_ref, xs_ref.at[pl.ds(0, td * sub), :], sem).wait()


def _dispatch(h_tiles, dest, zrow, n_used, n_rows):
    sub = ROW_TILE_SUB
    t = h_tiles.shape[0] // sub
    td = min(DISPATCH_TILE, t)
    nt = t // td
    dest_tiles = dest.reshape(TOP_K, nt, td).transpose(1, 0, 2).reshape(nt, 1, TOP_K * td)
    kern = functools.partial(_dispatch_kernel, td=td, blk=MOE_BLOCK, nblk=n_rows // MOE_BLOCK)
    return pl.pallas_call(
        kern,
        out_shape=jax.ShapeDtypeStruct((n_rows * sub, 128), F32),
        grid_spec=pltpu.PrefetchScalarGridSpec(
            num_scalar_prefetch=2,
            grid=(nt,),
            in_specs=[
                pl.BlockSpec((1, 1, TOP_K * td), lambda i, z, nu: (i, 0, 0),
                             memory_space=pltpu.SMEM),
                pl.BlockSpec((td * sub, 128), lambda i, z, nu: (i, 0)),
            ],
            out_specs=pl.BlockSpec(memory_space=pl.ANY),
            scratch_shapes=[
                pltpu.VMEM((MOE_BLOCK * sub, 128), F32),
                pltpu.SemaphoreType.DMA(()),
                pltpu.SemaphoreType.DMA(()),
            ],
        ),
        compiler_params=pltpu.CompilerParams(
            dimension_semantics=("arbitrary",), vmem_limit_bytes=VMEM_LIMIT_BYTES),
        name="moe_dispatch",
    )(zrow, n_used, dest_tiles, h_tiles)


def _expert_kernel(blk_e_ref, n_used_ref, nxt_e_ref, n_valid_ref, x_ref, w1_hbm, b1_ref, w2_hbm,
                   b2_ref, o_ref, w1s_ref, w2s_ref, w1b_ref, w2b_ref, wsem, *, layer):
    step = pl.program_id(0)
    sub_rows = MOE_BLOCK * ROW_TILE_SUB

    def weight_copies(expert):
        return (pltpu.make_async_copy(w1_hbm.at[layer, expert], w1s_ref, wsem.at[0]),
                pltpu.make_async_copy(w2_hbm.at[layer, expert], w2s_ref, wsem.at[1]))

    @pl.when(step == 0)
    def _():
        for cp in weight_copies(blk_e_ref[0]):
            cp.start()

    for sub in range(EXPERT_STEP_BLOCKS):
        i = step * EXPERT_STEP_BLOCKS + sub
        e = blk_e_ref[i]
        prev = blk_e_ref[jnp.maximum(i - 1, 0)]
        fresh = (i == 0) | (e != prev)
        live = i < n_used_ref[0]

        @pl.when(live & fresh)
        def _(i=i, e=e):
            for cp in weight_copies(e):
                cp.wait()
            w1b_ref[...] = w1s_ref[...].astype(BF16)
            w2b_ref[...] = w2s_ref[...].astype(BF16)
            nxt = nxt_e_ref[i]

            @pl.when(nxt >= 0)
            def _():
                for cp in weight_copies(nxt):
                    cp.start()

        def mlp(rows, sub=sub, e=e):
            r0 = sub * MOE_BLOCK
            x = _load_row_tiles(x_ref, rows, start=r0).astype(BF16)
            a = jnp.dot(x, w1b_ref[...], preferred_element_type=F32) + b1_ref[e]
            glu = jnp.minimum(a[:, :D_FF], SWIGLU_LIMIT)
            lin = jnp.clip(a[:, D_FF:], -SWIGLU_LIMIT, SWIGLU_LIMIT)
            act = glu * jax.nn.sigmoid(SWIGLU_ALPHA * glu) * (lin + 1.0)
            y = jnp.dot(act.astype(BF16), w2b_ref[...], preferred_element_type=F32) + b2_ref[e]
            _store_row_tiles(o_ref, y, rows, start=r0)
            if rows < MOE_BLOCK:
                o_ref[(r0 + rows) * ROW_TILE_SUB:(r0 + MOE_BLOCK) * ROW_TILE_SUB, :] = jnp.zeros(
                    ((MOE_BLOCK - rows) * ROW_TILE_SUB, 128), F32)

        half = MOE_BLOCK // 2
        short = n_valid_ref[i] <= half

        @pl.when(live & jnp.logical_not(short))
        def _(mlp=mlp):
            mlp(MOE_BLOCK)

        @pl.when(live & short)
        def _(mlp=mlp):
            mlp(half)

        @pl.when(jnp.logical_not(live))
        def _(sub=sub):
            o_ref[sub * sub_rows:(sub + 1) * sub_rows, :] = jnp.zeros((sub_rows, 128), F32)


def _experts(xs, blk_e, n_used, nxt_e, n_valid, layer, w1, b1, w2, b2):
    sub = ROW_TILE_SUB
    n_rows = xs.shape[0] // sub
    d = w2.shape[-1]
    nblk = n_rows // MOE_BLOCK
    depth = w1.shape[0]
    grp = EXPERT_STEP_BLOCKS
    assert nblk % grp == 0
    row = lambda i, be, nu, nx, nv: (jnp.minimum(i, (nu[0] - 1) // grp), 0)
    out_row = lambda i, be, nu, nx, nv: (i, 0)
    bias = lambda i, be, nu, nx, nv: (layer, 0, 0, 0)
    return pl.pallas_call(
        functools.partial(_expert_kernel, layer=layer),
        out_shape=jax.ShapeDtypeStruct((n_rows * sub, 128), F32),
        grid_spec=pltpu.PrefetchScalarGridSpec(
            num_scalar_prefetch=4,
            grid=(nblk // grp,),
            in_specs=[
                pl.BlockSpec((grp * MOE_BLOCK * sub, 128), row),
                pl.BlockSpec(memory_space=pl.ANY),
                pl.BlockSpec((None, N_EXPERTS, 1, 2 * D_FF), bias),
                pl.BlockSpec(memory_space=pl.ANY),
                pl.BlockSpec((None, N_EXPERTS, 1, d), bias),
            ],
            out_specs=pl.BlockSpec((grp * MOE_BLOCK * sub, 128), out_row),
            scratch_shapes=[
                pltpu.VMEM((d, 2 * D_FF), F32),
                pltpu.VMEM((D_FF, d), F32),
                pltpu.VMEM((d, 2 * D_FF), BF16),
                pltpu.VMEM((D_FF, d), BF16),
                pltpu.SemaphoreType.DMA((2,)),
            ],
        ),
        compiler_params=pltpu.CompilerParams(
            dimension_semantics=("arbitrary",), vmem_limit_bytes=VMEM_LIMIT_BYTES),
        name="moe_experts",
    )(blk_e, n_used, nxt_e, n_valid, xs, w1, b1.reshape(depth, N_EXPERTS, 1, 2 * D_FF), w2,
      b2.reshape(depth, N_EXPERTS, 1, d))


def _combine_kernel(dcur_ref, dnext_ref, x_ref, g_ref, ys_ref, gn_ref, o_ref, buf, sem,
                    *, tc, n_tiles, final_norm):
    i = pl.program_id(0)
    sub = ROW_TILE_SUB

    def issue(dref, slot):
        def body(t, carry):
            for k in range(TOP_K):
                d = dref[0, 0, k * tc + t]
                pltpu.make_async_copy(
                    ys_ref.at[pl.ds(pl.multiple_of(d * sub, sub), sub), :],
                    buf.at[slot, k, pl.ds(pl.multiple_of(t * sub, sub), sub), :],
                    sem.at[slot]).start(priority=k % 2)
            return carry
        lax.fori_loop(0, tc, body, 0, unroll=ISSUE_UNROLL)

    @pl.when(i == 0)
    def _():
        issue(dcur_ref, 0)

    @pl.when(i + 1 < n_tiles)
    def _():
        issue(dnext_ref, (i + 1) % 2)

    slot = i % 2
    for k in range(TOP_K):
        pltpu.make_async_copy(ys_ref.at[pl.ds(0, tc * sub), :], buf.at[slot, k],
                              sem.at[slot]).wait()
    acc = x_ref[...]
    gates = g_ref[...]
    for k in range(TOP_K):
        acc = acc + gates[:, k:k + 1] * _load_row_tiles(buf, tc, lead=(slot, k))
    if final_norm:
        acc = _rms(acc, gn_ref[...])
    o_ref[...] = acc


def _combine(x2d, gates_t, dest, ys, g_norm, final_norm):
    t, d = x2d.shape
    tc = min(COMBINE_TILE, t)
    nt = t // tc
    dest_tiles = dest.reshape(TOP_K, nt, tc).transpose(1, 0, 2).reshape(nt, 1, TOP_K * tc)
    kern = functools.partial(_combine_kernel, tc=tc, n_tiles=nt, final_norm=final_norm)
    return pl.pallas_call(
        kern,
        out_shape=jax.ShapeDtypeStruct((t, d), F32),
        grid=(nt,),
        in_specs=[
            pl.BlockSpec((1, 1, TOP_K * tc), lambda i: (i, 0, 0), memory_space=pltpu.SMEM),
            pl.BlockSpec((1, 1, TOP_K * tc), lambda i: (jnp.minimum(i + 1, nt - 1), 0, 0),
                         memory_space=pltpu.SMEM),
            pl.BlockSpec((tc, d), lambda i: (i, 0)),
            pl.BlockSpec((tc, TOP_K), lambda i: (i, 0)),
            pl.BlockSpec(memory_space=pl.ANY),
            pl.BlockSpec((1, d), lambda i: (0, 0)),
        ],
        out_specs=pl.BlockSpec((tc, d), lambda i: (i, 0)),
        scratch_shapes=[
            pltpu.VMEM((2, TOP_K, tc * ROW_TILE_SUB, 128), F32),
            pltpu.SemaphoreType.DMA((2,)),
        ],
        compiler_params=pltpu.CompilerParams(
            dimension_semantics=("arbitrary",), vmem_limit_bytes=VMEM_LIMIT_BYTES),
        name="moe_combine_final" if final_norm else "moe_combine",
    )(dest_tiles, dest_tiles, x2d, gates_t.T, ys, g_norm.reshape(1, d))


def _moe_experts(routed, layer, w1, b1, w2, b2):
    h, ids, gates, rank, sizes = routed
    t = ids.shape[1]
    sizes = sizes[:, 0].astype(I32)
    padded = ((sizes + MOE_BLOCK - 1) // MOE_BLOCK) * MOE_BLOCK
    pend = jnp.cumsum(padded)
    pstart = pend - padded
    onehot = ids[:, :, None] == jnp.arange(N_EXPERTS, dtype=I32)
    dest = rank + jnp.sum(jnp.where(onehot, pstart, 0), axis=-1)
    n_rows = t * TOP_K + N_EXPERTS * MOE_BLOCK
    nblk = n_rows // MOE_BLOCK
    blk_row = jnp.arange(nblk, dtype=I32) * MOE_BLOCK
    blk_e = jnp.minimum(jnp.sum((pend[None, :] <= blk_row[:, None]).astype(I32), axis=1),
                        N_EXPERTS - 1)
    n_used = (pend[-1:] // MOE_BLOCK).astype(I32)
    zrow = jnp.where(padded > 0, pend - MOE_BLOCK, -1).astype(I32)
    xs = _dispatch(h, dest, zrow, n_used, n_rows)
    blk_id = jnp.arange(nblk, dtype=I32)
    later = ((blk_id[None, :] > blk_id[:, None]) & (blk_e[None, :] != blk_e[:, None])
             & (blk_id[None, :] < n_used[0]))
    nxt_blk = jnp.min(jnp.where(later, blk_id[None, :], nblk), axis=1)
    nxt_e = jnp.sum(jnp.where(blk_id[None, :] == nxt_blk[:, None], blk_e[None, :] + 1, 0),
                    axis=1).astype(I32) - 1
    own = blk_e[:, None] == jnp.arange(N_EXPERTS, dtype=I32)[None, :]
    row_end = jnp.sum(jnp.where(own, (pstart + sizes)[None, :], 0), axis=1)
    n_valid = jnp.clip(row_end - blk_row, 0, MOE_BLOCK).astype(I32)
    ys = _experts(xs, blk_e, n_used, nxt_e, n_valid, layer, w1, b1, w2, b2)
    return ys, dest


def _qkv_kernel(x_ref, g_ref, w_ref, b_ref, q_ref, k_ref, v_ref):
    h = _rms(x_ref[...], g_ref[...])
    qkv = jnp.dot(h.astype(BF16), w_ref[...], preferred_element_type=F32) + b_ref[...]
    nq = N_HEADS * HEAD_DIM
    nk = 2 * N_KV_HEADS * HEAD_DIM
    q_ref[...] = (qkv[:, :nq] * (HEAD_DIM ** -0.5)).astype(BF16)
    k_ref[...] = qkv[:, nq:nq + nk].astype(BF16)
    v_ref[...] = qkv[:, nq + nk:].astype(BF16)


def _qkv(x2d, g, w_qkv, b_qkv):
    t, d = x2d.shape
    ts = min(ROW_TILE, t)
    nq = N_HEADS * HEAD_DIM
    kv = N_KV_HEADS * HEAD_DIM

    def dup_heads(m):
        parts = []
        for hd in range(N_KV_HEADS):
            sl = m[..., hd * HEAD_DIM:(hd + 1) * HEAD_DIM]
            parts += [sl, sl]
        return jnp.concatenate(parts, axis=-1)

    w = jnp.concatenate([w_qkv[:, :nq], dup_heads(w_qkv[:, nq:nq + kv]),
                         dup_heads(w_qkv[:, nq + kv:])], axis=1).astype(BF16)
    b = jnp.concatenate([b_qkv[:nq], dup_heads(b_qkv[nq:nq + kv]),
                         dup_heads(b_qkv[nq + kv:])]).reshape(1, -1)
    n_out = w.shape[1]
    row = lambda width: pl.BlockSpec((ts, width), lambda i: (i, 0))
    return pl.pallas_call(
        _qkv_kernel,
        out_shape=(
            jax.ShapeDtypeStruct((t, nq), BF16),
            jax.ShapeDtypeStruct((t, 2 * kv), BF16),
            jax.ShapeDtypeStruct((t, 2 * kv), BF16),
        ),
        grid=(t // ts,),
        in_specs=[
            row(d),
            pl.BlockSpec((1, d), lambda i: (0, 0)),
            pl.BlockSpec((d, n_out), lambda i: (0, 0)),
            pl.BlockSpec((1, n_out), lambda i: (0, 0)),
        ],
        out_specs=(row(nq), row(2 * kv), row(2 * kv)),
        compiler_params=pltpu.CompilerParams(
            dimension_semantics=("arbitrary",), vmem_limit_bytes=VMEM_LIMIT_BYTES),
        name="attn_qkv",
    )(x2d, g.reshape(1, d), w, b)


def _attn_kernel(sinks_ref, q_ref, kp_ref, kc_ref, vp_ref, vc_ref, bias_ref, o_ref):
    n = pl.program_id(1)
    rows = SLABS_PER_GROUP * ATT_BLOCK
    keys = 2 * ATT_BLOCK
    kall = jnp.concatenate([kp_ref[...], kc_ref[...]], axis=0)
    vall = jnp.concatenate([vp_ref[...], vc_ref[...]], axis=0)
    j = lax.broadcasted_iota(I32, (keys, rows), 0)
    r = lax.broadcasted_iota(I32, (keys, rows), 1) & (ATT_BLOCK - 1)
    in_window = (j > r) & (j <= r + WINDOW)
    lane_k = lax.broadcasted_iota(I32, (keys, 128), 1)
    halves = [lane_k < HEAD_DIM, lane_k >= HEAD_DIM]
    sub_o = lax.broadcasted_iota(I32, (128, rows), 0)
    zero = jnp.zeros((), BF16)
    for blk in range(ATT_STEP_BLOCKS):
        q = q_ref[blk * ATT_BLOCK:(blk + 1) * ATT_BLOCK, :]
        kcat = kall[blk * ATT_BLOCK:blk * ATT_BLOCK + keys]
        vcat = vall[blk * ATT_BLOCK:blk * ATT_BLOCK + keys]
        if blk == 0:
            mask = in_window & ((n > 0) | (j >= ATT_BLOCK))
        else:
            mask = in_window
        for g in range(N_KV_HEADS):
            s0 = g * SLABS_PER_GROUP
            qg = jnp.concatenate([q[:, (s0 + s) * 128:(s0 + s + 1) * 128]
                                  for s in range(SLABS_PER_GROUP)], axis=0)
            kg = kcat[:, g * 128:(g + 1) * 128]
            vg = vcat[:, g * 128:(g + 1) * 128]
            pts, invs = [], []
            for p in range(HEADS_PER_SLAB):
                kp = jnp.where(halves[p], kg, zero)
                sc = lax.dot_general(kp, qg, (((1,), (1,)), ((), ())),
                                     preferred_element_type=F32)
                sc = jnp.where(mask, sc + bias_ref[g, p], NEG_INF)
                sink = jnp.concatenate(
                    [jnp.full((1, ATT_BLOCK),
                              sinks_ref[g * GQA_GROUP + s * HEADS_PER_SLAB + p], F32)
                     for s in range(SLABS_PER_GROUP)], axis=1)
                m = jnp.maximum(jnp.max(sc, axis=0, keepdims=True), sink)
                pe = jnp.exp(sc - m)
                den = jnp.sum(pe, axis=0, keepdims=True) + jnp.exp(sink - m)
                pts.append(pe.astype(BF16))
                invs.append(1.0 / den)
            pcat = jnp.concatenate(pts, axis=0)
            vblk = jnp.concatenate([jnp.where(halves[p], vg, zero)
                                    for p in range(HEADS_PER_SLAB)], axis=0)
            og = lax.dot_general(vblk, pcat, (((0,), (0,)), ((), ())),
                                 preferred_element_type=F32)
            og = og * jnp.where(sub_o < HEAD_DIM, invs[0], invs[1])
            og = og.T
            for s in range(SLABS_PER_GROUP):
                o_ref[blk * ATT_BLOCK:(blk + 1) * ATT_BLOCK, (s0 + s) * 128:(s0 + s + 1) * 128] = (
                    og[s * ATT_BLOCK:(s + 1) * ATT_BLOCK].astype(BF16))


def _t5_bias_table(rel_bias):
    r = np.arange(ATT_BLOCK)[:, None]
    j = np.arange(2 * ATT_BLOCK)[None, :]
    rel = np.clip(ATT_BLOCK + r - j, 0, WINDOW - 1)
    max_exact = N_BUCKETS // 2
    nf = jnp.maximum(rel, max_exact).astype(F32)
    large = max_exact + (jnp.log(nf / max_exact) / math.log(MAX_DISTANCE / max_exact)
                         * (N_BUCKETS - max_exact)).astype(I32)
    large = jnp.minimum(large, N_BUCKETS - 1)
    bucket = jnp.where(rel < max_exact, rel, large)
    pick = (bucket[None] == jnp.arange(N_BUCKETS, dtype=I32)[:, None, None])
    bias = jnp.sum(jnp.where(pick[:, None], rel_bias.astype(F32)[:, :, None, None], 0.0),
                   axis=0)
    bias = bias.reshape(N_KV_HEADS, SLABS_PER_GROUP, HEADS_PER_SLAB, ATT_BLOCK, 2 * ATT_BLOCK)
    bias = jnp.transpose(bias, (0, 2, 4, 1, 3))
    return bias.reshape(N_KV_HEADS, HEADS_PER_SLAB, 2 * ATT_BLOCK, SLABS_PER_GROUP * ATT_BLOCK)


def _attention(q, kk, vv, sinks, rel_bias, bsz, seq):
    nq = N_HEADS * HEAD_DIM
    kvw = kk.shape[-1]
    step = ATT_STEP_BLOCKS * ATT_BLOCK
    q3 = q.reshape(bsz, seq, nq)
    k3 = kk.reshape(bsz, seq, kvw)
    v3 = vv.reshape(bsz, seq, kvw)
    bias = _t5_bias_table(rel_bias)
    prev = lambda bi, n: (bi, jnp.maximum(n * ATT_STEP_BLOCKS - 1, 0), 0)
    cur = lambda bi, n: (bi, n, 0)
    out = pl.pallas_call(
        _attn_kernel,
        out_shape=jax.ShapeDtypeStruct((bsz, seq, nq), BF16),
        grid=(bsz, seq // step),
        in_specs=[
            pl.BlockSpec(memory_space=pltpu.SMEM),
            pl.BlockSpec((None, step, nq), cur),
            pl.BlockSpec((None, ATT_BLOCK, kvw), prev),
            pl.BlockSpec((None, step, kvw), cur),
            pl.BlockSpec((None, ATT_BLOCK, kvw), prev),
            pl.BlockSpec((None, step, kvw), cur),
            pl.BlockSpec(bias.shape, lambda bi, n: (0, 0, 0, 0)),
        ],
        out_specs=pl.BlockSpec((None, step, nq), cur),
        compiler_params=pltpu.CompilerParams(
            dimension_semantics=("arbitrary", "arbitrary"),
            vmem_limit_bytes=VMEM_LIMIT_BYTES),
        name="attn_core",
    )(sinks.astype(F32), q3, k3, k3, v3, v3, bias)
    return out.reshape(bsz * seq, nq)


def _oproj_kernel(x_ref, o_ref, w_ref, b_ref, gr_ref, wr_ref, br_ref,
                  y_ref, h_ref, ids_ref, gates_ref, rank_ref, sizes_ref, carry_ref, *, ts):
    x_new = x_ref[...] + jnp.dot(o_ref[...], w_ref[...],
                                 preferred_element_type=F32) + b_ref[...]
    y_ref[...] = x_new
    _route_tile(x_new, pl.program_id(0) == 0, gr_ref, wr_ref, br_ref,
                h_ref, ids_ref, gates_ref, rank_ref, sizes_ref, carry_ref, ts)


def _out_proj(x2d, o, w_o, b_o, g_ffn, w_router, b_router):
    t, d = x2d.shape
    ts = min(ROW_TILE, t)
    nq = o.shape[1]
    r_in, r_shape, r_out = _route_specs(t, ts, d, lambda i: i)
    outs = pl.pallas_call(
        functools.partial(_oproj_kernel, ts=ts),
        out_shape=(jax.ShapeDtypeStruct((t, d), F32),) + r_shape,
        grid=(t // ts,),
        in_specs=[
            pl.BlockSpec((ts, d), lambda i: (i, 0)),
            pl.BlockSpec((ts, nq), lambda i: (i, 0)),
            pl.BlockSpec((nq, d), lambda i: (0, 0)),
            pl.BlockSpec((1, d), lambda i: (0, 0)),
        ] + r_in,
        out_specs=(pl.BlockSpec((ts, d), lambda i: (i, 0)),) + r_out,
        scratch_shapes=[pltpu.VMEM((N_EXPERTS, 128), F32)],
        compiler_params=pltpu.CompilerParams(
            dimension_semantics=("arbitrary",), vmem_limit_bytes=VMEM_LIMIT_BYTES),
        name="attn_out_proj_route",
    )(x2d, o, w_o.astype(BF16), b_o.reshape(1, d),
      *_route_operands(g_ffn, w_router, b_router))
    return outs[0], outs[1:]


def kernel(x, norm_mix, norm_ffn, norm_final, pool_w, pool_b, pool_scale, w_qkv, b_qkv,
           sinks, w_o, b_o, rel_bias, w_router, b_router, w1, b1, w2, b2):
    bsz, seq, d = x.shape
    t = bsz * seq
    x, routed = _pool_layer(x, norm_mix[0], pool_w[0], pool_b[0], pool_scale[0],
                            norm_ffn[0], w_router[0], b_router[0])
    x = x.reshape(t, d)
    ys, dest = _moe_experts(routed, 0, w1, b1, w2, b2)
    x = _combine(x, routed[2], dest, ys, norm_final, final_norm=False)
    q, kk, vv = _qkv(x, norm_mix[1], w_qkv[0], b_qkv[0])
    o = _attention(q, kk, vv, sinks[0], rel_bias, bsz, seq)
    x, routed = _out_proj(x, o, w_o[0], b_o[0], norm_ffn[1], w_router[1], b_router[1])
    ys, dest = _moe_experts(routed, 1, w1, b1, w2, b2)
    x = _combine(x, routed[2], dest, ys, norm_final, final_norm=True)
    return x.reshape(bsz, seq, d)
```

```python
import functools
import math

import jax
import jax.numpy as jnp
import numpy as np
from jax import lax
from jax.experimental import pallas as pl
from jax.experimental.pallas import tpu as pltpu

F32 = jnp.float32
BF16 = jnp.bfloat16
I32 = jnp.int32

D_MODEL = 1024
POOL_WINDOWS = (2, 4, 8, 16)
POOL_GROUP_CH = D_MODEL // len(POOL_WINDOWS)
POOL_HALO = 16
HEAD_DIM = 64
N_HEADS = 16
N_KV_HEADS = 2
GQA_GROUP = N_HEADS // N_KV_HEADS
HEADS_PER_SLAB = 128 // HEAD_DIM
SLABS_PER_GROUP = GQA_GROUP // HEADS_PER_SLAB
ATT_BLOCK = 128
ATT_STEP_BLOCKS = 2
WINDOW = 128
N_BUCKETS = 32
MAX_DISTANCE = 128
N_EXPERTS = 32
TOP_K = 4
D_FF = D_MODEL
SWIGLU_ALPHA = 1.702
SWIGLU_LIMIT = 7.0
RMS_EPS = 1e-5
NEG_INF = -1e30

ROW_TILE = 512
MOE_BLOCK = 512
EXPERT_STEP_BLOCKS = 2
DISPATCH_TILE = 2048
COMBINE_TILE = 256
ISSUE_UNROLL = 4
VMEM_LIMIT_BYTES = 56 * 1024 * 1024


def _rms(x, g):
    return x * lax.rsqrt(jnp.mean(x * x, axis=-1, keepdims=True) + RMS_EPS) * g


ROW_TILE_SUB = D_MODEL // 128


def _load_row_tiles(ref, n, lead=(), start=0):
    return jnp.concatenate(
        [ref[lead + (pl.ds(start * ROW_TILE_SUB + j, n, stride=ROW_TILE_SUB), slice(None))]
         for j in range(ROW_TILE_SUB)], axis=1)


def _store_row_tiles(ref, val, n, start=0):
    for j in range(ROW_TILE_SUB):
        ref[pl.ds(start * ROW_TILE_SUB + j, n, stride=ROW_TILE_SUB), :] = (
            val[:, j * 128:(j + 1) * 128])


POOL_PAD = 8


def _pool_kernel(x_ref, xh_ref, g_ref, w_ref, b_ref, s_ref, gr_ref, wr_ref, br_ref,
                 o_ref, h_ref, ids_ref, gates_ref, rank_ref, sizes_ref,
                 buf_ref, s1_ref, s2_ref, s3_ref, carry_ref, *, ts):
    i = pl.program_id(1)
    c = POOL_GROUP_CH
    h0 = POOL_PAD
    t0 = POOL_PAD + POOL_HALO
    r1 = t0 + ts
    x = x_ref[...]
    g = g_ref[...]
    h = _rms(x, g)
    hh = _rms(xh_ref[...], g)
    hh = jnp.where(i == 0, 0.0, hh)
    buf_ref[0:h0, :] = jnp.zeros((h0, D_MODEL), F32)
    buf_ref[h0:t0, :] = hh
    buf_ref[t0:r1, :] = h
    sum0 = h[:, 0:c] + buf_ref[t0 - 1:r1 - 1, 0:c]
    l1 = buf_ref[h0:r1, c:] + buf_ref[h0 - 1:r1 - 1, c:]
    s1_ref[0:h0, :] = jnp.zeros((h0, 3 * c), F32)
    s1_ref[h0:r1, :] = l1
    sum1 = s1_ref[t0:r1, 0:c] + s1_ref[t0 - 2:r1 - 2, 0:c]
    l2 = s1_ref[h0:r1, c:] + s1_ref[h0 - 2:r1 - 2, c:]
    s2_ref[0:h0, :] = jnp.zeros((h0, 2 * c), F32)
    s2_ref[h0:r1, :] = l2
    sum2 = s2_ref[t0:r1, 0:c] + s2_ref[t0 - 4:r1 - 4, 0:c]
    s3_ref[h0:r1, :] = s2_ref[h0:r1, c:] + s2_ref[h0 - 4:r1 - 4, c:]
    sum3 = s3_ref[t0:r1, :] + s3_ref[t0 - 8:r1 - 8, :]
    t = i * ts + lax.broadcasted_iota(I32, (ts, 1), 0)
    outs = []
    for gi, (win, acc) in enumerate(zip(POOL_WINDOWS, (sum0, sum1, sum2, sum3))):
        cnt = jnp.minimum(t + 1, win).astype(F32)
        dlt = acc / cnt - h[:, gi * c:(gi + 1) * c]
        outs.append(jnp.dot(dlt.astype(BF16), w_ref[gi], preferred_element_type=F32))
    y = jnp.concatenate(outs, axis=1)
    x_new = x + (y + b_ref[...]) * s_ref[...]
    o_ref[...] = x_new
    first = (pl.program_id(0) == 0) & (i == 0)
    _route_tile(x_new, first, gr_ref, wr_ref, br_ref,
                h_ref, ids_ref, gates_ref, rank_ref, sizes_ref, carry_ref, ts)


def _pool_layer(x, g, w, b, s, g_ffn, w_router, b_router):
    bsz, seq, d = x.shape
    ts = min(ROW_TILE, seq)
    tiles = seq // ts
    kern = functools.partial(_pool_kernel, ts=ts)
    rows = POOL_PAD + POOL_HALO + ts
    vec = lambda: pl.BlockSpec((1, d), lambda bi, i: (0, 0))
    r_in, r_shape, r_out = _route_specs(bsz * seq, ts, d, lambda bi, i: bi * tiles + i)
    outs = pl.pallas_call(
        kern,
        out_shape=(jax.ShapeDtypeStruct(x.shape, F32),) + r_shape,
        grid=(bsz, tiles),
        in_specs=[
            pl.BlockSpec((None, ts, d), lambda bi, i: (bi, i, 0)),
            pl.BlockSpec((None, POOL_HALO, d),
                         lambda bi, i: (bi, jnp.maximum(i * (ts // POOL_HALO) - 1, 0), 0)),
            vec(),
            pl.BlockSpec(w.shape, lambda bi, i: (0, 0, 0)),
            vec(),
            vec(),
        ] + r_in,
        out_specs=(pl.BlockSpec((None, ts, d), lambda bi, i: (bi, i, 0)),) + r_out,
        scratch_shapes=[pltpu.VMEM((rows, d), F32),
                        pltpu.VMEM((rows, 3 * POOL_GROUP_CH), F32),
                        pltpu.VMEM((rows, 2 * POOL_GROUP_CH), F32),
                        pltpu.VMEM((rows, POOL_GROUP_CH), F32),
                        pltpu.VMEM((N_EXPERTS, 128), F32)],
        compiler_params=pltpu.CompilerParams(
            dimension_semantics=("arbitrary", "arbitrary"),
            vmem_limit_bytes=VMEM_LIMIT_BYTES),
        name="pool_route",
    )(x, x, g.reshape(1, d), w.astype(BF16), b.reshape(1, d), s.reshape(1, d),
      *_route_operands(g_ffn, w_router, b_router))
    return outs[0], outs[1:]


def _route_tile(x, first, g_ref, wr_ref, br_ref,
                h_ref, ids_ref, gates_ref, rank_ref, sizes_ref, carry_ref, ts):
    @pl.when(first)
    def _():
        carry_ref[...] = jnp.zeros_like(carry_ref)

    h = _rms(x, g_ref[...])
    _store_row_tiles(h_ref, h, ts)
    h_hi = h.astype(BF16)
    h_lo = (h - h_hi.astype(F32)).astype(BF16)
    w = wr_ref[...]
    w_hi = w.astype(BF16)
    w_lo = (w - w_hi.astype(F32)).astype(BF16)
    dn = (((1,), (1,)), ((), ()))
    lg = (lax.dot_general(w_hi, h_hi, dn, preferred_element_type=F32)
          + lax.dot_general(w_lo, h_hi, dn, preferred_element_type=F32)
          + lax.dot_general(w_hi, h_lo, dn, preferred_element_type=F32))
    lg = lg + br_ref[:, 0:1]

    iota_e = lax.broadcasted_iota(I32, (N_EXPERTS, ts), 0)
    vals, ids = [], []
    for _ in range(TOP_K):
        m = jnp.max(lg, axis=0, keepdims=True)
        idx = jnp.min(jnp.where(lg == m, iota_e, N_EXPERTS), axis=0, keepdims=True)
        vals.append(m)
        ids.append(idx)
        lg = jnp.where(iota_e == idx, -jnp.inf, lg)
    ex = [jnp.exp(v - vals[0]) for v in vals]
    den = ex[0] + ex[1] + ex[2] + ex[3]
    gates_ref[...] = jnp.concatenate([e / den for e in ex], axis=0)
    ids_ref[...] = jnp.concatenate(ids, axis=0)

    hot = [(iota_e == idx) for idx in ids]
    multi = (hot[0] | hot[1] | hot[2] | hot[3])
    multi_f = multi.astype(F32)
    r = lax.broadcasted_iota(I32, (ts, ts), 0)
    c = lax.broadcasted_iota(I32, (ts, ts), 1)
    upper = (r < c).astype(BF16)
    before = jnp.dot(multi_f.astype(BF16), upper, preferred_element_type=F32)
    before = before + carry_ref[:, 0:1]
    ranks = [jnp.sum(jnp.where(hk, before, 0.0), axis=0, keepdims=True) for hk in hot]
    rank_ref[...] = jnp.concatenate(ranks, axis=0).astype(I32)
    carry_ref[...] = carry_ref[...] + jnp.sum(multi_f, axis=1, keepdims=True)
    sizes_ref[...] = carry_ref[...]


def _route_operands(g, w_router, b_router):
    d = g.shape[0]
    return (g.reshape(1, d), w_router.T,
            jnp.broadcast_to(b_router.reshape(N_EXPERTS, 1), (N_EXPERTS, 128)))


def _route_specs(t, ts, d, tile_of):
    const = lambda *idx: (0, 0)
    in_specs = [pl.BlockSpec((1, d), const), pl.BlockSpec((N_EXPERTS, d), const),
                pl.BlockSpec((N_EXPERTS, 128), const)]
    out_shape = (
        jax.ShapeDtypeStruct((t * ROW_TILE_SUB, 128), F32),
        jax.ShapeDtypeStruct((TOP_K, t), I32),
        jax.ShapeDtypeStruct((TOP_K, t), F32),
        jax.ShapeDtypeStruct((TOP_K, t), I32),
        jax.ShapeDtypeStruct((N_EXPERTS, 128), F32),
    )
    tok = lambda: pl.BlockSpec((TOP_K, ts), lambda *idx: (0, tile_of(*idx)))
    out_specs = (
        pl.BlockSpec((ts * ROW_TILE_SUB, 128), lambda *idx: (tile_of(*idx), 0)),
        tok(), tok(), tok(),
        pl.BlockSpec((N_EXPERTS, 128), const),
    )
    return in_specs, out_shape, out_specs


def _dispatch_kernel(zrow_ref, n_used_ref, dest_ref, h_ref, *rest, td, blk, nblk, zero_fill):
    xs_ref, zbuf, sem_z, sem = rest if zero_fill else rest[1:]
    i = pl.program_id(0)
    sub = ROW_TILE_SUB

    def zero_block(row):
        row = pl.multiple_of(row * sub, blk * sub)
        return pltpu.make_async_copy(zbuf, xs_ref.at[pl.ds(row, blk * sub), :], sem_z)

    def clear_padding():
        zbuf[...] = jnp.zeros_like(zbuf)
        for e in range(N_EXPERTS):
            @pl.when(zrow_ref[e] >= 0)
            def _():
                zero_block(zrow_ref[e]).start()

        def start_tail(b, carry):
            zero_block(b * blk).start()
            return carry

        def wait_tail(b, carry):
            zero_block(b * blk).wait()
            return carry

        lax.fori_loop(n_used_ref[0], nblk, start_tail, 0)
        for e in range(N_EXPERTS):
            @pl.when(zrow_ref[e] >= 0)
            def _():
                zero_block(zrow_ref[e]).wait()
        lax.fori_loop(n_used_ref[0], nblk, wait_tail, 0)

    if zero_fill:
        pl.when(i == 0)(clear_padding)

    def issue(t, carry):
        src = h_ref.at[pl.ds(pl.multiple_of(t * sub, sub), sub), :]
        for k in range(TOP_K):
            d = dest_ref[0, 0, k * td + t]
            pltpu.make_async_copy(src, xs_ref.at[pl.ds(pl.multiple_of(d * sub, sub), sub), :],
                                  sem).start(priority=k % 2)
        return carry

    lax.fori_loop(0, td, issue, 0, unroll=ISSUE_UNROLL)
    for k in range(TOP_K):
        pltpu.make_async_copy(h_ref, xs_ref.at[pl.ds(0, td * sub), :], sem).wait()


def _dispatch(h_tiles, dest, zrow, n_used, n_rows, reuse=None):
    sub = ROW_TILE_SUB
    t = h_tiles.shape[0] // sub
    td = min(DISPATCH_TILE, t)
    nt = t // td
    dest_tiles = dest.reshape(TOP_K, nt, td).transpose(1, 0, 2).reshape(nt, 1, TOP_K * td)
    zero_fill = reuse is None
    kern = functools.partial(_dispatch_kernel, td=td, blk=MOE_BLOCK, nblk=n_rows // MOE_BLOCK,
                             zero_fill=zero_fill)
    extra_specs = [] if zero_fill else [pl.BlockSpec(memory_space=pl.ANY)]
    extra_args = [] if zero_fill else [reuse]
    return pl.pallas_call(
        kern,
        out_shape=jax.ShapeDtypeStruct((n_rows * sub, 128), F32),
        grid_spec=pltpu.PrefetchScalarGridSpec(
            num_scalar_prefetch=2,
            grid=(nt,),
            in_specs=[
                pl.BlockSpec((1, 1, TOP_K * td), lambda i, z, nu: (i, 0, 0),
                             memory_space=pltpu.SMEM),
                pl.BlockSpec((td * sub, 128), lambda i, z, nu: (i, 0)),
            ] + extra_specs,
            out_specs=pl.BlockSpec(memory_space=pl.ANY),
            scratch_shapes=[
                pltpu.VMEM((MOE_BLOCK * sub, 128), F32),
                pltpu.SemaphoreType.DMA(()),
                pltpu.SemaphoreType.DMA(()),
            ],
        ),
        compiler_params=pltpu.CompilerParams(
            dimension_semantics=("arbitrary",), vmem_limit_bytes=VMEM_LIMIT_BYTES),
        input_output_aliases={} if zero_fill else {4: 0},
        name="moe_dispatch" if zero_fill else "moe_dispatch_reuse",
    )(zrow, n_used, dest_tiles, h_tiles, *extra_args)


def _expert_kernel(blk_e_ref, n_used_ref, nxt_e_ref, n_valid_ref, x_ref, w1_hbm, b1_ref, w2_hbm,
                   b2_ref, o_ref, w1s_ref, w2s_ref, w1b_ref, w2b_ref, wsem, *, layer):
    step = pl.program_id(0)
    sub_rows = MOE_BLOCK * ROW_TILE_SUB

    def weight_copies(expert):
        return (pltpu.make_async_copy(w1_hbm.at[layer, expert], w1s_ref, wsem.at[0]),
                pltpu.make_async_copy(w2_hbm.at[layer, expert], w2s_ref, wsem.at[1]))

    @pl.when(step == 0)
    def _():
        for cp in weight_copies(blk_e_ref[0]):
            cp.start()

    for sub in range(EXPERT_STEP_BLOCKS):
        i = step * EXPERT_STEP_BLOCKS + sub
        e = blk_e_ref[i]
        prev = blk_e_ref[jnp.maximum(i - 1, 0)]
        fresh = (i == 0) | (e != prev)
        live = i < n_used_ref[0]

        @pl.when(live & fresh)
        def _(i=i, e=e):
            for cp in weight_copies(e):
                cp.wait()
            w1b_ref[...] = w1s_ref[...].astype(BF16)
            w2b_ref[...] = w2s_ref[...].astype(BF16)
            nxt = nxt_e_ref[i]

            @pl.when(nxt >= 0)
            def _():
                for cp in weight_copies(nxt):
                    cp.start()

        def mlp(rows, sub=sub, e=e):
            r0 = sub * MOE_BLOCK
            x = _load_row_tiles(x_ref, rows, start=r0).astype(BF16)
            a = jnp.dot(x, w1b_ref[...], preferred_element_type=F32) + b1_ref[e]
            glu = jnp.minimum(a[:, :D_FF], SWIGLU_LIMIT)
            lin = jnp.clip(a[:, D_FF:], -SWIGLU_LIMIT, SWIGLU_LIMIT)
            act = glu * jax.nn.sigmoid(SWIGLU_ALPHA * glu) * (lin + 1.0)
            y = jnp.dot(act.astype(BF16), w2b_ref[...], preferred_element_type=F32) + b2_ref[e]
            _store_row_tiles(o_ref, y, rows, start=r0)
            if rows < MOE_BLOCK:
                o_ref[(r0 + rows) * ROW_TILE_SUB:(r0 + MOE_BLOCK) * ROW_TILE_SUB, :] = jnp.zeros(
                    ((MOE_BLOCK - rows) * ROW_TILE_SUB, 128), F32)

        half = MOE_BLOCK // 2
        short = n_valid_ref[i] <= half

        @pl.when(live & jnp.logical_not(short))
        def _(mlp=mlp):
            mlp(MOE_BLOCK)

        @pl.when(live & short)
        def _(mlp=mlp):
            mlp(half)

        @pl.when(jnp.logical_not(live) & (step * EXPERT_STEP_BLOCKS < n_used_ref[0]))
        def _(sub=sub):
            o_ref[sub * sub_rows:(sub + 1) * sub_rows, :] = jnp.zeros((sub_rows, 128), F32)


def _experts(xs, blk_e, n_used, nxt_e, n_valid, layer, w1, b1, w2, b2):
    sub = ROW_TILE_SUB
    n_rows = xs.shape[0] // sub
    d = w2.shape[-1]
    nblk = n_rows // MOE_BLOCK
    depth = w1.shape[0]
    grp = EXPERT_STEP_BLOCKS
    assert nblk % grp == 0
    row = lambda i, be, nu, nx, nv: (jnp.minimum(i, (nu[0] - 1) // grp), 0)
    out_row = row
    bias = lambda i, be, nu, nx, nv: (layer, 0, 0, 0)
    return pl.pallas_call(
        functools.partial(_expert_kernel, layer=layer),
        out_shape=jax.ShapeDtypeStruct((n_rows * sub, 128), F32),
        grid_spec=pltpu.PrefetchScalarGridSpec(
            num_scalar_prefetch=4,
            grid=(nblk // grp,),
            in_specs=[
                pl.BlockSpec((grp * MOE_BLOCK * sub, 128), row),
                pl.BlockSpec(memory_space=pl.ANY),
                pl.BlockSpec((None, N_EXPERTS, 1, 2 * D_FF), bias),
                pl.BlockSpec(memory_space=pl.ANY),
                pl.BlockSpec((None, N_EXPERTS, 1, d), bias),
            ],
            out_specs=pl.BlockSpec((grp * MOE_BLOCK * sub, 128), out_row),
            scratch_shapes=[
                pltpu.VMEM((d, 2 * D_FF), F32),
                pltpu.VMEM((D_FF, d), F32),
                pltpu.VMEM((d, 2 * D_FF), BF16),
                pltpu.VMEM((D_FF, d), BF16),
                pltpu.SemaphoreType.DMA((2,)),
            ],
        ),
        compiler_params=pltpu.CompilerParams(
            dimension_semantics=("arbitrary",), vmem_limit_bytes=VMEM_LIMIT_BYTES),
        input_output_aliases={4: 0},
        name="moe_experts",
    )(blk_e, n_used, nxt_e, n_valid, xs, w1, b1.reshape(depth, N_EXPERTS, 1, 2 * D_FF), w2,
      b2.reshape(depth, N_EXPERTS, 1, d))


def _combine_kernel(dcur_ref, dnext_ref, x_ref, g_ref, ys_ref, gn_ref, o_ref, buf, sem,
                    *, tc, n_tiles, final_norm):
    i = pl.program_id(0)
    sub = ROW_TILE_SUB

    def issue(dref, slot):
        def body(t, carry):
            for k in range(TOP_K):
                d = dref[0, 0, k * tc + t]
                pltpu.make_async_copy(
                    ys_ref.at[pl.ds(pl.multiple_of(d * sub, sub), sub), :],
                    buf.at[slot, k, pl.ds(pl.multiple_of(t * sub, sub), sub), :],
                    sem.at[slot]).start(priority=k % 2)
            return carry
        lax.fori_loop(0, tc, body, 0, unroll=ISSUE_UNROLL)

    @pl.when(i == 0)
    def _():
        issue(dcur_ref, 0)

    @pl.when(i + 1 < n_tiles)
    def _():
        issue(dnext_ref, (i + 1) % 2)

    slot = i % 2
    for k in range(TOP_K):
        pltpu.make_async_copy(ys_ref.at[pl.ds(0, tc * sub), :], buf.at[slot, k],
                              sem.at[slot]).wait()
    acc = x_ref[...]
    gates = g_ref[...]
    for k in range(TOP_K):
        acc = acc + gates[:, k:k + 1] * _load_row_tiles(buf, tc, lead=(slot, k))
    if final_norm:
        acc = _rms(acc, gn_ref[...])
    o_ref[...] = acc


def _combine(x2d, gates_t, dest, ys, g_norm, final_norm):
    t, d = x2d.shape
    tc = min(COMBINE_TILE, t)
    nt = t // tc
    dest_tiles = dest.reshape(TOP_K, nt, tc).transpose(1, 0, 2).reshape(nt, 1, TOP_K * tc)
    kern = functools.partial(_combine_kernel, tc=tc, n_tiles=nt, final_norm=final_norm)
    return pl.pallas_call(
        kern,
        out_shape=jax.ShapeDtypeStruct((t, d), F32),
        grid=(nt,),
        in_specs=[
            pl.BlockSpec((1, 1, TOP_K * tc), lambda i: (i, 0, 0), memory_space=pltpu.SMEM),
            pl.BlockSpec((1, 1, TOP_K * tc), lambda i: (jnp.minimum(i + 1, nt - 1), 0, 0),
                         memory_space=pltpu.SMEM),
            pl.BlockSpec((tc, d), lambda i: (i, 0)),
            pl.BlockSpec((tc, TOP_K), lambda i: (i, 0)),
            pl.BlockSpec(memory_space=pl.ANY),
            pl.BlockSpec((1, d), lambda i: (0, 0)),
        ],
        out_specs=pl.BlockSpec((tc, d), lambda i: (i, 0)),
        scratch_shapes=[
            pltpu.VMEM((2, TOP_K, tc * ROW_TILE_SUB, 128), F32),
            pltpu.SemaphoreType.DMA((2,)),
        ],
        compiler_params=pltpu.CompilerParams(
            dimension_semantics=("arbitrary",), vmem_limit_bytes=VMEM_LIMIT_BYTES),
        name="moe_combine_final" if final_norm else "moe_combine",
    )(dest_tiles, dest_tiles, x2d, gates_t.T, ys, g_norm.reshape(1, d))


def _moe_experts(routed, layer, w1, b1, w2, b2, reuse=None):
    h, ids, gates, rank, sizes = routed
    t = ids.shape[1]
    sizes = sizes[:, 0].astype(I32)
    padded = ((sizes + MOE_BLOCK - 1) // MOE_BLOCK) * MOE_BLOCK
    pend = jnp.cumsum(padded)
    pstart = pend - padded
    onehot = ids[:, :, None] == jnp.arange(N_EXPERTS, dtype=I32)
    dest = rank + jnp.sum(jnp.where(onehot, pstart, 0), axis=-1)
    n_rows = t * TOP_K + N_EXPERTS * MOE_BLOCK
    nblk = n_rows // MOE_BLOCK
    blk_row = jnp.arange(nblk, dtype=I32) * MOE_BLOCK
    blk_e = jnp.minimum(jnp.sum((pend[None, :] <= blk_row[:, None]).astype(I32), axis=1),
                        N_EXPERTS - 1)
    n_used = (pend[-1:] // MOE_BLOCK).astype(I32)
    zrow = jnp.where(padded > 0, pend - MOE_BLOCK, -1).astype(I32)
    xs = _dispatch(h, dest, zrow, n_used, n_rows, reuse)
    blk_id = jnp.arange(nblk, dtype=I32)
    later = ((blk_id[None, :] > blk_id[:, None]) & (blk_e[None, :] != blk_e[:, None])
             & (blk_id[None, :] < n_used[0]))
    nxt_blk = jnp.min(jnp.where(later, blk_id[None, :], nblk), axis=1)
    nxt_e = jnp.sum(jnp.where(blk_id[None, :] == nxt_blk[:, None], blk_e[None, :] + 1, 0),
                    axis=1).astype(I32) - 1
    own = blk_e[:, None] == jnp.arange(N_EXPERTS, dtype=I32)[None, :]
    row_end = jnp.sum(jnp.where(own, (pstart + sizes)[None, :], 0), axis=1)
    n_valid = jnp.clip(row_end - blk_row, 0, MOE_BLOCK).astype(I32)
    ys = _experts(xs, blk_e, n_used, nxt_e, n_valid, layer, w1, b1, w2, b2)
    return ys, dest


def _qkv_kernel(x_ref, g_ref, w_ref, b_ref, q_ref, k_ref, v_ref):
    h = _rms(x_ref[...], g_ref[...])
    qkv = jnp.dot(h.astype(BF16), w_ref[...], preferred_element_type=F32) + b_ref[...]
    nq = N_HEADS * HEAD_DIM
    nk = 2 * N_KV_HEADS * HEAD_DIM
    q_ref[...] = (qkv[:, :nq] * (HEAD_DIM ** -0.5)).astype(BF16)
    k_ref[...] = qkv[:, nq:nq + nk].astype(BF16)
    v_ref[...] = qkv[:, nq + nk:].astype(BF16)


def _qkv(x2d, g, w_qkv, b_qkv):
    t, d = x2d.shape
    ts = min(ROW_TILE, t)
    nq = N_HEADS * HEAD_DIM
    kv = N_KV_HEADS * HEAD_DIM

    def dup_heads(m):
        parts = []
        for hd in range(N_KV_HEADS):
            sl = m[..., hd * HEAD_DIM:(hd + 1) * HEAD_DIM]
            parts += [sl, sl]
        return jnp.concatenate(parts, axis=-1)

    w = jnp.concatenate([w_qkv[:, :nq], dup_heads(w_qkv[:, nq:nq + kv]),
                         dup_heads(w_qkv[:, nq + kv:])], axis=1).astype(BF16)
    b = jnp.concatenate([b_qkv[:nq], dup_heads(b_qkv[nq:nq + kv]),
                         dup_heads(b_qkv[nq + kv:])]).reshape(1, -1)
    n_out = w.shape[1]
    row = lambda width: pl.BlockSpec((ts, width), lambda i: (i, 0))
    return pl.pallas_call(
        _qkv_kernel,
        out_shape=(
            jax.ShapeDtypeStruct((t, nq), BF16),
            jax.ShapeDtypeStruct((t, 2 * kv), BF16),
            jax.ShapeDtypeStruct((t, 2 * kv), BF16),
        ),
        grid=(t // ts,),
        in_specs=[
            row(d),
            pl.BlockSpec((1, d), lambda i: (0, 0)),
            pl.BlockSpec((d, n_out), lambda i: (0, 0)),
            pl.BlockSpec((1, n_out), lambda i: (0, 0)),
        ],
        out_specs=(row(nq), row(2 * kv), row(2 * kv)),
        compiler_params=pltpu.CompilerParams(
            dimension_semantics=("arbitrary",), vmem_limit_bytes=VMEM_LIMIT_BYTES),
        name="attn_qkv",
    )(x2d, g.reshape(1, d), w, b)


def _attn_kernel(sinks_ref, q_ref, kp_ref, kc_ref, vp_ref, vc_ref, bias_ref, o_ref):
    n = pl.program_id(1)
    rows = SLABS_PER_GROUP * ATT_BLOCK
    keys = 2 * ATT_BLOCK
    kall = jnp.concatenate([kp_ref[...], kc_ref[...]], axis=0)
    vall = jnp.concatenate([vp_ref[...], vc_ref[...]], axis=0)
    j = lax.broadcasted_iota(I32, (keys, rows), 0)
    r = lax.broadcasted_iota(I32, (keys, rows), 1) & (ATT_BLOCK - 1)
    in_window = (j > r) & (j <= r + WINDOW)
    lane_k = lax.broadcasted_iota(I32, (keys, 128), 1)
    halves = [lane_k < HEAD_DIM, lane_k >= HEAD_DIM]
    sub_o = lax.broadcasted_iota(I32, (128, rows), 0)
    zero = jnp.zeros((), BF16)
    for blk in range(ATT_STEP_BLOCKS):
        q = q_ref[blk * ATT_BLOCK:(blk + 1) * ATT_BLOCK, :]
        kcat = kall[blk * ATT_BLOCK:blk * ATT_BLOCK + keys]
        vcat = vall[blk * ATT_BLOCK:blk * ATT_BLOCK + keys]
        if blk == 0:
            mask = in_window & ((n > 0) | (j >= ATT_BLOCK))
        else:
            mask = in_window
        for g in range(N_KV_HEADS):
            s0 = g * SLABS_PER_GROUP
            qg = jnp.concatenate([q[:, (s0 + s) * 128:(s0 + s + 1) * 128]
                                  for s in range(SLABS_PER_GROUP)], axis=0)
            kg = kcat[:, g * 128:(g + 1) * 128]
            vg = vcat[:, g * 128:(g + 1) * 128]
            pts, invs = [], []
            for p in range(HEADS_PER_SLAB):
                kp = jnp.where(halves[p], kg, zero)
                sc = lax.dot_general(kp, qg, (((1,), (1,)), ((), ())),
                                     preferred_element_type=F32)
                sc = jnp.where(mask, sc + bias_ref[g, p], NEG_INF)
                sink = jnp.concatenate(
                    [jnp.full((1, ATT_BLOCK),
                              sinks_ref[g * GQA_GROUP + s * HEADS_PER_SLAB + p], F32)
                     for s in range(SLABS_PER_GROUP)], axis=1)
                m = jnp.maximum(jnp.max(sc, axis=0, keepdims=True), sink)
                pe = jnp.exp(sc - m)
                den = jnp.sum(pe, axis=0, keepdims=True) + jnp.exp(sink - m)
                pts.append(pe.astype(BF16))
                invs.append(1.0 / den)
            pcat = jnp.concatenate(pts, axis=0)
            vblk = jnp.concatenate([jnp.where(halves[p], vg, zero)
                                    for p in range(HEADS_PER_SLAB)], axis=0)
            og = lax.dot_general(vblk, pcat, (((0,), (0,)), ((), ())),
                                 preferred_element_type=F32)
            og = og * jnp.where(sub_o < HEAD_DIM, invs[0], invs[1])
            og = og.T
            for s in range(SLABS_PER_GROUP):
                o_ref[blk * ATT_BLOCK:(blk + 1) * ATT_BLOCK, (s0 + s) * 128:(s0 + s + 1) * 128] = (
                    og[s * ATT_BLOCK:(s + 1) * ATT_BLOCK].astype(BF16))


def _t5_bias_table(rel_bias):
    r = np.arange(ATT_BLOCK)[:, None]
    j = np.arange(2 * ATT_BLOCK)[None, :]
    rel = np.clip(ATT_BLOCK + r - j, 0, WINDOW - 1)
    max_exact = N_BUCKETS // 2
    nf = jnp.maximum(rel, max_exact).astype(F32)
    large = max_exact + (jnp.log(nf / max_exact) / math.log(MAX_DISTANCE / max_exact)
                         * (N_BUCKETS - max_exact)).astype(I32)
    large = jnp.minimum(large, N_BUCKETS - 1)
    bucket = jnp.where(rel < max_exact, rel, large)
    pick = (bucket[None] == jnp.arange(N_BUCKETS, dtype=I32)[:, None, None])
    bias = jnp.sum(jnp.where(pick[:, None], rel_bias.astype(F32)[:, :, None, None], 0.0),
                   axis=0)
    bias = bias.reshape(N_KV_HEADS, SLABS_PER_GROUP, HEADS_PER_SLAB, ATT_BLOCK, 2 * ATT_BLOCK)
    bias = jnp.transpose(bias, (0, 2, 4, 1, 3))
    return bias.reshape(N_KV_HEADS, HEADS_PER_SLAB, 2 * ATT_BLOCK, SLABS_PER_GROUP * ATT_BLOCK)


def _attention(q, kk, vv, sinks, rel_bias, bsz, seq):
    nq = N_HEADS * HEAD_DIM
    kvw = kk.shape[-1]
    step = ATT_STEP_BLOCKS * ATT_BLOCK
    q3 = q.reshape(bsz, seq, nq)
    k3 = kk.reshape(bsz, seq, kvw)
    v3 = vv.reshape(bsz, seq, kvw)
    bias = _t5_bias_table(rel_bias)
    prev = lambda bi, n: (bi, jnp.maximum(n * ATT_STEP_BLOCKS - 1, 0), 0)
    cur = lambda bi, n: (bi, n, 0)
    out = pl.pallas_call(
        _attn_kernel,
        out_shape=jax.ShapeDtypeStruct((bsz, seq, nq), BF16),
        grid=(bsz, seq // step),
        in_specs=[
            pl.BlockSpec(memory_space=pltpu.SMEM),
            pl.BlockSpec((None, step, nq), cur),
            pl.BlockSpec((None, ATT_BLOCK, kvw), prev),
            pl.BlockSpec((None, step, kvw), cur),
            pl.BlockSpec((None, ATT_BLOCK, kvw), prev),
            pl.BlockSpec((None, step, kvw), cur),
            pl.BlockSpec(bias.shape, lambda bi, n: (0, 0, 0, 0)),
        ],
        out_specs=pl.BlockSpec((None, step, nq), cur),
        compiler_params=pltpu.CompilerParams(
            dimension_semantics=("arbitrary", "arbitrary"),
            vmem_limit_bytes=VMEM_LIMIT_BYTES),
        name="attn_core",
    )(sinks.astype(F32), q3, k3, k3, v3, v3, bias)
    return out.reshape(bsz * seq, nq)


def _oproj_kernel(x_ref, o_ref, w_ref, b_ref, gr_ref, wr_ref, br_ref,
                  y_ref, h_ref, ids_ref, gates_ref, rank_ref, sizes_ref, carry_ref, *, ts):
    x_new = x_ref[...] + jnp.dot(o_ref[...], w_ref[...],
                                 preferred_element_type=F32) + b_ref[...]
    y_ref[...] = x_new
    _route_tile(x_new, pl.program_id(0) == 0, gr_ref, wr_ref, br_ref,
                h_ref, ids_ref, gates_ref, rank_ref, sizes_ref, carry_ref, ts)


def _out_proj(x2d, o, w_o, b_o, g_ffn, w_router, b_router):
    t, d = x2d.shape
    ts = min(ROW_TILE, t)
    nq = o.shape[1]
    r_in, r_shape, r_out = _route_specs(t, ts, d, lambda i: i)
    outs = pl.pallas_call(
        functools.partial(_oproj_kernel, ts=ts),
        out_shape=(jax.ShapeDtypeStruct((t, d), F32),) + r_shape,
        grid=(t // ts,),
        in_specs=[
            pl.BlockSpec((ts, d), lambda i: (i, 0)),
            pl.BlockSpec((ts, nq), lambda i: (i, 0)),
            pl.BlockSpec((nq, d), lambda i: (0, 0)),
            pl.BlockSpec((1, d), lambda i: (0, 0)),
        ] + r_in,
        out_specs=(pl.BlockSpec((ts, d), lambda i: (i, 0)),) + r_out,
        scratch_shapes=[pltpu.VMEM((N_EXPERTS, 128), F32)],
        compiler_params=pltpu.CompilerParams(
            dimension_semantics=("arbitrary",), vmem_limit_bytes=VMEM_LIMIT_BYTES),
        name="attn_out_proj_route",
    )(x2d, o, w_o.astype(BF16), b_o.reshape(1, d),
      *_route_operands(g_ffn, w_router, b_router))
    return outs[0], outs[1:]


def kernel(x, norm_mix, norm_ffn, norm_final, pool_w, pool_b, pool_scale, w_qkv, b_qkv,
           sinks, w_o, b_o, rel_bias, w_router, b_router, w1, b1, w2, b2):
    bsz, seq, d = x.shape
    t = bsz * seq
    x, routed = _pool_layer(x, norm_mix[0], pool_w[0], pool_b[0], pool_scale[0],
                            norm_ffn[0], w_router[0], b_router[0])
    x = x.reshape(t, d)
    ys0, dest = _moe_experts(routed, 0, w1, b1, w2, b2)
    x = _combine(x, routed[2], dest, ys0, norm_final, final_norm=False)
    q, kk, vv = _qkv(x, norm_mix[1], w_qkv[0], b_qkv[0])
    o = _attention(q, kk, vv, sinks[0], rel_bias, bsz, seq)
    x, routed = _out_proj(x, o, w_o[0], b_o[0], norm_ffn[1], w_router[1], b_router[1])
    ys, dest = _moe_experts(routed, 1, w1, b1, w2, b2, reuse=ys0)
    x = _combine(x, routed[2], dest, ys, norm_final, final_norm=True)
    return x.reshape(bsz, seq, d)
```

```python
import functools
import math

import jax
import jax.numpy as jnp
import numpy as np
from jax import lax
from jax.experimental import pallas as pl
from jax.experimental.pallas import tpu as pltpu

F32 = jnp.float32
BF16 = jnp.bfloat16
I32 = jnp.int32

D_MODEL = 1024
POOL_WINDOWS = (2, 4, 8, 16)
POOL_GROUP_CH = D_MODEL // len(POOL_WINDOWS)
POOL_HALO = 16
HEAD_DIM = 64
N_HEADS = 16
N_KV_HEADS = 2
GQA_GROUP = N_HEADS // N_KV_HEADS
HEADS_PER_SLAB = 128 // HEAD_DIM
SLABS_PER_GROUP = GQA_GROUP // HEADS_PER_SLAB
ATT_BLOCK = 128
ATT_STEP_BLOCKS = 4
WINDOW = 128
N_BUCKETS = 32
MAX_DISTANCE = 128
N_EXPERTS = 32
TOP_K = 4
D_FF = D_MODEL
SWIGLU_ALPHA = 1.702
SWIGLU_LIMIT = 7.0
RMS_EPS = 1e-5
NEG_INF = -1e30

ROW_TILE = 512
MOE_BLOCK = 512
EXPERT_STEP_BLOCKS = 2
DISPATCH_TILE = 4096
COMBINE_TILE = 256
ISSUE_UNROLL = 4
VMEM_LIMIT_BYTES = 56 * 1024 * 1024


def _rms(x, g):
    return x * lax.rsqrt(jnp.mean(x * x, axis=-1, keepdims=True) + RMS_EPS) * g


ROW_TILE_SUB = D_MODEL // 128


def _load_row_tiles(ref, n, lead=(), start=0):
    return jnp.concatenate(
        [ref[lead + (pl.ds(start * ROW_TILE_SUB + j, n, stride=ROW_TILE_SUB), slice(None))]
         for j in range(ROW_TILE_SUB)], axis=1)


def _store_row_tiles(ref, val, n, start=0):
    for j in range(ROW_TILE_SUB):
        ref[pl.ds(start * ROW_TILE_SUB + j, n, stride=ROW_TILE_SUB), :] = (
            val[:, j * 128:(j + 1) * 128])


POOL_PAD = 8


def _pool_kernel(x_ref, xh_ref, g_ref, w_ref, b_ref, s_ref, gr_ref, wr_ref, br_ref,
                 o_ref, h_ref, ids_ref, gates_ref, rank_ref, sizes_ref,
                 buf_ref, s1_ref, s2_ref, s3_ref, carry_ref, *, ts):
    i = pl.program_id(1)
    c = POOL_GROUP_CH
    h0 = POOL_PAD
    t0 = POOL_PAD + POOL_HALO
    r1 = t0 + ts
    x = x_ref[...]
    g = g_ref[...]
    h = _rms(x, g)
    hh = _rms(xh_ref[...], g)
    hh = jnp.where(i == 0, 0.0, hh)
    buf_ref[0:h0, :] = jnp.zeros((h0, D_MODEL), F32)
    buf_ref[h0:t0, :] = hh
    buf_ref[t0:r1, :] = h
    sum0 = h[:, 0:c] + buf_ref[t0 - 1:r1 - 1, 0:c]
    l1 = buf_ref[h0:r1, c:] + buf_ref[h0 - 1:r1 - 1, c:]
    s1_ref[0:h0, :] = jnp.zeros((h0, 3 * c), F32)
    s1_ref[h0:r1, :] = l1
    sum1 = s1_ref[t0:r1, 0:c] + s1_ref[t0 - 2:r1 - 2, 0:c]
    l2 = s1_ref[h0:r1, c:] + s1_ref[h0 - 2:r1 - 2, c:]
    s2_ref[0:h0, :] = jnp.zeros((h0, 2 * c), F32)
    s2_ref[h0:r1, :] = l2
    sum2 = s2_ref[t0:r1, 0:c] + s2_ref[t0 - 4:r1 - 4, 0:c]
    s3_ref[h0:r1, :] = s2_ref[h0:r1, c:] + s2_ref[h0 - 4:r1 - 4, c:]
    sum3 = s3_ref[t0:r1, :] + s3_ref[t0 - 8:r1 - 8, :]
    t = i * ts + lax.broadcasted_iota(I32, (ts, 1), 0)
    outs = []
    for gi, (win, acc) in enumerate(zip(POOL_WINDOWS, (sum0, sum1, sum2, sum3))):
        cnt = jnp.minimum(t + 1, win).astype(F32)
        dlt = acc / cnt - h[:, gi * c:(gi + 1) * c]
        outs.append(jnp.dot(dlt.astype(BF16), w_ref[gi], preferred_element_type=F32))
    y = jnp.concatenate(outs, axis=1)
    x_new = x + (y + b_ref[...]) * s_ref[...]
    o_ref[...] = x_new
    first = (pl.program_id(0) == 0) & (i == 0)
    _route_tile(x_new, first, gr_ref, wr_ref, br_ref,
                h_ref, ids_ref, gates_ref, rank_ref, sizes_ref, carry_ref, ts)


def _pool_layer(x, g, w, b, s, g_ffn, w_router, b_router):
    bsz, seq, d = x.shape
    ts = min(ROW_TILE, seq)
    tiles = seq // ts
    kern = functools.partial(_pool_kernel, ts=ts)
    rows = POOL_PAD + POOL_HALO + ts
    vec = lambda: pl.BlockSpec((1, d), lambda bi, i: (0, 0))
    r_in, r_shape, r_out = _route_specs(bsz * seq, ts, d, lambda bi, i: bi * tiles + i)
    outs = pl.pallas_call(
        kern,
        out_shape=(jax.ShapeDtypeStruct(x.shape, F32),) + r_shape,
        grid=(bsz, tiles),
        in_specs=[
            pl.BlockSpec((None, ts, d), lambda bi, i: (bi, i, 0)),
            pl.BlockSpec((None, POOL_HALO, d),
                         lambda bi, i: (bi, jnp.maximum(i * (ts // POOL_HALO) - 1, 0), 0)),
            vec(),
            pl.BlockSpec(w.shape, lambda bi, i: (0, 0, 0)),
            vec(),
            vec(),
        ] + r_in,
        out_specs=(pl.BlockSpec((None, ts, d), lambda bi, i: (bi, i, 0)),) + r_out,
        scratch_shapes=[pltpu.VMEM((rows, d), F32),
                        pltpu.VMEM((rows, 3 * POOL_GROUP_CH), F32),
                        pltpu.VMEM((rows, 2 * POOL_GROUP_CH), F32),
                        pltpu.VMEM((rows, POOL_GROUP_CH), F32),
                        pltpu.VMEM((N_EXPERTS, 128), F32)],
        compiler_params=pltpu.CompilerParams(
            dimension_semantics=("arbitrary", "arbitrary"),
            vmem_limit_bytes=VMEM_LIMIT_BYTES),
        name="pool_route",
    )(x, x, g.reshape(1, d), w.astype(BF16), b.reshape(1, d), s.reshape(1, d),
      *_route_operands(g_ffn, w_router, b_router))
    return outs[0], outs[1:]


def _route_tile(x, first, g_ref, wr_ref, br_ref,
                h_ref, ids_ref, gates_ref, rank_ref, sizes_ref, carry_ref, ts):
    @pl.when(first)
    def _():
        carry_ref[...] = jnp.zeros_like(carry_ref)

    h = _rms(x, g_ref[...])
    _store_row_tiles(h_ref, h, ts)
    h_hi = h.astype(BF16)
    h_lo = (h - h_hi.astype(F32)).astype(BF16)
    w = wr_ref[...]
    w_hi = w.astype(BF16)
    w_lo = (w - w_hi.astype(F32)).astype(BF16)
    dn = (((1,), (1,)), ((), ()))
    lg = (lax.dot_general(w_hi, h_hi, dn, preferred_element_type=F32)
          + lax.dot_general(w_lo, h_hi, dn, preferred_element_type=F32)
          + lax.dot_general(w_hi, h_lo, dn, preferred_element_type=F32))
    lg = lg + br_ref[:, 0:1]

    iota_e = lax.broadcasted_iota(I32, (N_EXPERTS, ts), 0)
    vals, ids = [], []
    for _ in range(TOP_K):
        m = jnp.max(lg, axis=0, keepdims=True)
        idx = jnp.min(jnp.where(lg == m, iota_e, N_EXPERTS), axis=0, keepdims=True)
        vals.append(m)
        ids.append(idx)
        lg = jnp.where(iota_e == idx, -jnp.inf, lg)
    ex = [jnp.exp(v - vals[0]) for v in vals]
    den = ex[0] + ex[1] + ex[2] + ex[3]
    gates_ref[...] = jnp.concatenate([e / den for e in ex], axis=0)
    ids_ref[...] = jnp.concatenate(ids, axis=0)

    hot = [(iota_e == idx) for idx in ids]
    multi = (hot[0] | hot[1] | hot[2] | hot[3])
    multi_f = multi.astype(F32)
    r = lax.broadcasted_iota(I32, (ts, ts), 0)
    c = lax.broadcasted_iota(I32, (ts, ts), 1)
    upper = (r < c).astype(BF16)
    before = jnp.dot(multi_f.astype(BF16), upper, preferred_element_type=F32)
    before = before + carry_ref[:, 0:1]
    ranks = [jnp.sum(jnp.where(hk, before, 0.0), axis=0, keepdims=True) for hk in hot]
    rank_ref[...] = jnp.concatenate(ranks, axis=0).astype(I32)
    carry_ref[...] = carry_ref[...] + jnp.sum(multi_f, axis=1, keepdims=True)
    sizes_ref[...] = carry_ref[...]


def _route_operands(g, w_router, b_router):
    d = g.shape[0]
    return (g.reshape(1, d), w_router.T,
            jnp.broadcast_to(b_router.reshape(N_EXPERTS, 1), (N_EXPERTS, 128)))


def _route_specs(t, ts, d, tile_of):
    const = lambda *idx: (0, 0)
    in_specs = [pl.BlockSpec((1, d), const), pl.BlockSpec((N_EXPERTS, d), const),
                pl.BlockSpec((N_EXPERTS, 128), const)]
    out_shape = (
        jax.ShapeDtypeStruct((t * ROW_TILE_SUB, 128), F32),
        jax.ShapeDtypeStruct((TOP_K, t), I32),
        jax.ShapeDtypeStruct((TOP_K, t), F32),
        jax.ShapeDtypeStruct((TOP_K, t), I32),
        jax.ShapeDtypeStruct((N_EXPERTS, 128), F32),
    )
    tok = lambda: pl.BlockSpec((TOP_K, ts), lambda *idx: (0, tile_of(*idx)))
    out_specs = (
        pl.BlockSpec((ts * ROW_TILE_SUB, 128), lambda *idx: (tile_of(*idx), 0)),
        tok(), tok(), tok(),
        pl.BlockSpec((N_EXPERTS, 128), const),
    )
    return in_specs, out_shape, out_specs


def _dispatch_kernel(zrow_ref, n_used_ref, dest_ref, h_ref, *rest, td, blk, nblk, zero_fill):
    xs_ref, zbuf, sem_z, sem = rest if zero_fill else rest[1:]
    i = pl.program_id(0)
    sub = ROW_TILE_SUB

    def zero_block(row):
        row = pl.multiple_of(row * sub, blk * sub)
        return pltpu.make_async_copy(zbuf, xs_ref.at[pl.ds(row, blk * sub), :], sem_z)

    def clear_padding():
        zbuf[...] = jnp.zeros_like(zbuf)
        for e in range(N_EXPERTS):
            @pl.when(zrow_ref[e] >= 0)
            def _():
                zero_block(zrow_ref[e]).start()

        def start_tail(b, carry):
            zero_block(b * blk).start()
            return carry

        def wait_tail(b, carry):
            zero_block(b * blk).wait()
            return carry

        lax.fori_loop(n_used_ref[0], nblk, start_tail, 0)
        for e in range(N_EXPERTS):
            @pl.when(zrow_ref[e] >= 0)
            def _():
                zero_block(zrow_ref[e]).wait()
        lax.fori_loop(n_used_ref[0], nblk, wait_tail, 0)

    if zero_fill:
        pl.when(i == 0)(clear_padding)

    def issue(t, carry):
        src = h_ref.at[pl.ds(pl.multiple_of(t * sub, sub), sub), :]
        for k in range(TOP_K):
            d = dest_ref[0, 0, k * td + t]
            pltpu.make_async_copy(src, xs_ref.at[pl.ds(pl.multiple_of(d * sub, sub), sub), :],
                                  sem).start(priority=k % 2)
        return carry

    lax.fori_loop(0, td, issue, 0, unroll=ISSUE_UNROLL)
    for k in range(TOP_K):
        pltpu.make_async_copy(h_ref, xs_ref.at[pl.ds(0, td * sub), :], sem).wait()


def _dispatch(h_tiles, dest, zrow, n_used, n_rows, reuse=None):
    sub = ROW_TILE_SUB
    t = h_tiles.shape[0] // sub
    td = min(DISPATCH_TILE, t)
    nt = t // td
    dest_tiles = dest.reshape(TOP_K, nt, td).transpose(1, 0, 2).reshape(nt, 1, TOP_K * td)
    zero_fill = reuse is None
    kern = functools.partial(_dispatch_kernel, td=td, blk=MOE_BLOCK, nblk=n_rows // MOE_BLOCK,
                             zero_fill=zero_fill)
    extra_specs = [] if zero_fill else [pl.BlockSpec(memory_space=pl.ANY)]
    extra_args = [] if zero_fill else [reuse]
    return pl.pallas_call(
        kern,
        out_shape=jax.ShapeDtypeStruct((n_rows * sub, 128), F32),
        grid_spec=pltpu.PrefetchScalarGridSpec(
            num_scalar_prefetch=2,
            grid=(nt,),
            in_specs=[
                pl.BlockSpec((1, 1, TOP_K * td), lambda i, z, nu: (i, 0, 0),
                             memory_space=pltpu.SMEM),
                pl.BlockSpec((td * sub, 128), lambda i, z, nu: (i, 0)),
            ] + extra_specs,
            out_specs=pl.BlockSpec(memory_space=pl.ANY),
            scratch_shapes=[
                pltpu.VMEM((MOE_BLOCK * sub, 128), F32),
                pltpu.SemaphoreType.DMA(()),
                pltpu.SemaphoreType.DMA(()),
            ],
        ),
        compiler_params=pltpu.CompilerParams(
            dimension_semantics=("arbitrary",), vmem_limit_bytes=VMEM_LIMIT_BYTES),
        input_output_aliases={} if zero_fill else {4: 0},
        name="moe_dispatch" if zero_fill else "moe_dispatch_reuse",
    )(zrow, n_used, dest_tiles, h_tiles, *extra_args)


def _expert_kernel(blk_e_ref, n_used_ref, nxt_e_ref, n_valid_ref, x_ref, w1_hbm, b1_ref, w2_hbm,
                   b2_ref, o_ref, w1s_ref, w2s_ref, w1b_ref, w2b_ref, wsem, *, layer):
    step = pl.program_id(0)
    sub_rows = MOE_BLOCK * ROW_TILE_SUB

    def weight_copies(expert):
        return (pltpu.make_async_copy(w1_hbm.at[layer, expert], w1s_ref, wsem.at[0]),
                pltpu.make_async_copy(w2_hbm.at[layer, expert], w2s_ref, wsem.at[1]))

    @pl.when(step == 0)
    def _():
        for cp in weight_copies(blk_e_ref[0]):
            cp.start()

    for sub in range(EXPERT_STEP_BLOCKS):
        i = step * EXPERT_STEP_BLOCKS + sub
        e = blk_e_ref[i]
        prev = blk_e_ref[jnp.maximum(i - 1, 0)]
        fresh = (i == 0) | (e != prev)
        live = i < n_used_ref[0]

        @pl.when(live & fresh)
        def _(i=i, e=e):
            for cp in weight_copies(e):
                cp.wait()
            w1b_ref[...] = w1s_ref[...].astype(BF16)
            w2b_ref[...] = w2s_ref[...].astype(BF16)
            nxt = nxt_e_ref[i]

            @pl.when(nxt >= 0)
            def _():
                for cp in weight_copies(nxt):
                    cp.start()

        def mlp(rows, sub=sub, e=e):
            r0 = sub * MOE_BLOCK
            x = _load_row_tiles(x_ref, rows, start=r0).astype(BF16)
            a = jnp.dot(x, w1b_ref[...], preferred_element_type=F32) + b1_ref[e]
            glu = jnp.minimum(a[:, :D_FF], SWIGLU_LIMIT)
            lin = jnp.clip(a[:, D_FF:], -SWIGLU_LIMIT, SWIGLU_LIMIT)
            act = glu * jax.nn.sigmoid(SWIGLU_ALPHA * glu) * (lin + 1.0)
            y = jnp.dot(act.astype(BF16), w2b_ref[...], preferred_element_type=F32) + b2_ref[e]
            _store_row_tiles(o_ref, y, rows, start=r0)
            if rows < MOE_BLOCK:
                o_ref[(r0 + rows) * ROW_TILE_SUB:(r0 + MOE_BLOCK) * ROW_TILE_SUB, :] = jnp.zeros(
                    ((MOE_BLOCK - rows) * ROW_TILE_SUB, 128), F32)

        half = MOE_BLOCK // 2
        short = n_valid_ref[i] <= half

        @pl.when(live & jnp.logical_not(short))
        def _(mlp=mlp):
            mlp(MOE_BLOCK)

        @pl.when(live & short)
        def _(mlp=mlp):
            mlp(half)

        @pl.when(jnp.logical_not(live) & (step * EXPERT_STEP_BLOCKS < n_used_ref[0]))
        def _(sub=sub):
            o_ref[sub * sub_rows:(sub + 1) * sub_rows, :] = jnp.zeros((sub_rows, 128), F32)


def _experts(xs, blk_e, n_used, nxt_e, n_valid, layer, w1, b1, w2, b2):
    sub = ROW_TILE_SUB
    n_rows = xs.shape[0] // sub
    d = w2.shape[-1]
    nblk = n_rows // MOE_BLOCK
    depth = w1.shape[0]
    grp = EXPERT_STEP_BLOCKS
    assert nblk % grp == 0
    row = lambda i, be, nu, nx, nv: (jnp.minimum(i, (nu[0] - 1) // grp), 0)
    out_row = row
    bias = lambda i, be, nu, nx, nv: (layer, 0, 0, 0)
    return pl.pallas_call(
        functools.partial(_expert_kernel, layer=layer),
        out_shape=jax.ShapeDtypeStruct((n_rows * sub, 128), F32),
        grid_spec=pltpu.PrefetchScalarGridSpec(
            num_scalar_prefetch=4,
            grid=(nblk // grp,),
            in_specs=[
                pl.BlockSpec((grp * MOE_BLOCK * sub, 128), row),
                pl.BlockSpec(memory_space=pl.ANY),
                pl.BlockSpec((None, N_EXPERTS, 1, 2 * D_FF), bias),
                pl.BlockSpec(memory_space=pl.ANY),
                pl.BlockSpec((None, N_EXPERTS, 1, d), bias),
            ],
            out_specs=pl.BlockSpec((grp * MOE_BLOCK * sub, 128), out_row),
            scratch_shapes=[
                pltpu.VMEM((d, 2 * D_FF), F32),
                pltpu.VMEM((D_FF, d), F32),
                pltpu.VMEM((d, 2 * D_FF), BF16),
                pltpu.VMEM((D_FF, d), BF16),
                pltpu.SemaphoreType.DMA((2,)),
            ],
        ),
        compiler_params=pltpu.CompilerParams(
            dimension_semantics=("arbitrary",), vmem_limit_bytes=VMEM_LIMIT_BYTES),
        input_output_aliases={4: 0},
        name="moe_experts",
    )(blk_e, n_used, nxt_e, n_valid, xs, w1, b1.reshape(depth, N_EXPERTS, 1, 2 * D_FF), w2,
      b2.reshape(depth, N_EXPERTS, 1, d))


def _combine_kernel(dcur_ref, dnext_ref, x_ref, g_ref, ys_ref, gn_ref, o_ref, buf, sem,
                    *, tc, n_tiles, final_norm):
    i = pl.program_id(0)
    sub = ROW_TILE_SUB

    def issue(dref, slot):
        def body(t, carry):
            for k in range(TOP_K):
                d = dref[0, 0, k * tc + t]
                pltpu.make_async_copy(
                    ys_ref.at[pl.ds(pl.multiple_of(d * sub, sub), sub), :],
                    buf.at[slot, k, pl.ds(pl.multiple_of(t * sub, sub), sub), :],
                    sem.at[slot]).start(priority=k % 2)
            return carry
        lax.fori_loop(0, tc, body, 0, unroll=ISSUE_UNROLL)

    @pl.when(i == 0)
    def _():
        issue(dcur_ref, 0)

    @pl.when(i + 1 < n_tiles)
    def _():
        issue(dnext_ref, (i + 1) % 2)

    slot = i % 2
    for k in range(TOP_K):
        pltpu.make_async_copy(ys_ref.at[pl.ds(0, tc * sub), :], buf.at[slot, k],
                              sem.at[slot]).wait()
    acc = x_ref[...]
    gates = g_ref[...]
    for k in range(TOP_K):
        acc = acc + gates[:, k:k + 1] * _load_row_tiles(buf, tc, lead=(slot, k))
    if final_norm:
        acc = _rms(acc, gn_ref[...])
    o_ref[...] = acc


def _combine(x2d, gates_t, dest, ys, g_norm, final_norm):
    t, d = x2d.shape
    tc = min(COMBINE_TILE, t)
    nt = t // tc
    dest_tiles = dest.reshape(TOP_K, nt, tc).transpose(1, 0, 2).reshape(nt, 1, TOP_K * tc)
    kern = functools.partial(_combine_kernel, tc=tc, n_tiles=nt, final_norm=final_norm)
    return pl.pallas_call(
        kern,
        out_shape=jax.ShapeDtypeStruct((t, d), F32),
        grid=(nt,),
        in_specs=[
            pl.BlockSpec((1, 1, TOP_K * tc), lambda i: (i, 0, 0), memory_space=pltpu.SMEM),
            pl.BlockSpec((1, 1, TOP_K * tc), lambda i: (jnp.minimum(i + 1, nt - 1), 0, 0),
                         memory_space=pltpu.SMEM),
            pl.BlockSpec((tc, d), lambda i: (i, 0)),
            pl.BlockSpec((tc, TOP_K), lambda i: (i, 0)),
            pl.BlockSpec(memory_space=pl.ANY),
            pl.BlockSpec((1, d), lambda i: (0, 0)),
        ],
        out_specs=pl.BlockSpec((tc, d), lambda i: (i, 0)),
        scratch_shapes=[
            pltpu.VMEM((2, TOP_K, tc * ROW_TILE_SUB, 128), F32),
            pltpu.SemaphoreType.DMA((2,)),
        ],
        compiler_params=pltpu.CompilerParams(
            dimension_semantics=("arbitrary",), vmem_limit_bytes=VMEM_LIMIT_BYTES),
        name="moe_combine_final" if final_norm else "moe_combine",
    )(dest_tiles, dest_tiles, x2d, gates_t.T, ys, g_norm.reshape(1, d))


def _moe_experts(routed, layer, w1, b1, w2, b2, reuse=None):
    h, ids, gates, rank, sizes = routed
    t = ids.shape[1]
    sizes = sizes[:, 0].astype(I32)
    padded = ((sizes + MOE_BLOCK - 1) // MOE_BLOCK) * MOE_BLOCK
    pend = jnp.cumsum(padded)
    pstart = pend - padded
    onehot = ids[:, :, None] == jnp.arange(N_EXPERTS, dtype=I32)
    dest = rank + jnp.sum(jnp.where(onehot, pstart, 0), axis=-1)
    n_rows = t * TOP_K + N_EXPERTS * MOE_BLOCK
    nblk = n_rows // MOE_BLOCK
    blk_row = jnp.arange(nblk, dtype=I32) * MOE_BLOCK
    blk_e = jnp.minimum(jnp.sum((pend[None, :] <= blk_row[:, None]).astype(I32), axis=1),
                        N_EXPERTS - 1)
    n_used = (pend[-1:] // MOE_BLOCK).astype(I32)
    zrow = jnp.where(padded > 0, pend - MOE_BLOCK, -1).astype(I32)
    xs = _dispatch(h, dest, zrow, n_used, n_rows, reuse)
    blk_id = jnp.arange(nblk, dtype=I32)
    later = ((blk_id[None, :] > blk_id[:, None]) & (blk_e[None, :] != blk_e[:, None])
             & (blk_id[None, :] < n_used[0]))
    nxt_blk = jnp.min(jnp.where(later, blk_id[None, :], nblk), axis=1)
    nxt_e = jnp.sum(jnp.where(blk_id[None, :] == nxt_blk[:, None], blk_e[None, :] + 1, 0),
                    axis=1).astype(I32) - 1
    own = blk_e[:, None] == jnp.arange(N_EXPERTS, dtype=I32)[None, :]
    row_end = jnp.sum(jnp.where(own, (pstart + sizes)[None, :], 0), axis=1)
    n_valid = jnp.clip(row_end - blk_row, 0, MOE_BLOCK).astype(I32)
    ys = _experts(xs, blk_e, n_used, nxt_e, n_valid, layer, w1, b1, w2, b2)
    return ys, dest


def _qkv_kernel(x_ref, g_ref, w_ref, b_ref, q_ref, k_ref, v_ref):
    h = _rms(x_ref[...], g_ref[...])
    qkv = jnp.dot(h.astype(BF16), w_ref[...], preferred_element_type=F32) + b_ref[...]
    nq = N_HEADS * HEAD_DIM
    nk = 2 * N_KV_HEADS * HEAD_DIM
    q_ref[...] = (qkv[:, :nq] * (HEAD_DIM ** -0.5)).astype(BF16)
    k_ref[...] = qkv[:, nq:nq + nk].astype(BF16)
    v_ref[...] = qkv[:, nq + nk:].astype(BF16)


def _qkv(x2d, g, w_qkv, b_qkv):
    t, d = x2d.shape
    ts = min(ROW_TILE, t)
    nq = N_HEADS * HEAD_DIM
    kv = N_KV_HEADS * HEAD_DIM

    def dup_heads(m):
        parts = []
        for hd in range(N_KV_HEADS):
            sl = m[..., hd * HEAD_DIM:(hd + 1) * HEAD_DIM]
            parts += [sl, sl]
        return jnp.concatenate(parts, axis=-1)

    w = jnp.concatenate([w_qkv[:, :nq], dup_heads(w_qkv[:, nq:nq + kv]),
                         dup_heads(w_qkv[:, nq + kv:])], axis=1).astype(BF16)
    b = jnp.concatenate([b_qkv[:nq], dup_heads(b_qkv[nq:nq + kv]),
                         dup_heads(b_qkv[nq + kv:])]).reshape(1, -1)
    n_out = w.shape[1]
    row = lambda width: pl.BlockSpec((ts, width), lambda i: (i, 0))
    return pl.pallas_call(
        _qkv_kernel,
        out_shape=(
            jax.ShapeDtypeStruct((t, nq), BF16),
            jax.ShapeDtypeStruct((t, 2 * kv), BF16),
            jax.ShapeDtypeStruct((t, 2 * kv), BF16),
        ),
        grid=(t // ts,),
        in_specs=[
            row(d),
            pl.BlockSpec((1, d), lambda i: (0, 0)),
            pl.BlockSpec((d, n_out), lambda i: (0, 0)),
            pl.BlockSpec((1, n_out), lambda i: (0, 0)),
        ],
        out_specs=(row(nq), row(2 * kv), row(2 * kv)),
        compiler_params=pltpu.CompilerParams(
            dimension_semantics=("arbitrary",), vmem_limit_bytes=VMEM_LIMIT_BYTES),
        name="attn_qkv",
    )(x2d, g.reshape(1, d), w, b)


def _attn_kernel(sinks_ref, q_ref, kp_ref, kc_ref, vp_ref, vc_ref, bias_ref, o_ref):
    n = pl.program_id(1)
    rows = SLABS_PER_GROUP * ATT_BLOCK
    keys = 2 * ATT_BLOCK
    kall = jnp.concatenate([kp_ref[...], kc_ref[...]], axis=0)
    vall = jnp.concatenate([vp_ref[...], vc_ref[...]], axis=0)
    j = lax.broadcasted_iota(I32, (keys, rows), 0)
    r = lax.broadcasted_iota(I32, (keys, rows), 1) & (ATT_BLOCK - 1)
    in_window = (j > r) & (j <= r + WINDOW)
    lane_k = lax.broadcasted_iota(I32, (keys, 128), 1)
    halves = [lane_k < HEAD_DIM, lane_k >= HEAD_DIM]
    sub_o = lax.broadcasted_iota(I32, (128, rows), 0)
    zero = jnp.zeros((), BF16)
    for blk in range(ATT_STEP_BLOCKS):
        q = q_ref[blk * ATT_BLOCK:(blk + 1) * ATT_BLOCK, :]
        kcat = kall[blk * ATT_BLOCK:blk * ATT_BLOCK + keys]
        vcat = vall[blk * ATT_BLOCK:blk * ATT_BLOCK + keys]
        if blk == 0:
            mask = in_window & ((n > 0) | (j >= ATT_BLOCK))
        else:
            mask = in_window
        for g in range(N_KV_HEADS):
            s0 = g * SLABS_PER_GROUP
            qg = jnp.concatenate([q[:, (s0 + s) * 128:(s0 + s + 1) * 128]
                                  for s in range(SLABS_PER_GROUP)], axis=0)
            kg = kcat[:, g * 128:(g + 1) * 128]
            vg = vcat[:, g * 128:(g + 1) * 128]
            pts, invs = [], []
            for p in range(HEADS_PER_SLAB):
                kp = jnp.where(halves[p], kg, zero)
                sc = lax.dot_general(kp, qg, (((1,), (1,)), ((), ())),
                                     preferred_element_type=F32)
                sc = jnp.where(mask, sc + bias_ref[g, p], NEG_INF)
                sink = jnp.concatenate(
                    [jnp.full((1, ATT_BLOCK),
                              sinks_ref[g * GQA_GROUP + s * HEADS_PER_SLAB + p], F32)
                     for s in range(SLABS_PER_GROUP)], axis=1)
                m = jnp.maximum(jnp.max(sc, axis=0, keepdims=True), sink)
                pe = jnp.exp(sc - m)
                den = jnp.sum(pe, axis=0, keepdims=True) + jnp.exp(sink - m)
                pts.append(pe.astype(BF16))
                invs.append(1.0 / den)
            pcat = jnp.concatenate(pts, axis=0)
            vblk = jnp.concatenate([jnp.where(halves[p], vg, zero)
                                    for p in range(HEADS_PER_SLAB)], axis=0)
            og = lax.dot_general(vblk, pcat, (((0,), (0,)), ((), ())),
                                 preferred_element_type=F32)
            og = og * jnp.where(sub_o < HEAD_DIM, invs[0], invs[1])
            og = og.T
            for s in range(SLABS_PER_GROUP):
                o_ref[blk * ATT_BLOCK:(blk + 1) * ATT_BLOCK, (s0 + s) * 128:(s0 + s + 1) * 128] = (
                    og[s * ATT_BLOCK:(s + 1) * ATT_BLOCK].astype(BF16))


def _t5_bias_table(rel_bias):
    r = np.arange(ATT_BLOCK)[:, None]
    j = np.arange(2 * ATT_BLOCK)[None, :]
    rel = np.clip(ATT_BLOCK + r - j, 0, WINDOW - 1)
    max_exact = N_BUCKETS // 2
    nf = jnp.maximum(rel, max_exact).astype(F32)
    large = max_exact + (jnp.log(nf / max_exact) / math.log(MAX_DISTANCE / max_exact)
                         * (N_BUCKETS - max_exact)).astype(I32)
    large = jnp.minimum(large, N_BUCKETS - 1)
    bucket = jnp.where(rel < max_exact, rel, large)
    pick = (bucket[None] == jnp.arange(N_BUCKETS, dtype=I32)[:, None, None])
    bias = jnp.sum(jnp.where(pick[:, None], rel_bias.astype(F32)[:, :, None, None], 0.0),
                   axis=0)
    bias = bias.reshape(N_KV_HEADS, SLABS_PER_GROUP, HEADS_PER_SLAB, ATT_BLOCK, 2 * ATT_BLOCK)
    bias = jnp.transpose(bias, (0, 2, 4, 1, 3))
    return bias.reshape(N_KV_HEADS, HEADS_PER_SLAB, 2 * ATT_BLOCK, SLABS_PER_GROUP * ATT_BLOCK)


def _attention(q, kk, vv, sinks, rel_bias, bsz, seq):
    nq = N_HEADS * HEAD_DIM
    kvw = kk.shape[-1]
    step = ATT_STEP_BLOCKS * ATT_BLOCK
    q3 = q.reshape(bsz, seq, nq)
    k3 = kk.reshape(bsz, seq, kvw)
    v3 = vv.reshape(bsz, seq, kvw)
    bias = _t5_bias_table(rel_bias)
    prev = lambda bi, n: (bi, jnp.maximum(n * ATT_STEP_BLOCKS - 1, 0), 0)
    cur = lambda bi, n: (bi, n, 0)
    out = pl.pallas_call(
        _attn_kernel,
        out_shape=jax.ShapeDtypeStruct((bsz, seq, nq), BF16),
        grid=(bsz, seq // step),
        in_specs=[
            pl.BlockSpec(memory_space=pltpu.SMEM),
            pl.BlockSpec((None, step, nq), cur),
            pl.BlockSpec((None, ATT_BLOCK, kvw), prev),
            pl.BlockSpec((None, step, kvw), cur),
            pl.BlockSpec((None, ATT_BLOCK, kvw), prev),
            pl.BlockSpec((None, step, kvw), cur),
            pl.BlockSpec(bias.shape, lambda bi, n: (0, 0, 0, 0)),
        ],
        out_specs=pl.BlockSpec((None, step, nq), cur),
        compiler_params=pltpu.CompilerParams(
            dimension_semantics=("arbitrary", "arbitrary"),
            vmem_limit_bytes=VMEM_LIMIT_BYTES),
        name="attn_core",
    )(sinks.astype(F32), q3, k3, k3, v3, v3, bias)
    return out.reshape(bsz * seq, nq)


def _oproj_kernel(x_ref, o_ref, w_ref, b_ref, gr_ref, wr_ref, br_ref,
                  y_ref, h_ref, ids_ref, gates_ref, rank_ref, sizes_ref, carry_ref, *, ts):
    x_new = x_ref[...] + jnp.dot(o_ref[...], w_ref[...],
                                 preferred_element_type=F32) + b_ref[...]
    y_ref[...] = x_new
    _route_tile(x_new, pl.program_id(0) == 0, gr_ref, wr_ref, br_ref,
                h_ref, ids_ref, gates_ref, rank_ref, sizes_ref, carry_ref, ts)


def _out_proj(x2d, o, w_o, b_o, g_ffn, w_router, b_router):
    t, d = x2d.shape
    ts = min(ROW_TILE, t)
    nq = o.shape[1]
    r_in, r_shape, r_out = _route_specs(t, ts, d, lambda i: i)
    outs = pl.pallas_call(
        functools.partial(_oproj_kernel, ts=ts),
        out_shape=(jax.ShapeDtypeStruct((t, d), F32),) + r_shape,
        grid=(t // ts,),
        in_specs=[
            pl.BlockSpec((ts, d), lambda i: (i, 0)),
            pl.BlockSpec((ts, nq), lambda i: (i, 0)),
            pl.BlockSpec((nq, d), lambda i: (0, 0)),
            pl.BlockSpec((1, d), lambda i: (0, 0)),
        ] + r_in,
        out_specs=(pl.BlockSpec((ts, d), lambda i: (i, 0)),) + r_out,
        scratch_shapes=[pltpu.VMEM((N_EXPERTS, 128), F32)],
        compiler_params=pltpu.CompilerParams(
            dimension_semantics=("arbitrary",), vmem_limit_bytes=VMEM_LIMIT_BYTES),
        name="attn_out_proj_route",
    )(x2d, o, w_o.astype(BF16), b_o.reshape(1, d),
      *_route_operands(g_ffn, w_router, b_router))
    return outs[0], outs[1:]


def kernel(x, norm_mix, norm_ffn, norm_final, pool_w, pool_b, pool_scale, w_qkv, b_qkv,
           sinks, w_o, b_o, rel_bias, w_router, b_router, w1, b1, w2, b2):
    bsz, seq, d = x.shape
    t = bsz * seq
    x, routed = _pool_layer(x, norm_mix[0], pool_w[0], pool_b[0], pool_scale[0],
                            norm_ffn[0], w_router[0], b_router[0])
    x = x.reshape(t, d)
    ys0, dest = _moe_experts(routed, 0, w1, b1, w2, b2)
    x = _combine(x, routed[2], dest, ys0, norm_final, final_norm=False)
    q, kk, vv = _qkv(x, norm_mix[1], w_qkv[0], b_qkv[0])
    o = _attention(q, kk, vv, sinks[0], rel_bias, bsz, seq)
    x, routed = _out_proj(x, o, w_o[0], b_o[0], norm_ffn[1], w_router[1], b_router[1])
    ys, dest = _moe_experts(routed, 1, w1, b1, w2, b2, reuse=ys0)
    x = _combine(x, routed[2], dest, ys, norm_final, final_norm=True)
    return x.reshape(bsz, seq, d)
```

```python
import functools
import math

import jax
import jax.numpy as jnp
import numpy as np
from jax import lax
from jax.experimental import pallas as pl
from jax.experimental.pallas import tpu as pltpu

F32 = jnp.float32
BF16 = jnp.bfloat16
I32 = jnp.int32

D_MODEL = 1024
POOL_WINDOWS = (2, 4, 8, 16)
POOL_GROUP_CH = D_MODEL // len(POOL_WINDOWS)
POOL_HALO = 16
HEAD_DIM = 64
N_HEADS = 16
N_KV_HEADS = 2
GQA_GROUP = N_HEADS // N_KV_HEADS
HEADS_PER_SLAB = 128 // HEAD_DIM
SLABS_PER_GROUP = GQA_GROUP // HEADS_PER_SLAB
ATT_BLOCK = 128
ATT_STEP_BLOCKS = 4
WINDOW = 128
N_BUCKETS = 32
MAX_DISTANCE = 128
N_EXPERTS = 32
TOP_K = 4
D_FF = D_MODEL
SWIGLU_ALPHA = 1.702
SWIGLU_LIMIT = 7.0
RMS_EPS = 1e-5
NEG_INF = -1e30

ROW_TILE = 512
MOE_BLOCK = 512
EXPERT_STEP_BLOCKS = 2
DISPATCH_TILE = 4096
COMBINE_TILE = 256
ISSUE_UNROLL = 4
VMEM_LIMIT_BYTES = 56 * 1024 * 1024


def _rms(x, g):
    return x * lax.rsqrt(jnp.mean(x * x, axis=-1, keepdims=True) + RMS_EPS) * g


ROW_TILE_SUB = D_MODEL // 128


def _load_row_tiles(ref, n, lead=(), start=0):
    return jnp.concatenate(
        [ref[lead + (pl.ds(start * ROW_TILE_SUB + j, n, stride=ROW_TILE_SUB), slice(None))]
         for j in range(ROW_TILE_SUB)], axis=1)


def _store_row_tiles(ref, val, n, start=0):
    for j in range(ROW_TILE_SUB):
        ref[pl.ds(start * ROW_TILE_SUB + j, n, stride=ROW_TILE_SUB), :] = (
            val[:, j * 128:(j + 1) * 128])


POOL_PAD = 8


def _pool_kernel(x_ref, xh_ref, g_ref, w_ref, b_ref, s_ref, gr_ref, wr_ref, br_ref,
                 o_ref, h_ref, ids_ref, gates_ref, rank_ref, sizes_ref,
                 buf_ref, s1_ref, s2_ref, s3_ref, carry_ref, *, ts):
    i = pl.program_id(1)
    c = POOL_GROUP_CH
    h0 = POOL_PAD
    t0 = POOL_PAD + POOL_HALO
    r1 = t0 + ts
    x = x_ref[...]
    g = g_ref[...]
    h = _rms(x, g)
    hh = _rms(xh_ref[...], g)
    hh = jnp.where(i == 0, 0.0, hh)
    buf_ref[0:h0, :] = jnp.zeros((h0, D_MODEL), F32)
    buf_ref[h0:t0, :] = hh
    buf_ref[t0:r1, :] = h
    sum0 = h[:, 0:c] + buf_ref[t0 - 1:r1 - 1, 0:c]
    l1 = buf_ref[h0:r1, c:] + buf_ref[h0 - 1:r1 - 1, c:]
    s1_ref[0:h0, :] = jnp.zeros((h0, 3 * c), F32)
    s1_ref[h0:r1, :] = l1
    sum1 = s1_ref[t0:r1, 0:c] + s1_ref[t0 - 2:r1 - 2, 0:c]
    l2 = s1_ref[h0:r1, c:] + s1_ref[h0 - 2:r1 - 2, c:]
    s2_ref[0:h0, :] = jnp.zeros((h0, 2 * c), F32)
    s2_ref[h0:r1, :] = l2
    sum2 = s2_ref[t0:r1, 0:c] + s2_ref[t0 - 4:r1 - 4, 0:c]
    s3_ref[h0:r1, :] = s2_ref[h0:r1, c:] + s2_ref[h0 - 4:r1 - 4, c:]
    sum3 = s3_ref[t0:r1, :] + s3_ref[t0 - 8:r1 - 8, :]
    t = i * ts + lax.broadcasted_iota(I32, (ts, 1), 0)
    outs = []
    for gi, (win, acc) in enumerate(zip(POOL_WINDOWS, (sum0, sum1, sum2, sum3))):
        cnt = jnp.minimum(t + 1, win).astype(F32)
        dlt = acc / cnt - h[:, gi * c:(gi + 1) * c]
        outs.append(jnp.dot(dlt.astype(BF16), w_ref[gi], preferred_element_type=F32))
    y = jnp.concatenate(outs, axis=1)
    x_new = x + (y + b_ref[...]) * s_ref[...]
    o_ref[...] = x_new
    first = (pl.program_id(0) == 0) & (i == 0)
    _route_tile(x_new, first, gr_ref, wr_ref, br_ref,
                h_ref, ids_ref, gates_ref, rank_ref, sizes_ref, carry_ref, ts)


def _pool_layer(x, g, w, b, s, g_ffn, w_router, b_router):
    bsz, seq, d = x.shape
    ts = min(ROW_TILE, seq)
    tiles = seq // ts
    kern = functools.partial(_pool_kernel, ts=ts)
    rows = POOL_PAD + POOL_HALO + ts
    vec = lambda: pl.BlockSpec((1, d), lambda bi, i: (0, 0))
    r_in, r_shape, r_out = _route_specs(bsz * seq, ts, d, lambda bi, i: bi * tiles + i)
    outs = pl.pallas_call(
        kern,
        out_shape=(jax.ShapeDtypeStruct(x.shape, F32),) + r_shape,
        grid=(bsz, tiles),
        in_specs=[
            pl.BlockSpec((None, ts, d), lambda bi, i: (bi, i, 0)),
            pl.BlockSpec((None, POOL_HALO, d),
                         lambda bi, i: (bi, jnp.maximum(i * (ts // POOL_HALO) - 1, 0), 0)),
            vec(),
            pl.BlockSpec(w.shape, lambda bi, i: (0, 0, 0)),
            vec(),
            vec(),
        ] + r_in,
        out_specs=(pl.BlockSpec((None, ts, d), lambda bi, i: (bi, i, 0)),) + r_out,
        scratch_shapes=[pltpu.VMEM((rows, d), F32),
                        pltpu.VMEM((rows, 3 * POOL_GROUP_CH), F32),
                        pltpu.VMEM((rows, 2 * POOL_GROUP_CH), F32),
                        pltpu.VMEM((rows, POOL_GROUP_CH), F32),
                        pltpu.VMEM((N_EXPERTS, 128), F32)],
        compiler_params=pltpu.CompilerParams(
            dimension_semantics=("arbitrary", "arbitrary"),
            vmem_limit_bytes=VMEM_LIMIT_BYTES),
        name="pool_route",
    )(x, x, g.reshape(1, d), w.astype(BF16), b.reshape(1, d), s.reshape(1, d),
      *_route_operands(g_ffn, w_router, b_router))
    return outs[0], outs[1:]


def _route_tile(x, first, g_ref, wr_ref, br_ref,
                h_ref, ids_ref, gates_ref, rank_ref, sizes_ref, carry_ref, ts):
    @pl.when(first)
    def _():
        carry_ref[...] = jnp.zeros_like(carry_ref)

    h = _rms(x, g_ref[...])
    _store_row_tiles(h_ref, h, ts)
    h_hi = h.astype(BF16)
    h_lo = (h - h_hi.astype(F32)).astype(BF16)
    w = wr_ref[...]
    w_hi = w.astype(BF16)
    w_lo = (w - w_hi.astype(F32)).astype(BF16)
    dn = (((1,), (1,)), ((), ()))
    lg = (lax.dot_general(w_hi, h_hi, dn, preferred_element_type=F32)
          + lax.dot_general(w_lo, h_hi, dn, preferred_element_type=F32)
          + lax.dot_general(w_hi, h_lo, dn, preferred_element_type=F32))
    lg = lg + br_ref[:, 0:1]

    iota_e = lax.broadcasted_iota(I32, (N_EXPERTS, ts), 0)
    vals, ids = [], []
    for _ in range(TOP_K):
        m = jnp.max(lg, axis=0, keepdims=True)
        idx = jnp.min(jnp.where(lg == m, iota_e, N_EXPERTS), axis=0, keepdims=True)
        vals.append(m)
        ids.append(idx)
        lg = jnp.where(iota_e == idx, -jnp.inf, lg)
    ex = [jnp.exp(v - vals[0]) for v in vals]
    den = ex[0] + ex[1] + ex[2] + ex[3]
    gates_ref[...] = jnp.concatenate([e / den for e in ex], axis=0)
    ids_ref[...] = jnp.concatenate(ids, axis=0)

    hot = [(iota_e == idx) for idx in ids]
    multi = (hot[0] | hot[1] | hot[2] | hot[3])
    multi_f = multi.astype(F32)
    r = lax.broadcasted_iota(I32, (ts, ts), 0)
    c = lax.broadcasted_iota(I32, (ts, ts), 1)
    upper = (r < c).astype(BF16)
    before = jnp.dot(multi_f.astype(BF16), upper, preferred_element_type=F32)
    before = before + carry_ref[:, 0:1]
    ranks = [jnp.sum(jnp.where(hk, before, 0.0), axis=0, keepdims=True) for hk in hot]
    rank_ref[...] = jnp.concatenate(ranks, axis=0).astype(I32)
    carry_ref[...] = carry_ref[...] + jnp.sum(multi_f, axis=1, keepdims=True)
    sizes_ref[...] = carry_ref[...]


def _route_operands(g, w_router, b_router):
    d = g.shape[0]
    return (g.reshape(1, d), w_router.T,
            jnp.broadcast_to(b_router.reshape(N_EXPERTS, 1), (N_EXPERTS, 128)))


def _route_specs(t, ts, d, tile_of):
    const = lambda *idx: (0, 0)
    in_specs = [pl.BlockSpec((1, d), const), pl.BlockSpec((N_EXPERTS, d), const),
                pl.BlockSpec((N_EXPERTS, 128), const)]
    out_shape = (
        jax.ShapeDtypeStruct((t * ROW_TILE_SUB, 128), F32),
        jax.ShapeDtypeStruct((TOP_K, t), I32),
        jax.ShapeDtypeStruct((TOP_K, t), F32),
        jax.ShapeDtypeStruct((TOP_K, t), I32),
        jax.ShapeDtypeStruct((N_EXPERTS, 128), F32),
    )
    tok = lambda: pl.BlockSpec((TOP_K, ts), lambda *idx: (0, tile_of(*idx)))
    out_specs = (
        pl.BlockSpec((ts * ROW_TILE_SUB, 128), lambda *idx: (tile_of(*idx), 0)),
        tok(), tok(), tok(),
        pl.BlockSpec((N_EXPERTS, 128), const),
    )
    return in_specs, out_shape, out_specs


def _dispatch_kernel(zrow_ref, n_used_ref, *refs, td, blk, nblk, zero_fill):
    dest_refs, h_ref, rest = refs[:TOP_K], refs[TOP_K], refs[TOP_K + 1:]
    xs_ref, zbuf, sem_z, sem = rest if zero_fill else rest[1:]
    i = pl.program_id(0)
    sub = ROW_TILE_SUB

    def zero_block(row):
        row = pl.multiple_of(row * sub, blk * sub)
        return pltpu.make_async_copy(zbuf, xs_ref.at[pl.ds(row, blk * sub), :], sem_z)

    def clear_padding():
        zbuf[...] = jnp.zeros_like(zbuf)
        for e in range(N_EXPERTS):
            @pl.when(zrow_ref[e] >= 0)
            def _():
                zero_block(zrow_ref[e]).start()

        def start_tail(b, carry):
            zero_block(b * blk).start()
            return carry

        def wait_tail(b, carry):
            zero_block(b * blk).wait()
            return carry

        lax.fori_loop(n_used_ref[0], nblk, start_tail, 0)
        for e in range(N_EXPERTS):
            @pl.when(zrow_ref[e] >= 0)
            def _():
                zero_block(zrow_ref[e]).wait()
        lax.fori_loop(n_used_ref[0], nblk, wait_tail, 0)

    if zero_fill:
        pl.when(i == 0)(clear_padding)

    def issue(t, carry):
        src = h_ref.at[pl.ds(pl.multiple_of(t * sub, sub), sub), :]
        for k in range(TOP_K):
            d = dest_refs[k][0, 0, t]
            pltpu.make_async_copy(src, xs_ref.at[pl.ds(pl.multiple_of(d * sub, sub), sub), :],
                                  sem).start(priority=k % 2)
        return carry

    lax.fori_loop(0, td, issue, 0, unroll=ISSUE_UNROLL)
    for k in range(TOP_K):
        pltpu.make_async_copy(h_ref, xs_ref.at[pl.ds(0, td * sub), :], sem).wait()


def _dispatch(h_tiles, dest, zrow, n_used, n_rows, reuse=None):
    sub = ROW_TILE_SUB
    t = h_tiles.shape[0] // sub
    td = min(DISPATCH_TILE, t)
    nt = t // td
    dest_k = [dest[k].reshape(nt, 1, td) for k in range(TOP_K)]
    zero_fill = reuse is None
    kern = functools.partial(_dispatch_kernel, td=td, blk=MOE_BLOCK, nblk=n_rows // MOE_BLOCK,
                             zero_fill=zero_fill)
    extra_specs = [] if zero_fill else [pl.BlockSpec(memory_space=pl.ANY)]
    extra_args = [] if zero_fill else [reuse]
    return pl.pallas_call(
        kern,
        out_shape=jax.ShapeDtypeStruct((n_rows * sub, 128), F32),
        grid_spec=pltpu.PrefetchScalarGridSpec(
            num_scalar_prefetch=2,
            grid=(nt,),
            in_specs=[pl.BlockSpec((1, 1, td), lambda i, z, nu: (i, 0, 0),
                                   memory_space=pltpu.SMEM)] * TOP_K + [
                pl.BlockSpec((td * sub, 128), lambda i, z, nu: (i, 0)),
            ] + extra_specs,
            out_specs=pl.BlockSpec(memory_space=pl.ANY),
            scratch_shapes=[
                pltpu.VMEM((MOE_BLOCK * sub, 128), F32),
                pltpu.SemaphoreType.DMA(()),
                pltpu.SemaphoreType.DMA(()),
            ],
        ),
        compiler_params=pltpu.CompilerParams(
            dimension_semantics=("arbitrary",), vmem_limit_bytes=VMEM_LIMIT_BYTES),
        input_output_aliases={} if zero_fill else {3 + TOP_K: 0},
        name="moe_dispatch" if zero_fill else "moe_dispatch_reuse",
    )(zrow, n_used, *dest_k, h_tiles, *extra_args)


def _expert_kernel(blk_e_ref, n_used_ref, nxt_e_ref, n_valid_ref, x_ref, w1_hbm, b1_ref, w2_hbm,
                   b2_ref, o_ref, w1s_ref, w2s_ref, w1b_ref, w2b_ref, wsem, *, layer):
    step = pl.program_id(0)
    sub_rows = MOE_BLOCK * ROW_TILE_SUB

    def weight_copies(expert):
        return (pltpu.make_async_copy(w1_hbm.at[layer, expert], w1s_ref, wsem.at[0]),
                pltpu.make_async_copy(w2_hbm.at[layer, expert], w2s_ref, wsem.at[1]))

    @pl.when(step == 0)
    def _():
        for cp in weight_copies(blk_e_ref[0]):
            cp.start()

    for sub in range(EXPERT_STEP_BLOCKS):
        i = step * EXPERT_STEP_BLOCKS + sub
        e = blk_e_ref[i]
        prev = blk_e_ref[jnp.maximum(i - 1, 0)]
        fresh = (i == 0) | (e != prev)
        live = i < n_used_ref[0]

        @pl.when(live & fresh)
        def _(i=i, e=e):
            for cp in weight_copies(e):
                cp.wait()
            w1b_ref[...] = w1s_ref[...].astype(BF16)
            w2b_ref[...] = w2s_ref[...].astype(BF16)
            nxt = nxt_e_ref[i]

            @pl.when(nxt >= 0)
            def _():
                for cp in weight_copies(nxt):
                    cp.start()

        def mlp(rows, sub=sub, e=e):
            r0 = sub * MOE_BLOCK
            x = _load_row_tiles(x_ref, rows, start=r0).astype(BF16)
            a = jnp.dot(x, w1b_ref[...], preferred_element_type=F32) + b1_ref[e]
            glu = jnp.minimum(a[:, :D_FF], SWIGLU_LIMIT)
            lin = jnp.clip(a[:, D_FF:], -SWIGLU_LIMIT, SWIGLU_LIMIT)
            act = glu * jax.nn.sigmoid(SWIGLU_ALPHA * glu) * (lin + 1.0)
            y = jnp.dot(act.astype(BF16), w2b_ref[...], preferred_element_type=F32) + b2_ref[e]
            _store_row_tiles(o_ref, y, rows, start=r0)
            if rows < MOE_BLOCK:
                o_ref[(r0 + rows) * ROW_TILE_SUB:(r0 + MOE_BLOCK) * ROW_TILE_SUB, :] = jnp.zeros(
                    ((MOE_BLOCK - rows) * ROW_TILE_SUB, 128), F32)

        half = MOE_BLOCK // 2
        short = n_valid_ref[i] <= half

        @pl.when(live & jnp.logical_not(short))
        def _(mlp=mlp):
            mlp(MOE_BLOCK)

        @pl.when(live & short)
        def _(mlp=mlp):
            mlp(half)

        @pl.when(jnp.logical_not(live) & (step * EXPERT_STEP_BLOCKS < n_used_ref[0]))
        def _(sub=sub):
            o_ref[sub * sub_rows:(sub + 1) * sub_rows, :] = jnp.zeros((sub_rows, 128), F32)


def _experts(xs, blk_e, n_used, nxt_e, n_valid, layer, w1, b1, w2, b2):
    sub = ROW_TILE_SUB
    n_rows = xs.shape[0] // sub
    d = w2.shape[-1]
    nblk = n_rows // MOE_BLOCK
    depth = w1.shape[0]
    grp = EXPERT_STEP_BLOCKS
    assert nblk % grp == 0
    row = lambda i, be, nu, nx, nv: (jnp.minimum(i, (nu[0] - 1) // grp), 0)
    out_row = row
    bias = lambda i, be, nu, nx, nv: (layer, 0, 0, 0)
    return pl.pallas_call(
        functools.partial(_expert_kernel, layer=layer),
        out_shape=jax.ShapeDtypeStruct((n_rows * sub, 128), F32),
        grid_spec=pltpu.PrefetchScalarGridSpec(
            num_scalar_prefetch=4,
            grid=(nblk // grp,),
            in_specs=[
                pl.BlockSpec((grp * MOE_BLOCK * sub, 128), row),
                pl.BlockSpec(memory_space=pl.ANY),
                pl.BlockSpec((None, N_EXPERTS, 1, 2 * D_FF), bias),
                pl.BlockSpec(memory_space=pl.ANY),
                pl.BlockSpec((None, N_EXPERTS, 1, d), bias),
            ],
            out_specs=pl.BlockSpec((grp * MOE_BLOCK * sub, 128), out_row),
            scratch_shapes=[
                pltpu.VMEM((d, 2 * D_FF), F32),
                pltpu.VMEM((D_FF, d), F32),
                pltpu.VMEM((d, 2 * D_FF), BF16),
                pltpu.VMEM((D_FF, d), BF16),
                pltpu.SemaphoreType.DMA((2,)),
            ],
        ),
        compiler_params=pltpu.CompilerParams(
            dimension_semantics=("arbitrary",), vmem_limit_bytes=VMEM_LIMIT_BYTES),
        input_output_aliases={4: 0},
        name="moe_experts",
    )(blk_e, n_used, nxt_e, n_valid, xs, w1, b1.reshape(depth, N_EXPERTS, 1, 2 * D_FF), w2,
      b2.reshape(depth, N_EXPERTS, 1, d))


def _combine_kernel(*refs, tc, n_tiles, final_norm):
    dcur_refs, dnext_refs = refs[:TOP_K], refs[TOP_K:2 * TOP_K]
    x_ref, g_ref, ys_ref, gn_ref, o_ref, buf, sem = refs[2 * TOP_K:]
    i = pl.program_id(0)
    sub = ROW_TILE_SUB

    def issue(drefs, slot):
        def body(t, carry):
            for k in range(TOP_K):
                d = drefs[k][0, 0, t]
                pltpu.make_async_copy(
                    ys_ref.at[pl.ds(pl.multiple_of(d * sub, sub), sub), :],
                    buf.at[slot, k, pl.ds(pl.multiple_of(t * sub, sub), sub), :],
                    sem.at[slot]).start(priority=k % 2)
            return carry
        lax.fori_loop(0, tc, body, 0, unroll=ISSUE_UNROLL)

    @pl.when(i == 0)
    def _():
        issue(dcur_refs, 0)

    @pl.when(i + 1 < n_tiles)
    def _():
        issue(dnext_refs, (i + 1) % 2)

    slot = i % 2
    for k in range(TOP_K):
        pltpu.make_async_copy(ys_ref.at[pl.ds(0, tc * sub), :], buf.at[slot, k],
                              sem.at[slot]).wait()
    acc = x_ref[...]
    gates = g_ref[...]
    for k in range(TOP_K):
        acc = acc + gates[:, k:k + 1] * _load_row_tiles(buf, tc, lead=(slot, k))
    if final_norm:
        acc = _rms(acc, gn_ref[...])
    o_ref[...] = acc


def _combine(x2d, gates_t, dest, ys, g_norm, final_norm):
    t, d = x2d.shape
    tc = min(COMBINE_TILE, t)
    nt = t // tc
    dest_k = [dest[k].reshape(nt, 1, tc) for k in range(TOP_K)]
    cur = pl.BlockSpec((1, 1, tc), lambda i: (i, 0, 0), memory_space=pltpu.SMEM)
    nxt = pl.BlockSpec((1, 1, tc), lambda i: (jnp.minimum(i + 1, nt - 1), 0, 0),
                       memory_space=pltpu.SMEM)
    kern = functools.partial(_combine_kernel, tc=tc, n_tiles=nt, final_norm=final_norm)
    return pl.pallas_call(
        kern,
        out_shape=jax.ShapeDtypeStruct((t, d), F32),
        grid=(nt,),
        in_specs=[cur] * TOP_K + [nxt] * TOP_K + [
            pl.BlockSpec((tc, d), lambda i: (i, 0)),
            pl.BlockSpec((tc, TOP_K), lambda i: (i, 0)),
            pl.BlockSpec(memory_space=pl.ANY),
            pl.BlockSpec((1, d), lambda i: (0, 0)),
        ],
        out_specs=pl.BlockSpec((tc, d), lambda i: (i, 0)),
        scratch_shapes=[
            pltpu.VMEM((2, TOP_K, tc * ROW_TILE_SUB, 128), F32),
            pltpu.SemaphoreType.DMA((2,)),
        ],
        compiler_params=pltpu.CompilerParams(
            dimension_semantics=("arbitrary",), vmem_limit_bytes=VMEM_LIMIT_BYTES),
        name="moe_combine_final" if final_norm else "moe_combine",
    )(*dest_k, *dest_k, x2d, gates_t.T, ys, g_norm.reshape(1, d))


def _moe_experts(routed, layer, w1, b1, w2, b2, reuse=None):
    h, ids, gates, rank, sizes = routed
    t = ids.shape[1]
    sizes = sizes[:, 0].astype(I32)
    padded = ((sizes + MOE_BLOCK - 1) // MOE_BLOCK) * MOE_BLOCK
    pend = jnp.cumsum(padded)
    pstart = pend - padded
    onehot = ids[:, :, None] == jnp.arange(N_EXPERTS, dtype=I32)
    dest = rank + jnp.sum(jnp.where(onehot, pstart, 0), axis=-1)
    n_rows = t * TOP_K + N_EXPERTS * MOE_BLOCK
    nblk = n_rows // MOE_BLOCK
    blk_row = jnp.arange(nblk, dtype=I32) * MOE_BLOCK
    blk_e = jnp.minimum(jnp.sum((pend[None, :] <= blk_row[:, None]).astype(I32), axis=1),
                        N_EXPERTS - 1)
    n_used = (pend[-1:] // MOE_BLOCK).astype(I32)
    zrow = jnp.where(padded > 0, pend - MOE_BLOCK, -1).astype(I32)
    xs = _dispatch(h, dest, zrow, n_used, n_rows, reuse)
    blk_id = jnp.arange(nblk, dtype=I32)
    later = ((blk_id[None, :] > blk_id[:, None]) & (blk_e[None, :] != blk_e[:, None])
             & (blk_id[None, :] < n_used[0]))
    nxt_blk = jnp.min(jnp.where(later, blk_id[None, :], nblk), axis=1)
    nxt_e = jnp.sum(jnp.where(blk_id[None, :] == nxt_blk[:, None], blk_e[None, :] + 1, 0),
                    axis=1).astype(I32) - 1
    own = blk_e[:, None] == jnp.arange(N_EXPERTS, dtype=I32)[None, :]
    row_end = jnp.sum(jnp.where(own, (pstart + sizes)[None, :], 0), axis=1)
    n_valid = jnp.clip(row_end - blk_row, 0, MOE_BLOCK).astype(I32)
    ys = _experts(xs, blk_e, n_used, nxt_e, n_valid, layer, w1, b1, w2, b2)
    return ys, dest


def _qkv_kernel(x_ref, g_ref, w_ref, b_ref, q_ref, k_ref, v_ref):
    h = _rms(x_ref[...], g_ref[...])
    qkv = jnp.dot(h.astype(BF16), w_ref[...], preferred_element_type=F32) + b_ref[...]
    nq = N_HEADS * HEAD_DIM
    nk = 2 * N_KV_HEADS * HEAD_DIM
    q_ref[...] = (qkv[:, :nq] * (HEAD_DIM ** -0.5)).astype(BF16)
    k_ref[...] = qkv[:, nq:nq + nk].astype(BF16)
    v_ref[...] = qkv[:, nq + nk:].astype(BF16)


def _qkv(x2d, g, w_qkv, b_qkv):
    t, d = x2d.shape
    ts = min(ROW_TILE, t)
    nq = N_HEADS * HEAD_DIM
    kv = N_KV_HEADS * HEAD_DIM

    def dup_heads(m):
        parts = []
        for hd in range(N_KV_HEADS):
            sl = m[..., hd * HEAD_DIM:(hd + 1) * HEAD_DIM]
            parts += [sl, sl]
        return jnp.concatenate(parts, axis=-1)

    w = jnp.concatenate([w_qkv[:, :nq], dup_heads(w_qkv[:, nq:nq + kv]),
                         dup_heads(w_qkv[:, nq + kv:])], axis=1).astype(BF16)
    b = jnp.concatenate([b_qkv[:nq], dup_heads(b_qkv[nq:nq + kv]),
                         dup_heads(b_qkv[nq + kv:])]).reshape(1, -1)
    n_out = w.shape[1]
    row = lambda width: pl.BlockSpec((ts, width), lambda i: (i, 0))
    return pl.pallas_call(
        _qkv_kernel,
        out_shape=(
            jax.ShapeDtypeStruct((t, nq), BF16),
            jax.ShapeDtypeStruct((t, 2 * kv), BF16),
            jax.ShapeDtypeStruct((t, 2 * kv), BF16),
        ),
        grid=(t // ts,),
        in_specs=[
            row(d),
            pl.BlockSpec((1, d), lambda i: (0, 0)),
            pl.BlockSpec((d, n_out), lambda i: (0, 0)),
            pl.BlockSpec((1, n_out), lambda i: (0, 0)),
        ],
        out_specs=(row(nq), row(2 * kv), row(2 * kv)),
        compiler_params=pltpu.CompilerParams(
            dimension_semantics=("arbitrary",), vmem_limit_bytes=VMEM_LIMIT_BYTES),
        name="attn_qkv",
    )(x2d, g.reshape(1, d), w, b)


def _attn_kernel(sinks_ref, q_ref, kp_ref, kc_ref, vp_ref, vc_ref, bias_ref, o_ref):
    n = pl.program_id(1)
    rows = SLABS_PER_GROUP * ATT_BLOCK
    keys = 2 * ATT_BLOCK
    kall = jnp.concatenate([kp_ref[...], kc_ref[...]], axis=0)
    vall = jnp.concatenate([vp_ref[...], vc_ref[...]], axis=0)
    j = lax.broadcasted_iota(I32, (keys, rows), 0)
    r = lax.broadcasted_iota(I32, (keys, rows), 1) & (ATT_BLOCK - 1)
    in_window = (j > r) & (j <= r + WINDOW)
    lane_k = lax.broadcasted_iota(I32, (keys, 128), 1)
    halves = [lane_k < HEAD_DIM, lane_k >= HEAD_DIM]
    sub_o = lax.broadcasted_iota(I32, (128, rows), 0)
    zero = jnp.zeros((), BF16)
    for blk in range(ATT_STEP_BLOCKS):
        q = q_ref[blk * ATT_BLOCK:(blk + 1) * ATT_BLOCK, :]
        kcat = kall[blk * ATT_BLOCK:blk * ATT_BLOCK + keys]
        vcat = vall[blk * ATT_BLOCK:blk * ATT_BLOCK + keys]
        if blk == 0:
            mask = in_window & ((n > 0) | (j >= ATT_BLOCK))
        else:
            mask = in_window
        for g in range(N_KV_HEADS):
            s0 = g * SLABS_PER_GROUP
            qg = jnp.concatenate([q[:, (s0 + s) * 128:(s0 + s + 1) * 128]
                                  for s in range(SLABS_PER_GROUP)], axis=0)
            kg = kcat[:, g * 128:(g + 1) * 128]
            vg = vcat[:, g * 128:(g + 1) * 128]
            pts, invs = [], []
            for p in range(HEADS_PER_SLAB):
                kp = jnp.where(halves[p], kg, zero)
                sc = lax.dot_general(kp, qg, (((1,), (1,)), ((), ())),
                                     preferred_element_type=F32)
                sc = jnp.where(mask, sc + bias_ref[g, p], NEG_INF)
                sink = jnp.concatenate(
                    [jnp.full((1, ATT_BLOCK),
                              sinks_ref[g * GQA_GROUP + s * HEADS_PER_SLAB + p], F32)
                     for s in range(SLABS_PER_GROUP)], axis=1)
                m = jnp.maximum(jnp.max(sc, axis=0, keepdims=True), sink)
                pe = jnp.exp(sc - m)
                den = jnp.sum(pe, axis=0, keepdims=True) + jnp.exp(sink - m)
                pts.append(pe.astype(BF16))
                invs.append(1.0 / den)
            pcat = jnp.concatenate(pts, axis=0)
            vblk = jnp.concatenate([jnp.where(halves[p], vg, zero)
                                    for p in range(HEADS_PER_SLAB)], axis=0)
            og = lax.dot_general(vblk, pcat, (((0,), (0,)), ((), ())),
                                 preferred_element_type=F32)
            og = og * jnp.where(sub_o < HEAD_DIM, invs[0], invs[1])
            og = og.T
            for s in range(SLABS_PER_GROUP):
                o_ref[blk * ATT_BLOCK:(blk + 1) * ATT_BLOCK, (s0 + s) * 128:(s0 + s + 1) * 128] = (
                    og[s * ATT_BLOCK:(s + 1) * ATT_BLOCK].astype(BF16))


def _t5_bias_table(rel_bias):
    r = np.arange(ATT_BLOCK)[:, None]
    j = np.arange(2 * ATT_BLOCK)[None, :]
    rel = np.clip(ATT_BLOCK + r - j, 0, WINDOW - 1)
    max_exact = N_BUCKETS // 2
    nf = jnp.maximum(rel, max_exact).astype(F32)
    large = max_exact + (jnp.log(nf / max_exact) / math.log(MAX_DISTANCE / max_exact)
                         * (N_BUCKETS - max_exact)).astype(I32)
    large = jnp.minimum(large, N_BUCKETS - 1)
    bucket = jnp.where(rel < max_exact, rel, large)
    pick = (bucket[None] == jnp.arange(N_BUCKETS, dtype=I32)[:, None, None])
    bias = jnp.sum(jnp.where(pick[:, None], rel_bias.astype(F32)[:, :, None, None], 0.0),
                   axis=0)
    bias = bias.reshape(N_KV_HEADS, SLABS_PER_GROUP, HEADS_PER_SLAB, ATT_BLOCK, 2 * ATT_BLOCK)
    bias = jnp.transpose(bias, (0, 2, 4, 1, 3))
    return bias.reshape(N_KV_HEADS, HEADS_PER_SLAB, 2 * ATT_BLOCK, SLABS_PER_GROUP * ATT_BLOCK)


def _attention(q, kk, vv, sinks, rel_bias, bsz, seq):
    nq = N_HEADS * HEAD_DIM
    kvw = kk.shape[-1]
    step = ATT_STEP_BLOCKS * ATT_BLOCK
    q3 = q.reshape(bsz, seq, nq)
    k3 = kk.reshape(bsz, seq, kvw)
    v3 = vv.reshape(bsz, seq, kvw)
    bias = _t5_bias_table(rel_bias)
    prev = lambda bi, n: (bi, jnp.maximum(n * ATT_STEP_BLOCKS - 1, 0), 0)
    cur = lambda bi, n: (bi, n, 0)
    out = pl.pallas_call(
        _attn_kernel,
        out_shape=jax.ShapeDtypeStruct((bsz, seq, nq), BF16),
        grid=(bsz, seq // step),
        in_specs=[
            pl.BlockSpec(memory_space=pltpu.SMEM),
            pl.BlockSpec((None, step, nq), cur),
            pl.BlockSpec((None, ATT_BLOCK, kvw), prev),
            pl.BlockSpec((None, step, kvw), cur),
            pl.BlockSpec((None, ATT_BLOCK, kvw), prev),
            pl.BlockSpec((None, step, kvw), cur),
            pl.BlockSpec(bias.shape, lambda bi, n: (0, 0, 0, 0)),
        ],
        out_specs=pl.BlockSpec((None, step, nq), cur),
        compiler_params=pltpu.CompilerParams(
            dimension_semantics=("arbitrary", "arbitrary"),
            vmem_limit_bytes=VMEM_LIMIT_BYTES),
        name="attn_core",
    )(sinks.astype(F32), q3, k3, k3, v3, v3, bias)
    return out.reshape(bsz * seq, nq)


def _oproj_kernel(x_ref, o_ref, w_ref, b_ref, gr_ref, wr_ref, br_ref,
                  y_ref, h_ref, ids_ref, gates_ref, rank_ref, sizes_ref, carry_ref, *, ts):
    x_new = x_ref[...] + jnp.dot(o_ref[...], w_ref[...],
                                 preferred_element_type=F32) + b_ref[...]
    y_ref[...] = x_new
    _route_tile(x_new, pl.program_id(0) == 0, gr_ref, wr_ref, br_ref,
                h_ref, ids_ref, gates_ref, rank_ref, sizes_ref, carry_ref, ts)


def _out_proj(x2d, o, w_o, b_o, g_ffn, w_router, b_router):
    t, d = x2d.shape
    ts = min(ROW_TILE, t)
    nq = o.shape[1]
    r_in, r_shape, r_out = _route_specs(t, ts, d, lambda i: i)
    outs = pl.pallas_call(
        functools.partial(_oproj_kernel, ts=ts),
        out_shape=(jax.ShapeDtypeStruct((t, d), F32),) + r_shape,
        grid=(t // ts,),
        in_specs=[
            pl.BlockSpec((ts, d), lambda i: (i, 0)),
            pl.BlockSpec((ts, nq), lambda i: (i, 0)),
            pl.BlockSpec((nq, d), lambda i: (0, 0)),
            pl.BlockSpec((1, d), lambda i: (0, 0)),
        ] + r_in,
        out_specs=(pl.BlockSpec((ts, d), lambda i: (i, 0)),) + r_out,
        scratch_shapes=[pltpu.VMEM((N_EXPERTS, 128), F32)],
        compiler_params=pltpu.CompilerParams(
            dimension_semantics=("arbitrary",), vmem_limit_bytes=VMEM_LIMIT_BYTES),
        name="attn_out_proj_route",
    )(x2d, o, w_o.astype(BF16), b_o.reshape(1, d),
      *_route_operands(g_ffn, w_router, b_router))
    return outs[0], outs[1:]


def kernel(x, norm_mix, norm_ffn, norm_final, pool_w, pool_b, pool_scale, w_qkv, b_qkv,
           sinks, w_o, b_o, rel_bias, w_router, b_router, w1, b1, w2, b2):
    bsz, seq, d = x.shape
    t = bsz * seq
    x, routed = _pool_layer(x, norm_mix[0], pool_w[0], pool_b[0], pool_scale[0],
                            norm_ffn[0], w_router[0], b_router[0])
    x = x.reshape(t, d)
    ys0, dest = _moe_experts(routed, 0, w1, b1, w2, b2)
    x = _combine(x, routed[2], dest, ys0, norm_final, final_norm=False)
    q, kk, vv = _qkv(x, norm_mix[1], w_qkv[0], b_qkv[0])
    o = _attention(q, kk, vv, sinks[0], rel_bias, bsz, seq)
    x, routed = _out_proj(x, o, w_o[0], b_o[0], norm_ffn[1], w_router[1], b_router[1])
    ys, dest = _moe_experts(routed, 1, w1, b1, w2, b2, reuse=ys0)
    x = _combine(x, routed[2], dest, ys, norm_final, final_norm=True)
    return x.reshape(bsz, seq, d)
```

```python
import functools
import math

import jax
import jax.numpy as jnp
import numpy as np
from jax import lax
from jax.experimental import pallas as pl
from jax.experimental.pallas import tpu as pltpu

F32 = jnp.float32
BF16 = jnp.bfloat16
I32 = jnp.int32

D_MODEL = 1024
POOL_WINDOWS = (2, 4, 8, 16)
POOL_GROUP_CH = D_MODEL // len(POOL_WINDOWS)
POOL_HALO = 16
HEAD_DIM = 64
N_HEADS = 16
N_KV_HEADS = 2
GQA_GROUP = N_HEADS // N_KV_HEADS
HEADS_PER_SLAB = 128 // HEAD_DIM
SLABS_PER_GROUP = GQA_GROUP // HEADS_PER_SLAB
ATT_BLOCK = 128
ATT_STEP_BLOCKS = 4
WINDOW = 128
N_BUCKETS = 32
MAX_DISTANCE = 128
N_EXPERTS = 32
TOP_K = 4
D_FF = D_MODEL
SWIGLU_ALPHA = 1.702
SWIGLU_LIMIT = 7.0
RMS_EPS = 1e-5
NEG_INF = -1e30

ROW_TILE = 512
MOE_BLOCK = 512
EXPERT_STEP_BLOCKS = 2
DISPATCH_TILE = 4096
COMBINE_TILE = 256
ISSUE_UNROLL = 4
HBM_SOURCE_FROM = 2
VMEM_LIMIT_BYTES = 56 * 1024 * 1024


def _rms(x, g):
    return x * lax.rsqrt(jnp.mean(x * x, axis=-1, keepdims=True) + RMS_EPS) * g


ROW_TILE_SUB = D_MODEL // 128


def _load_row_tiles(ref, n, lead=(), start=0):
    return jnp.concatenate(
        [ref[lead + (pl.ds(start * ROW_TILE_SUB + j, n, stride=ROW_TILE_SUB), slice(None))]
         for j in range(ROW_TILE_SUB)], axis=1)


def _store_row_tiles(ref, val, n, start=0):
    for j in range(ROW_TILE_SUB):
        ref[pl.ds(start * ROW_TILE_SUB + j, n, stride=ROW_TILE_SUB), :] = (
            val[:, j * 128:(j + 1) * 128])


POOL_PAD = 8


def _pool_kernel(x_ref, xh_ref, g_ref, w_ref, b_ref, s_ref, gr_ref, wr_ref, br_ref,
                 o_ref, h_ref, ids_ref, gates_ref, rank_ref, sizes_ref,
                 buf_ref, s1_ref, s2_ref, s3_ref, carry_ref, *, ts):
    i = pl.program_id(1)
    c = POOL_GROUP_CH
    h0 = POOL_PAD
    t0 = POOL_PAD + POOL_HALO
    r1 = t0 + ts
    x = x_ref[...]
    g = g_ref[...]
    h = _rms(x, g)
    hh = _rms(xh_ref[...], g)
    hh = jnp.where(i == 0, 0.0, hh)
    buf_ref[0:h0, :] = jnp.zeros((h0, D_MODEL), F32)
    buf_ref[h0:t0, :] = hh
    buf_ref[t0:r1, :] = h
    sum0 = h[:, 0:c] + buf_ref[t0 - 1:r1 - 1, 0:c]
    l1 = buf_ref[h0:r1, c:] + buf_ref[h0 - 1:r1 - 1, c:]
    s1_ref[0:h0, :] = jnp.zeros((h0, 3 * c), F32)
    s1_ref[h0:r1, :] = l1
    sum1 = s1_ref[t0:r1, 0:c] + s1_ref[t0 - 2:r1 - 2, 0:c]
    l2 = s1_ref[h0:r1, c:] + s1_ref[h0 - 2:r1 - 2, c:]
    s2_ref[0:h0, :] = jnp.zeros((h0, 2 * c), F32)
    s2_ref[h0:r1, :] = l2
    sum2 = s2_ref[t0:r1, 0:c] + s2_ref[t0 - 4:r1 - 4, 0:c]
    s3_ref[h0:r1, :] = s2_ref[h0:r1, c:] + s2_ref[h0 - 4:r1 - 4, c:]
    sum3 = s3_ref[t0:r1, :] + s3_ref[t0 - 8:r1 - 8, :]
    t = i * ts + lax.broadcasted_iota(I32, (ts, 1), 0)
    outs = []
    for gi, (win, acc) in enumerate(zip(POOL_WINDOWS, (sum0, sum1, sum2, sum3))):
        cnt = jnp.minimum(t + 1, win).astype(F32)
        dlt = acc / cnt - h[:, gi * c:(gi + 1) * c]
        outs.append(jnp.dot(dlt.astype(BF16), w_ref[gi], preferred_element_type=F32))
    y = jnp.concatenate(outs, axis=1)
    x_new = x + (y + b_ref[...]) * s_ref[...]
    o_ref[...] = x_new
    first = (pl.program_id(0) == 0) & (i == 0)
    _route_tile(x_new, first, gr_ref, wr_ref, br_ref,
                h_ref, ids_ref, gates_ref, rank_ref, sizes_ref, carry_ref, ts)


def _pool_layer(x, g, w, b, s, g_ffn, w_router, b_router):
    bsz, seq, d = x.shape
    ts = min(ROW_TILE, seq)
    tiles = seq // ts
    kern = functools.partial(_pool_kernel, ts=ts)
    rows = POOL_PAD + POOL_HALO + ts
    vec = lambda: pl.BlockSpec((1, d), lambda bi, i: (0, 0))
    r_in, r_shape, r_out = _route_specs(bsz * seq, ts, d, lambda bi, i: bi * tiles + i)
    outs = pl.pallas_call(
        kern,
        out_shape=(jax.ShapeDtypeStruct(x.shape, F32),) + r_shape,
        grid=(bsz, tiles),
        in_specs=[
            pl.BlockSpec((None, ts, d), lambda bi, i: (bi, i, 0)),
            pl.BlockSpec((None, POOL_HALO, d),
                         lambda bi, i: (bi, jnp.maximum(i * (ts // POOL_HALO) - 1, 0), 0)),
            vec(),
            pl.BlockSpec(w.shape, lambda bi, i: (0, 0, 0)),
            vec(),
            vec(),
        ] + r_in,
        out_specs=(pl.BlockSpec((None, ts, d), lambda bi, i: (bi, i, 0)),) + r_out,
        scratch_shapes=[pltpu.VMEM((rows, d), F32),
                        pltpu.VMEM((rows, 3 * POOL_GROUP_CH), F32),
                        pltpu.VMEM((rows, 2 * POOL_GROUP_CH), F32),
                        pltpu.VMEM((rows, POOL_GROUP_CH), F32),
                        pltpu.VMEM((N_EXPERTS, 128), F32)],
        compiler_params=pltpu.CompilerParams(
            dimension_semantics=("arbitrary", "arbitrary"),
            vmem_limit_bytes=VMEM_LIMIT_BYTES),
        name="pool_route",
    )(x, x, g.reshape(1, d), w.astype(BF16), b.reshape(1, d), s.reshape(1, d),
      *_route_operands(g_ffn, w_router, b_router))
    return outs[0], outs[1:]


def _route_tile(x, first, g_ref, wr_ref, br_ref,
                h_ref, ids_ref, gates_ref, rank_ref, sizes_ref, carry_ref, ts):
    @pl.when(first)
    def _():
        carry_ref[...] = jnp.zeros_like(carry_ref)

    h = _rms(x, g_ref[...])
    _store_row_tiles(h_ref, h, ts)
    h_hi = h.astype(BF16)
    h_lo = (h - h_hi.astype(F32)).astype(BF16)
    w = wr_ref[...]
    w_hi = w.astype(BF16)
    w_lo = (w - w_hi.astype(F32)).astype(BF16)
    dn = (((1,), (1,)), ((), ()))
    lg = (lax.dot_general(w_hi, h_hi, dn, preferred_element_type=F32)
          + lax.dot_general(w_lo, h_hi, dn, preferred_element_type=F32)
          + lax.dot_general(w_hi, h_lo, dn, preferred_element_type=F32))
    lg = lg + br_ref[:, 0:1]

    iota_e = lax.broadcasted_iota(I32, (N_EXPERTS, ts), 0)
    vals, ids = [], []
    for _ in range(TOP_K):
        m = jnp.max(lg, axis=0, keepdims=True)
        idx = jnp.min(jnp.where(lg == m, iota_e, N_EXPERTS), axis=0, keepdims=True)
        vals.append(m)
        ids.append(idx)
        lg = jnp.where(iota_e == idx, -jnp.inf, lg)
    ex = [jnp.exp(v - vals[0]) for v in vals]
    den = ex[0] + ex[1] + ex[2] + ex[3]
    gates_ref[...] = jnp.concatenate([e / den for e in ex], axis=0)
    ids_ref[...] = jnp.concatenate(ids, axis=0)

    hot = [(iota_e == idx) for idx in ids]
    multi = (hot[0] | hot[1] | hot[2] | hot[3])
    multi_f = multi.astype(F32)
    r = lax.broadcasted_iota(I32, (ts, ts), 0)
    c = lax.broadcasted_iota(I32, (ts, ts), 1)
    upper = (r < c).astype(BF16)
    before = jnp.dot(multi_f.astype(BF16), upper, preferred_element_type=F32)
    before = before + carry_ref[:, 0:1]
    ranks = [jnp.sum(jnp.where(hk, before, 0.0), axis=0, keepdims=True) for hk in hot]
    rank_ref[...] = jnp.concatenate(ranks, axis=0).astype(I32)
    carry_ref[...] = carry_ref[...] + jnp.sum(multi_f, axis=1, keepdims=True)
    sizes_ref[...] = carry_ref[...]


def _route_operands(g, w_router, b_router):
    d = g.shape[0]
    return (g.reshape(1, d), w_router.T,
            jnp.broadcast_to(b_router.reshape(N_EXPERTS, 1), (N_EXPERTS, 128)))


def _route_specs(t, ts, d, tile_of):
    const = lambda *idx: (0, 0)
    in_specs = [pl.BlockSpec((1, d), const), pl.BlockSpec((N_EXPERTS, d), const),
                pl.BlockSpec((N_EXPERTS, 128), const)]
    out_shape = (
        jax.ShapeDtypeStruct((t * ROW_TILE_SUB, 128), F32),
        jax.ShapeDtypeStruct((TOP_K, t), I32),
        jax.ShapeDtypeStruct((TOP_K, t), F32),
        jax.ShapeDtypeStruct((TOP_K, t), I32),
        jax.ShapeDtypeStruct((N_EXPERTS, 128), F32),
    )
    tok = lambda: pl.BlockSpec((TOP_K, ts), lambda *idx: (0, tile_of(*idx)))
    out_specs = (
        pl.BlockSpec((ts * ROW_TILE_SUB, 128), lambda *idx: (tile_of(*idx), 0)),
        tok(), tok(), tok(),
        pl.BlockSpec((N_EXPERTS, 128), const),
    )
    return in_specs, out_shape, out_specs


def _dispatch_kernel(zrow_ref, n_used_ref, *refs, td, blk, nblk, zero_fill):
    dest_refs, h_ref, h_hbm, rest = refs[:TOP_K], refs[TOP_K], refs[TOP_K + 1], refs[TOP_K + 2:]
    xs_ref, zbuf, sem_z, sem = rest if zero_fill else rest[1:]
    i = pl.program_id(0)
    sub = ROW_TILE_SUB

    def zero_block(row):
        row = pl.multiple_of(row * sub, blk * sub)
        return pltpu.make_async_copy(zbuf, xs_ref.at[pl.ds(row, blk * sub), :], sem_z)

    def clear_padding():
        zbuf[...] = jnp.zeros_like(zbuf)
        for e in range(N_EXPERTS):
            @pl.when(zrow_ref[e] >= 0)
            def _():
                zero_block(zrow_ref[e]).start()

        def start_tail(b, carry):
            zero_block(b * blk).start()
            return carry

        def wait_tail(b, carry):
            zero_block(b * blk).wait()
            return carry

        lax.fori_loop(n_used_ref[0], nblk, start_tail, 0)
        for e in range(N_EXPERTS):
            @pl.when(zrow_ref[e] >= 0)
            def _():
                zero_block(zrow_ref[e]).wait()
        lax.fori_loop(n_used_ref[0], nblk, wait_tail, 0)

    if zero_fill:
        pl.when(i == 0)(clear_padding)

    def issue(t, carry):
        src_vmem = h_ref.at[pl.ds(pl.multiple_of(t * sub, sub), sub), :]
        src_hbm = h_hbm.at[pl.ds(pl.multiple_of((i * td + t) * sub, sub), sub), :]
        for k in range(TOP_K):
            d = dest_refs[k][0, 0, t]
            src = src_vmem if k < HBM_SOURCE_FROM else src_hbm
            pltpu.make_async_copy(src, xs_ref.at[pl.ds(pl.multiple_of(d * sub, sub), sub), :],
                                  sem).start(priority=k % 2)
        return carry

    lax.fori_loop(0, td, issue, 0, unroll=ISSUE_UNROLL)
    for k in range(TOP_K):
        pltpu.make_async_copy(h_ref, xs_ref.at[pl.ds(0, td * sub), :], sem).wait()


def _dispatch(h_tiles, dest, zrow, n_used, n_rows, reuse=None):
    sub = ROW_TILE_SUB
    t = h_tiles.shape[0] // sub
    td = min(DISPATCH_TILE, t)
    nt = t // td
    dest_k = [dest[k].reshape(nt, 1, td) for k in range(TOP_K)]
    zero_fill = reuse is None
    kern = functools.partial(_dispatch_kernel, td=td, blk=MOE_BLOCK, nblk=n_rows // MOE_BLOCK,
                             zero_fill=zero_fill)
    extra_specs = [] if zero_fill else [pl.BlockSpec(memory_space=pl.ANY)]
    extra_args = [] if zero_fill else [reuse]
    return pl.pallas_call(
        kern,
        out_shape=jax.ShapeDtypeStruct((n_rows * sub, 128), F32),
        grid_spec=pltpu.PrefetchScalarGridSpec(
            num_scalar_prefetch=2,
            grid=(nt,),
            in_specs=[pl.BlockSpec((1, 1, td), lambda i, z, nu: (i, 0, 0),
                                   memory_space=pltpu.SMEM)] * TOP_K + [
                pl.BlockSpec((td * sub, 128), lambda i, z, nu: (i, 0)),
                pl.BlockSpec(memory_space=pl.ANY),
            ] + extra_specs,
            out_specs=pl.BlockSpec(memory_space=pl.ANY),
            scratch_shapes=[
                pltpu.VMEM((MOE_BLOCK * sub, 128), F32),
                pltpu.SemaphoreType.DMA(()),
                pltpu.SemaphoreType.DMA(()),
            ],
        ),
        compiler_params=pltpu.CompilerParams(
            dimension_semantics=("arbitrary",), vmem_limit_bytes=VMEM_LIMIT_BYTES),
        input_output_aliases={} if zero_fill else {4 + TOP_K: 0},
        name="moe_dispatch" if zero_fill else "moe_dispatch_reuse",
    )(zrow, n_used, *dest_k, h_tiles, h_tiles, *extra_args)


def _expert_kernel(blk_e_ref, n_used_ref, nxt_e_ref, n_valid_ref, x_ref, w1_hbm, b1_ref, w2_hbm,
                   b2_ref, o_ref, w1s_ref, w2s_ref, w1b_ref, w2b_ref, wsem, *, layer):
    step = pl.program_id(0)
    sub_rows = MOE_BLOCK * ROW_TILE_SUB

    def weight_copies(expert):
        return (pltpu.make_async_copy(w1_hbm.at[layer, expert], w1s_ref, wsem.at[0]),
                pltpu.make_async_copy(w2_hbm.at[layer, expert], w2s_ref, wsem.at[1]))

    @pl.when(step == 0)
    def _():
        for cp in weight_copies(blk_e_ref[0]):
            cp.start()

    for sub in range(EXPERT_STEP_BLOCKS):
        i = step * EXPERT_STEP_BLOCKS + sub
        e = blk_e_ref[i]
        prev = blk_e_ref[jnp.maximum(i - 1, 0)]
        fresh = (i == 0) | (e != prev)
        live = i < n_used_ref[0]

        @pl.when(live & fresh)
        def _(i=i, e=e):
            for cp in weight_copies(e):
                cp.wait()
            w1b_ref[...] = w1s_ref[...].astype(BF16)
            w2b_ref[...] = w2s_ref[...].astype(BF16)
            nxt = nxt_e_ref[i]

            @pl.when(nxt >= 0)
            def _():
                for cp in weight_copies(nxt):
                    cp.start()

        def mlp(rows, sub=sub, e=e):
            r0 = sub * MOE_BLOCK
            x = _load_row_tiles(x_ref, rows, start=r0).astype(BF16)
            a = jnp.dot(x, w1b_ref[...], preferred_element_type=F32) + b1_ref[e]
            glu = jnp.minimum(a[:, :D_FF], SWIGLU_LIMIT)
            lin = jnp.clip(a[:, D_FF:], -SWIGLU_LIMIT, SWIGLU_LIMIT)
            act = glu * jax.nn.sigmoid(SWIGLU_ALPHA * glu) * (lin + 1.0)
            y = jnp.dot(act.astype(BF16), w2b_ref[...], preferred_element_type=F32) + b2_ref[e]
            _store_row_tiles(o_ref, y, rows, start=r0)
            if rows < MOE_BLOCK:
                o_ref[(r0 + rows) * ROW_TILE_SUB:(r0 + MOE_BLOCK) * ROW_TILE_SUB, :] = jnp.zeros(
                    ((MOE_BLOCK - rows) * ROW_TILE_SUB, 128), F32)

        half = MOE_BLOCK // 2
        short = n_valid_ref[i] <= half

        @pl.when(live & jnp.logical_not(short))
        def _(mlp=mlp):
            mlp(MOE_BLOCK)

        @pl.when(live & short)
        def _(mlp=mlp):
            mlp(half)

        @pl.when(jnp.logical_not(live) & (step * EXPERT_STEP_BLOCKS < n_used_ref[0]))
        def _(sub=sub):
            o_ref[sub * sub_rows:(sub + 1) * sub_rows, :] = jnp.zeros((sub_rows, 128), F32)


def _experts(xs, blk_e, n_used, nxt_e, n_valid, layer, w1, b1, w2, b2):
    sub = ROW_TILE_SUB
    n_rows = xs.shape[0] // sub
    d = w2.shape[-1]
    nblk = n_rows // MOE_BLOCK
    depth = w1.shape[0]
    grp = EXPERT_STEP_BLOCKS
    assert nblk % grp == 0
    row = lambda i, be, nu, nx, nv: (jnp.minimum(i, (nu[0] - 1) // grp), 0)
    out_row = row
    bias = lambda i, be, nu, nx, nv: (layer, 0, 0, 0)
    return pl.pallas_call(
        functools.partial(_expert_kernel, layer=layer),
        out_shape=jax.ShapeDtypeStruct((n_rows * sub, 128), F32),
        grid_spec=pltpu.PrefetchScalarGridSpec(
            num_scalar_prefetch=4,
            grid=(nblk // grp,),
            in_specs=[
                pl.BlockSpec((grp * MOE_BLOCK * sub, 128), row),
                pl.BlockSpec(memory_space=pl.ANY),
                pl.BlockSpec((None, N_EXPERTS, 1, 2 * D_FF), bias),
                pl.BlockSpec(memory_space=pl.ANY),
                pl.BlockSpec((None, N_EXPERTS, 1, d), bias),
            ],
            out_specs=pl.BlockSpec((grp * MOE_BLOCK * sub, 128), out_row),
            scratch_shapes=[
                pltpu.VMEM((d, 2 * D_FF), F32),
                pltpu.VMEM((D_FF, d), F32),
                pltpu.VMEM((d, 2 * D_FF), BF16),
                pltpu.VMEM((D_FF, d), BF16),
                pltpu.SemaphoreType.DMA((2,)),
            ],
        ),
        compiler_params=pltpu.CompilerParams(
            dimension_semantics=("arbitrary",), vmem_limit_bytes=VMEM_LIMIT_BYTES),
        input_output_aliases={4: 0},
        name="moe_experts",
    )(blk_e, n_used, nxt_e, n_valid, xs, w1, b1.reshape(depth, N_EXPERTS, 1, 2 * D_FF), w2,
      b2.reshape(depth, N_EXPERTS, 1, d))


def _combine_kernel(*refs, tc, n_tiles, final_norm):
    dcur_refs, dnext_refs = refs[:TOP_K], refs[TOP_K:2 * TOP_K]
    x_ref, g_ref, ys_ref, gn_ref, o_ref, buf, sem = refs[2 * TOP_K:]
    i = pl.program_id(0)
    sub = ROW_TILE_SUB

    def issue(drefs, slot):
        def body(t, carry):
            for k in range(TOP_K):
                d = drefs[k][0, 0, t]
                pltpu.make_async_copy(
                    ys_ref.at[pl.ds(pl.multiple_of(d * sub, sub), sub), :],
                    buf.at[slot, k, pl.ds(pl.multiple_of(t * sub, sub), sub), :],
                    sem.at[slot]).start(priority=k % 2)
            return carry
        lax.fori_loop(0, tc, body, 0, unroll=ISSUE_UNROLL)

    @pl.when(i == 0)
    def _():
        issue(dcur_refs, 0)

    @pl.when(i + 1 < n_tiles)
    def _():
        issue(dnext_refs, (i + 1) % 2)

    slot = i % 2
    for k in range(TOP_K):
        pltpu.make_async_copy(ys_ref.at[pl.ds(0, tc * sub), :], buf.at[slot, k],
                              sem.at[slot]).wait()
    acc = x_ref[...]
    gates = g_ref[...]
    for k in range(TOP_K):
        acc = acc + gates[:, k:k + 1] * _load_row_tiles(buf, tc, lead=(slot, k))
    if final_norm:
        acc = _rms(acc, gn_ref[...])
    o_ref[...] = acc


def _combine(x2d, gates_t, dest, ys, g_norm, final_norm):
    t, d = x2d.shape
    tc = min(COMBINE_TILE, t)
    nt = t // tc
    dest_k = [dest[k].reshape(nt, 1, tc) for k in range(TOP_K)]
    cur = pl.BlockSpec((1, 1, tc), lambda i: (i, 0, 0), memory_space=pltpu.SMEM)
    nxt = pl.BlockSpec((1, 1, tc), lambda i: (jnp.minimum(i + 1, nt - 1), 0, 0),
                       memory_space=pltpu.SMEM)
    kern = functools.partial(_combine_kernel, tc=tc, n_tiles=nt, final_norm=final_norm)
    return pl.pallas_call(
        kern,
        out_shape=jax.ShapeDtypeStruct((t, d), F32),
        grid=(nt,),
        in_specs=[cur] * TOP_K + [nxt] * TOP_K + [
            pl.BlockSpec((tc, d), lambda i: (i, 0)),
            pl.BlockSpec((tc, TOP_K), lambda i: (i, 0)),
            pl.BlockSpec(memory_space=pl.ANY),
            pl.BlockSpec((1, d), lambda i: (0, 0)),
        ],
        out_specs=pl.BlockSpec((tc, d), lambda i: (i, 0)),
        scratch_shapes=[
            pltpu.VMEM((2, TOP_K, tc * ROW_TILE_SUB, 128), F32),
            pltpu.SemaphoreType.DMA((2,)),
        ],
        compiler_params=pltpu.CompilerParams(
            dimension_semantics=("arbitrary",), vmem_limit_bytes=VMEM_LIMIT_BYTES),
        name="moe_combine_final" if final_norm else "moe_combine",
    )(*dest_k, *dest_k, x2d, gates_t.T, ys, g_norm.reshape(1, d))


def _moe_experts(routed, layer, w1, b1, w2, b2, reuse=None):
    h, ids, gates, rank, sizes = routed
    t = ids.shape[1]
    sizes = sizes[:, 0].astype(I32)
    padded = ((sizes + MOE_BLOCK - 1) // MOE_BLOCK) * MOE_BLOCK
    pend = jnp.cumsum(padded)
    pstart = pend - padded
    onehot = ids[:, :, None] == jnp.arange(N_EXPERTS, dtype=I32)
    dest = rank + jnp.sum(jnp.where(onehot, pstart, 0), axis=-1)
    n_rows = t * TOP_K + N_EXPERTS * MOE_BLOCK
    nblk = n_rows // MOE_BLOCK
    blk_row = jnp.arange(nblk, dtype=I32) * MOE_BLOCK
    blk_e = jnp.minimum(jnp.sum((pend[None, :] <= blk_row[:, None]).astype(I32), axis=1),
                        N_EXPERTS - 1)
    n_used = (pend[-1:] // MOE_BLOCK).astype(I32)
    zrow = jnp.where(padded > 0, pend - MOE_BLOCK, -1).astype(I32)
    xs = _dispatch(h, dest, zrow, n_used, n_rows, reuse)
    blk_id = jnp.arange(nblk, dtype=I32)
    later = ((blk_id[None, :] > blk_id[:, None]) & (blk_e[None, :] != blk_e[:, None])
             & (blk_id[None, :] < n_used[0]))
    nxt_blk = jnp.min(jnp.where(later, blk_id[None, :], nblk), axis=1)
    nxt_e = jnp.sum(jnp.where(blk_id[None, :] == nxt_blk[:, None], blk_e[None, :] + 1, 0),
                    axis=1).astype(I32) - 1
    own = blk_e[:, None] == jnp.arange(N_EXPERTS, dtype=I32)[None, :]
    row_end = jnp.sum(jnp.where(own, (pstart + sizes)[None, :], 0), axis=1)
    n_valid = jnp.clip(row_end - blk_row, 0, MOE_BLOCK).astype(I32)
    ys = _experts(xs, blk_e, n_used, nxt_e, n_valid, layer, w1, b1, w2, b2)
    return ys, dest


def _qkv_kernel(x_ref, g_ref, w_ref, b_ref, q_ref, k_ref, v_ref):
    h = _rms(x_ref[...], g_ref[...])
    qkv = jnp.dot(h.astype(BF16), w_ref[...], preferred_element_type=F32) + b_ref[...]
    nq = N_HEADS * HEAD_DIM
    nk = 2 * N_KV_HEADS * HEAD_DIM
    q_ref[...] = (qkv[:, :nq] * (HEAD_DIM ** -0.5)).astype(BF16)
    k_ref[...] = qkv[:, nq:nq + nk].astype(BF16)
    v_ref[...] = qkv[:, nq + nk:].astype(BF16)


def _qkv(x2d, g, w_qkv, b_qkv):
    t, d = x2d.shape
    ts = min(ROW_TILE, t)
    nq = N_HEADS * HEAD_DIM
    kv = N_KV_HEADS * HEAD_DIM

    def dup_heads(m):
        parts = []
        for hd in range(N_KV_HEADS):
            sl = m[..., hd * HEAD_DIM:(hd + 1) * HEAD_DIM]
            parts += [sl, sl]
        return jnp.concatenate(parts, axis=-1)

    w = jnp.concatenate([w_qkv[:, :nq], dup_heads(w_qkv[:, nq:nq + kv]),
                         dup_heads(w_qkv[:, nq + kv:])], axis=1).astype(BF16)
    b = jnp.concatenate([b_qkv[:nq], dup_heads(b_qkv[nq:nq + kv]),
                         dup_heads(b_qkv[nq + kv:])]).reshape(1, -1)
    n_out = w.shape[1]
    row = lambda width: pl.BlockSpec((ts, width), lambda i: (i, 0))
    return pl.pallas_call(
        _qkv_kernel,
        out_shape=(
            jax.ShapeDtypeStruct((t, nq), BF16),
            jax.ShapeDtypeStruct((t, 2 * kv), BF16),
            jax.ShapeDtypeStruct((t, 2 * kv), BF16),
        ),
        grid=(t // ts,),
        in_specs=[
            row(d),
            pl.BlockSpec((1, d), lambda i: (0, 0)),
            pl.BlockSpec((d, n_out), lambda i: (0, 0)),
            pl.BlockSpec((1, n_out), lambda i: (0, 0)),
        ],
        out_specs=(row(nq), row(2 * kv), row(2 * kv)),
        compiler_params=pltpu.CompilerParams(
            dimension_semantics=("arbitrary",), vmem_limit_bytes=VMEM_LIMIT_BYTES),
        name="attn_qkv",
    )(x2d, g.reshape(1, d), w, b)


def _attn_kernel(sinks_ref, q_ref, kp_ref, kc_ref, vp_ref, vc_ref, bias_ref, o_ref):
    n = pl.program_id(1)
    rows = SLABS_PER_GROUP * ATT_BLOCK
    keys = 2 * ATT_BLOCK
    kall = jnp.concatenate([kp_ref[...], kc_ref[...]], axis=0)
    vall = jnp.concatenate([vp_ref[...], vc_ref[...]], axis=0)
    j = lax.broadcasted_iota(I32, (keys, rows), 0)
    r = lax.broadcasted_iota(I32, (keys, rows), 1) & (ATT_BLOCK - 1)
    in_window = (j > r) & (j <= r + WINDOW)
    lane_k = lax.broadcasted_iota(I32, (keys, 128), 1)
    halves = [lane_k < HEAD_DIM, lane_k >= HEAD_DIM]
    sub_o = lax.broadcasted_iota(I32, (128, rows), 0)
    zero = jnp.zeros((), BF16)
    for blk in range(ATT_STEP_BLOCKS):
        q = q_ref[blk * ATT_BLOCK:(blk + 1) * ATT_BLOCK, :]
        kcat = kall[blk * ATT_BLOCK:blk * ATT_BLOCK + keys]
        vcat = vall[blk * ATT_BLOCK:blk * ATT_BLOCK + keys]
        if blk == 0:
            mask = in_window & ((n > 0) | (j >= ATT_BLOCK))
        else:
            mask = in_window
        for g in range(N_KV_HEADS):
            s0 = g * SLABS_PER_GROUP
            qg = jnp.concatenate([q[:, (s0 + s) * 128:(s0 + s + 1) * 128]
                                  for s in range(SLABS_PER_GROUP)], axis=0)
            kg = kcat[:, g * 128:(g + 1) * 128]
            vg = vcat[:, g * 128:(g + 1) * 128]
            pts, invs = [], []
            for p in range(HEADS_PER_SLAB):
                kp = jnp.where(halves[p], kg, zero)
                sc = lax.dot_general(kp, qg, (((1,), (1,)), ((), ())),
                                     preferred_element_type=F32)
                sc = jnp.where(mask, sc + bias_ref[g, p], NEG_INF)
                sink = jnp.concatenate(
                    [jnp.full((1, ATT_BLOCK),
                              sinks_ref[g * GQA_GROUP + s * HEADS_PER_SLAB + p], F32)
                     for s in range(SLABS_PER_GROUP)], axis=1)
                m = jnp.maximum(jnp.max(sc, axis=0, keepdims=True), sink)
                pe = jnp.exp(sc - m)
                den = jnp.sum(pe, axis=0, keepdims=True) + jnp.exp(sink - m)
                pts.append(pe.astype(BF16))
                invs.append(1.0 / den)
            pcat = jnp.concatenate(pts, axis=0)
            vblk = jnp.concatenate([jnp.where(halves[p], vg, zero)
                                    for p in range(HEADS_PER_SLAB)], axis=0)
            og = lax.dot_general(vblk, pcat, (((0,), (0,)), ((), ())),
                                 preferred_element_type=F32)
            og = og * jnp.where(sub_o < HEAD_DIM, invs[0], invs[1])
            og = og.T
            for s in range(SLABS_PER_GROUP):
                o_ref[blk * ATT_BLOCK:(blk + 1) * ATT_BLOCK, (s0 + s) * 128:(s0 + s + 1) * 128] = (
                    og[s * ATT_BLOCK:(s + 1) * ATT_BLOCK].astype(BF16))


def _t5_bias_table(rel_bias):
    r = np.arange(ATT_BLOCK)[:, None]
    j = np.arange(2 * ATT_BLOCK)[None, :]
    rel = np.clip(ATT_BLOCK + r - j, 0, WINDOW - 1)
    max_exact = N_BUCKETS // 2
    nf = jnp.maximum(rel, max_exact).astype(F32)
    large = max_exact + (jnp.log(nf / max_exact) / math.log(MAX_DISTANCE / max_exact)
                         * (N_BUCKETS - max_exact)).astype(I32)
    large = jnp.minimum(large, N_BUCKETS - 1)
    bucket = jnp.where(rel < max_exact, rel, large)
    pick = (bucket[None] == jnp.arange(N_BUCKETS, dtype=I32)[:, None, None])
    bias = jnp.sum(jnp.where(pick[:, None], rel_bias.astype(F32)[:, :, None, None], 0.0),
                   axis=0)
    bias = bias.reshape(N_KV_HEADS, SLABS_PER_GROUP, HEADS_PER_SLAB, ATT_BLOCK, 2 * ATT_BLOCK)
    bias = jnp.transpose(bias, (0, 2, 4, 1, 3))
    return bias.reshape(N_KV_HEADS, HEADS_PER_SLAB, 2 * ATT_BLOCK, SLABS_PER_GROUP * ATT_BLOCK)


def _attention(q, kk, vv, sinks, rel_bias, bsz, seq):
    nq = N_HEADS * HEAD_DIM
    kvw = kk.shape[-1]
    step = ATT_STEP_BLOCKS * ATT_BLOCK
    q3 = q.reshape(bsz, seq, nq)
    k3 = kk.reshape(bsz, seq, kvw)
    v3 = vv.reshape(bsz, seq, kvw)
    bias = _t5_bias_table(rel_bias)
    prev = lambda bi, n: (bi, jnp.maximum(n * ATT_STEP_BLOCKS - 1, 0), 0)
    cur = lambda bi, n: (bi, n, 0)
    out = pl.pallas_call(
        _attn_kernel,
        out_shape=jax.ShapeDtypeStruct((bsz, seq, nq), BF16),
        grid=(bsz, seq // step),
        in_specs=[
            pl.BlockSpec(memory_space=pltpu.SMEM),
            pl.BlockSpec((None, step, nq), cur),
            pl.BlockSpec((None, ATT_BLOCK, kvw), prev),
            pl.BlockSpec((None, step, kvw), cur),
            pl.BlockSpec((None, ATT_BLOCK, kvw), prev),
            pl.BlockSpec((None, step, kvw), cur),
            pl.BlockSpec(bias.shape, lambda bi, n: (0, 0, 0, 0)),
        ],
        out_specs=pl.BlockSpec((None, step, nq), cur),
        compiler_params=pltpu.CompilerParams(
            dimension_semantics=("arbitrary", "arbitrary"),
            vmem_limit_bytes=VMEM_LIMIT_BYTES),
        name="attn_core",
    )(sinks.astype(F32), q3, k3, k3, v3, v3, bias)
    return out.reshape(bsz * seq, nq)


def _oproj_kernel(x_ref, o_ref, w_ref, b_ref, gr_ref, wr_ref, br_ref,
                  y_ref, h_ref, ids_ref, gates_ref, rank_ref, sizes_ref, carry_ref, *, ts):
    x_new = x_ref[...] + jnp.dot(o_ref[...], w_ref[...],
                                 preferred_element_type=F32) + b_ref[...]
    y_ref[...] = x_new
    _route_tile(x_new, pl.program_id(0) == 0, gr_ref, wr_ref, br_ref,
                h_ref, ids_ref, gates_ref, rank_ref, sizes_ref, carry_ref, ts)


def _out_proj(x2d, o, w_o, b_o, g_ffn, w_router, b_router):
    t, d = x2d.shape
    ts = min(ROW_TILE, t)
    nq = o.shape[1]
    r_in, r_shape, r_out = _route_specs(t, ts, d, lambda i: i)
    outs = pl.pallas_call(
        functools.partial(_oproj_kernel, ts=ts),
        out_shape=(jax.ShapeDtypeStruct((t, d), F32),) + r_shape,
        grid=(t // ts,),
        in_specs=[
            pl.BlockSpec((ts, d), lambda i: (i, 0)),
            pl.BlockSpec((ts, nq), lambda i: (i, 0)),
            pl.BlockSpec((nq, d), lambda i: (0, 0)),
            pl.BlockSpec((1, d), lambda i: (0, 0)),
        ] + r_in,
        out_specs=(pl.BlockSpec((ts, d), lambda i: (i, 0)),) + r_out,
        scratch_shapes=[pltpu.VMEM((N_EXPERTS, 128), F32)],
        compiler_params=pltpu.CompilerParams(
            dimension_semantics=("arbitrary",), vmem_limit_bytes=VMEM_LIMIT_BYTES),
        name="attn_out_proj_route",
    )(x2d, o, w_o.astype(BF16), b_o.reshape(1, d),
      *_route_operands(g_ffn, w_router, b_router))
    return outs[0], outs[1:]


def kernel(x, norm_mix, norm_ffn, norm_final, pool_w, pool_b, pool_scale, w_qkv, b_qkv,
           sinks, w_o, b_o, rel_bias, w_router, b_router, w1, b1, w2, b2):
    bsz, seq, d = x.shape
    t = bsz * seq
    x, routed = _pool_layer(x, norm_mix[0], pool_w[0], pool_b[0], pool_scale[0],
                            norm_ffn[0], w_router[0], b_router[0])
    x = x.reshape(t, d)
    ys0, dest = _moe_experts(routed, 0, w1, b1, w2, b2)
    x = _combine(x, routed[2], dest, ys0, norm_final, final_norm=False)
    q, kk, vv = _qkv(x, norm_mix[1], w_qkv[0], b_qkv[0])
    o = _attention(q, kk, vv, sinks[0], rel_bias, bsz, seq)
    x, routed = _out_proj(x, o, w_o[0], b_o[0], norm_ffn[1], w_router[1], b_router[1])
    ys, dest = _moe_experts(routed, 1, w1, b1, w2, b2, reuse=ys0)
    x = _combine(x, routed[2], dest, ys, norm_final, final_norm=True)
    return x.reshape(bsz, seq, d)
```

```python
import functools
import math

import jax
import jax.numpy as jnp
import numpy as np
from jax import lax
from jax.experimental import pallas as pl
from jax.experimental.pallas import tpu as pltpu

F32 = jnp.float32
BF16 = jnp.bfloat16
I32 = jnp.int32

D_MODEL = 1024
POOL_WINDOWS = (2, 4, 8, 16)
POOL_GROUP_CH = D_MODEL // len(POOL_WINDOWS)
POOL_HALO = 16
HEAD_DIM = 64
N_HEADS = 16
N_KV_HEADS = 2
GQA_GROUP = N_HEADS // N_KV_HEADS
HEADS_PER_SLAB = 128 // HEAD_DIM
SLABS_PER_GROUP = GQA_GROUP // HEADS_PER_SLAB
ATT_BLOCK = 128
ATT_STEP_BLOCKS = 4
WINDOW = 128
N_BUCKETS = 32
MAX_DISTANCE = 128
N_EXPERTS = 32
TOP_K = 4
D_FF = D_MODEL
SWIGLU_ALPHA = 1.702
SWIGLU_LIMIT = 7.0
RMS_EPS = 1e-5
NEG_INF = -1e30

ROW_TILE = 512
MOE_BLOCK = 512
EXPERT_STEP_BLOCKS = 2
TAIL_STEPS = 4
DISPATCH_TILE = 4096
COMBINE_TILE = 256
ISSUE_UNROLL = 4
VMEM_LIMIT_BYTES = 56 * 1024 * 1024


def _rms(x, g):
    return x * lax.rsqrt(jnp.mean(x * x, axis=-1, keepdims=True) + RMS_EPS) * g


ROW_TILE_SUB = D_MODEL // 128


def _load_row_tiles(ref, n, lead=(), start=0):
    return jnp.concatenate(
        [ref[lead + (pl.ds(start * ROW_TILE_SUB + j, n, stride=ROW_TILE_SUB), slice(None))]
         for j in range(ROW_TILE_SUB)], axis=1)


def _store_row_tiles(ref, val, n, start=0):
    for j in range(ROW_TILE_SUB):
        ref[pl.ds(start * ROW_TILE_SUB + j, n, stride=ROW_TILE_SUB), :] = (
            val[:, j * 128:(j + 1) * 128])


POOL_PAD = 8


def _pool_kernel(x_ref, xh_ref, g_ref, w_ref, b_ref, s_ref, gr_ref, wr_ref, br_ref,
                 o_ref, h_ref, ids_ref, gates_ref, rank_ref, sizes_ref,
                 buf_ref, s1_ref, s2_ref, s3_ref, carry_ref, *, ts):
    i = pl.program_id(1)
    c = POOL_GROUP_CH
    h0 = POOL_PAD
    t0 = POOL_PAD + POOL_HALO
    r1 = t0 + ts
    x = x_ref[...]
    g = g_ref[...]
    h = _rms(x, g)
    hh = _rms(xh_ref[...], g)
    hh = jnp.where(i == 0, 0.0, hh)
    buf_ref[0:h0, :] = jnp.zeros((h0, D_MODEL), F32)
    buf_ref[h0:t0, :] = hh
    buf_ref[t0:r1, :] = h
    sum0 = h[:, 0:c] + buf_ref[t0 - 1:r1 - 1, 0:c]
    l1 = buf_ref[h0:r1, c:] + buf_ref[h0 - 1:r1 - 1, c:]
    s1_ref[0:h0, :] = jnp.zeros((h0, 3 * c), F32)
    s1_ref[h0:r1, :] = l1
    sum1 = s1_ref[t0:r1, 0:c] + s1_ref[t0 - 2:r1 - 2, 0:c]
    l2 = s1_ref[h0:r1, c:] + s1_ref[h0 - 2:r1 - 2, c:]
    s2_ref[0:h0, :] = jnp.zeros((h0, 2 * c), F32)
    s2_ref[h0:r1, :] = l2
    sum2 = s2_ref[t0:r1, 0:c] + s2_ref[t0 - 4:r1 - 4, 0:c]
    s3_ref[h0:r1, :] = s2_ref[h0:r1, c:] + s2_ref[h0 - 4:r1 - 4, c:]
    sum3 = s3_ref[t0:r1, :] + s3_ref[t0 - 8:r1 - 8, :]
    t = i * ts + lax.broadcasted_iota(I32, (ts, 1), 0)
    outs = []
    for gi, (win, acc) in enumerate(zip(POOL_WINDOWS, (sum0, sum1, sum2, sum3))):
        cnt = jnp.minimum(t + 1, win).astype(F32)
        dlt = acc / cnt - h[:, gi * c:(gi + 1) * c]
        outs.append(jnp.dot(dlt.astype(BF16), w_ref[gi], preferred_element_type=F32))
    y = jnp.concatenate(outs, axis=1)
    x_new = x + (y + b_ref[...]) * s_ref[...]
    o_ref[...] = x_new
    first = (pl.program_id(0) == 0) & (i == 0)
    _route_tile(x_new, first, gr_ref, wr_ref, br_ref,
                h_ref, ids_ref, gates_ref, rank_ref, sizes_ref, carry_ref, ts)


def _pool_layer(x, g, w, b, s, g_ffn, w_router, b_router):
    bsz, seq, d = x.shape
    ts = min(ROW_TILE, seq)
    tiles = seq // ts
    kern = functools.partial(_pool_kernel, ts=ts)
    rows = POOL_PAD + POOL_HALO + ts
    vec = lambda: pl.BlockSpec((1, d), lambda bi, i: (0, 0))
    r_in, r_shape, r_out = _route_specs(bsz * seq, ts, d, lambda bi, i: bi * tiles + i)
    outs = pl.pallas_call(
        kern,
        out_shape=(jax.ShapeDtypeStruct(x.shape, F32),) + r_shape,
        grid=(bsz, tiles),
        in_specs=[
            pl.BlockSpec((None, ts, d), lambda bi, i: (bi, i, 0)),
            pl.BlockSpec((None, POOL_HALO, d),
                         lambda bi, i: (bi, jnp.maximum(i * (ts // POOL_HALO) - 1, 0), 0)),
            vec(),
            pl.BlockSpec(w.shape, lambda bi, i: (0, 0, 0)),
            vec(),
            vec(),
        ] + r_in,
        out_specs=(pl.BlockSpec((None, ts, d), lambda bi, i: (bi, i, 0)),) + r_out,
        scratch_shapes=[pltpu.VMEM((rows, d), F32),
                        pltpu.VMEM((rows, 3 * POOL_GROUP_CH), F32),
                        pltpu.VMEM((rows, 2 * POOL_GROUP_CH), F32),
                        pltpu.VMEM((rows, POOL_GROUP_CH), F32),
                        pltpu.VMEM((N_EXPERTS, 128), F32)],
        compiler_params=pltpu.CompilerParams(
            dimension_semantics=("arbitrary", "arbitrary"),
            vmem_limit_bytes=VMEM_LIMIT_BYTES),
        name="pool_route",
    )(x, x, g.reshape(1, d), w.astype(BF16), b.reshape(1, d), s.reshape(1, d),
      *_route_operands(g_ffn, w_router, b_router))
    return outs[0], outs[1:]


def _route_tile(x, first, g_ref, wr_ref, br_ref,
                h_ref, ids_ref, gates_ref, rank_ref, sizes_ref, carry_ref, ts):
    @pl.when(first)
    def _():
        carry_ref[...] = jnp.zeros_like(carry_ref)

    h = _rms(x, g_ref[...])
    _store_row_tiles(h_ref, h, ts)
    h_hi = h.astype(BF16)
    h_lo = (h - h_hi.astype(F32)).astype(BF16)
    w = wr_ref[...]
    w_hi = w.astype(BF16)
    w_lo = (w - w_hi.astype(F32)).astype(BF16)
    dn = (((1,), (1,)), ((), ()))
    lg = (lax.dot_general(w_hi, h_hi, dn, preferred_element_type=F32)
          + lax.dot_general(w_lo, h_hi, dn, preferred_element_type=F32)
          + lax.dot_general(w_hi, h_lo, dn, preferred_element_type=F32))
    lg = lg + br_ref[:, 0:1]

    iota_e = lax.broadcasted_iota(I32, (N_EXPERTS, ts), 0)
    vals, ids = [], []
    for _ in range(TOP_K):
        m = jnp.max(lg, axis=0, keepdims=True)
        idx = jnp.min(jnp.where(lg == m, iota_e, N_EXPERTS), axis=0, keepdims=True)
        vals.append(m)
        ids.append(idx)
        lg = jnp.where(iota_e == idx, -jnp.inf, lg)
    ex = [jnp.exp(v - vals[0]) for v in vals]
    den = ex[0] + ex[1] + ex[2] + ex[3]
    gates_ref[...] = jnp.concatenate([e / den for e in ex], axis=0)
    ids_ref[...] = jnp.concatenate(ids, axis=0)

    hot = [(iota_e == idx) for idx in ids]
    multi = (hot[0] | hot[1] | hot[2] | hot[3])
    multi_f = multi.astype(F32)
    r = lax.broadcasted_iota(I32, (ts, ts), 0)
    c = lax.broadcasted_iota(I32, (ts, ts), 1)
    upper = (r < c).astype(BF16)
    before = jnp.dot(multi_f.astype(BF16), upper, preferred_element_type=F32)
    before = before + carry_ref[:, 0:1]
    ranks = [jnp.sum(jnp.where(hk, before, 0.0), axis=0, keepdims=True) for hk in hot]
    rank_ref[...] = jnp.concatenate(ranks, axis=0).astype(I32)
    carry_ref[...] = carry_ref[...] + jnp.sum(multi_f, axis=1, keepdims=True)
    sizes_ref[...] = carry_ref[...]


def _route_operands(g, w_router, b_router):
    d = g.shape[0]
    return (g.reshape(1, d), w_router.T,
            jnp.broadcast_to(b_router.reshape(N_EXPERTS, 1), (N_EXPERTS, 128)))


def _route_specs(t, ts, d, tile_of):
    const = lambda *idx: (0, 0)
    in_specs = [pl.BlockSpec((1, d), const), pl.BlockSpec((N_EXPERTS, d), const),
                pl.BlockSpec((N_EXPERTS, 128), const)]
    out_shape = (
        jax.ShapeDtypeStruct((t * ROW_TILE_SUB, 128), F32),
        jax.ShapeDtypeStruct((TOP_K, t), I32),
        jax.ShapeDtypeStruct((TOP_K, t), F32),
        jax.ShapeDtypeStruct((TOP_K, t), I32),
        jax.ShapeDtypeStruct((N_EXPERTS, 128), F32),
    )
    tok = lambda: pl.BlockSpec((TOP_K, ts), lambda *idx: (0, tile_of(*idx)))
    out_specs = (
        pl.BlockSpec((ts * ROW_TILE_SUB, 128), lambda *idx: (tile_of(*idx), 0)),
        tok(), tok(), tok(),
        pl.BlockSpec((N_EXPERTS, 128), const),
    )
    return in_specs, out_shape, out_specs


def _dispatch_kernel(zrow_ref, n_used_ref, *refs, td, blk, nblk, zero_fill):
    dest_refs, h_ref, rest = refs[:TOP_K], refs[TOP_K], refs[TOP_K + 1:]
    xs_ref, zbuf, sem_z, sem = rest if zero_fill else rest[1:]
    i = pl.program_id(0)
    sub = ROW_TILE_SUB

    def zero_block(row):
        row = pl.multiple_of(row * sub, blk * sub)
        return pltpu.make_async_copy(zbuf, xs_ref.at[pl.ds(row, blk * sub), :], sem_z)

    def clear_padding():
        zbuf[...] = jnp.zeros_like(zbuf)
        for e in range(N_EXPERTS):
            @pl.when(zrow_ref[e] >= 0)
            def _():
                zero_block(zrow_ref[e]).start()

        def start_tail(b, carry):
            zero_block(b * blk).start()
            return carry

        def wait_tail(b, carry):
            zero_block(b * blk).wait()
            return carry

        lax.fori_loop(n_used_ref[0], nblk, start_tail, 0)
        for e in range(N_EXPERTS):
            @pl.when(zrow_ref[e] >= 0)
            def _():
                zero_block(zrow_ref[e]).wait()
        lax.fori_loop(n_used_ref[0], nblk, wait_tail, 0)

    if zero_fill:
        pl.when(i == 0)(clear_padding)

    def issue(t, carry):
        src = h_ref.at[pl.ds(pl.multiple_of(t * sub, sub), sub), :]
        for k in range(TOP_K):
            d = dest_refs[k][0, 0, t]
            pltpu.make_async_copy(src, xs_ref.at[pl.ds(pl.multiple_of(d * sub, sub), sub), :],
                                  sem).start(priority=k % 2)
        return carry

    lax.fori_loop(0, td, issue, 0, unroll=ISSUE_UNROLL)
    for k in range(TOP_K):
        pltpu.make_async_copy(h_ref, xs_ref.at[pl.ds(0, td * sub), :], sem).wait()


def _dispatch(h_tiles, dest, zrow, n_used, n_rows, reuse=None):
    sub = ROW_TILE_SUB
    t = h_tiles.shape[0] // sub
    td = min(DISPATCH_TILE, t)
    nt = t // td
    dest_k = [dest[k].reshape(nt, 1, td) for k in range(TOP_K)]
    zero_fill = reuse is None
    kern = functools.partial(_dispatch_kernel, td=td, blk=MOE_BLOCK, nblk=n_rows // MOE_BLOCK,
                             zero_fill=zero_fill)
    extra_specs = [] if zero_fill else [pl.BlockSpec(memory_space=pl.ANY)]
    extra_args = [] if zero_fill else [reuse]
    return pl.pallas_call(
        kern,
        out_shape=jax.ShapeDtypeStruct((n_rows * sub, 128), F32),
        grid_spec=pltpu.PrefetchScalarGridSpec(
            num_scalar_prefetch=2,
            grid=(nt,),
            in_specs=[pl.BlockSpec((1, 1, td), lambda i, z, nu: (i, 0, 0),
                                   memory_space=pltpu.SMEM)] * TOP_K + [
                pl.BlockSpec((td * sub, 128), lambda i, z, nu: (i, 0)),
            ] + extra_specs,
            out_specs=pl.BlockSpec(memory_space=pl.ANY),
            scratch_shapes=[
                pltpu.VMEM((MOE_BLOCK * sub, 128), F32),
                pltpu.SemaphoreType.DMA(()),
                pltpu.SemaphoreType.DMA(()),
            ],
        ),
        compiler_params=pltpu.CompilerParams(
            dimension_semantics=("arbitrary",), vmem_limit_bytes=VMEM_LIMIT_BYTES),
        input_output_aliases={} if zero_fill else {3 + TOP_K: 0},
        name="moe_dispatch" if zero_fill else "moe_dispatch_reuse",
    )(zrow, n_used, *dest_k, h_tiles, *extra_args)


def _expert_kernel(blk_e_ref, n_used_ref, nxt_e_ref, n_valid_ref, x_ref, w1_hbm, b1_ref, w2_hbm,
                   b2_ref, o_ref, w1s_ref, w2s_ref, w1b_ref, w2b_ref, wsem, *, layer):
    step = pl.program_id(0)
    sub_rows = MOE_BLOCK * ROW_TILE_SUB

    def weight_copies(expert):
        return (pltpu.make_async_copy(w1_hbm.at[layer, expert], w1s_ref, wsem.at[0]),
                pltpu.make_async_copy(w2_hbm.at[layer, expert], w2s_ref, wsem.at[1]))

    @pl.when(step == 0)
    def _():
        for cp in weight_copies(blk_e_ref[0]):
            cp.start()

    for sub in range(EXPERT_STEP_BLOCKS):
        i = step * EXPERT_STEP_BLOCKS + sub
        e = blk_e_ref[i]
        prev = blk_e_ref[jnp.maximum(i - 1, 0)]
        fresh = (i == 0) | (e != prev)
        live = i < n_used_ref[0]

        @pl.when(live & fresh)
        def _(i=i, e=e):
            for cp in weight_copies(e):
                cp.wait()
            w1b_ref[...] = w1s_ref[...].astype(BF16)
            w2b_ref[...] = w2s_ref[...].astype(BF16)
            nxt = nxt_e_ref[i]

            @pl.when(nxt >= 0)
            def _():
                for cp in weight_copies(nxt):
                    cp.start()

        def mlp(rows, sub=sub, e=e):
            r0 = sub * MOE_BLOCK
            x = _load_row_tiles(x_ref, rows, start=r0).astype(BF16)
            a = jnp.dot(x, w1b_ref[...], preferred_element_type=F32) + b1_ref[e]
            glu = jnp.minimum(a[:, :D_FF], SWIGLU_LIMIT)
            lin = jnp.clip(a[:, D_FF:], -SWIGLU_LIMIT, SWIGLU_LIMIT)
            act = glu * jax.nn.sigmoid(SWIGLU_ALPHA * glu) * (lin + 1.0)
            y = jnp.dot(act.astype(BF16), w2b_ref[...], preferred_element_type=F32) + b2_ref[e]
            _store_row_tiles(o_ref, y, rows, start=r0)
            if rows < MOE_BLOCK:
                o_ref[(r0 + rows) * ROW_TILE_SUB:(r0 + MOE_BLOCK) * ROW_TILE_SUB, :] = jnp.zeros(
                    ((MOE_BLOCK - rows) * ROW_TILE_SUB, 128), F32)

        quarter = MOE_BLOCK // TAIL_STEPS
        quarters = (n_valid_ref[i] + quarter - 1) // quarter
        for nq in range(1, TAIL_STEPS + 1):
            @pl.when(live & (quarters == nq))
            def _(mlp=mlp, nq=nq):
                mlp(nq * quarter)

        @pl.when(jnp.logical_not(live) & (step * EXPERT_STEP_BLOCKS < n_used_ref[0]))
        def _(sub=sub):
            o_ref[sub * sub_rows:(sub + 1) * sub_rows, :] = jnp.zeros((sub_rows, 128), F32)


def _experts(xs, blk_e, n_used, nxt_e, n_valid, layer, w1, b1, w2, b2):
    sub = ROW_TILE_SUB
    n_rows = xs.shape[0] // sub
    d = w2.shape[-1]
    nblk = n_rows // MOE_BLOCK
    depth = w1.shape[0]
    grp = EXPERT_STEP_BLOCKS
    assert nblk % grp == 0
    row = lambda i, be, nu, nx, nv: (jnp.minimum(i, (nu[0] - 1) // grp), 0)
    out_row = row
    bias = lambda i, be, nu, nx, nv: (layer, 0, 0, 0)
    return pl.pallas_call(
        functools.partial(_expert_kernel, layer=layer),
        out_shape=jax.ShapeDtypeStruct((n_rows * sub, 128), F32),
        grid_spec=pltpu.PrefetchScalarGridSpec(
            num_scalar_prefetch=4,
            grid=(nblk // grp,),
            in_specs=[
                pl.BlockSpec((grp * MOE_BLOCK * sub, 128), row),
                pl.BlockSpec(memory_space=pl.ANY),
                pl.BlockSpec((None, N_EXPERTS, 1, 2 * D_FF), bias),
                pl.BlockSpec(memory_space=pl.ANY),
                pl.BlockSpec((None, N_EXPERTS, 1, d), bias),
            ],
            out_specs=pl.BlockSpec((grp * MOE_BLOCK * sub, 128), out_row),
            scratch_shapes=[
                pltpu.VMEM((d, 2 * D_FF), F32),
                pltpu.VMEM((D_FF, d), F32),
                pltpu.VMEM((d, 2 * D_FF), BF16),
                pltpu.VMEM((D_FF, d), BF16),
                pltpu.SemaphoreType.DMA((2,)),
            ],
        ),
        compiler_params=pltpu.CompilerParams(
            dimension_semantics=("arbitrary",), vmem_limit_bytes=VMEM_LIMIT_BYTES),
        input_output_aliases={4: 0},
        name="moe_experts",
    )(blk_e, n_used, nxt_e, n_valid, xs, w1, b1.reshape(depth, N_EXPERTS, 1, 2 * D_FF), w2,
      b2.reshape(depth, N_EXPERTS, 1, d))


def _combine_kernel(*refs, tc, n_tiles, final_norm):
    dcur_refs, dnext_refs = refs[:TOP_K], refs[TOP_K:2 * TOP_K]
    x_ref, g_ref, ys_ref, gn_ref, o_ref, buf, sem = refs[2 * TOP_K:]
    i = pl.program_id(0)
    sub = ROW_TILE_SUB

    def issue(drefs, slot):
        def body(t, carry):
            for k in range(TOP_K):
                d = drefs[k][0, 0, t]
                pltpu.make_async_copy(
                    ys_ref.at[pl.ds(pl.multiple_of(d * sub, sub), sub), :],
                    buf.at[slot, k, pl.ds(pl.multiple_of(t * sub, sub), sub), :],
                    sem.at[slot]).start(priority=k % 2)
            return carry
        lax.fori_loop(0, tc, body, 0, unroll=ISSUE_UNROLL)

    @pl.when(i == 0)
    def _():
        issue(dcur_refs, 0)

    @pl.when(i + 1 < n_tiles)
    def _():
        issue(dnext_refs, (i + 1) % 2)

    slot = i % 2
    for k in range(TOP_K):
        pltpu.make_async_copy(ys_ref.at[pl.ds(0, tc * sub), :], buf.at[slot, k],
                              sem.at[slot]).wait()
    acc = x_ref[...]
    gates = g_ref[...]
    for k in range(TOP_K):
        acc = acc + gates[:, k:k + 1] * _load_row_tiles(buf, tc, lead=(slot, k))
    if final_norm:
        acc = _rms(acc, gn_ref[...])
    o_ref[...] = acc


def _combine(x2d, gates_t, dest, ys, g_norm, final_norm):
    t, d = x2d.shape
    tc = min(COMBINE_TILE, t)
    nt = t // tc
    dest_k = [dest[k].reshape(nt, 1, tc) for k in range(TOP_K)]
    cur = pl.BlockSpec((1, 1, tc), lambda i: (i, 0, 0), memory_space=pltpu.SMEM)
    nxt = pl.BlockSpec((1, 1, tc), lambda i: (jnp.minimum(i + 1, nt - 1), 0, 0),
                       memory_space=pltpu.SMEM)
    kern = functools.partial(_combine_kernel, tc=tc, n_tiles=nt, final_norm=final_norm)
    return pl.pallas_call(
        kern,
        out_shape=jax.ShapeDtypeStruct((t, d), F32),
        grid=(nt,),
        in_specs=[cur] * TOP_K + [nxt] * TOP_K + [
            pl.BlockSpec((tc, d), lambda i: (i, 0)),
            pl.BlockSpec((tc, TOP_K), lambda i: (i, 0)),
            pl.BlockSpec(memory_space=pl.ANY),
            pl.BlockSpec((1, d), lambda i: (0, 0)),
        ],
        out_specs=pl.BlockSpec((tc, d), lambda i: (i, 0)),
        scratch_shapes=[
            pltpu.VMEM((2, TOP_K, tc * ROW_TILE_SUB, 128), F32),
            pltpu.SemaphoreType.DMA((2,)),
        ],
        compiler_params=pltpu.CompilerParams(
            dimension_semantics=("arbitrary",), vmem_limit_bytes=VMEM_LIMIT_BYTES),
        name="moe_combine_final" if final_norm else "moe_combine",
    )(*dest_k, *dest_k, x2d, gates_t.T, ys, g_norm.reshape(1, d))


def _moe_experts(routed, layer, w1, b1, w2, b2, reuse=None):
    h, ids, gates, rank, sizes = routed
    t = ids.shape[1]
    sizes = sizes[:, 0].astype(I32)
    padded = ((sizes + MOE_BLOCK - 1) // MOE_BLOCK) * MOE_BLOCK
    pend = jnp.cumsum(padded)
    pstart = pend - padded
    onehot = ids[:, :, None] == jnp.arange(N_EXPERTS, dtype=I32)
    dest = rank + jnp.sum(jnp.where(onehot, pstart, 0), axis=-1)
    n_rows = t * TOP_K + N_EXPERTS * MOE_BLOCK
    nblk = n_rows // MOE_BLOCK
    blk_row = jnp.arange(nblk, dtype=I32) * MOE_BLOCK
    blk_e = jnp.minimum(jnp.sum((pend[None, :] <= blk_row[:, None]).astype(I32), axis=1),
                        N_EXPERTS - 1)
    n_used = (pend[-1:] // MOE_BLOCK).astype(I32)
    zrow = jnp.where(padded > 0, pend - MOE_BLOCK, -1).astype(I32)
    xs = _dispatch(h, dest, zrow, n_used, n_rows, reuse)
    blk_id = jnp.arange(nblk, dtype=I32)
    later = ((blk_id[None, :] > blk_id[:, None]) & (blk_e[None, :] != blk_e[:, None])
             & (blk_id[None, :] < n_used[0]))
    nxt_blk = jnp.min(jnp.where(later, blk_id[None, :], nblk), axis=1)
    nxt_e = jnp.sum(jnp.where(blk_id[None, :] == nxt_blk[:, None], blk_e[None, :] + 1, 0),
                    axis=1).astype(I32) - 1
    own = blk_e[:, None] == jnp.arange(N_EXPERTS, dtype=I32)[None, :]
    row_end = jnp.sum(jnp.where(own, (pstart + sizes)[None, :], 0), axis=1)
    n_valid = jnp.clip(row_end - blk_row, 0, MOE_BLOCK).astype(I32)
    ys = _experts(xs, blk_e, n_used, nxt_e, n_valid, layer, w1, b1, w2, b2)
    return ys, dest


def _qkv_kernel(x_ref, g_ref, w_ref, b_ref, q_ref, k_ref, v_ref):
    h = _rms(x_ref[...], g_ref[...])
    qkv = jnp.dot(h.astype(BF16), w_ref[...], preferred_element_type=F32) + b_ref[...]
    nq = N_HEADS * HEAD_DIM
    nk = 2 * N_KV_HEADS * HEAD_DIM
    q_ref[...] = (qkv[:, :nq] * (HEAD_DIM ** -0.5)).astype(BF16)
    k_ref[...] = qkv[:, nq:nq + nk].astype(BF16)
    v_ref[...] = qkv[:, nq + nk:].astype(BF16)


def _qkv(x2d, g, w_qkv, b_qkv):
    t, d = x2d.shape
    ts = min(ROW_TILE, t)
    nq = N_HEADS * HEAD_DIM
    kv = N_KV_HEADS * HEAD_DIM

    def dup_heads(m):
        parts = []
        for hd in range(N_KV_HEADS):
            sl = m[..., hd * HEAD_DIM:(hd + 1) * HEAD_DIM]
            parts += [sl, sl]
        return jnp.concatenate(parts, axis=-1)

    w = jnp.concatenate([w_qkv[:, :nq], dup_heads(w_qkv[:, nq:nq + kv]),
                         dup_heads(w_qkv[:, nq + kv:])], axis=1).astype(BF16)
    b = jnp.concatenate([b_qkv[:nq], dup_heads(b_qkv[nq:nq + kv]),
                         dup_heads(b_qkv[nq + kv:])]).reshape(1, -1)
    n_out = w.shape[1]
    row = lambda width: pl.BlockSpec((ts, width), lambda i: (i, 0))
    return pl.pallas_call(
        _qkv_kernel,
        out_shape=(
            jax.ShapeDtypeStruct((t, nq), BF16),
            jax.ShapeDtypeStruct((t, 2 * kv), BF16),
            jax.ShapeDtypeStruct((t, 2 * kv), BF16),
        ),
        grid=(t // ts,),
        in_specs=[
            row(d),
            pl.BlockSpec((1, d), lambda i: (0, 0)),
            pl.BlockSpec((d, n_out), lambda i: (0, 0)),
            pl.BlockSpec((1, n_out), lambda i: (0, 0)),
        ],
        out_specs=(row(nq), row(2 * kv), row(2 * kv)),
        compiler_params=pltpu.CompilerParams(
            dimension_semantics=("arbitrary",), vmem_limit_bytes=VMEM_LIMIT_BYTES),
        name="attn_qkv",
    )(x2d, g.reshape(1, d), w, b)


def _attn_kernel(sinks_ref, q_ref, kp_ref, kc_ref, vp_ref, vc_ref, bias_ref, o_ref):
    n = pl.program_id(1)
    rows = SLABS_PER_GROUP * ATT_BLOCK
    keys = 2 * ATT_BLOCK
    kall = jnp.concatenate([kp_ref[...], kc_ref[...]], axis=0)
    vall = jnp.concatenate([vp_ref[...], vc_ref[...]], axis=0)
    j = lax.broadcasted_iota(I32, (keys, rows), 0)
    r = lax.broadcasted_iota(I32, (keys, rows), 1) & (ATT_BLOCK - 1)
    in_window = (j > r) & (j <= r + WINDOW)
    lane_k = lax.broadcasted_iota(I32, (keys, 128), 1)
    halves = [lane_k < HEAD_DIM, lane_k >= HEAD_DIM]
    sub_o = lax.broadcasted_iota(I32, (128, rows), 0)
    zero = jnp.zeros((), BF16)
    for blk in range(ATT_STEP_BLOCKS):
        q = q_ref[blk * ATT_BLOCK:(blk + 1) * ATT_BLOCK, :]
        kcat = kall[blk * ATT_BLOCK:blk * ATT_BLOCK + keys]
        vcat = vall[blk * ATT_BLOCK:blk * ATT_BLOCK + keys]
        if blk == 0:
            mask = in_window & ((n > 0) | (j >= ATT_BLOCK))
        else:
            mask = in_window
        for g in range(N_KV_HEADS):
            s0 = g * SLABS_PER_GROUP
            qg = jnp.concatenate([q[:, (s0 + s) * 128:(s0 + s + 1) * 128]
                                  for s in range(SLABS_PER_GROUP)], axis=0)
            kg = kcat[:, g * 128:(g + 1) * 128]
            vg = vcat[:, g * 128:(g + 1) * 128]
            pts, invs = [], []
            for p in range(HEADS_PER_SLAB):
                kp = jnp.where(halves[p], kg, zero)
                sc = lax.dot_general(kp, qg, (((1,), (1,)), ((), ())),
                                     preferred_element_type=F32)
                sc = jnp.where(mask, sc + bias_ref[g, p], NEG_INF)
                sink = jnp.concatenate(
                    [jnp.full((1, ATT_BLOCK),
                              sinks_ref[g * GQA_GROUP + s * HEADS_PER_SLAB + p], F32)
                     for s in range(SLABS_PER_GROUP)], axis=1)
                m = jnp.maximum(jnp.max(sc, axis=0, keepdims=True), sink)
                pe = jnp.exp(sc - m)
                den = jnp.sum(pe, axis=0, keepdims=True) + jnp.exp(sink - m)
                pts.append(pe.astype(BF16))
                invs.append(1.0 / den)
            pcat = jnp.concatenate(pts, axis=0)
            vblk = jnp.concatenate([jnp.where(halves[p], vg, zero)
                                    for p in range(HEADS_PER_SLAB)], axis=0)
            og = lax.dot_general(vblk, pcat, (((0,), (0,)), ((), ())),
                                 preferred_element_type=F32)
            og = og * jnp.where(sub_o < HEAD_DIM, invs[0], invs[1])
            og = og.T
            for s in range(SLABS_PER_GROUP):
                o_ref[blk * ATT_BLOCK:(blk + 1) * ATT_BLOCK, (s0 + s) * 128:(s0 + s + 1) * 128] = (
                    og[s * ATT_BLOCK:(s + 1) * ATT_BLOCK].astype(BF16))


def _t5_bias_table(rel_bias):
    r = np.arange(ATT_BLOCK)[:, None]
    j = np.arange(2 * ATT_BLOCK)[None, :]
    rel = np.clip(ATT_BLOCK + r - j, 0, WINDOW - 1)
    max_exact = N_BUCKETS // 2
    nf = jnp.maximum(rel, max_exact).astype(F32)
    large = max_exact + (jnp.log(nf / max_exact) / math.log(MAX_DISTANCE / max_exact)
                         * (N_BUCKETS - max_exact)).astype(I32)
    large = jnp.minimum(large, N_BUCKETS - 1)
    bucket = jnp.where(rel < max_exact, rel, large)
    pick = (bucket[None] == jnp.arange(N_BUCKETS, dtype=I32)[:, None, None])
    bias = jnp.sum(jnp.where(pick[:, None], rel_bias.astype(F32)[:, :, None, None], 0.0),
                   axis=0)
    bias = bias.reshape(N_KV_HEADS, SLABS_PER_GROUP, HEADS_PER_SLAB, ATT_BLOCK, 2 * ATT_BLOCK)
    bias = jnp.transpose(bias, (0, 2, 4, 1, 3))
    return bias.reshape(N_KV_HEADS, HEADS_PER_SLAB, 2 * ATT_BLOCK, SLABS_PER_GROUP * ATT_BLOCK)


def _attention(q, kk, vv, sinks, rel_bias, bsz, seq):
    nq = N_HEADS * HEAD_DIM
    kvw = kk.shape[-1]
    step = ATT_STEP_BLOCKS * ATT_BLOCK
    q3 = q.reshape(bsz, seq, nq)
    k3 = kk.reshape(bsz, seq, kvw)
    v3 = vv.reshape(bsz, seq, kvw)
    bias = _t5_bias_table(rel_bias)
    prev = lambda bi, n: (bi, jnp.maximum(n * ATT_STEP_BLOCKS - 1, 0), 0)
    cur = lambda bi, n: (bi, n, 0)
    out = pl.pallas_call(
        _attn_kernel,
        out_shape=jax.ShapeDtypeStruct((bsz, seq, nq), BF16),
        grid=(bsz, seq // step),
        in_specs=[
            pl.BlockSpec(memory_space=pltpu.SMEM),
            pl.BlockSpec((None, step, nq), cur),
            pl.BlockSpec((None, ATT_BLOCK, kvw), prev),
            pl.BlockSpec((None, step, kvw), cur),
            pl.BlockSpec((None, ATT_BLOCK, kvw), prev),
            pl.BlockSpec((None, step, kvw), cur),
            pl.BlockSpec(bias.shape, lambda bi, n: (0, 0, 0, 0)),
        ],
        out_specs=pl.BlockSpec((None, step, nq), cur),
        compiler_params=pltpu.CompilerParams(
            dimension_semantics=("arbitrary", "arbitrary"),
            vmem_limit_bytes=VMEM_LIMIT_BYTES),
        name="attn_core",
    )(sinks.astype(F32), q3, k3, k3, v3, v3, bias)
    return out.reshape(bsz * seq, nq)


def _oproj_kernel(x_ref, o_ref, w_ref, b_ref, gr_ref, wr_ref, br_ref,
                  y_ref, h_ref, ids_ref, gates_ref, rank_ref, sizes_ref, carry_ref, *, ts):
    x_new = x_ref[...] + jnp.dot(o_ref[...], w_ref[...],
                                 preferred_element_type=F32) + b_ref[...]
    y_ref[...] = x_new
    _route_tile(x_new, pl.program_id(0) == 0, gr_ref, wr_ref, br_ref,
                h_ref, ids_ref, gates_ref, rank_ref, sizes_ref, carry_ref, ts)


def _out_proj(x2d, o, w_o, b_o, g_ffn, w_router, b_router):
    t, d = x2d.shape
    ts = min(ROW_TILE, t)
    nq = o.shape[1]
    r_in, r_shape, r_out = _route_specs(t, ts, d, lambda i: i)
    outs = pl.pallas_call(
        functools.partial(_oproj_kernel, ts=ts),
        out_shape=(jax.ShapeDtypeStruct((t, d), F32),) + r_shape,
        grid=(t // ts,),
        in_specs=[
            pl.BlockSpec((ts, d), lambda i: (i, 0)),
            pl.BlockSpec((ts, nq), lambda i: (i, 0)),
            pl.BlockSpec((nq, d), lambda i: (0, 0)),
            pl.BlockSpec((1, d), lambda i: (0, 0)),
        ] + r_in,
        out_specs=(pl.BlockSpec((ts, d), lambda i: (i, 0)),) + r_out,
        scratch_shapes=[pltpu.VMEM((N_EXPERTS, 128), F32)],
        compiler_params=pltpu.CompilerParams(
            dimension_semantics=("arbitrary",), vmem_limit_bytes=VMEM_LIMIT_BYTES),
        name="attn_out_proj_route",
    )(x2d, o, w_o.astype(BF16), b_o.reshape(1, d),
      *_route_operands(g_ffn, w_router, b_router))
    return outs[0], outs[1:]


def kernel(x, norm_mix, norm_ffn, norm_final, pool_w, pool_b, pool_scale, w_qkv, b_qkv,
           sinks, w_o, b_o, rel_bias, w_router, b_router, w1, b1, w2, b2):
    bsz, seq, d = x.shape
    t = bsz * seq
    x, routed = _pool_layer(x, norm_mix[0], pool_w[0], pool_b[0], pool_scale[0],
                            norm_ffn[0], w_router[0], b_router[0])
    x = x.reshape(t, d)
    ys0, dest = _moe_experts(routed, 0, w1, b1, w2, b2)
    x = _combine(x, routed[2], dest, ys0, norm_final, final_norm=False)
    q, kk, vv = _qkv(x, norm_mix[1], w_qkv[0], b_qkv[0])
    o = _attention(q, kk, vv, sinks[0], rel_bias, bsz, seq)
    x, routed = _out_proj(x, o, w_o[0], b_o[0], norm_ffn[1], w_router[1], b_router[1])
    ys, dest = _moe_experts(routed, 1, w1, b1, w2, b2, reuse=ys0)
    x = _combine(x, routed[2], dest, ys, norm_final, final_norm=True)
    return x.reshape(bsz, seq, d)
```

```python
import functools
import math

import jax
import jax.numpy as jnp
import numpy as np
from jax import lax
from jax.experimental import pallas as pl
from jax.experimental.pallas import tpu as pltpu

F32 = jnp.float32
BF16 = jnp.bfloat16
I32 = jnp.int32

LANES = 128
D_MODEL = 1024
POOL_WINDOWS = (2, 4, 8, 16)
POOL_GROUP_CH = D_MODEL // len(POOL_WINDOWS)
POOL_HALO = 16
HEAD_DIM = 64
N_HEADS = 16
N_KV_HEADS = 2
GQA_GROUP = N_HEADS // N_KV_HEADS
HEADS_PER_SLAB = LANES // HEAD_DIM
SLABS_PER_GROUP = GQA_GROUP // HEADS_PER_SLAB
ATT_BLOCK = 128
ATT_STEP_BLOCKS = 4
WINDOW = 128
N_BUCKETS = 32
MAX_DISTANCE = 128
N_EXPERTS = 32
TOP_K = 4
D_FF = D_MODEL
SWIGLU_ALPHA = 1.702
SWIGLU_LIMIT = 7.0
RMS_EPS = 1e-5
NEG_INF = -1e30

ROW_TILE = 512
MOE_BLOCK = 512
EXPERT_STEP_BLOCKS = 2
DISPATCH_TILE = 4096
COMBINE_TILE = 256
ISSUE_UNROLL = 4
VMEM_LIMIT_BYTES = 56 * 1024 * 1024


def _rms(x, g):
    return x * lax.rsqrt(jnp.mean(x * x, axis=-1, keepdims=True) + RMS_EPS) * g


ROW_TILE_SUB = D_MODEL // LANES


def _load_row_tiles(ref, n, lead=(), start=0):
    return jnp.concatenate(
        [ref[lead + (pl.ds(start * ROW_TILE_SUB + j, n, stride=ROW_TILE_SUB), slice(None))]
         for j in range(ROW_TILE_SUB)], axis=1)


def _store_row_tiles(ref, val, n, start=0):
    for j in range(ROW_TILE_SUB):
        ref[pl.ds(start * ROW_TILE_SUB + j, n, stride=ROW_TILE_SUB), :] = (
            val[:, j * LANES:(j + 1) * LANES])


POOL_PAD = 8


def _pool_kernel(x_ref, xh_ref, g_ref, w_ref, b_ref, s_ref, gr_ref, wr_ref, br_ref,
                 o_ref, h_ref, ids_ref, gates_ref, rank_ref, sizes_ref,
                 buf_ref, s1_ref, s2_ref, s3_ref, carry_ref, *, ts):
    i = pl.program_id(1)
    c = POOL_GROUP_CH
    h0 = POOL_PAD
    t0 = POOL_PAD + POOL_HALO
    r1 = t0 + ts
    x = x_ref[...]
    g = g_ref[...]
    h = _rms(x, g)
    hh = _rms(xh_ref[...], g)
    hh = jnp.where(i == 0, 0.0, hh)
    buf_ref[0:h0, :] = jnp.zeros((h0, D_MODEL), F32)
    buf_ref[h0:t0, :] = hh
    buf_ref[t0:r1, :] = h
    sum0 = h[:, 0:c] + buf_ref[t0 - 1:r1 - 1, 0:c]
    l1 = buf_ref[h0:r1, c:] + buf_ref[h0 - 1:r1 - 1, c:]
    s1_ref[0:h0, :] = jnp.zeros((h0, 3 * c), F32)
    s1_ref[h0:r1, :] = l1
    sum1 = s1_ref[t0:r1, 0:c] + s1_ref[t0 - 2:r1 - 2, 0:c]
    l2 = s1_ref[h0:r1, c:] + s1_ref[h0 - 2:r1 - 2, c:]
    s2_ref[0:h0, :] = jnp.zeros((h0, 2 * c), F32)
    s2_ref[h0:r1, :] = l2
    sum2 = s2_ref[t0:r1, 0:c] + s2_ref[t0 - 4:r1 - 4, 0:c]
    s3_ref[h0:r1, :] = s2_ref[h0:r1, c:] + s2_ref[h0 - 4:r1 - 4, c:]
    sum3 = s3_ref[t0:r1, :] + s3_ref[t0 - 8:r1 - 8, :]
    t = i * ts + lax.broadcasted_iota(I32, (ts, 1), 0)
    outs = []
    for gi, (win, acc) in enumerate(zip(POOL_WINDOWS, (sum0, sum1, sum2, sum3))):
        cnt = jnp.minimum(t + 1, win).astype(F32)
        dlt = acc / cnt - h[:, gi * c:(gi + 1) * c]
        outs.append(jnp.dot(dlt.astype(BF16), w_ref[gi], preferred_element_type=F32))
    y = jnp.concatenate(outs, axis=1)
    x_new = x + (y + b_ref[...]) * s_ref[...]
    o_ref[...] = x_new
    first = (pl.program_id(0) == 0) & (i == 0)
    _route_tile(x_new, first, gr_ref, wr_ref, br_ref,
                h_ref, ids_ref, gates_ref, rank_ref, sizes_ref, carry_ref, ts)


def _pool_layer(x, g, w, b, s, g_ffn, w_router, b_router):
    bsz, seq, d = x.shape
    ts = min(ROW_TILE, seq)
    tiles = seq // ts
    kern = functools.partial(_pool_kernel, ts=ts)
    rows = POOL_PAD + POOL_HALO + ts
    vec = lambda: pl.BlockSpec((1, d), lambda bi, i: (0, 0))
    r_in, r_shape, r_out = _route_specs(bsz * seq, ts, d, lambda bi, i: bi * tiles + i)
    outs = pl.pallas_call(
        kern,
        out_shape=(jax.ShapeDtypeStruct(x.shape, F32),) + r_shape,
        grid=(bsz, tiles),
        in_specs=[
            pl.BlockSpec((None, ts, d), lambda bi, i: (bi, i, 0)),
            pl.BlockSpec((None, POOL_HALO, d),
                         lambda bi, i: (bi, jnp.maximum(i * (ts // POOL_HALO) - 1, 0), 0)),
            vec(),
            pl.BlockSpec(w.shape, lambda bi, i: (0, 0, 0)),
            vec(),
            vec(),
        ] + r_in,
        out_specs=(pl.BlockSpec((None, ts, d), lambda bi, i: (bi, i, 0)),) + r_out,
        scratch_shapes=[pltpu.VMEM((rows, d), F32),
                        pltpu.VMEM((rows, 3 * POOL_GROUP_CH), F32),
                        pltpu.VMEM((rows, 2 * POOL_GROUP_CH), F32),
                        pltpu.VMEM((rows, POOL_GROUP_CH), F32),
                        pltpu.VMEM((N_EXPERTS, LANES), F32)],
        compiler_params=pltpu.CompilerParams(
            dimension_semantics=("arbitrary", "arbitrary"),
            vmem_limit_bytes=VMEM_LIMIT_BYTES),
        name="pool_route",
    )(x, x, g.reshape(1, d), w.astype(BF16), b.reshape(1, d), s.reshape(1, d),
      *_route_operands(g_ffn, w_router, b_router))
    return outs[0], outs[1:]


def _route_tile(x, first, g_ref, wr_ref, br_ref,
                h_ref, ids_ref, gates_ref, rank_ref, sizes_ref, carry_ref, ts):
    @pl.when(first)
    def _():
        carry_ref[...] = jnp.zeros_like(carry_ref)

    h = _rms(x, g_ref[...])
    _store_row_tiles(h_ref, h, ts)
    h_hi = h.astype(BF16)
    h_lo = (h - h_hi.astype(F32)).astype(BF16)
    w = wr_ref[...]
    w_hi = w.astype(BF16)
    w_lo = (w - w_hi.astype(F32)).astype(BF16)
    dn = (((1,), (1,)), ((), ()))
    lg = (lax.dot_general(w_hi, h_hi, dn, preferred_element_type=F32)
          + lax.dot_general(w_lo, h_hi, dn, preferred_element_type=F32)
          + lax.dot_general(w_hi, h_lo, dn, preferred_element_type=F32))
    lg = lg + br_ref[:, 0:1]

    iota_e = lax.broadcasted_iota(I32, (N_EXPERTS, ts), 0)
    vals, ids = [], []
    for _ in range(TOP_K):
        m = jnp.max(lg, axis=0, keepdims=True)
        idx = jnp.min(jnp.where(lg == m, iota_e, N_EXPERTS), axis=0, keepdims=True)
        vals.append(m)
        ids.append(idx)
        lg = jnp.where(iota_e == idx, -jnp.inf, lg)
    ex = [jnp.exp(v - vals[0]) for v in vals]
    den = ex[0] + ex[1] + ex[2] + ex[3]
    gates_ref[...] = jnp.concatenate([e / den for e in ex], axis=0)
    ids_ref[...] = jnp.concatenate(ids, axis=0)

    hot = [(iota_e == idx) for idx in ids]
    multi = (hot[0] | hot[1] | hot[2] | hot[3])
    multi_f = multi.astype(F32)
    r = lax.broadcasted_iota(I32, (ts, ts), 0)
    c = lax.broadcasted_iota(I32, (ts, ts), 1)
    upper = (r < c).astype(BF16)
    before = jnp.dot(multi_f.astype(BF16), upper, preferred_element_type=F32)
    before = before + carry_ref[:, 0:1]
    ranks = [jnp.sum(jnp.where(hk, before, 0.0), axis=0, keepdims=True) for hk in hot]
    rank_ref[...] = jnp.concatenate(ranks, axis=0).astype(I32)
    carry_ref[...] = carry_ref[...] + jnp.sum(multi_f, axis=1, keepdims=True)
    sizes_ref[...] = carry_ref[...]


def _route_operands(g, w_router, b_router):
    d = g.shape[0]
    return (g.reshape(1, d), w_router.T,
            jnp.broadcast_to(b_router.reshape(N_EXPERTS, 1), (N_EXPERTS, LANES)))


def _route_specs(t, ts, d, tile_of):
    const = lambda *idx: (0, 0)
    in_specs = [pl.BlockSpec((1, d), const), pl.BlockSpec((N_EXPERTS, d), const),
                pl.BlockSpec((N_EXPERTS, LANES), const)]
    out_shape = (
        jax.ShapeDtypeStruct((t * ROW_TILE_SUB, LANES), F32),
        jax.ShapeDtypeStruct((TOP_K, t), I32),
        jax.ShapeDtypeStruct((TOP_K, t), F32),
        jax.ShapeDtypeStruct((TOP_K, t), I32),
        jax.ShapeDtypeStruct((N_EXPERTS, LANES), F32),
    )
    tok = lambda: pl.BlockSpec((TOP_K, ts), lambda *idx: (0, tile_of(*idx)))
    out_specs = (
        pl.BlockSpec((ts * ROW_TILE_SUB, LANES), lambda *idx: (tile_of(*idx), 0)),
        tok(), tok(), tok(),
        pl.BlockSpec((N_EXPERTS, LANES), const),
    )
    return in_specs, out_shape, out_specs


def _dispatch_kernel(zrow_ref, n_used_ref, *refs, td, blk, nblk, zero_fill):
    dest_refs, h_ref, rest = refs[:TOP_K], refs[TOP_K], refs[TOP_K + 1:]
    xs_ref, zbuf, sem_z, sem = rest if zero_fill else rest[1:]
    i = pl.program_id(0)
    sub = ROW_TILE_SUB

    def zero_block(row):
        row = pl.multiple_of(row * sub, blk * sub)
        return pltpu.make_async_copy(zbuf, xs_ref.at[pl.ds(row, blk * sub), :], sem_z)

    def clear_padding():
        zbuf[...] = jnp.zeros_like(zbuf)
        for e in range(N_EXPERTS):
            @pl.when(zrow_ref[e] >= 0)
            def _():
                zero_block(zrow_ref[e]).start()

        def start_tail(b, carry):
            zero_block(b * blk).start()
            return carry

        def wait_tail(b, carry):
            zero_block(b * blk).wait()
            return carry

        lax.fori_loop(n_used_ref[0], nblk, start_tail, 0)
        for e in range(N_EXPERTS):
            @pl.when(zrow_ref[e] >= 0)
            def _():
                zero_block(zrow_ref[e]).wait()
        lax.fori_loop(n_used_ref[0], nblk, wait_tail, 0)

    if zero_fill:
        pl.when(i == 0)(clear_padding)

    def issue(t, carry):
        src = h_ref.at[pl.ds(pl.multiple_of(t * sub, sub), sub), :]
        for k in range(TOP_K):
            d = dest_refs[k][0, 0, t]
            pltpu.make_async_copy(src, xs_ref.at[pl.ds(pl.multiple_of(d * sub, sub), sub), :],
                                  sem).start(priority=k % 2)
        return carry

    lax.fori_loop(0, td, issue, 0, unroll=ISSUE_UNROLL)
    for k in range(TOP_K):
        pltpu.make_async_copy(h_ref, xs_ref.at[pl.ds(0, td * sub), :], sem).wait()


def _dispatch(h_tiles, dest, zrow, n_used, n_rows, reuse=None):
    sub = ROW_TILE_SUB
    t = h_tiles.shape[0] // sub
    td = min(DISPATCH_TILE, t)
    nt = t // td
    dest_k = [dest[k].reshape(nt, 1, td) for k in range(TOP_K)]
    zero_fill = reuse is None
    kern = functools.partial(_dispatch_kernel, td=td, blk=MOE_BLOCK, nblk=n_rows // MOE_BLOCK,
                             zero_fill=zero_fill)
    extra_specs = [] if zero_fill else [pl.BlockSpec(memory_space=pl.ANY)]
    extra_args = [] if zero_fill else [reuse]
    return pl.pallas_call(
        kern,
        out_shape=jax.ShapeDtypeStruct((n_rows * sub, LANES), F32),
        grid_spec=pltpu.PrefetchScalarGridSpec(
            num_scalar_prefetch=2,
            grid=(nt,),
            in_specs=[pl.BlockSpec((1, 1, td), lambda i, z, nu: (i, 0, 0),
                                   memory_space=pltpu.SMEM)] * TOP_K + [
                pl.BlockSpec((td * sub, LANES), lambda i, z, nu: (i, 0)),
            ] + extra_specs,
            out_specs=pl.BlockSpec(memory_space=pl.ANY),
            scratch_shapes=[
                pltpu.VMEM((MOE_BLOCK * sub, LANES), F32),
                pltpu.SemaphoreType.DMA(()),
                pltpu.SemaphoreType.DMA(()),
            ],
        ),
        compiler_params=pltpu.CompilerParams(
            dimension_semantics=("arbitrary",), vmem_limit_bytes=VMEM_LIMIT_BYTES),
        input_output_aliases={} if zero_fill else {3 + TOP_K: 0},
        name="moe_dispatch" if zero_fill else "moe_dispatch_reuse",
    )(zrow, n_used, *dest_k, h_tiles, *extra_args)


def _expert_kernel(blk_e_ref, n_used_ref, nxt_e_ref, n_valid_ref, x_ref, w1_hbm, b1_ref, w2_hbm,
                   b2_ref, o_ref, w1s_ref, w2s_ref, w1b_ref, w2b_ref, wsem, *, layer):
    step = pl.program_id(0)
    sub_rows = MOE_BLOCK * ROW_TILE_SUB

    def weight_copies(expert):
        return (pltpu.make_async_copy(w1_hbm.at[layer, expert], w1s_ref, wsem.at[0]),
                pltpu.make_async_copy(w2_hbm.at[layer, expert], w2s_ref, wsem.at[1]))

    @pl.when(step == 0)
    def _():
        for cp in weight_copies(blk_e_ref[0]):
            cp.start()

    for sub in range(EXPERT_STEP_BLOCKS):
        i = step * EXPERT_STEP_BLOCKS + sub
        e = blk_e_ref[i]
        prev = blk_e_ref[jnp.maximum(i - 1, 0)]
        fresh = (i == 0) | (e != prev)
        live = i < n_used_ref[0]

        @pl.when(live & fresh)
        def _(i=i, e=e):
            for cp in weight_copies(e):
                cp.wait()
            w1b_ref[...] = w1s_ref[...].astype(BF16)
            w2b_ref[...] = w2s_ref[...].astype(BF16)
            nxt = nxt_e_ref[i]

            @pl.when(nxt >= 0)
            def _():
                for cp in weight_copies(nxt):
                    cp.start()

        def mlp(rows, sub=sub, e=e):
            r0 = sub * MOE_BLOCK
            x = _load_row_tiles(x_ref, rows, start=r0).astype(BF16)
            a = jnp.dot(x, w1b_ref[...], preferred_element_type=F32) + b1_ref[e]
            glu = jnp.minimum(a[:, :D_FF], SWIGLU_LIMIT)
            lin = jnp.clip(a[:, D_FF:], -SWIGLU_LIMIT, SWIGLU_LIMIT)
            act = glu * jax.nn.sigmoid(SWIGLU_ALPHA * glu) * (lin + 1.0)
            y = jnp.dot(act.astype(BF16), w2b_ref[...], preferred_element_type=F32) + b2_ref[e]
            _store_row_tiles(o_ref, y, rows, start=r0)
            if rows < MOE_BLOCK:
                o_ref[(r0 + rows) * ROW_TILE_SUB:(r0 + MOE_BLOCK) * ROW_TILE_SUB, :] = jnp.zeros(
                    ((MOE_BLOCK - rows) * ROW_TILE_SUB, LANES), F32)

        half = MOE_BLOCK // 2
        short = n_valid_ref[i] <= half

        @pl.when(live & jnp.logical_not(short))
        def _(mlp=mlp):
            mlp(MOE_BLOCK)

        @pl.when(live & short)
        def _(mlp=mlp):
            mlp(half)

        @pl.when(jnp.logical_not(live) & (step * EXPERT_STEP_BLOCKS < n_used_ref[0]))
        def _(sub=sub):
            o_ref[sub * sub_rows:(sub + 1) * sub_rows, :] = jnp.zeros((sub_rows, LANES), F32)


def _experts(xs, blk_e, n_used, nxt_e, n_valid, layer, w1, b1, w2, b2):
    sub = ROW_TILE_SUB
    n_rows = xs.shape[0] // sub
    d = w2.shape[-1]
    nblk = n_rows // MOE_BLOCK
    depth = w1.shape[0]
    grp = EXPERT_STEP_BLOCKS
    assert nblk % grp == 0
    row = lambda i, be, nu, nx, nv: (jnp.minimum(i, (nu[0] - 1) // grp), 0)
    out_row = row
    bias = lambda i, be, nu, nx, nv: (layer, 0, 0, 0)
    return pl.pallas_call(
        functools.partial(_expert_kernel, layer=layer),
        out_shape=jax.ShapeDtypeStruct((n_rows * sub, LANES), F32),
        grid_spec=pltpu.PrefetchScalarGridSpec(
            num_scalar_prefetch=4,
            grid=(nblk // grp,),
            in_specs=[
                pl.BlockSpec((grp * MOE_BLOCK * sub, LANES), row),
                pl.BlockSpec(memory_space=pl.ANY),
                pl.BlockSpec((None, N_EXPERTS, 1, 2 * D_FF), bias),
                pl.BlockSpec(memory_space=pl.ANY),
                pl.BlockSpec((None, N_EXPERTS, 1, d), bias),
            ],
            out_specs=pl.BlockSpec((grp * MOE_BLOCK * sub, LANES), out_row),
            scratch_shapes=[
                pltpu.VMEM((d, 2 * D_FF), F32),
                pltpu.VMEM((D_FF, d), F32),
                pltpu.VMEM((d, 2 * D_FF), BF16),
                pltpu.VMEM((D_FF, d), BF16),
                pltpu.SemaphoreType.DMA((2,)),
            ],
        ),
        compiler_params=pltpu.CompilerParams(
            dimension_semantics=("arbitrary",), vmem_limit_bytes=VMEM_LIMIT_BYTES),
        input_output_aliases={4: 0},
        name="moe_experts",
    )(blk_e, n_used, nxt_e, n_valid, xs, w1, b1.reshape(depth, N_EXPERTS, 1, 2 * D_FF), w2,
      b2.reshape(depth, N_EXPERTS, 1, d))


def _combine_kernel(*refs, tc, n_tiles, final_norm):
    dcur_refs, dnext_refs = refs[:TOP_K], refs[TOP_K:2 * TOP_K]
    x_ref, g_ref, ys_ref, gn_ref, o_ref, buf, sem = refs[2 * TOP_K:]
    i = pl.program_id(0)
    sub = ROW_TILE_SUB

    def issue(drefs, slot):
        def body(t, carry):
            for k in range(TOP_K):
                d = drefs[k][0, 0, t]
                pltpu.make_async_copy(
                    ys_ref.at[pl.ds(pl.multiple_of(d * sub, sub), sub), :],
                    buf.at[slot, k, pl.ds(pl.multiple_of(t * sub, sub), sub), :],
                    sem.at[slot]).start(priority=k % 2)
            return carry
        lax.fori_loop(0, tc, body, 0, unroll=ISSUE_UNROLL)

    @pl.when(i == 0)
    def _():
        issue(dcur_refs, 0)

    @pl.when(i + 1 < n_tiles)
    def _():
        issue(dnext_refs, (i + 1) % 2)

    slot = i % 2
    for k in range(TOP_K):
        pltpu.make_async_copy(ys_ref.at[pl.ds(0, tc * sub), :], buf.at[slot, k],
                              sem.at[slot]).wait()
    acc = x_ref[...]
    gates = g_ref[...]
    for k in range(TOP_K):
        acc = acc + gates[:, k:k + 1] * _load_row_tiles(buf, tc, lead=(slot, k))
    if final_norm:
        acc = _rms(acc, gn_ref[...])
    o_ref[...] = acc


def _combine(x2d, gates_t, dest, ys, g_norm, final_norm):
    t, d = x2d.shape
    tc = min(COMBINE_TILE, t)
    nt = t // tc
    dest_k = [dest[k].reshape(nt, 1, tc) for k in range(TOP_K)]
    cur = pl.BlockSpec((1, 1, tc), lambda i: (i, 0, 0), memory_space=pltpu.SMEM)
    nxt = pl.BlockSpec((1, 1, tc), lambda i: (jnp.minimum(i + 1, nt - 1), 0, 0),
                       memory_space=pltpu.SMEM)
    kern = functools.partial(_combine_kernel, tc=tc, n_tiles=nt, final_norm=final_norm)
    return pl.pallas_call(
        kern,
        out_shape=jax.ShapeDtypeStruct((t, d), F32),
        grid=(nt,),
        in_specs=[cur] * TOP_K + [nxt] * TOP_K + [
            pl.BlockSpec((tc, d), lambda i: (i, 0)),
            pl.BlockSpec((tc, TOP_K), lambda i: (i, 0)),
            pl.BlockSpec(memory_space=pl.ANY),
            pl.BlockSpec((1, d), lambda i: (0, 0)),
        ],
        out_specs=pl.BlockSpec((tc, d), lambda i: (i, 0)),
        scratch_shapes=[
            pltpu.VMEM((2, TOP_K, tc * ROW_TILE_SUB, LANES), F32),
            pltpu.SemaphoreType.DMA((2,)),
        ],
        compiler_params=pltpu.CompilerParams(
            dimension_semantics=("arbitrary",), vmem_limit_bytes=VMEM_LIMIT_BYTES),
        name="moe_combine_final" if final_norm else "moe_combine",
    )(*dest_k, *dest_k, x2d, gates_t.T, ys, g_norm.reshape(1, d))


def _moe_experts(routed, layer, w1, b1, w2, b2, reuse=None):
    h, ids, gates, rank, sizes = routed
    t = ids.shape[1]
    sizes = sizes[:, 0].astype(I32)
    padded = ((sizes + MOE_BLOCK - 1) // MOE_BLOCK) * MOE_BLOCK
    pend = jnp.cumsum(padded)
    pstart = pend - padded
    onehot = ids[:, :, None] == jnp.arange(N_EXPERTS, dtype=I32)
    dest = rank + jnp.sum(jnp.where(onehot, pstart, 0), axis=-1)
    n_rows = t * TOP_K + N_EXPERTS * MOE_BLOCK
    nblk = n_rows // MOE_BLOCK
    blk_row = jnp.arange(nblk, dtype=I32) * MOE_BLOCK
    blk_e = jnp.minimum(jnp.sum((pend[None, :] <= blk_row[:, None]).astype(I32), axis=1),
                        N_EXPERTS - 1)
    n_used = (pend[-1:] // MOE_BLOCK).astype(I32)
    zrow = jnp.where(padded > 0, pend - MOE_BLOCK, -1).astype(I32)
    xs = _dispatch(h, dest, zrow, n_used, n_rows, reuse)
    blk_id = jnp.arange(nblk, dtype=I32)
    later = ((blk_id[None, :] > blk_id[:, None]) & (blk_e[None, :] != blk_e[:, None])
             & (blk_id[None, :] < n_used[0]))
    nxt_blk = jnp.min(jnp.where(later, blk_id[None, :], nblk), axis=1)
    nxt_e = jnp.sum(jnp.where(blk_id[None, :] == nxt_blk[:, None], blk_e[None, :] + 1, 0),
                    axis=1).astype(I32) - 1
    own = blk_e[:, None] == jnp.arange(N_EXPERTS, dtype=I32)[None, :]
    row_end = jnp.sum(jnp.where(own, (pstart + sizes)[None, :], 0), axis=1)
    n_valid = jnp.clip(row_end - blk_row, 0, MOE_BLOCK).astype(I32)
    ys = _experts(xs, blk_e, n_used, nxt_e, n_valid, layer, w1, b1, w2, b2)
    return ys, dest


def _qkv_kernel(x_ref, g_ref, w_ref, b_ref, q_ref, k_ref, v_ref):
    h = _rms(x_ref[...], g_ref[...])
    qkv = jnp.dot(h.astype(BF16), w_ref[...], preferred_element_type=F32) + b_ref[...]
    nq = N_HEADS * HEAD_DIM
    nk = 2 * N_KV_HEADS * HEAD_DIM
    q_ref[...] = (qkv[:, :nq] * (HEAD_DIM ** -0.5)).astype(BF16)
    k_ref[...] = qkv[:, nq:nq + nk].astype(BF16)
    v_ref[...] = qkv[:, nq + nk:].astype(BF16)


def _qkv(x2d, g, w_qkv, b_qkv):
    t, d = x2d.shape
    ts = min(ROW_TILE, t)
    nq = N_HEADS * HEAD_DIM
    kv = N_KV_HEADS * HEAD_DIM

    def dup_heads(m):
        parts = []
        for hd in range(N_KV_HEADS):
            sl = m[..., hd * HEAD_DIM:(hd + 1) * HEAD_DIM]
            parts += [sl, sl]
        return jnp.concatenate(parts, axis=-1)

    w = jnp.concatenate([w_qkv[:, :nq], dup_heads(w_qkv[:, nq:nq + kv]),
                         dup_heads(w_qkv[:, nq + kv:])], axis=1).astype(BF16)
    b = jnp.concatenate([b_qkv[:nq], dup_heads(b_qkv[nq:nq + kv]),
                         dup_heads(b_qkv[nq + kv:])]).reshape(1, -1)
    n_out = w.shape[1]
    row = lambda width: pl.BlockSpec((ts, width), lambda i: (i, 0))
    return pl.pallas_call(
        _qkv_kernel,
        out_shape=(
            jax.ShapeDtypeStruct((t, nq), BF16),
            jax.ShapeDtypeStruct((t, 2 * kv), BF16),
            jax.ShapeDtypeStruct((t, 2 * kv), BF16),
        ),
        grid=(t // ts,),
        in_specs=[
            row(d),
            pl.BlockSpec((1, d), lambda i: (0, 0)),
            pl.BlockSpec((d, n_out), lambda i: (0, 0)),
            pl.BlockSpec((1, n_out), lambda i: (0, 0)),
        ],
        out_specs=(row(nq), row(2 * kv), row(2 * kv)),
        compiler_params=pltpu.CompilerParams(
            dimension_semantics=("arbitrary",), vmem_limit_bytes=VMEM_LIMIT_BYTES),
        name="attn_qkv",
    )(x2d, g.reshape(1, d), w, b)


def _attn_kernel(sinks_ref, q_ref, kp_ref, kc_ref, vp_ref, vc_ref, bias_ref, o_ref):
    n = pl.program_id(1)
    rows = SLABS_PER_GROUP * ATT_BLOCK
    keys = 2 * ATT_BLOCK
    kall = jnp.concatenate([kp_ref[...], kc_ref[...]], axis=0)
    vall = jnp.concatenate([vp_ref[...], vc_ref[...]], axis=0)
    j = lax.broadcasted_iota(I32, (keys, rows), 0)
    r = lax.broadcasted_iota(I32, (keys, rows), 1) & (ATT_BLOCK - 1)
    in_window = (j > r) & (j <= r + WINDOW)
    lane_k = lax.broadcasted_iota(I32, (keys, LANES), 1)
    halves = [lane_k < HEAD_DIM, lane_k >= HEAD_DIM]
    sub_o = lax.broadcasted_iota(I32, (LANES, rows), 0)
    zero = jnp.zeros((), BF16)
    for blk in range(ATT_STEP_BLOCKS):
        q = q_ref[blk * ATT_BLOCK:(blk + 1) * ATT_BLOCK, :]
        kcat = kall[blk * ATT_BLOCK:blk * ATT_BLOCK + keys]
        vcat = vall[blk * ATT_BLOCK:blk * ATT_BLOCK + keys]
        if blk == 0:
            mask = in_window & ((n > 0) | (j >= ATT_BLOCK))
        else:
            mask = in_window
        for g in range(N_KV_HEADS):
            s0 = g * SLABS_PER_GROUP
            qg = jnp.concatenate([q[:, (s0 + s) * LANES:(s0 + s + 1) * LANES]
                                  for s in range(SLABS_PER_GROUP)], axis=0)
            kg = kcat[:, g * LANES:(g + 1) * LANES]
            vg = vcat[:, g * LANES:(g + 1) * LANES]
            pts, invs = [], []
            for p in range(HEADS_PER_SLAB):
                kp = jnp.where(halves[p], kg, zero)
                sc = lax.dot_general(kp, qg, (((1,), (1,)), ((), ())),
                                     preferred_element_type=F32)
                sc = jnp.where(mask, sc + bias_ref[g, p], NEG_INF)
                sink = jnp.concatenate(
                    [jnp.full((1, ATT_BLOCK),
                              sinks_ref[g * GQA_GROUP + s * HEADS_PER_SLAB + p], F32)
                     for s in range(SLABS_PER_GROUP)], axis=1)
                m = jnp.maximum(jnp.max(sc, axis=0, keepdims=True), sink)
                pe = jnp.exp(sc - m)
                den = jnp.sum(pe, axis=0, keepdims=True) + jnp.exp(sink - m)
                pts.append(pe.astype(BF16))
                invs.append(1.0 / den)
            pcat = jnp.concatenate(pts, axis=0)
            vblk = jnp.concatenate([jnp.where(halves[p], vg, zero)
                                    for p in range(HEADS_PER_SLAB)], axis=0)
            og = lax.dot_general(vblk, pcat, (((0,), (0,)), ((), ())),
                                 preferred_element_type=F32)
            og = og * jnp.where(sub_o < HEAD_DIM, invs[0], invs[1])
            og = og.T
            for s in range(SLABS_PER_GROUP):
                o_ref[blk * ATT_BLOCK:(blk + 1) * ATT_BLOCK,
                      (s0 + s) * LANES:(s0 + s + 1) * LANES] = (
                    og[s * ATT_BLOCK:(s + 1) * ATT_BLOCK].astype(BF16))


def _t5_bias_table(rel_bias):
    r = np.arange(ATT_BLOCK)[:, None]
    j = np.arange(2 * ATT_BLOCK)[None, :]
    rel = np.clip(ATT_BLOCK + r - j, 0, WINDOW - 1)
    max_exact = N_BUCKETS // 2
    nf = jnp.maximum(rel, max_exact).astype(F32)
    large = max_exact + (jnp.log(nf / max_exact) / math.log(MAX_DISTANCE / max_exact)
                         * (N_BUCKETS - max_exact)).astype(I32)
    large = jnp.minimum(large, N_BUCKETS - 1)
    bucket = jnp.where(rel < max_exact, rel, large)
    pick = (bucket[None] == jnp.arange(N_BUCKETS, dtype=I32)[:, None, None])
    bias = jnp.sum(jnp.where(pick[:, None], rel_bias.astype(F32)[:, :, None, None], 0.0),
                   axis=0)
    bias = bias.reshape(N_KV_HEADS, SLABS_PER_GROUP, HEADS_PER_SLAB, ATT_BLOCK, 2 * ATT_BLOCK)
    bias = jnp.transpose(bias, (0, 2, 4, 1, 3))
    return bias.reshape(N_KV_HEADS, HEADS_PER_SLAB, 2 * ATT_BLOCK, SLABS_PER_GROUP * ATT_BLOCK)


def _attention(q, kk, vv, sinks, rel_bias, bsz, seq):
    nq = N_HEADS * HEAD_DIM
    kvw = kk.shape[-1]
    step = ATT_STEP_BLOCKS * ATT_BLOCK
    q3 = q.reshape(bsz, seq, nq)
    k3 = kk.reshape(bsz, seq, kvw)
    v3 = vv.reshape(bsz, seq, kvw)
    bias = _t5_bias_table(rel_bias)
    prev = lambda bi, n: (bi, jnp.maximum(n * ATT_STEP_BLOCKS - 1, 0), 0)
    cur = lambda bi, n: (bi, n, 0)
    out = pl.pallas_call(
        _attn_kernel,
        out_shape=jax.ShapeDtypeStruct((bsz, seq, nq), BF16),
        grid=(bsz, seq // step),
        in_specs=[
            pl.BlockSpec(memory_space=pltpu.SMEM),
            pl.BlockSpec((None, step, nq), cur),
            pl.BlockSpec((None, ATT_BLOCK, kvw), prev),
            pl.BlockSpec((None, step, kvw), cur),
            pl.BlockSpec((None, ATT_BLOCK, kvw), prev),
            pl.BlockSpec((None, step, kvw), cur),
            pl.BlockSpec(bias.shape, lambda bi, n: (0, 0, 0, 0)),
        ],
        out_specs=pl.BlockSpec((None, step, nq), cur),
        compiler_params=pltpu.CompilerParams(
            dimension_semantics=("arbitrary", "arbitrary"),
            vmem_limit_bytes=VMEM_LIMIT_BYTES),
        name="attn_core",
    )(sinks.astype(F32), q3, k3, k3, v3, v3, bias)
    return out.reshape(bsz * seq, nq)


def _oproj_kernel(x_ref, o_ref, w_ref, b_ref, gr_ref, wr_ref, br_ref,
                  y_ref, h_ref, ids_ref, gates_ref, rank_ref, sizes_ref, carry_ref, *, ts):
    x_new = x_ref[...] + jnp.dot(o_ref[...], w_ref[...],
                                 preferred_element_type=F32) + b_ref[...]
    y_ref[...] = x_new
    _route_tile(x_new, pl.program_id(0) == 0, gr_ref, wr_ref, br_ref,
                h_ref, ids_ref, gates_ref, rank_ref, sizes_ref, carry_ref, ts)


def _out_proj(x2d, o, w_o, b_o, g_ffn, w_router, b_router):
    t, d = x2d.shape
    ts = min(ROW_TILE, t)
    nq = o.shape[1]
    r_in, r_shape, r_out = _route_specs(t, ts, d, lambda i: i)
    outs = pl.pallas_call(
        functools.partial(_oproj_kernel, ts=ts),
        out_shape=(jax.ShapeDtypeStruct((t, d), F32),) + r_shape,
        grid=(t // ts,),
        in_specs=[
            pl.BlockSpec((ts, d), lambda i: (i, 0)),
            pl.BlockSpec((ts, nq), lambda i: (i, 0)),
            pl.BlockSpec((nq, d), lambda i: (0, 0)),
            pl.BlockSpec((1, d), lambda i: (0, 0)),
        ] + r_in,
        out_specs=(pl.BlockSpec((ts, d), lambda i: (i, 0)),) + r_out,
        scratch_shapes=[pltpu.VMEM((N_EXPERTS, LANES), F32)],
        compiler_params=pltpu.CompilerParams(
            dimension_semantics=("arbitrary",), vmem_limit_bytes=VMEM_LIMIT_BYTES),
        name="attn_out_proj_route",
    )(x2d, o, w_o.astype(BF16), b_o.reshape(1, d),
      *_route_operands(g_ffn, w_router, b_router))
    return outs[0], outs[1:]


def kernel(x, norm_mix, norm_ffn, norm_final, pool_w, pool_b, pool_scale, w_qkv, b_qkv,
           sinks, w_o, b_o, rel_bias, w_router, b_router, w1, b1, w2, b2):
    bsz, seq, d = x.shape
    t = bsz * seq
    x, routed = _pool_layer(x, norm_mix[0], pool_w[0], pool_b[0], pool_scale[0],
                            norm_ffn[0], w_router[0], b_router[0])
    x = x.reshape(t, d)
    ys0, dest = _moe_experts(routed, 0, w1, b1, w2, b2)
    x = _combine(x, routed[2], dest, ys0, norm_final, final_norm=False)
    q, kk, vv = _qkv(x, norm_mix[1], w_qkv[0], b_qkv[0])
    o = _attention(q, kk, vv, sinks[0], rel_bias, bsz, seq)
    x, routed = _out_proj(x, o, w_o[0], b_o[0], norm_ffn[1], w_router[1], b_router[1])
    ys, dest = _moe_experts(routed, 1, w1, b1, w2, b2, reuse=ys0)
    x = _combine(x, routed[2], dest, ys, norm_final, final_norm=True)
    return x.reshape(bsz, seq, d)
```

```python
import functools
import math

import jax
import jax.numpy as jnp
import numpy as np
from jax import lax
from jax.experimental import pallas as pl
from jax.experimental.pallas import tpu as pltpu

F32 = jnp.float32
BF16 = jnp.bfloat16
I32 = jnp.int32

LANES = 128
D_MODEL = 1024
POOL_WINDOWS = (2, 4, 8, 16)
POOL_GROUP_CH = D_MODEL // len(POOL_WINDOWS)
POOL_HALO = 16
HEAD_DIM = 64
N_HEADS = 16
N_KV_HEADS = 2
GQA_GROUP = N_HEADS // N_KV_HEADS
HEADS_PER_SLAB = LANES // HEAD_DIM
SLABS_PER_GROUP = GQA_GROUP // HEADS_PER_SLAB
ATT_BLOCK = 128
ATT_STEP_BLOCKS = 4
WINDOW = 128
N_BUCKETS = 32
MAX_DISTANCE = 128
N_EXPERTS = 32
TOP_K = 4
D_FF = D_MODEL
SWIGLU_ALPHA = 1.702
SWIGLU_LIMIT = 7.0
RMS_EPS = 1e-5
NEG_INF = -1e30

ROW_TILE = 512
MOE_BLOCK = 512
EXPERT_STEP_BLOCKS = 2
DISPATCH_TILE = 4096
COMBINE_TILE = 256
ISSUE_UNROLL = 4
VMEM_LIMIT_BYTES = 56 * 1024 * 1024


def _rms(x, g):
    return x * lax.rsqrt(jnp.mean(x * x, axis=-1, keepdims=True) + RMS_EPS) * g


ROW_TILE_SUB = D_MODEL // LANES


def _load_row_tiles(ref, n, lead=(), start=0):
    return jnp.concatenate(
        [ref[lead + (pl.ds(start * ROW_TILE_SUB + j, n, stride=ROW_TILE_SUB), slice(None))]
         for j in range(ROW_TILE_SUB)], axis=1)


def _store_row_tiles(ref, val, n, start=0):
    for j in range(ROW_TILE_SUB):
        ref[pl.ds(start * ROW_TILE_SUB + j, n, stride=ROW_TILE_SUB), :] = (
            val[:, j * LANES:(j + 1) * LANES])


POOL_PAD = 8


def _pool_kernel(x_ref, xh_ref, g_ref, w_ref, b_ref, s_ref, gr_ref, wr_ref, br_ref,
                 o_ref, h_ref, ids_ref, gates_ref, rank_ref, sizes_ref,
                 buf_ref, s1_ref, s2_ref, s3_ref, carry_ref, *, ts):
    i = pl.program_id(1)
    c = POOL_GROUP_CH
    h0 = POOL_PAD
    t0 = POOL_PAD + POOL_HALO
    r1 = t0 + ts
    x = x_ref[...]
    g = g_ref[...]
    h = _rms(x, g)
    hh = _rms(xh_ref[...], g)
    hh = jnp.where(i == 0, 0.0, hh)
    buf_ref[0:h0, :] = jnp.zeros((h0, D_MODEL), F32)
    buf_ref[h0:t0, :] = hh
    buf_ref[t0:r1, :] = h
    sum0 = h[:, 0:c] + buf_ref[t0 - 1:r1 - 1, 0:c]
    l1 = buf_ref[h0:r1, c:] + buf_ref[h0 - 1:r1 - 1, c:]
    s1_ref[0:h0, :] = jnp.zeros((h0, 3 * c), F32)
    s1_ref[h0:r1, :] = l1
    sum1 = s1_ref[t0:r1, 0:c] + s1_ref[t0 - 2:r1 - 2, 0:c]
    l2 = s1_ref[h0:r1, c:] + s1_ref[h0 - 2:r1 - 2, c:]
    s2_ref[0:h0, :] = jnp.zeros((h0, 2 * c), F32)
    s2_ref[h0:r1, :] = l2
    sum2 = s2_ref[t0:r1, 0:c] + s2_ref[t0 - 4:r1 - 4, 0:c]
    s3_ref[h0:r1, :] = s2_ref[h0:r1, c:] + s2_ref[h0 - 4:r1 - 4, c:]
    sum3 = s3_ref[t0:r1, :] + s3_ref[t0 - 8:r1 - 8, :]
    t = i * ts + lax.broadcasted_iota(I32, (ts, 1), 0)
    outs = []
    for gi, (win, acc) in enumerate(zip(POOL_WINDOWS, (sum0, sum1, sum2, sum3))):
        cnt = jnp.minimum(t + 1, win).astype(F32)
        dlt = acc / cnt - h[:, gi * c:(gi + 1) * c]
        outs.append(jnp.dot(dlt.astype(BF16), w_ref[gi], preferred_element_type=F32))
    y = jnp.concatenate(outs, axis=1)
    x_new = x + (y + b_ref[...]) * s_ref[...]
    o_ref[...] = x_new
    first = (pl.program_id(0) == 0) & (i == 0)
    _route_tile(x_new, first, gr_ref, wr_ref, br_ref,
                h_ref, ids_ref, gates_ref, rank_ref, sizes_ref, carry_ref, ts)


def _pool_layer(x, g, w, b, s, g_ffn, w_router, b_router):
    bsz, seq, d = x.shape
    ts = min(ROW_TILE, seq)
    tiles = seq // ts
    kern = functools.partial(_pool_kernel, ts=ts)
    rows = POOL_PAD + POOL_HALO + ts
    vec = lambda: pl.BlockSpec((1, d), lambda bi, i: (0, 0))
    r_in, r_shape, r_out = _route_specs(bsz * seq, ts, d, lambda bi, i: bi * tiles + i)
    outs = pl.pallas_call(
        kern,
        out_shape=(jax.ShapeDtypeStruct(x.shape, F32),) + r_shape,
        grid=(bsz, tiles),
        in_specs=[
            pl.BlockSpec((None, ts, d), lambda bi, i: (bi, i, 0)),
            pl.BlockSpec((None, POOL_HALO, d),
                         lambda bi, i: (bi, jnp.maximum(i * (ts // POOL_HALO) - 1, 0), 0)),
            vec(),
            pl.BlockSpec(w.shape, lambda bi, i: (0, 0, 0)),
            vec(),
            vec(),
        ] + r_in,
        out_specs=(pl.BlockSpec((None, ts, d), lambda bi, i: (bi, i, 0)),) + r_out,
        scratch_shapes=[pltpu.VMEM((rows, d), F32),
                        pltpu.VMEM((rows, 3 * POOL_GROUP_CH), F32),
                        pltpu.VMEM((rows, 2 * POOL_GROUP_CH), F32),
                        pltpu.VMEM((rows, POOL_GROUP_CH), F32),
                        pltpu.VMEM((N_EXPERTS, LANES), F32)],
        compiler_params=pltpu.CompilerParams(
            dimension_semantics=("arbitrary", "arbitrary"),
            vmem_limit_bytes=VMEM_LIMIT_BYTES),
        name="pool_route",
    )(x, x, g.reshape(1, d), w.astype(BF16), b.reshape(1, d), s.reshape(1, d),
      *_route_operands(g_ffn, w_router, b_router))
    return outs[0], outs[1:]


def _route_tile(x, first, g_ref, wr_ref, br_ref,
                h_ref, ids_ref, gates_ref, rank_ref, sizes_ref, carry_ref, ts):
    @pl.when(first)
    def _():
        carry_ref[...] = jnp.zeros_like(carry_ref)

    h = _rms(x, g_ref[...])
    _store_row_tiles(h_ref, h, ts)
    h_hi = h.astype(BF16)
    h_lo = (h - h_hi.astype(F32)).astype(BF16)
    w = wr_ref[...]
    w_hi = w.astype(BF16)
    w_lo = (w - w_hi.astype(F32)).astype(BF16)
    dn = (((1,), (1,)), ((), ()))
    lg = (lax.dot_general(w_hi, h_hi, dn, preferred_element_type=F32)
          + lax.dot_general(w_lo, h_hi, dn, preferred_element_type=F32)
          + lax.dot_general(w_hi, h_lo, dn, preferred_element_type=F32))
    lg = lg + br_ref[:, 0:1]

    iota_e = lax.broadcasted_iota(I32, (N_EXPERTS, ts), 0)
    vals, ids = [], []
    for _ in range(TOP_K):
        m = jnp.max(lg, axis=0, keepdims=True)
        idx = jnp.min(jnp.where(lg == m, iota_e, N_EXPERTS), axis=0, keepdims=True)
        vals.append(m)
        ids.append(idx)
        lg = jnp.where(iota_e == idx, -jnp.inf, lg)
    ex = [jnp.exp(v - vals[0]) for v in vals]
    den = ex[0] + ex[1] + ex[2] + ex[3]
    gates_ref[...] = jnp.concatenate([e / den for e in ex], axis=0)
    ids_ref[...] = jnp.concatenate(ids, axis=0)

    hot = [(iota_e == idx) for idx in ids]
    multi = (hot[0] | hot[1] | hot[2] | hot[3])
    multi_f = multi.astype(F32)
    r = lax.broadcasted_iota(I32, (ts, ts), 0)
    c = lax.broadcasted_iota(I32, (ts, ts), 1)
    upper = (r < c).astype(BF16)
    before = jnp.dot(multi_f.astype(BF16), upper, preferred_element_type=F32)
    before = before + carry_ref[:, 0:1]
    ranks = [jnp.sum(jnp.where(hk, before, 0.0), axis=0, keepdims=True) for hk in hot]
    rank_ref[...] = jnp.concatenate(ranks, axis=0).astype(I32)
    carry_ref[...] = carry_ref[...] + jnp.sum(multi_f, axis=1, keepdims=True)
    sizes_ref[...] = carry_ref[...]


def _route_operands(g, w_router, b_router):
    d = g.shape[0]
    return (g.reshape(1, d), w_router.T,
            jnp.broadcast_to(b_router.reshape(N_EXPERTS, 1), (N_EXPERTS, LANES)))


def _route_specs(t, ts, d, tile_of):
    const = lambda *idx: (0, 0)
    in_specs = [pl.BlockSpec((1, d), const), pl.BlockSpec((N_EXPERTS, d), const),
                pl.BlockSpec((N_EXPERTS, LANES), const)]
    out_shape = (
        jax.ShapeDtypeStruct((t * ROW_TILE_SUB, LANES), F32),
        jax.ShapeDtypeStruct((TOP_K, t), I32),
        jax.ShapeDtypeStruct((TOP_K, t), F32),
        jax.ShapeDtypeStruct((TOP_K, t), I32),
        jax.ShapeDtypeStruct((N_EXPERTS, LANES), F32),
    )
    tok = lambda: pl.BlockSpec((TOP_K, ts), lambda *idx: (0, tile_of(*idx)))
    out_specs = (
        pl.BlockSpec((ts * ROW_TILE_SUB, LANES), lambda *idx: (tile_of(*idx), 0)),
        tok(), tok(), tok(),
        pl.BlockSpec((N_EXPERTS, LANES), const),
    )
    return in_specs, out_shape, out_specs


def _dispatch_kernel(pad_ref, n_used_ref, *refs, td, blk, nblk, zero_fill):
    dest_refs, h_ref, rest = refs[:TOP_K], refs[TOP_K], refs[TOP_K + 1:]
    xs_ref, zbuf, sem_z, sem = rest if zero_fill else rest[1:]
    i = pl.program_id(0)
    sub = ROW_TILE_SUB

    def zero_rows(row, n):
        return pltpu.make_async_copy(
            zbuf.at[pl.ds(0, n * sub), :],
            xs_ref.at[pl.ds(pl.multiple_of(row * sub, sub), n * sub), :], sem_z)

    def padding_copies(act):
        for e in range(N_EXPERTS):
            pos = pad_ref[e]
            length = pad_ref[N_EXPERTS + e]
            n = blk // 2
            while n >= 1:
                has = (length & n) != 0

                @pl.when(has)
                def _(pos=pos, n=n):
                    act(zero_rows(pos, n))
                pos = pos + jnp.where(has, n, 0)
                n //= 2

        def tail(b, carry):
            act(zero_rows(b * blk, blk))
            return carry
        lax.fori_loop(n_used_ref[0], nblk, tail, 0)

    if zero_fill:
        @pl.when(i == 0)
        def _():
            zbuf[...] = jnp.zeros_like(zbuf)
            padding_copies(lambda cp: cp.start())

    def issue(t, carry):
        src = h_ref.at[pl.ds(pl.multiple_of(t * sub, sub), sub), :]
        for k in range(TOP_K):
            d = dest_refs[k][0, 0, t]
            pltpu.make_async_copy(src, xs_ref.at[pl.ds(pl.multiple_of(d * sub, sub), sub), :],
                                  sem).start(priority=k % 2)
        return carry

    lax.fori_loop(0, td, issue, 0, unroll=ISSUE_UNROLL)
    for k in range(TOP_K):
        pltpu.make_async_copy(h_ref, xs_ref.at[pl.ds(0, td * sub), :], sem).wait()

    if zero_fill:
        @pl.when(i == 0)
        def _():
            padding_copies(lambda cp: cp.wait())


def _dispatch(h_tiles, dest, pad, n_used, n_rows, reuse=None):
    sub = ROW_TILE_SUB
    t = h_tiles.shape[0] // sub
    td = min(DISPATCH_TILE, t)
    nt = t // td
    dest_k = [dest[k].reshape(nt, 1, td) for k in range(TOP_K)]
    zero_fill = reuse is None
    kern = functools.partial(_dispatch_kernel, td=td, blk=MOE_BLOCK, nblk=n_rows // MOE_BLOCK,
                             zero_fill=zero_fill)
    extra_specs = [] if zero_fill else [pl.BlockSpec(memory_space=pl.ANY)]
    extra_args = [] if zero_fill else [reuse]
    return pl.pallas_call(
        kern,
        out_shape=jax.ShapeDtypeStruct((n_rows * sub, LANES), F32),
        grid_spec=pltpu.PrefetchScalarGridSpec(
            num_scalar_prefetch=2,
            grid=(nt,),
            in_specs=[pl.BlockSpec((1, 1, td), lambda i, z, nu: (i, 0, 0),
                                   memory_space=pltpu.SMEM)] * TOP_K + [
                pl.BlockSpec((td * sub, LANES), lambda i, z, nu: (i, 0)),
            ] + extra_specs,
            out_specs=pl.BlockSpec(memory_space=pl.ANY),
            scratch_shapes=[
                pltpu.VMEM((MOE_BLOCK * sub, LANES), F32),
                pltpu.SemaphoreType.DMA(()),
                pltpu.SemaphoreType.DMA(()),
            ],
        ),
        compiler_params=pltpu.CompilerParams(
            dimension_semantics=("arbitrary",), vmem_limit_bytes=VMEM_LIMIT_BYTES),
        input_output_aliases={} if zero_fill else {3 + TOP_K: 0},
        name="moe_dispatch" if zero_fill else "moe_dispatch_reuse",
    )(pad, n_used, *dest_k, h_tiles, *extra_args)


def _expert_kernel(blk_e_ref, n_used_ref, nxt_e_ref, n_valid_ref, x_ref, w1_hbm, b1_ref, w2_hbm,
                   b2_ref, o_ref, w1s_ref, w2s_ref, w1b_ref, w2b_ref, wsem, *, layer):
    step = pl.program_id(0)
    sub_rows = MOE_BLOCK * ROW_TILE_SUB

    def weight_copies(expert):
        return (pltpu.make_async_copy(w1_hbm.at[layer, expert], w1s_ref, wsem.at[0]),
                pltpu.make_async_copy(w2_hbm.at[layer, expert], w2s_ref, wsem.at[1]))

    @pl.when(step == 0)
    def _():
        for cp in weight_copies(blk_e_ref[0]):
            cp.start()

    for sub in range(EXPERT_STEP_BLOCKS):
        i = step * EXPERT_STEP_BLOCKS + sub
        e = blk_e_ref[i]
        prev = blk_e_ref[jnp.maximum(i - 1, 0)]
        fresh = (i == 0) | (e != prev)
        live = i < n_used_ref[0]

        @pl.when(live & fresh)
        def _(i=i, e=e):
            for cp in weight_copies(e):
                cp.wait()
            w1b_ref[...] = w1s_ref[...].astype(BF16)
            w2b_ref[...] = w2s_ref[...].astype(BF16)
            nxt = nxt_e_ref[i]

            @pl.when(nxt >= 0)
            def _():
                for cp in weight_copies(nxt):
                    cp.start()

        def mlp(rows, sub=sub, e=e):
            r0 = sub * MOE_BLOCK
            x = _load_row_tiles(x_ref, rows, start=r0).astype(BF16)
            a = jnp.dot(x, w1b_ref[...], preferred_element_type=F32) + b1_ref[e]
            glu = jnp.minimum(a[:, :D_FF], SWIGLU_LIMIT)
            lin = jnp.clip(a[:, D_FF:], -SWIGLU_LIMIT, SWIGLU_LIMIT)
            act = glu * jax.nn.sigmoid(SWIGLU_ALPHA * glu) * (lin + 1.0)
            y = jnp.dot(act.astype(BF16), w2b_ref[...], preferred_element_type=F32) + b2_ref[e]
            _store_row_tiles(o_ref, y, rows, start=r0)
            if rows < MOE_BLOCK:
                o_ref[(r0 + rows) * ROW_TILE_SUB:(r0 + MOE_BLOCK) * ROW_TILE_SUB, :] = jnp.zeros(
                    ((MOE_BLOCK - rows) * ROW_TILE_SUB, LANES), F32)

        half = MOE_BLOCK // 2
        short = n_valid_ref[i] <= half

        @pl.when(live & jnp.logical_not(short))
        def _(mlp=mlp):
            mlp(MOE_BLOCK)

        @pl.when(live & short)
        def _(mlp=mlp):
            mlp(half)

        @pl.when(jnp.logical_not(live) & (step * EXPERT_STEP_BLOCKS < n_used_ref[0]))
        def _(sub=sub):
            o_ref[sub * sub_rows:(sub + 1) * sub_rows, :] = jnp.zeros((sub_rows, LANES), F32)


def _experts(xs, blk_e, n_used, nxt_e, n_valid, layer, w1, b1, w2, b2):
    sub = ROW_TILE_SUB
    n_rows = xs.shape[0] // sub
    d = w2.shape[-1]
    nblk = n_rows // MOE_BLOCK
    depth = w1.shape[0]
    grp = EXPERT_STEP_BLOCKS
    assert nblk % grp == 0
    row = lambda i, be, nu, nx, nv: (jnp.minimum(i, (nu[0] - 1) // grp), 0)
    out_row = row
    bias = lambda i, be, nu, nx, nv: (layer, 0, 0, 0)
    return pl.pallas_call(
        functools.partial(_expert_kernel, layer=layer),
        out_shape=jax.ShapeDtypeStruct((n_rows * sub, LANES), F32),
        grid_spec=pltpu.PrefetchScalarGridSpec(
            num_scalar_prefetch=4,
            grid=(nblk // grp,),
            in_specs=[
                pl.BlockSpec((grp * MOE_BLOCK * sub, LANES), row),
                pl.BlockSpec(memory_space=pl.ANY),
                pl.BlockSpec((None, N_EXPERTS, 1, 2 * D_FF), bias),
                pl.BlockSpec(memory_space=pl.ANY),
                pl.BlockSpec((None, N_EXPERTS, 1, d), bias),
            ],
            out_specs=pl.BlockSpec((grp * MOE_BLOCK * sub, LANES), out_row),
            scratch_shapes=[
                pltpu.VMEM((d, 2 * D_FF), F32),
                pltpu.VMEM((D_FF, d), F32),
                pltpu.VMEM((d, 2 * D_FF), BF16),
                pltpu.VMEM((D_FF, d), BF16),
                pltpu.SemaphoreType.DMA((2,)),
            ],
        ),
        compiler_params=pltpu.CompilerParams(
            dimension_semantics=("arbitrary",), vmem_limit_bytes=VMEM_LIMIT_BYTES),
        input_output_aliases={4: 0},
        name="moe_experts",
    )(blk_e, n_used, nxt_e, n_valid, xs, w1, b1.reshape(depth, N_EXPERTS, 1, 2 * D_FF), w2,
      b2.reshape(depth, N_EXPERTS, 1, d))


def _combine_kernel(*refs, tc, n_tiles, final_norm):
    dcur_refs, dnext_refs = refs[:TOP_K], refs[TOP_K:2 * TOP_K]
    x_ref, g_ref, ys_ref, gn_ref, o_ref, buf, sem = refs[2 * TOP_K:]
    i = pl.program_id(0)
    sub = ROW_TILE_SUB

    def issue(drefs, slot):
        def body(t, carry):
            for k in range(TOP_K):
                d = drefs[k][0, 0, t]
                pltpu.make_async_copy(
                    ys_ref.at[pl.ds(pl.multiple_of(d * sub, sub), sub), :],
                    buf.at[slot, k, pl.ds(pl.multiple_of(t * sub, sub), sub), :],
                    sem.at[slot]).start(priority=k % 2)
            return carry
        lax.fori_loop(0, tc, body, 0, unroll=ISSUE_UNROLL)

    @pl.when(i == 0)
    def _():
        issue(dcur_refs, 0)

    @pl.when(i + 1 < n_tiles)
    def _():
        issue(dnext_refs, (i + 1) % 2)

    slot = i % 2
    for k in range(TOP_K):
        pltpu.make_async_copy(ys_ref.at[pl.ds(0, tc * sub), :], buf.at[slot, k],
                              sem.at[slot]).wait()
    acc = x_ref[...]
    gates = g_ref[...]
    for k in range(TOP_K):
        acc = acc + gates[:, k:k + 1] * _load_row_tiles(buf, tc, lead=(slot, k))
    if final_norm:
        acc = _rms(acc, gn_ref[...])
    o_ref[...] = acc


def _combine(x2d, gates_t, dest, ys, g_norm, final_norm):
    t, d = x2d.shape
    tc = min(COMBINE_TILE, t)
    nt = t // tc
    dest_k = [dest[k].reshape(nt, 1, tc) for k in range(TOP_K)]
    cur = pl.BlockSpec((1, 1, tc), lambda i: (i, 0, 0), memory_space=pltpu.SMEM)
    nxt = pl.BlockSpec((1, 1, tc), lambda i: (jnp.minimum(i + 1, nt - 1), 0, 0),
                       memory_space=pltpu.SMEM)
    kern = functools.partial(_combine_kernel, tc=tc, n_tiles=nt, final_norm=final_norm)
    return pl.pallas_call(
        kern,
        out_shape=jax.ShapeDtypeStruct((t, d), F32),
        grid=(nt,),
        in_specs=[cur] * TOP_K + [nxt] * TOP_K + [
            pl.BlockSpec((tc, d), lambda i: (i, 0)),
            pl.BlockSpec((tc, TOP_K), lambda i: (i, 0)),
            pl.BlockSpec(memory_space=pl.ANY),
            pl.BlockSpec((1, d), lambda i: (0, 0)),
        ],
        out_specs=pl.BlockSpec((tc, d), lambda i: (i, 0)),
        scratch_shapes=[
            pltpu.VMEM((2, TOP_K, tc * ROW_TILE_SUB, LANES), F32),
            pltpu.SemaphoreType.DMA((2,)),
        ],
        compiler_params=pltpu.CompilerParams(
            dimension_semantics=("arbitrary",), vmem_limit_bytes=VMEM_LIMIT_BYTES),
        name="moe_combine_final" if final_norm else "moe_combine",
    )(*dest_k, *dest_k, x2d, gates_t.T, ys, g_norm.reshape(1, d))


def _moe_experts(routed, layer, w1, b1, w2, b2, reuse=None):
    h, ids, gates, rank, sizes = routed
    t = ids.shape[1]
    sizes = sizes[:, 0].astype(I32)
    padded = ((sizes + MOE_BLOCK - 1) // MOE_BLOCK) * MOE_BLOCK
    pend = jnp.cumsum(padded)
    pstart = pend - padded
    onehot = ids[:, :, None] == jnp.arange(N_EXPERTS, dtype=I32)
    dest = rank + jnp.sum(jnp.where(onehot, pstart, 0), axis=-1)
    n_rows = t * TOP_K + N_EXPERTS * MOE_BLOCK
    nblk = n_rows // MOE_BLOCK
    blk_row = jnp.arange(nblk, dtype=I32) * MOE_BLOCK
    blk_e = jnp.minimum(jnp.sum((pend[None, :] <= blk_row[:, None]).astype(I32), axis=1),
                        N_EXPERTS - 1)
    n_used = (pend[-1:] // MOE_BLOCK).astype(I32)
    pad = jnp.concatenate([pstart + sizes, padded - sizes]).astype(I32)
    xs = _dispatch(h, dest, pad, n_used, n_rows, reuse)
    blk_id = jnp.arange(nblk, dtype=I32)
    later = ((blk_id[None, :] > blk_id[:, None]) & (blk_e[None, :] != blk_e[:, None])
             & (blk_id[None, :] < n_used[0]))
    nxt_blk = jnp.min(jnp.where(later, blk_id[None, :], nblk), axis=1)
    nxt_e = jnp.sum(jnp.where(blk_id[None, :] == nxt_blk[:, None], blk_e[None, :] + 1, 0),
                    axis=1).astype(I32) - 1
    own = blk_e[:, None] == jnp.arange(N_EXPERTS, dtype=I32)[None, :]
    row_end = jnp.sum(jnp.where(own, (pstart + sizes)[None, :], 0), axis=1)
    n_valid = jnp.clip(row_end - blk_row, 0, MOE_BLOCK).astype(I32)
    ys = _experts(xs, blk_e, n_used, nxt_e, n_valid, layer, w1, b1, w2, b2)
    return ys, dest


def _qkv_kernel(x_ref, g_ref, w_ref, b_ref, q_ref, k_ref, v_ref):
    h = _rms(x_ref[...], g_ref[...])
    qkv = jnp.dot(h.astype(BF16), w_ref[...], preferred_element_type=F32) + b_ref[...]
    nq = N_HEADS * HEAD_DIM
    nk = 2 * N_KV_HEADS * HEAD_DIM
    q_ref[...] = (qkv[:, :nq] * (HEAD_DIM ** -0.5)).astype(BF16)
    k_ref[...] = qkv[:, nq:nq + nk].astype(BF16)
    v_ref[...] = qkv[:, nq + nk:].astype(BF16)


def _qkv(x2d, g, w_qkv, b_qkv):
    t, d = x2d.shape
    ts = min(ROW_TILE, t)
    nq = N_HEADS * HEAD_DIM
    kv = N_KV_HEADS * HEAD_DIM

    def dup_heads(m):
        parts = []
        for hd in range(N_KV_HEADS):
            sl = m[..., hd * HEAD_DIM:(hd + 1) * HEAD_DIM]
            parts += [sl, sl]
        return jnp.concatenate(parts, axis=-1)

    w = jnp.concatenate([w_qkv[:, :nq], dup_heads(w_qkv[:, nq:nq + kv]),
                         dup_heads(w_qkv[:, nq + kv:])], axis=1).astype(BF16)
    b = jnp.concatenate([b_qkv[:nq], dup_heads(b_qkv[nq:nq + kv]),
                         dup_heads(b_qkv[nq + kv:])]).reshape(1, -1)
    n_out = w.shape[1]
    row = lambda width: pl.BlockSpec((ts, width), lambda i: (i, 0))
    return pl.pallas_call(
        _qkv_kernel,
        out_shape=(
            jax.ShapeDtypeStruct((t, nq), BF16),
            jax.ShapeDtypeStruct((t, 2 * kv), BF16),
            jax.ShapeDtypeStruct((t, 2 * kv), BF16),
        ),
        grid=(t // ts,),
        in_specs=[
            row(d),
            pl.BlockSpec((1, d), lambda i: (0, 0)),
            pl.BlockSpec((d, n_out), lambda i: (0, 0)),
            pl.BlockSpec((1, n_out), lambda i: (0, 0)),
        ],
        out_specs=(row(nq), row(2 * kv), row(2 * kv)),
        compiler_params=pltpu.CompilerParams(
            dimension_semantics=("arbitrary",), vmem_limit_bytes=VMEM_LIMIT_BYTES),
        name="attn_qkv",
    )(x2d, g.reshape(1, d), w, b)


def _attn_kernel(sinks_ref, q_ref, kp_ref, kc_ref, vp_ref, vc_ref, bias_ref, o_ref):
    n = pl.program_id(1)
    rows = SLABS_PER_GROUP * ATT_BLOCK
    keys = 2 * ATT_BLOCK
    kall = jnp.concatenate([kp_ref[...], kc_ref[...]], axis=0)
    vall = jnp.concatenate([vp_ref[...], vc_ref[...]], axis=0)
    j = lax.broadcasted_iota(I32, (keys, rows), 0)
    r = lax.broadcasted_iota(I32, (keys, rows), 1) & (ATT_BLOCK - 1)
    in_window = (j > r) & (j <= r + WINDOW)
    lane_k = lax.broadcasted_iota(I32, (keys, LANES), 1)
    halves = [lane_k < HEAD_DIM, lane_k >= HEAD_DIM]
    sub_o = lax.broadcasted_iota(I32, (LANES, rows), 0)
    zero = jnp.zeros((), BF16)
    for blk in range(ATT_STEP_BLOCKS):
        q = q_ref[blk * ATT_BLOCK:(blk + 1) * ATT_BLOCK, :]
        kcat = kall[blk * ATT_BLOCK:blk * ATT_BLOCK + keys]
        vcat = vall[blk * ATT_BLOCK:blk * ATT_BLOCK + keys]
        if blk == 0:
            mask = in_window & ((n > 0) | (j >= ATT_BLOCK))
        else:
            mask = in_window
        for g in range(N_KV_HEADS):
            s0 = g * SLABS_PER_GROUP
            qg = jnp.concatenate([q[:, (s0 + s) * LANES:(s0 + s + 1) * LANES]
                                  for s in range(SLABS_PER_GROUP)], axis=0)
            kg = kcat[:, g * LANES:(g + 1) * LANES]
            vg = vcat[:, g * LANES:(g + 1) * LANES]
            pts, invs = [], []
            for p in range(HEADS_PER_SLAB):
                kp = jnp.where(halves[p], kg, zero)
                sc = lax.dot_general(kp, qg, (((1,), (1,)), ((), ())),
                                     preferred_element_type=F32)
                sc = jnp.where(mask, sc + bias_ref[g, p], NEG_INF)
                sink = jnp.concatenate(
                    [jnp.full((1, ATT_BLOCK),
                              sinks_ref[g * GQA_GROUP + s * HEADS_PER_SLAB + p], F32)
                     for s in range(SLABS_PER_GROUP)], axis=1)
                m = jnp.maximum(jnp.max(sc, axis=0, keepdims=True), sink)
                pe = jnp.exp(sc - m)
                den = jnp.sum(pe, axis=0, keepdims=True) + jnp.exp(sink - m)
                pts.append(pe.astype(BF16))
                invs.append(1.0 / den)
            pcat = jnp.concatenate(pts, axis=0)
            vblk = jnp.concatenate([jnp.where(halves[p], vg, zero)
                                    for p in range(HEADS_PER_SLAB)], axis=0)
            og = lax.dot_general(vblk, pcat, (((0,), (0,)), ((), ())),
                                 preferred_element_type=F32)
            og = og * jnp.where(sub_o < HEAD_DIM, invs[0], invs[1])
            og = og.T
            for s in range(SLABS_PER_GROUP):
                o_ref[blk * ATT_BLOCK:(blk + 1) * ATT_BLOCK,
                      (s0 + s) * LANES:(s0 + s + 1) * LANES] = (
                    og[s * ATT_BLOCK:(s + 1) * ATT_BLOCK].astype(BF16))


def _t5_bias_table(rel_bias):
    r = np.arange(ATT_BLOCK)[:, None]
    j = np.arange(2 * ATT_BLOCK)[None, :]
    rel = np.clip(ATT_BLOCK + r - j, 0, WINDOW - 1)
    max_exact = N_BUCKETS // 2
    nf = jnp.maximum(rel, max_exact).astype(F32)
    large = max_exact + (jnp.log(nf / max_exact) / math.log(MAX_DISTANCE / max_exact)
                         * (N_BUCKETS - max_exact)).astype(I32)
    large = jnp.minimum(large, N_BUCKETS - 1)
    bucket = jnp.where(rel < max_exact, rel, large)
    pick = (bucket[None] == jnp.arange(N_BUCKETS, dtype=I32)[:, None, None])
    bias = jnp.sum(jnp.where(pick[:, None], rel_bias.astype(F32)[:, :, None, None], 0.0),
                   axis=0)
    bias = bias.reshape(N_KV_HEADS, SLABS_PER_GROUP, HEADS_PER_SLAB, ATT_BLOCK, 2 * ATT_BLOCK)
    bias = jnp.transpose(bias, (0, 2, 4, 1, 3))
    return bias.reshape(N_KV_HEADS, HEADS_PER_SLAB, 2 * ATT_BLOCK, SLABS_PER_GROUP * ATT_BLOCK)


def _attention(q, kk, vv, sinks, rel_bias, bsz, seq):
    nq = N_HEADS * HEAD_DIM
    kvw = kk.shape[-1]
    step = ATT_STEP_BLOCKS * ATT_BLOCK
    q3 = q.reshape(bsz, seq, nq)
    k3 = kk.reshape(bsz, seq, kvw)
    v3 = vv.reshape(bsz, seq, kvw)
    bias = _t5_bias_table(rel_bias)
    prev = lambda bi, n: (bi, jnp.maximum(n * ATT_STEP_BLOCKS - 1, 0), 0)
    cur = lambda bi, n: (bi, n, 0)
    out = pl.pallas_call(
        _attn_kernel,
        out_shape=jax.ShapeDtypeStruct((bsz, seq, nq), BF16),
        grid=(bsz, seq // step),
        in_specs=[
            pl.BlockSpec(memory_space=pltpu.SMEM),
            pl.BlockSpec((None, step, nq), cur),
            pl.BlockSpec((None, ATT_BLOCK, kvw), prev),
            pl.BlockSpec((None, step, kvw), cur),
            pl.BlockSpec((None, ATT_BLOCK, kvw), prev),
            pl.BlockSpec((None, step, kvw), cur),
            pl.BlockSpec(bias.shape, lambda bi, n: (0, 0, 0, 0)),
        ],
        out_specs=pl.BlockSpec((None, step, nq), cur),
        compiler_params=pltpu.CompilerParams(
            dimension_semantics=("arbitrary", "arbitrary"),
            vmem_limit_bytes=VMEM_LIMIT_BYTES),
        name="attn_core",
    )(sinks.astype(F32), q3, k3, k3, v3, v3, bias)
    return out.reshape(bsz * seq, nq)


def _oproj_kernel(x_ref, o_ref, w_ref, b_ref, gr_ref, wr_ref, br_ref,
                  y_ref, h_ref, ids_ref, gates_ref, rank_ref, sizes_ref, carry_ref, *, ts):
    x_new = x_ref[...] + jnp.dot(o_ref[...], w_ref[...],
                                 preferred_element_type=F32) + b_ref[...]
    y_ref[...] = x_new
    _route_tile(x_new, pl.program_id(0) == 0, gr_ref, wr_ref, br_ref,
                h_ref, ids_ref, gates_ref, rank_ref, sizes_ref, carry_ref, ts)


def _out_proj(x2d, o, w_o, b_o, g_ffn, w_router, b_router):
    t, d = x2d.shape
    ts = min(ROW_TILE, t)
    nq = o.shape[1]
    r_in, r_shape, r_out = _route_specs(t, ts, d, lambda i: i)
    outs = pl.pallas_call(
        functools.partial(_oproj_kernel, ts=ts),
        out_shape=(jax.ShapeDtypeStruct((t, d), F32),) + r_shape,
        grid=(t // ts,),
        in_specs=[
            pl.BlockSpec((ts, d), lambda i: (i, 0)),
            pl.BlockSpec((ts, nq), lambda i: (i, 0)),
            pl.BlockSpec((nq, d), lambda i: (0, 0)),
            pl.BlockSpec((1, d), lambda i: (0, 0)),
        ] + r_in,
        out_specs=(pl.BlockSpec((ts, d), lambda i: (i, 0)),) + r_out,
        scratch_shapes=[pltpu.VMEM((N_EXPERTS, LANES), F32)],
        compiler_params=pltpu.CompilerParams(
            dimension_semantics=("arbitrary",), vmem_limit_bytes=VMEM_LIMIT_BYTES),
        name="attn_out_proj_route",
    )(x2d, o, w_o.astype(BF16), b_o.reshape(1, d),
      *_route_operands(g_ffn, w_router, b_router))
    return outs[0], outs[1:]


def kernel(x, norm_mix, norm_ffn, norm_final, pool_w, pool_b, pool_scale, w_qkv, b_qkv,
           sinks, w_o, b_o, rel_bias, w_router, b_router, w1, b1, w2, b2):
    bsz, seq, d = x.shape
    t = bsz * seq
    x, routed = _pool_layer(x, norm_mix[0], pool_w[0], pool_b[0], pool_scale[0],
                            norm_ffn[0], w_router[0], b_router[0])
    x = x.reshape(t, d)
    ys0, dest = _moe_experts(routed, 0, w1, b1, w2, b2)
    x = _combine(x, routed[2], dest, ys0, norm_final, final_norm=False)
    q, kk, vv = _qkv(x, norm_mix[1], w_qkv[0], b_qkv[0])
    o = _attention(q, kk, vv, sinks[0], rel_bias, bsz, seq)
    x, routed = _out_proj(x, o, w_o[0], b_o[0], norm_ffn[1], w_router[1], b_router[1])
    ys, dest = _moe_experts(routed, 1, w1, b1, w2, b2, reuse=ys0)
    x = _combine(x, routed[2], dest, ys, norm_final, final_norm=True)
    return x.reshape(bsz, seq, d)
```

```python
import functools
import math

import jax
import jax.numpy as jnp
import numpy as np
from jax import lax
from jax.experimental import pallas as pl
from jax.experimental.pallas import tpu as pltpu

F32 = jnp.float32
BF16 = jnp.bfloat16
I32 = jnp.int32

LANES = 128
D_MODEL = 1024
POOL_WINDOWS = (2, 4, 8, 16)
POOL_GROUP_CH = D_MODEL // len(POOL_WINDOWS)
POOL_HALO = 16
HEAD_DIM = 64
N_HEADS = 16
N_KV_HEADS = 2
GQA_GROUP = N_HEADS // N_KV_HEADS
HEADS_PER_SLAB = LANES // HEAD_DIM
SLABS_PER_GROUP = GQA_GROUP // HEADS_PER_SLAB
ATT_BLOCK = 128
ATT_STEP_BLOCKS = 8
WINDOW = 128
N_BUCKETS = 32
MAX_DISTANCE = 128
N_EXPERTS = 32
TOP_K = 4
D_FF = D_MODEL
SWIGLU_ALPHA = 1.702
SWIGLU_LIMIT = 7.0
RMS_EPS = 1e-5
NEG_INF = -1e30

ROW_TILE = 512
QKV_TILE = 1024
MOE_BLOCK = 512
EXPERT_STEP_BLOCKS = 2
DISPATCH_TILE = 4096
COMBINE_TILE = 256
ISSUE_UNROLL = 4
VMEM_LIMIT_BYTES = 56 * 1024 * 1024


def _rms(x, g):
    return x * lax.rsqrt(jnp.mean(x * x, axis=-1, keepdims=True) + RMS_EPS) * g


ROW_TILE_SUB = D_MODEL // LANES


def _load_row_tiles(ref, n, lead=(), start=0):
    return jnp.concatenate(
        [ref[lead + (pl.ds(start * ROW_TILE_SUB + j, n, stride=ROW_TILE_SUB), slice(None))]
         for j in range(ROW_TILE_SUB)], axis=1)


def _store_row_tiles(ref, val, n, start=0):
    for j in range(ROW_TILE_SUB):
        ref[pl.ds(start * ROW_TILE_SUB + j, n, stride=ROW_TILE_SUB), :] = (
            val[:, j * LANES:(j + 1) * LANES])


POOL_PAD = 8


def _pool_kernel(x_ref, xh_ref, g_ref, w_ref, b_ref, s_ref, gr_ref, wr_ref, br_ref,
                 o_ref, h_ref, ids_ref, gates_ref, rank_ref, sizes_ref,
                 buf_ref, s1_ref, s2_ref, s3_ref, carry_ref, *, ts):
    i = pl.program_id(1)
    c = POOL_GROUP_CH
    h0 = POOL_PAD
    t0 = POOL_PAD + POOL_HALO
    r1 = t0 + ts
    x = x_ref[...]
    g = g_ref[...]
    h = _rms(x, g)
    hh = _rms(xh_ref[...], g)
    hh = jnp.where(i == 0, 0.0, hh)
    buf_ref[0:h0, :] = jnp.zeros((h0, D_MODEL), F32)
    buf_ref[h0:t0, :] = hh
    buf_ref[t0:r1, :] = h
    sum0 = h[:, 0:c] + buf_ref[t0 - 1:r1 - 1, 0:c]
    l1 = buf_ref[h0:r1, c:] + buf_ref[h0 - 1:r1 - 1, c:]
    s1_ref[0:h0, :] = jnp.zeros((h0, 3 * c), F32)
    s1_ref[h0:r1, :] = l1
    sum1 = s1_ref[t0:r1, 0:c] + s1_ref[t0 - 2:r1 - 2, 0:c]
    l2 = s1_ref[h0:r1, c:] + s1_ref[h0 - 2:r1 - 2, c:]
    s2_ref[0:h0, :] = jnp.zeros((h0, 2 * c), F32)
    s2_ref[h0:r1, :] = l2
    sum2 = s2_ref[t0:r1, 0:c] + s2_ref[t0 - 4:r1 - 4, 0:c]
    s3_ref[h0:r1, :] = s2_ref[h0:r1, c:] + s2_ref[h0 - 4:r1 - 4, c:]
    sum3 = s3_ref[t0:r1, :] + s3_ref[t0 - 8:r1 - 8, :]
    t = i * ts + lax.broadcasted_iota(I32, (ts, 1), 0)
    outs = []
    for gi, (win, acc) in enumerate(zip(POOL_WINDOWS, (sum0, sum1, sum2, sum3))):
        cnt = jnp.minimum(t + 1, win).astype(F32)
        dlt = acc / cnt - h[:, gi * c:(gi + 1) * c]
        outs.append(jnp.dot(dlt.astype(BF16), w_ref[gi], preferred_element_type=F32))
    y = jnp.concatenate(outs, axis=1)
    x_new = x + (y + b_ref[...]) * s_ref[...]
    o_ref[...] = x_new
    first = (pl.program_id(0) == 0) & (i == 0)
    _route_tile(x_new, first, gr_ref, wr_ref, br_ref,
                h_ref, ids_ref, gates_ref, rank_ref, sizes_ref, carry_ref, ts)


def _pool_layer(x, g, w, b, s, g_ffn, w_router, b_router):
    bsz, seq, d = x.shape
    ts = min(ROW_TILE, seq)
    tiles = seq // ts
    kern = functools.partial(_pool_kernel, ts=ts)
    rows = POOL_PAD + POOL_HALO + ts
    vec = lambda: pl.BlockSpec((1, d), lambda bi, i: (0, 0))
    r_in, r_shape, r_out = _route_specs(bsz * seq, ts, d, lambda bi, i: bi * tiles + i)
    outs = pl.pallas_call(
        kern,
        out_shape=(jax.ShapeDtypeStruct(x.shape, F32),) + r_shape,
        grid=(bsz, tiles),
        in_specs=[
            pl.BlockSpec((None, ts, d), lambda bi, i: (bi, i, 0)),
            pl.BlockSpec((None, POOL_HALO, d),
                         lambda bi, i: (bi, jnp.maximum(i * (ts // POOL_HALO) - 1, 0), 0)),
            vec(),
            pl.BlockSpec(w.shape, lambda bi, i: (0, 0, 0)),
            vec(),
            vec(),
        ] + r_in,
        out_specs=(pl.BlockSpec((None, ts, d), lambda bi, i: (bi, i, 0)),) + r_out,
        scratch_shapes=[pltpu.VMEM((rows, d), F32),
                        pltpu.VMEM((rows, 3 * POOL_GROUP_CH), F32),
                        pltpu.VMEM((rows, 2 * POOL_GROUP_CH), F32),
                        pltpu.VMEM((rows, POOL_GROUP_CH), F32),
                        pltpu.VMEM((N_EXPERTS, LANES), F32)],
        compiler_params=pltpu.CompilerParams(
            dimension_semantics=("arbitrary", "arbitrary"),
            vmem_limit_bytes=VMEM_LIMIT_BYTES),
        name="pool_route",
    )(x, x, g.reshape(1, d), w.astype(BF16), b.reshape(1, d), s.reshape(1, d),
      *_route_operands(g_ffn, w_router, b_router))
    return outs[0], outs[1:]


def _route_tile(x, first, g_ref, wr_ref, br_ref,
                h_ref, ids_ref, gates_ref, rank_ref, sizes_ref, carry_ref, ts):
    @pl.when(first)
    def _():
        carry_ref[...] = jnp.zeros_like(carry_ref)

    h = _rms(x, g_ref[...])
    _store_row_tiles(h_ref, h, ts)
    h_hi = h.astype(BF16)
    h_lo = (h - h_hi.astype(F32)).astype(BF16)
    w = wr_ref[...]
    w_hi = w.astype(BF16)
    w_lo = (w - w_hi.astype(F32)).astype(BF16)
    dn = (((1,), (1,)), ((), ()))
    lg = (lax.dot_general(w_hi, h_hi, dn, preferred_element_type=F32)
          + lax.dot_general(w_lo, h_hi, dn, preferred_element_type=F32)
          + lax.dot_general(w_hi, h_lo, dn, preferred_element_type=F32))
    lg = lg + br_ref[:, 0:1]

    iota_e = lax.broadcasted_iota(I32, (N_EXPERTS, ts), 0)
    vals, ids = [], []
    for _ in range(TOP_K):
        m = jnp.max(lg, axis=0, keepdims=True)
        idx = jnp.min(jnp.where(lg == m, iota_e, N_EXPERTS), axis=0, keepdims=True)
        vals.append(m)
        ids.append(idx)
        lg = jnp.where(iota_e == idx, -jnp.inf, lg)
    ex = [jnp.exp(v - vals[0]) for v in vals]
    den = ex[0] + ex[1] + ex[2] + ex[3]
    gates_ref[...] = jnp.concatenate([e / den for e in ex], axis=0)
    ids_ref[...] = jnp.concatenate(ids, axis=0)

    hot = [(iota_e == idx) for idx in ids]
    multi = (hot[0] | hot[1] | hot[2] | hot[3])
    multi_f = multi.astype(F32)
    r = lax.broadcasted_iota(I32, (ts, ts), 0)
    c = lax.broadcasted_iota(I32, (ts, ts), 1)
    upper = (r < c).astype(BF16)
    before = jnp.dot(multi_f.astype(BF16), upper, preferred_element_type=F32)
    before = before + carry_ref[:, 0:1]
    ranks = [jnp.sum(jnp.where(hk, before, 0.0), axis=0, keepdims=True) for hk in hot]
    rank_ref[...] = jnp.concatenate(ranks, axis=0).astype(I32)
    carry_ref[...] = carry_ref[...] + jnp.sum(multi_f, axis=1, keepdims=True)
    sizes_ref[...] = carry_ref[...]


def _route_operands(g, w_router, b_router):
    d = g.shape[0]
    return (g.reshape(1, d), w_router.T,
            jnp.broadcast_to(b_router.reshape(N_EXPERTS, 1), (N_EXPERTS, LANES)))


def _route_specs(t, ts, d, tile_of):
    const = lambda *idx: (0, 0)
    in_specs = [pl.BlockSpec((1, d), const), pl.BlockSpec((N_EXPERTS, d), const),
                pl.BlockSpec((N_EXPERTS, LANES), const)]
    out_shape = (
        jax.ShapeDtypeStruct((t * ROW_TILE_SUB, LANES), F32),
        jax.ShapeDtypeStruct((TOP_K, t), I32),
        jax.ShapeDtypeStruct((TOP_K, t), F32),
        jax.ShapeDtypeStruct((TOP_K, t), I32),
        jax.ShapeDtypeStruct((N_EXPERTS, LANES), F32),
    )
    tok = lambda: pl.BlockSpec((TOP_K, ts), lambda *idx: (0, tile_of(*idx)))
    out_specs = (
        pl.BlockSpec((ts * ROW_TILE_SUB, LANES), lambda *idx: (tile_of(*idx), 0)),
        tok(), tok(), tok(),
        pl.BlockSpec((N_EXPERTS, LANES), const),
    )
    return in_specs, out_shape, out_specs


def _dispatch_kernel(pad_ref, n_used_ref, *refs, td, blk, nblk, zero_fill):
    dest_refs, h_ref, rest = refs[:TOP_K], refs[TOP_K], refs[TOP_K + 1:]
    xs_ref, zbuf, sem_z, sem = rest if zero_fill else rest[1:]
    i = pl.program_id(0)
    sub = ROW_TILE_SUB

    def zero_rows(row, n):
        return pltpu.make_async_copy(
            zbuf.at[pl.ds(0, n * sub), :],
            xs_ref.at[pl.ds(pl.multiple_of(row * sub, sub), n * sub), :], sem_z)

    def padding_copies(act):
        for e in range(N_EXPERTS):
            pos = pad_ref[e]
            length = pad_ref[N_EXPERTS + e]
            n = blk // 2
            while n >= 1:
                has = (length & n) != 0

                @pl.when(has)
                def _(pos=pos, n=n):
                    act(zero_rows(pos, n))
                pos = pos + jnp.where(has, n, 0)
                n //= 2

        def tail(b, carry):
            act(zero_rows(b * blk, blk))
            return carry
        lax.fori_loop(n_used_ref[0], nblk, tail, 0)

    if zero_fill:
        @pl.when(i == 0)
        def _():
            zbuf[...] = jnp.zeros_like(zbuf)
            padding_copies(lambda cp: cp.start())

    def issue(t, carry):
        src = h_ref.at[pl.ds(pl.multiple_of(t * sub, sub), sub), :]
        for k in range(TOP_K):
            d = dest_refs[k][0, 0, t]
            pltpu.make_async_copy(src, xs_ref.at[pl.ds(pl.multiple_of(d * sub, sub), sub), :],
                                  sem).start(priority=k % 2)
        return carry

    lax.fori_loop(0, td, issue, 0, unroll=ISSUE_UNROLL)
    for k in range(TOP_K):
        pltpu.make_async_copy(h_ref, xs_ref.at[pl.ds(0, td * sub), :], sem).wait()

    if zero_fill:
        @pl.when(i == 0)
        def _():
            padding_copies(lambda cp: cp.wait())


def _dispatch(h_tiles, dest, pad, n_used, n_rows, reuse=None):
    sub = ROW_TILE_SUB
    t = h_tiles.shape[0] // sub
    td = min(DISPATCH_TILE, t)
    nt = t // td
    dest_k = [dest[k].reshape(nt, 1, td) for k in range(TOP_K)]
    zero_fill = reuse is None
    kern = functools.partial(_dispatch_kernel, td=td, blk=MOE_BLOCK, nblk=n_rows // MOE_BLOCK,
                             zero_fill=zero_fill)
    extra_specs = [] if zero_fill else [pl.BlockSpec(memory_space=pl.ANY)]
    extra_args = [] if zero_fill else [reuse]
    return pl.pallas_call(
        kern,
        out_shape=jax.ShapeDtypeStruct((n_rows * sub, LANES), F32),
        grid_spec=pltpu.PrefetchScalarGridSpec(
            num_scalar_prefetch=2,
            grid=(nt,),
            in_specs=[pl.BlockSpec((1, 1, td), lambda i, z, nu: (i, 0, 0),
                                   memory_space=pltpu.SMEM)] * TOP_K + [
                pl.BlockSpec((td * sub, LANES), lambda i, z, nu: (i, 0)),
            ] + extra_specs,
            out_specs=pl.BlockSpec(memory_space=pl.ANY),
            scratch_shapes=[
                pltpu.VMEM((MOE_BLOCK * sub, LANES), F32),
                pltpu.SemaphoreType.DMA(()),
                pltpu.SemaphoreType.DMA(()),
            ],
        ),
        compiler_params=pltpu.CompilerParams(
            dimension_semantics=("arbitrary",), vmem_limit_bytes=VMEM_LIMIT_BYTES),
        input_output_aliases={} if zero_fill else {3 + TOP_K: 0},
        name="moe_dispatch" if zero_fill else "moe_dispatch_reuse",
    )(pad, n_used, *dest_k, h_tiles, *extra_args)


def _expert_kernel(blk_e_ref, n_used_ref, nxt_e_ref, n_valid_ref, x_ref, w1_hbm, b1_ref, w2_hbm,
                   b2_ref, o_ref, w1s_ref, w2s_ref, w1b_ref, w2b_ref, wsem, *, layer):
    step = pl.program_id(0)
    sub_rows = MOE_BLOCK * ROW_TILE_SUB

    def weight_copies(expert):
        return (pltpu.make_async_copy(w1_hbm.at[layer, expert], w1s_ref, wsem.at[0]),
                pltpu.make_async_copy(w2_hbm.at[layer, expert], w2s_ref, wsem.at[1]))

    @pl.when(step == 0)
    def _():
        for cp in weight_copies(blk_e_ref[0]):
            cp.start()

    for sub in range(EXPERT_STEP_BLOCKS):
        i = step * EXPERT_STEP_BLOCKS + sub
        e = blk_e_ref[i]
        prev = blk_e_ref[jnp.maximum(i - 1, 0)]
        fresh = (i == 0) | (e != prev)
        live = i < n_used_ref[0]

        @pl.when(live & fresh)
        def _(i=i, e=e):
            for cp in weight_copies(e):
                cp.wait()
            w1b_ref[...] = w1s_ref[...].astype(BF16)
            w2b_ref[...] = w2s_ref[...].astype(BF16)
            nxt = nxt_e_ref[i]

            @pl.when(nxt >= 0)
            def _():
                for cp in weight_copies(nxt):
                    cp.start()

        def mlp(rows, sub=sub, e=e):
            r0 = sub * MOE_BLOCK
            x = _load_row_tiles(x_ref, rows, start=r0).astype(BF16)
            a = jnp.dot(x, w1b_ref[...], preferred_element_type=F32) + b1_ref[e]
            glu = jnp.minimum(a[:, :D_FF], SWIGLU_LIMIT)
            lin = jnp.clip(a[:, D_FF:], -SWIGLU_LIMIT, SWIGLU_LIMIT)
            act = glu * jax.nn.sigmoid(SWIGLU_ALPHA * glu) * (lin + 1.0)
            y = jnp.dot(act.astype(BF16), w2b_ref[...], preferred_element_type=F32) + b2_ref[e]
            _store_row_tiles(o_ref, y, rows, start=r0)
            if rows < MOE_BLOCK:
                o_ref[(r0 + rows) * ROW_TILE_SUB:(r0 + MOE_BLOCK) * ROW_TILE_SUB, :] = jnp.zeros(
                    ((MOE_BLOCK - rows) * ROW_TILE_SUB, LANES), F32)

        half = MOE_BLOCK // 2
        short = n_valid_ref[i] <= half

        @pl.when(live & jnp.logical_not(short))
        def _(mlp=mlp):
            mlp(MOE_BLOCK)

        @pl.when(live & short)
        def _(mlp=mlp):
            mlp(half)

        @pl.when(jnp.logical_not(live) & (step * EXPERT_STEP_BLOCKS < n_used_ref[0]))
        def _(sub=sub):
            o_ref[sub * sub_rows:(sub + 1) * sub_rows, :] = jnp.zeros((sub_rows, LANES), F32)


def _experts(xs, blk_e, n_used, nxt_e, n_valid, layer, w1, b1, w2, b2):
    sub = ROW_TILE_SUB
    n_rows = xs.shape[0] // sub
    d = w2.shape[-1]
    nblk = n_rows // MOE_BLOCK
    depth = w1.shape[0]
    grp = EXPERT_STEP_BLOCKS
    assert nblk % grp == 0
    row = lambda i, be, nu, nx, nv: (jnp.minimum(i, (nu[0] - 1) // grp), 0)
    out_row = row
    bias = lambda i, be, nu, nx, nv: (layer, 0, 0, 0)
    return pl.pallas_call(
        functools.partial(_expert_kernel, layer=layer),
        out_shape=jax.ShapeDtypeStruct((n_rows * sub, LANES), F32),
        grid_spec=pltpu.PrefetchScalarGridSpec(
            num_scalar_prefetch=4,
            grid=(nblk // grp,),
            in_specs=[
                pl.BlockSpec((grp * MOE_BLOCK * sub, LANES), row),
                pl.BlockSpec(memory_space=pl.ANY),
                pl.BlockSpec((None, N_EXPERTS, 1, 2 * D_FF), bias),
                pl.BlockSpec(memory_space=pl.ANY),
                pl.BlockSpec((None, N_EXPERTS, 1, d), bias),
            ],
            out_specs=pl.BlockSpec((grp * MOE_BLOCK * sub, LANES), out_row),
            scratch_shapes=[
                pltpu.VMEM((d, 2 * D_FF), F32),
                pltpu.VMEM((D_FF, d), F32),
                pltpu.VMEM((d, 2 * D_FF), BF16),
                pltpu.VMEM((D_FF, d), BF16),
                pltpu.SemaphoreType.DMA((2,)),
            ],
        ),
        compiler_params=pltpu.CompilerParams(
            dimension_semantics=("arbitrary",), vmem_limit_bytes=VMEM_LIMIT_BYTES),
        input_output_aliases={4: 0},
        name="moe_experts",
    )(blk_e, n_used, nxt_e, n_valid, xs, w1, b1.reshape(depth, N_EXPERTS, 1, 2 * D_FF), w2,
      b2.reshape(depth, N_EXPERTS, 1, d))


def _combine_kernel(*refs, tc, n_tiles, final_norm):
    dcur_refs, dnext_refs = refs[:TOP_K], refs[TOP_K:2 * TOP_K]
    x_ref, g_ref, ys_ref, gn_ref, o_ref, buf, sem = refs[2 * TOP_K:]
    i = pl.program_id(0)
    sub = ROW_TILE_SUB

    def issue(drefs, slot):
        def body(t, carry):
            for k in range(TOP_K):
                d = drefs[k][0, 0, t]
                pltpu.make_async_copy(
                    ys_ref.at[pl.ds(pl.multiple_of(d * sub, sub), sub), :],
                    buf.at[slot, k, pl.ds(pl.multiple_of(t * sub, sub), sub), :],
                    sem.at[slot]).start(priority=k % 2)
            return carry
        lax.fori_loop(0, tc, body, 0, unroll=ISSUE_UNROLL)

    @pl.when(i == 0)
    def _():
        issue(dcur_refs, 0)

    @pl.when(i + 1 < n_tiles)
    def _():
        issue(dnext_refs, (i + 1) % 2)

    slot = i % 2
    for k in range(TOP_K):
        pltpu.make_async_copy(ys_ref.at[pl.ds(0, tc * sub), :], buf.at[slot, k],
                              sem.at[slot]).wait()
    acc = x_ref[...]
    gates = g_ref[...]
    for k in range(TOP_K):
        acc = acc + gates[:, k:k + 1] * _load_row_tiles(buf, tc, lead=(slot, k))
    if final_norm:
        acc = _rms(acc, gn_ref[...])
    o_ref[...] = acc


def _combine(x2d, gates_t, dest, ys, g_norm, final_norm):
    t, d = x2d.shape
    tc = min(COMBINE_TILE, t)
    nt = t // tc
    dest_k = [dest[k].reshape(nt, 1, tc) for k in range(TOP_K)]
    cur = pl.BlockSpec((1, 1, tc), lambda i: (i, 0, 0), memory_space=pltpu.SMEM)
    nxt = pl.BlockSpec((1, 1, tc), lambda i: (jnp.minimum(i + 1, nt - 1), 0, 0),
                       memory_space=pltpu.SMEM)
    kern = functools.partial(_combine_kernel, tc=tc, n_tiles=nt, final_norm=final_norm)
    return pl.pallas_call(
        kern,
        out_shape=jax.ShapeDtypeStruct((t, d), F32),
        grid=(nt,),
        in_specs=[cur] * TOP_K + [nxt] * TOP_K + [
            pl.BlockSpec((tc, d), lambda i: (i, 0)),
            pl.BlockSpec((tc, TOP_K), lambda i: (i, 0)),
            pl.BlockSpec(memory_space=pl.ANY),
            pl.BlockSpec((1, d), lambda i: (0, 0)),
        ],
        out_specs=pl.BlockSpec((tc, d), lambda i: (i, 0)),
        scratch_shapes=[
            pltpu.VMEM((2, TOP_K, tc * ROW_TILE_SUB, LANES), F32),
            pltpu.SemaphoreType.DMA((2,)),
        ],
        compiler_params=pltpu.CompilerParams(
            dimension_semantics=("arbitrary",), vmem_limit_bytes=VMEM_LIMIT_BYTES),
        name="moe_combine_final" if final_norm else "moe_combine",
    )(*dest_k, *dest_k, x2d, gates_t.T, ys, g_norm.reshape(1, d))


def _moe_experts(routed, layer, w1, b1, w2, b2, reuse=None):
    h, ids, gates, rank, sizes = routed
    t = ids.shape[1]
    sizes = sizes[:, 0].astype(I32)
    padded = ((sizes + MOE_BLOCK - 1) // MOE_BLOCK) * MOE_BLOCK
    pend = jnp.cumsum(padded)
    pstart = pend - padded
    onehot = ids[:, :, None] == jnp.arange(N_EXPERTS, dtype=I32)
    dest = rank + jnp.sum(jnp.where(onehot, pstart, 0), axis=-1)
    n_rows = t * TOP_K + N_EXPERTS * MOE_BLOCK
    nblk = n_rows // MOE_BLOCK
    blk_row = jnp.arange(nblk, dtype=I32) * MOE_BLOCK
    blk_e = jnp.minimum(jnp.sum((pend[None, :] <= blk_row[:, None]).astype(I32), axis=1),
                        N_EXPERTS - 1)
    n_used = (pend[-1:] // MOE_BLOCK).astype(I32)
    pad = jnp.concatenate([pstart + sizes, padded - sizes]).astype(I32)
    xs = _dispatch(h, dest, pad, n_used, n_rows, reuse)
    blk_id = jnp.arange(nblk, dtype=I32)
    later = ((blk_id[None, :] > blk_id[:, None]) & (blk_e[None, :] != blk_e[:, None])
             & (blk_id[None, :] < n_used[0]))
    nxt_blk = jnp.min(jnp.where(later, blk_id[None, :], nblk), axis=1)
    nxt_e = jnp.sum(jnp.where(blk_id[None, :] == nxt_blk[:, None], blk_e[None, :] + 1, 0),
                    axis=1).astype(I32) - 1
    own = blk_e[:, None] == jnp.arange(N_EXPERTS, dtype=I32)[None, :]
    row_end = jnp.sum(jnp.where(own, (pstart + sizes)[None, :], 0), axis=1)
    n_valid = jnp.clip(row_end - blk_row, 0, MOE_BLOCK).astype(I32)
    ys = _experts(xs, blk_e, n_used, nxt_e, n_valid, layer, w1, b1, w2, b2)
    return ys, dest


def _qkv_kernel(x_ref, g_ref, w_ref, b_ref, q_ref, k_ref, v_ref):
    h = _rms(x_ref[...], g_ref[...])
    qkv = jnp.dot(h.astype(BF16), w_ref[...], preferred_element_type=F32) + b_ref[...]
    nq = N_HEADS * HEAD_DIM
    nk = 2 * N_KV_HEADS * HEAD_DIM
    q_ref[...] = (qkv[:, :nq] * (HEAD_DIM ** -0.5)).astype(BF16)
    k_ref[...] = qkv[:, nq:nq + nk].astype(BF16)
    v_ref[...] = qkv[:, nq + nk:].astype(BF16)


def _qkv(x2d, g, w_qkv, b_qkv):
    t, d = x2d.shape
    ts = min(QKV_TILE, t)
    nq = N_HEADS * HEAD_DIM
    kv = N_KV_HEADS * HEAD_DIM

    def dup_heads(m):
        parts = []
        for hd in range(N_KV_HEADS):
            sl = m[..., hd * HEAD_DIM:(hd + 1) * HEAD_DIM]
            parts += [sl, sl]
        return jnp.concatenate(parts, axis=-1)

    w = jnp.concatenate([w_qkv[:, :nq], dup_heads(w_qkv[:, nq:nq + kv]),
                         dup_heads(w_qkv[:, nq + kv:])], axis=1).astype(BF16)
    b = jnp.concatenate([b_qkv[:nq], dup_heads(b_qkv[nq:nq + kv]),
                         dup_heads(b_qkv[nq + kv:])]).reshape(1, -1)
    n_out = w.shape[1]
    row = lambda width: pl.BlockSpec((ts, width), lambda i: (i, 0))
    return pl.pallas_call(
        _qkv_kernel,
        out_shape=(
            jax.ShapeDtypeStruct((t, nq), BF16),
            jax.ShapeDtypeStruct((t, 2 * kv), BF16),
            jax.ShapeDtypeStruct((t, 2 * kv), BF16),
        ),
        grid=(t // ts,),
        in_specs=[
            row(d),
            pl.BlockSpec((1, d), lambda i: (0, 0)),
            pl.BlockSpec((d, n_out), lambda i: (0, 0)),
            pl.BlockSpec((1, n_out), lambda i: (0, 0)),
        ],
        out_specs=(row(nq), row(2 * kv), row(2 * kv)),
        compiler_params=pltpu.CompilerParams(
            dimension_semantics=("arbitrary",), vmem_limit_bytes=VMEM_LIMIT_BYTES),
        name="attn_qkv",
    )(x2d, g.reshape(1, d), w, b)


def _attn_kernel(sinks_ref, q_ref, kp_ref, kc_ref, vp_ref, vc_ref, bias_ref, o_ref):
    n = pl.program_id(1)
    rows = SLABS_PER_GROUP * ATT_BLOCK
    keys = 2 * ATT_BLOCK
    kall = jnp.concatenate([kp_ref[...], kc_ref[...]], axis=0)
    vall = jnp.concatenate([vp_ref[...], vc_ref[...]], axis=0)
    j = lax.broadcasted_iota(I32, (keys, rows), 0)
    r = lax.broadcasted_iota(I32, (keys, rows), 1) & (ATT_BLOCK - 1)
    in_window = (j > r) & (j <= r + WINDOW)
    lane_k = lax.broadcasted_iota(I32, (keys, LANES), 1)
    halves = [lane_k < HEAD_DIM, lane_k >= HEAD_DIM]
    sub_o = lax.broadcasted_iota(I32, (LANES, rows), 0)
    zero = jnp.zeros((), BF16)
    for blk in range(ATT_STEP_BLOCKS):
        q = q_ref[blk * ATT_BLOCK:(blk + 1) * ATT_BLOCK, :]
        kcat = kall[blk * ATT_BLOCK:blk * ATT_BLOCK + keys]
        vcat = vall[blk * ATT_BLOCK:blk * ATT_BLOCK + keys]
        if blk == 0:
            mask = in_window & ((n > 0) | (j >= ATT_BLOCK))
        else:
            mask = in_window
        for g in range(N_KV_HEADS):
            s0 = g * SLABS_PER_GROUP
            qg = jnp.concatenate([q[:, (s0 + s) * LANES:(s0 + s + 1) * LANES]
                                  for s in range(SLABS_PER_GROUP)], axis=0)
            kg = kcat[:, g * LANES:(g + 1) * LANES]
            vg = vcat[:, g * LANES:(g + 1) * LANES]
            pts, invs = [], []
            for p in range(HEADS_PER_SLAB):
                kp = jnp.where(halves[p], kg, zero)
                sc = lax.dot_general(kp, qg, (((1,), (1,)), ((), ())),
                                     preferred_element_type=F32)
                sc = jnp.where(mask, sc + bias_ref[g, p], NEG_INF)
                sink = jnp.concatenate(
                    [jnp.full((1, ATT_BLOCK),
                              sinks_ref[g * GQA_GROUP + s * HEADS_PER_SLAB + p], F32)
                     for s in range(SLABS_PER_GROUP)], axis=1)
                m = jnp.maximum(jnp.max(sc, axis=0, keepdims=True), sink)
                pe = jnp.exp(sc - m)
                den = jnp.sum(pe, axis=0, keepdims=True) + jnp.exp(sink - m)
                pts.append(pe.astype(BF16))
                invs.append(1.0 / den)
            pcat = jnp.concatenate(pts, axis=0)
            vblk = jnp.concatenate([jnp.where(halves[p], vg, zero)
                                    for p in range(HEADS_PER_SLAB)], axis=0)
            og = lax.dot_general(vblk, pcat, (((0,), (0,)), ((), ())),
                                 preferred_element_type=F32)
            og = og * jnp.where(sub_o < HEAD_DIM, invs[0], invs[1])
            og = og.T
            for s in range(SLABS_PER_GROUP):
                o_ref[blk * ATT_BLOCK:(blk + 1) * ATT_BLOCK,
                      (s0 + s) * LANES:(s0 + s + 1) * LANES] = (
                    og[s * ATT_BLOCK:(s + 1) * ATT_BLOCK].astype(BF16))


def _t5_bias_table(rel_bias):
    r = np.arange(ATT_BLOCK)[:, None]
    j = np.arange(2 * ATT_BLOCK)[None, :]
    rel = np.clip(ATT_BLOCK + r - j, 0, WINDOW - 1)
    max_exact = N_BUCKETS // 2
    nf = jnp.maximum(rel, max_exact).astype(F32)
    large = max_exact + (jnp.log(nf / max_exact) / math.log(MAX_DISTANCE / max_exact)
                         * (N_BUCKETS - max_exact)).astype(I32)
    large = jnp.minimum(large, N_BUCKETS - 1)
    bucket = jnp.where(rel < max_exact, rel, large)
    pick = (bucket[None] == jnp.arange(N_BUCKETS, dtype=I32)[:, None, None])
    bias = jnp.sum(jnp.where(pick[:, None], rel_bias.astype(F32)[:, :, None, None], 0.0),
                   axis=0)
    bias = bias.reshape(N_KV_HEADS, SLABS_PER_GROUP, HEADS_PER_SLAB, ATT_BLOCK, 2 * ATT_BLOCK)
    bias = jnp.transpose(bias, (0, 2, 4, 1, 3))
    return bias.reshape(N_KV_HEADS, HEADS_PER_SLAB, 2 * ATT_BLOCK, SLABS_PER_GROUP * ATT_BLOCK)


def _attention(q, kk, vv, sinks, rel_bias, bsz, seq):
    nq = N_HEADS * HEAD_DIM
    kvw = kk.shape[-1]
    step = ATT_STEP_BLOCKS * ATT_BLOCK
    q3 = q.reshape(bsz, seq, nq)
    k3 = kk.reshape(bsz, seq, kvw)
    v3 = vv.reshape(bsz, seq, kvw)
    bias = _t5_bias_table(rel_bias)
    prev = lambda bi, n: (bi, jnp.maximum(n * ATT_STEP_BLOCKS - 1, 0), 0)
    cur = lambda bi, n: (bi, n, 0)
    out = pl.pallas_call(
        _attn_kernel,
        out_shape=jax.ShapeDtypeStruct((bsz, seq, nq), BF16),
        grid=(bsz, seq // step),
        in_specs=[
            pl.BlockSpec(memory_space=pltpu.SMEM),
            pl.BlockSpec((None, step, nq), cur),
            pl.BlockSpec((None, ATT_BLOCK, kvw), prev),
            pl.BlockSpec((None, step, kvw), cur),
            pl.BlockSpec((None, ATT_BLOCK, kvw), prev),
            pl.BlockSpec((None, step, kvw), cur),
            pl.BlockSpec(bias.shape, lambda bi, n: (0, 0, 0, 0)),
        ],
        out_specs=pl.BlockSpec((None, step, nq), cur),
        compiler_params=pltpu.CompilerParams(
            dimension_semantics=("arbitrary", "arbitrary"),
            vmem_limit_bytes=VMEM_LIMIT_BYTES),
        name="attn_core",
    )(sinks.astype(F32), q3, k3, k3, v3, v3, bias)
    return out.reshape(bsz * seq, nq)


def _oproj_kernel(x_ref, o_ref, w_ref, b_ref, gr_ref, wr_ref, br_ref,
                  y_ref, h_ref, ids_ref, gates_ref, rank_ref, sizes_ref, carry_ref, *, ts):
    x_new = x_ref[...] + jnp.dot(o_ref[...], w_ref[...],
                                 preferred_element_type=F32) + b_ref[...]
    y_ref[...] = x_new
    _route_tile(x_new, pl.program_id(0) == 0, gr_ref, wr_ref, br_ref,
                h_ref, ids_ref, gates_ref, rank_ref, sizes_ref, carry_ref, ts)


def _out_proj(x2d, o, w_o, b_o, g_ffn, w_router, b_router):
    t, d = x2d.shape
    ts = min(ROW_TILE, t)
    nq = o.shape[1]
    r_in, r_shape, r_out = _route_specs(t, ts, d, lambda i: i)
    outs = pl.pallas_call(
        functools.partial(_oproj_kernel, ts=ts),
        out_shape=(jax.ShapeDtypeStruct((t, d), F32),) + r_shape,
        grid=(t // ts,),
        in_specs=[
            pl.BlockSpec((ts, d), lambda i: (i, 0)),
            pl.BlockSpec((ts, nq), lambda i: (i, 0)),
            pl.BlockSpec((nq, d), lambda i: (0, 0)),
            pl.BlockSpec((1, d), lambda i: (0, 0)),
        ] + r_in,
        out_specs=(pl.BlockSpec((ts, d), lambda i: (i, 0)),) + r_out,
        scratch_shapes=[pltpu.VMEM((N_EXPERTS, LANES), F32)],
        compiler_params=pltpu.CompilerParams(
            dimension_semantics=("arbitrary",), vmem_limit_bytes=VMEM_LIMIT_BYTES),
        name="attn_out_proj_route",
    )(x2d, o, w_o.astype(BF16), b_o.reshape(1, d),
      *_route_operands(g_ffn, w_router, b_router))
    return outs[0], outs[1:]


def kernel(x, norm_mix, norm_ffn, norm_final, pool_w, pool_b, pool_scale, w_qkv, b_qkv,
           sinks, w_o, b_o, rel_bias, w_router, b_router, w1, b1, w2, b2):
    bsz, seq, d = x.shape
    t = bsz * seq
    x, routed = _pool_layer(x, norm_mix[0], pool_w[0], pool_b[0], pool_scale[0],
                            norm_ffn[0], w_router[0], b_router[0])
    x = x.reshape(t, d)
    ys0, dest = _moe_experts(routed, 0, w1, b1, w2, b2)
    x = _combine(x, routed[2], dest, ys0, norm_final, final_norm=False)
    q, kk, vv = _qkv(x, norm_mix[1], w_qkv[0], b_qkv[0])
    o = _attention(q, kk, vv, sinks[0], rel_bias, bsz, seq)
    x, routed = _out_proj(x, o, w_o[0], b_o[0], norm_ffn[1], w_router[1], b_router[1])
    ys, dest = _moe_experts(routed, 1, w1, b1, w2, b2, reuse=ys0)
    x = _combine(x, routed[2], dest, ys, norm_final, final_norm=True)
    return x.reshape(bsz, seq, d)
```

```python
import functools
import math

import jax
import jax.numpy as jnp
import numpy as np
from jax import lax
from jax.experimental import pallas as pl
from jax.experimental.pallas import tpu as pltpu

F32 = jnp.float32
BF16 = jnp.bfloat16
I32 = jnp.int32

LANES = 128
D_MODEL = 1024
POOL_WINDOWS = (2, 4, 8, 16)
POOL_GROUP_CH = D_MODEL // len(POOL_WINDOWS)
POOL_HALO = 16
HEAD_DIM = 64
N_HEADS = 16
N_KV_HEADS = 2
GQA_GROUP = N_HEADS // N_KV_HEADS
HEADS_PER_SLAB = LANES // HEAD_DIM
SLABS_PER_GROUP = GQA_GROUP // HEADS_PER_SLAB
ATT_BLOCK = 128
ATT_STEP_BLOCKS = 8
WINDOW = 128
N_BUCKETS = 32
MAX_DISTANCE = 128
N_EXPERTS = 32
TOP_K = 4
D_FF = D_MODEL
SWIGLU_ALPHA = 1.702
SWIGLU_LIMIT = 7.0
RMS_EPS = 1e-5
NEG_INF = -1e30

ROW_TILE = 512
MOE_BLOCK = 512
EXPERT_STEP_BLOCKS = 2
DISPATCH_TILE = 4096
COMBINE_TILE = 256
ISSUE_UNROLL = 4
VMEM_LIMIT_BYTES = 56 * 1024 * 1024


def _rms(x, g):
    return x * lax.rsqrt(jnp.mean(x * x, axis=-1, keepdims=True) + RMS_EPS) * g


ROW_TILE_SUB = D_MODEL // LANES


def _load_row_tiles(ref, n, lead=(), start=0):
    return jnp.concatenate(
        [ref[lead + (pl.ds(start * ROW_TILE_SUB + j, n, stride=ROW_TILE_SUB), slice(None))]
         for j in range(ROW_TILE_SUB)], axis=1)


def _store_row_tiles(ref, val, n, start=0):
    for j in range(ROW_TILE_SUB):
        ref[pl.ds(start * ROW_TILE_SUB + j, n, stride=ROW_TILE_SUB), :] = (
            val[:, j * LANES:(j + 1) * LANES])


POOL_PAD = 8


def _pool_kernel(x_ref, xh_ref, g_ref, w_ref, b_ref, s_ref, gr_ref, wr_ref, br_ref,
                 o_ref, h_ref, ids_ref, gates_ref, rank_ref, sizes_ref,
                 buf_ref, s1_ref, s2_ref, s3_ref, carry_ref, *, ts):
    i = pl.program_id(1)
    c = POOL_GROUP_CH
    h0 = POOL_PAD
    t0 = POOL_PAD + POOL_HALO
    r1 = t0 + ts
    x = x_ref[...]
    g = g_ref[...]
    h = _rms(x, g)
    hh = _rms(xh_ref[...], g)
    hh = jnp.where(i == 0, 0.0, hh)
    buf_ref[0:h0, :] = jnp.zeros((h0, D_MODEL), F32)
    buf_ref[h0:t0, :] = hh
    buf_ref[t0:r1, :] = h
    sum0 = h[:, 0:c] + buf_ref[t0 - 1:r1 - 1, 0:c]
    l1 = buf_ref[h0:r1, c:] + buf_ref[h0 - 1:r1 - 1, c:]
    s1_ref[0:h0, :] = jnp.zeros((h0, 3 * c), F32)
    s1_ref[h0:r1, :] = l1
    sum1 = s1_ref[t0:r1, 0:c] + s1_ref[t0 - 2:r1 - 2, 0:c]
    l2 = s1_ref[h0:r1, c:] + s1_ref[h0 - 2:r1 - 2, c:]
    s2_ref[0:h0, :] = jnp.zeros((h0, 2 * c), F32)
    s2_ref[h0:r1, :] = l2
    sum2 = s2_ref[t0:r1, 0:c] + s2_ref[t0 - 4:r1 - 4, 0:c]
    s3_ref[h0:r1, :] = s2_ref[h0:r1, c:] + s2_ref[h0 - 4:r1 - 4, c:]
    sum3 = s3_ref[t0:r1, :] + s3_ref[t0 - 8:r1 - 8, :]
    t = i * ts + lax.broadcasted_iota(I32, (ts, 1), 0)
    outs = []
    for gi, (win, acc) in enumerate(zip(POOL_WINDOWS, (sum0, sum1, sum2, sum3))):
        cnt = jnp.minimum(t + 1, win).astype(F32)
        dlt = acc / cnt - h[:, gi * c:(gi + 1) * c]
        outs.append(jnp.dot(dlt.astype(BF16), w_ref[gi], preferred_element_type=F32))
    y = jnp.concatenate(outs, axis=1)
    x_new = x + (y + b_ref[...]) * s_ref[...]
    o_ref[...] = x_new
    first = (pl.program_id(0) == 0) & (i == 0)
    _route_tile(x_new, first, gr_ref, wr_ref, br_ref,
                h_ref, ids_ref, gates_ref, rank_ref, sizes_ref, carry_ref, ts)


def _pool_layer(x, g, w, b, s, g_ffn, w_router, b_router):
    bsz, seq, d = x.shape
    ts = min(ROW_TILE, seq)
    tiles = seq // ts
    kern = functools.partial(_pool_kernel, ts=ts)
    rows = POOL_PAD + POOL_HALO + ts
    vec = lambda: pl.BlockSpec((1, d), lambda bi, i: (0, 0))
    r_in, r_shape, r_out = _route_specs(bsz * seq, ts, d, lambda bi, i: bi * tiles + i)
    outs = pl.pallas_call(
        kern,
        out_shape=(jax.ShapeDtypeStruct(x.shape, F32),) + r_shape,
        grid=(bsz, tiles),
        in_specs=[
            pl.BlockSpec((None, ts, d), lambda bi, i: (bi, i, 0)),
            pl.BlockSpec((None, POOL_HALO, d),
                         lambda bi, i: (bi, jnp.maximum(i * (ts // POOL_HALO) - 1, 0), 0)),
            vec(),
            pl.BlockSpec(w.shape, lambda bi, i: (0, 0, 0)),
            vec(),
            vec(),
        ] + r_in,
        out_specs=(pl.BlockSpec((None, ts, d), lambda bi, i: (bi, i, 0)),) + r_out,
        scratch_shapes=[pltpu.VMEM((rows, d), F32),
                        pltpu.VMEM((rows, 3 * POOL_GROUP_CH), F32),
                        pltpu.VMEM((rows, 2 * POOL_GROUP_CH), F32),
                        pltpu.VMEM((rows, POOL_GROUP_CH), F32),
                        pltpu.VMEM((N_EXPERTS, LANES), F32)],
        compiler_params=pltpu.CompilerParams(
            dimension_semantics=("arbitrary", "arbitrary"),
            vmem_limit_bytes=VMEM_LIMIT_BYTES),
        name="pool_route",
    )(x, x, g.reshape(1, d), w.astype(BF16), b.reshape(1, d), s.reshape(1, d),
      *_route_operands(g_ffn, w_router, b_router))
    return outs[0], outs[1:]


def _route_tile(x, first, g_ref, wr_ref, br_ref,
                h_ref, ids_ref, gates_ref, rank_ref, sizes_ref, carry_ref, ts):
    @pl.when(first)
    def _():
        carry_ref[...] = jnp.zeros_like(carry_ref)

    h = _rms(x, g_ref[...])
    _store_row_tiles(h_ref, h, ts)
    h_hi = h.astype(BF16)
    h_lo = (h - h_hi.astype(F32)).astype(BF16)
    w = wr_ref[...]
    w_hi = w.astype(BF16)
    w_lo = (w - w_hi.astype(F32)).astype(BF16)
    dn = (((1,), (1,)), ((), ()))
    lg = (lax.dot_general(w_hi, h_hi, dn, preferred_element_type=F32)
          + lax.dot_general(w_lo, h_hi, dn, preferred_element_type=F32)
          + lax.dot_general(w_hi, h_lo, dn, preferred_element_type=F32))
    lg = lg + br_ref[:, 0:1]

    iota_e = lax.broadcasted_iota(I32, (N_EXPERTS, ts), 0)
    vals, ids = [], []
    for _ in range(TOP_K):
        m = jnp.max(lg, axis=0, keepdims=True)
        idx = jnp.min(jnp.where(lg == m, iota_e, N_EXPERTS), axis=0, keepdims=True)
        vals.append(m)
        ids.append(idx)
        lg = jnp.where(iota_e == idx, -jnp.inf, lg)
    ex = [jnp.exp(v - vals[0]) for v in vals]
    den = ex[0] + ex[1] + ex[2] + ex[3]
    gates_ref[...] = jnp.concatenate([e / den for e in ex], axis=0)
    ids_ref[...] = jnp.concatenate(ids, axis=0)

    hot = [(iota_e == idx) for idx in ids]
    multi = (hot[0] | hot[1] | hot[2] | hot[3])
    multi_f = multi.astype(F32)
    r = lax.broadcasted_iota(I32, (ts, ts), 0)
    c = lax.broadcasted_iota(I32, (ts, ts), 1)
    upper = (r < c).astype(BF16)
    before = jnp.dot(multi_f.astype(BF16), upper, preferred_element_type=F32)
    before = before + carry_ref[:, 0:1]
    ranks = [jnp.sum(jnp.where(hk, before, 0.0), axis=0, keepdims=True) for hk in hot]
    rank_ref[...] = jnp.concatenate(ranks, axis=0).astype(I32)
    carry_ref[...] = carry_ref[...] + jnp.sum(multi_f, axis=1, keepdims=True)
    sizes_ref[...] = carry_ref[...]


def _route_operands(g, w_router, b_router):
    d = g.shape[0]
    return (g.reshape(1, d), w_router.T,
            jnp.broadcast_to(b_router.reshape(N_EXPERTS, 1), (N_EXPERTS, LANES)))


def _route_specs(t, ts, d, tile_of):
    const = lambda *idx: (0, 0)
    in_specs = [pl.BlockSpec((1, d), const), pl.BlockSpec((N_EXPERTS, d), const),
                pl.BlockSpec((N_EXPERTS, LANES), const)]
    out_shape = (
        jax.ShapeDtypeStruct((t * ROW_TILE_SUB, LANES), F32),
        jax.ShapeDtypeStruct((TOP_K, t), I32),
        jax.ShapeDtypeStruct((TOP_K, t), F32),
        jax.ShapeDtypeStruct((TOP_K, t), I32),
        jax.ShapeDtypeStruct((N_EXPERTS, LANES), F32),
    )
    tok = lambda: pl.BlockSpec((TOP_K, ts), lambda *idx: (0, tile_of(*idx)))
    out_specs = (
        pl.BlockSpec((ts * ROW_TILE_SUB, LANES), lambda *idx: (tile_of(*idx), 0)),
        tok(), tok(), tok(),
        pl.BlockSpec((N_EXPERTS, LANES), const),
    )
    return in_specs, out_shape, out_specs


def _dispatch_kernel(pad_ref, n_used_ref, *refs, td, blk, nblk, zero_fill):
    dest_refs, h_ref, rest = refs[:TOP_K], refs[TOP_K], refs[TOP_K + 1:]
    xs_ref, zbuf, sem_z, sem = rest if zero_fill else rest[1:]
    i = pl.program_id(0)
    sub = ROW_TILE_SUB

    def zero_rows(row, n):
        return pltpu.make_async_copy(
            zbuf.at[pl.ds(0, n * sub), :],
            xs_ref.at[pl.ds(pl.multiple_of(row * sub, sub), n * sub), :], sem_z)

    def padding_copies(act):
        for e in range(N_EXPERTS):
            pos = pad_ref[e]
            length = pad_ref[N_EXPERTS + e]
            n = blk // 2
            while n >= 1:
                has = (length & n) != 0

                @pl.when(has)
                def _(pos=pos, n=n):
                    act(zero_rows(pos, n))
                pos = pos + jnp.where(has, n, 0)
                n //= 2

        def tail(b, carry):
            act(zero_rows(b * blk, blk))
            return carry
        lax.fori_loop(n_used_ref[0], nblk, tail, 0)

    if zero_fill:
        @pl.when(i == 0)
        def _():
            zbuf[...] = jnp.zeros_like(zbuf)
            padding_copies(lambda cp: cp.start())

    def issue(t, carry):
        src = h_ref.at[pl.ds(pl.multiple_of(t * sub, sub), sub), :]
        for k in range(TOP_K):
            d = dest_refs[k][0, 0, t]
            pltpu.make_async_copy(src, xs_ref.at[pl.ds(pl.multiple_of(d * sub, sub), sub), :],
                                  sem).start(priority=k % 2)
        return carry

    lax.fori_loop(0, td, issue, 0, unroll=ISSUE_UNROLL)
    for k in range(TOP_K):
        pltpu.make_async_copy(h_ref, xs_ref.at[pl.ds(0, td * sub), :], sem).wait()

    if zero_fill:
        @pl.when(i == 0)
        def _():
            padding_copies(lambda cp: cp.wait())


def _dispatch(h_tiles, dest, pad, n_used, n_rows, reuse=None):
    sub = ROW_TILE_SUB
    t = h_tiles.shape[0] // sub
    td = min(DISPATCH_TILE, t)
    nt = t // td
    dest_k = [dest[k].reshape(nt, 1, td) for k in range(TOP_K)]
    zero_fill = reuse is None
    kern = functools.partial(_dispatch_kernel, td=td, blk=MOE_BLOCK, nblk=n_rows // MOE_BLOCK,
                             zero_fill=zero_fill)
    extra_specs = [] if zero_fill else [pl.BlockSpec(memory_space=pl.ANY)]
    extra_args = [] if zero_fill else [reuse]
    return pl.pallas_call(
        kern,
        out_shape=jax.ShapeDtypeStruct((n_rows * sub, LANES), F32),
        grid_spec=pltpu.PrefetchScalarGridSpec(
            num_scalar_prefetch=2,
            grid=(nt,),
            in_specs=[pl.BlockSpec((1, 1, td), lambda i, z, nu: (i, 0, 0),
                                   memory_space=pltpu.SMEM)] * TOP_K + [
                pl.BlockSpec((td * sub, LANES), lambda i, z, nu: (i, 0)),
            ] + extra_specs,
            out_specs=pl.BlockSpec(memory_space=pl.ANY),
            scratch_shapes=[
                pltpu.VMEM((MOE_BLOCK * sub, LANES), F32),
                pltpu.SemaphoreType.DMA(()),
                pltpu.SemaphoreType.DMA(()),
            ],
        ),
        compiler_params=pltpu.CompilerParams(
            dimension_semantics=("arbitrary",), vmem_limit_bytes=VMEM_LIMIT_BYTES),
        input_output_aliases={} if zero_fill else {3 + TOP_K: 0},
        name="moe_dispatch" if zero_fill else "moe_dispatch_reuse",
    )(pad, n_used, *dest_k, h_tiles, *extra_args)


def _expert_kernel(blk_e_ref, n_used_ref, nxt_e_ref, n_valid_ref, x_ref, w1_hbm, b1_ref, w2_hbm,
                   b2_ref, o_ref, w1s_ref, w2s_ref, w1b_ref, w2b_ref, wsem, *, layer):
    step = pl.program_id(0)
    sub_rows = MOE_BLOCK * ROW_TILE_SUB

    def weight_copies(expert):
        return (pltpu.make_async_copy(w1_hbm.at[layer, expert], w1s_ref, wsem.at[0]),
                pltpu.make_async_copy(w2_hbm.at[layer, expert], w2s_ref, wsem.at[1]))

    @pl.when(step == 0)
    def _():
        for cp in weight_copies(blk_e_ref[0]):
            cp.start()

    for sub in range(EXPERT_STEP_BLOCKS):
        i = step * EXPERT_STEP_BLOCKS + sub
        e = blk_e_ref[i]
        prev = blk_e_ref[jnp.maximum(i - 1, 0)]
        fresh = (i == 0) | (e != prev)
        live = i < n_used_ref[0]

        @pl.when(live & fresh)
        def _(i=i, e=e):
            for cp in weight_copies(e):
                cp.wait()
            w1b_ref[...] = w1s_ref[...].astype(BF16)
            w2b_ref[...] = w2s_ref[...].astype(BF16)
            nxt = nxt_e_ref[i]

            @pl.when(nxt >= 0)
            def _():
                for cp in weight_copies(nxt):
                    cp.start()

        def mlp(rows, sub=sub, e=e):
            r0 = sub * MOE_BLOCK
            x = _load_row_tiles(x_ref, rows, start=r0).astype(BF16)
            a = jnp.dot(x, w1b_ref[...], preferred_element_type=F32) + b1_ref[e]
            glu = jnp.minimum(a[:, :D_FF], SWIGLU_LIMIT)
            lin = jnp.clip(a[:, D_FF:], -SWIGLU_LIMIT, SWIGLU_LIMIT)
            act = glu * jax.nn.sigmoid(SWIGLU_ALPHA * glu) * (lin + 1.0)
            y = jnp.dot(act.astype(BF16), w2b_ref[...], preferred_element_type=F32) + b2_ref[e]
            _store_row_tiles(o_ref, y, rows, start=r0)
            if rows < MOE_BLOCK:
                o_ref[(r0 + rows) * ROW_TILE_SUB:(r0 + MOE_BLOCK) * ROW_TILE_SUB, :] = jnp.zeros(
                    ((MOE_BLOCK - rows) * ROW_TILE_SUB, LANES), F32)

        half = MOE_BLOCK // 2
        short = n_valid_ref[i] <= half

        @pl.when(live & jnp.logical_not(short))
        def _(mlp=mlp):
            mlp(MOE_BLOCK)

        @pl.when(live & short)
        def _(mlp=mlp):
            mlp(half)

        @pl.when(jnp.logical_not(live) & (step * EXPERT_STEP_BLOCKS < n_used_ref[0]))
        def _(sub=sub):
            o_ref[sub * sub_rows:(sub + 1) * sub_rows, :] = jnp.zeros((sub_rows, LANES), F32)


def _experts(xs, blk_e, n_used, nxt_e, n_valid, layer, w1, b1, w2, b2):
    sub = ROW_TILE_SUB
    n_rows = xs.shape[0] // sub
    d = w2.shape[-1]
    nblk = n_rows // MOE_BLOCK
    depth = w1.shape[0]
    grp = EXPERT_STEP_BLOCKS
    assert nblk % grp == 0
    row = lambda i, be, nu, nx, nv: (jnp.minimum(i, (nu[0] - 1) // grp), 0)
    out_row = row
    bias = lambda i, be, nu, nx, nv: (layer, 0, 0, 0)
    return pl.pallas_call(
        functools.partial(_expert_kernel, layer=layer),
        out_shape=jax.ShapeDtypeStruct((n_rows * sub, LANES), F32),
        grid_spec=pltpu.PrefetchScalarGridSpec(
            num_scalar_prefetch=4,
            grid=(nblk // grp,),
            in_specs=[
                pl.BlockSpec((grp * MOE_BLOCK * sub, LANES), row),
                pl.BlockSpec(memory_space=pl.ANY),
                pl.BlockSpec((None, N_EXPERTS, 1, 2 * D_FF), bias),
                pl.BlockSpec(memory_space=pl.ANY),
                pl.BlockSpec((None, N_EXPERTS, 1, d), bias),
            ],
            out_specs=pl.BlockSpec((grp * MOE_BLOCK * sub, LANES), out_row),
            scratch_shapes=[
                pltpu.VMEM((d, 2 * D_FF), F32),
                pltpu.VMEM((D_FF, d), F32),
                pltpu.VMEM((d, 2 * D_FF), BF16),
                pltpu.VMEM((D_FF, d), BF16),
                pltpu.SemaphoreType.DMA((2,)),
            ],
        ),
        compiler_params=pltpu.CompilerParams(
            dimension_semantics=("arbitrary",), vmem_limit_bytes=VMEM_LIMIT_BYTES),
        input_output_aliases={4: 0},
        name="moe_experts",
    )(blk_e, n_used, nxt_e, n_valid, xs, w1, b1.reshape(depth, N_EXPERTS, 1, 2 * D_FF), w2,
      b2.reshape(depth, N_EXPERTS, 1, d))


def _combine_kernel(*refs, tc, n_tiles, final_norm):
    dcur_refs, dnext_refs = refs[:TOP_K], refs[TOP_K:2 * TOP_K]
    x_ref, g_ref, ys_ref, gn_ref, o_ref, buf, sem = refs[2 * TOP_K:]
    i = pl.program_id(0)
    sub = ROW_TILE_SUB

    def issue(drefs, slot):
        def body(t, carry):
            for k in range(TOP_K):
                d = drefs[k][0, 0, t]
                pltpu.make_async_copy(
                    ys_ref.at[pl.ds(pl.multiple_of(d * sub, sub), sub), :],
                    buf.at[slot, k, pl.ds(pl.multiple_of(t * sub, sub), sub), :],
                    sem.at[slot]).start(priority=k % 2)
            return carry
        lax.fori_loop(0, tc, body, 0, unroll=ISSUE_UNROLL)

    @pl.when(i == 0)
    def _():
        issue(dcur_refs, 0)

    @pl.when(i + 1 < n_tiles)
    def _():
        issue(dnext_refs, (i + 1) % 2)

    slot = i % 2
    for k in range(TOP_K):
        pltpu.make_async_copy(ys_ref.at[pl.ds(0, tc * sub), :], buf.at[slot, k],
                              sem.at[slot]).wait()
    acc = x_ref[...]
    gates = g_ref[...]
    for k in range(TOP_K):
        acc = acc + gates[:, k:k + 1] * _load_row_tiles(buf, tc, lead=(slot, k))
    if final_norm:
        acc = _rms(acc, gn_ref[...])
    o_ref[...] = acc


def _combine(x2d, gates_t, dest, ys, g_norm, final_norm):
    t, d = x2d.shape
    tc = min(COMBINE_TILE, t)
    nt = t // tc
    dest_k = [dest[k].reshape(nt, 1, tc) for k in range(TOP_K)]
    cur = pl.BlockSpec((1, 1, tc), lambda i: (i, 0, 0), memory_space=pltpu.SMEM)
    nxt = pl.BlockSpec((1, 1, tc), lambda i: (jnp.minimum(i + 1, nt - 1), 0, 0),
                       memory_space=pltpu.SMEM)
    kern = functools.partial(_combine_kernel, tc=tc, n_tiles=nt, final_norm=final_norm)
    return pl.pallas_call(
        kern,
        out_shape=jax.ShapeDtypeStruct((t, d), F32),
        grid=(nt,),
        in_specs=[cur] * TOP_K + [nxt] * TOP_K + [
            pl.BlockSpec((tc, d), lambda i: (i, 0)),
            pl.BlockSpec((tc, TOP_K), lambda i: (i, 0)),
            pl.BlockSpec(memory_space=pl.ANY),
            pl.BlockSpec((1, d), lambda i: (0, 0)),
        ],
        out_specs=pl.BlockSpec((tc, d), lambda i: (i, 0)),
        scratch_shapes=[
            pltpu.VMEM((2, TOP_K, tc * ROW_TILE_SUB, LANES), F32),
            pltpu.SemaphoreType.DMA((2,)),
        ],
        compiler_params=pltpu.CompilerParams(
            dimension_semantics=("arbitrary",), vmem_limit_bytes=VMEM_LIMIT_BYTES),
        name="moe_combine_final" if final_norm else "moe_combine",
    )(*dest_k, *dest_k, x2d, gates_t.T, ys, g_norm.reshape(1, d))


def _moe_experts(routed, layer, w1, b1, w2, b2, reuse=None):
    h, ids, gates, rank, sizes = routed
    t = ids.shape[1]
    sizes = sizes[:, 0].astype(I32)
    padded = ((sizes + MOE_BLOCK - 1) // MOE_BLOCK) * MOE_BLOCK
    pend = jnp.cumsum(padded)
    pstart = pend - padded
    onehot = ids[:, :, None] == jnp.arange(N_EXPERTS, dtype=I32)
    dest = rank + jnp.sum(jnp.where(onehot, pstart, 0), axis=-1)
    n_rows = t * TOP_K + N_EXPERTS * MOE_BLOCK
    nblk = n_rows // MOE_BLOCK
    blk_row = jnp.arange(nblk, dtype=I32) * MOE_BLOCK
    blk_e = jnp.minimum(jnp.sum((pend[None, :] <= blk_row[:, None]).astype(I32), axis=1),
                        N_EXPERTS - 1)
    n_used = (pend[-1:] // MOE_BLOCK).astype(I32)
    pad = jnp.concatenate([pstart + sizes, padded - sizes]).astype(I32)
    xs = _dispatch(h, dest, pad, n_used, n_rows, reuse)
    blk_id = jnp.arange(nblk, dtype=I32)
    later = ((blk_id[None, :] > blk_id[:, None]) & (blk_e[None, :] != blk_e[:, None])
             & (blk_id[None, :] < n_used[0]))
    nxt_blk = jnp.min(jnp.where(later, blk_id[None, :], nblk), axis=1)
    nxt_e = jnp.sum(jnp.where(blk_id[None, :] == nxt_blk[:, None], blk_e[None, :] + 1, 0),
                    axis=1).astype(I32) - 1
    own = blk_e[:, None] == jnp.arange(N_EXPERTS, dtype=I32)[None, :]
    row_end = jnp.sum(jnp.where(own, (pstart + sizes)[None, :], 0), axis=1)
    n_valid = jnp.clip(row_end - blk_row, 0, MOE_BLOCK).astype(I32)
    ys = _experts(xs, blk_e, n_used, nxt_e, n_valid, layer, w1, b1, w2, b2)
    return ys, dest


def _qkv_kernel(x_ref, g_ref, w_ref, b_ref, q_ref, k_ref, v_ref):
    h = _rms(x_ref[...], g_ref[...])
    qkv = jnp.dot(h.astype(BF16), w_ref[...], preferred_element_type=F32) + b_ref[...]
    nq = N_HEADS * HEAD_DIM
    nk = 2 * N_KV_HEADS * HEAD_DIM
    q_ref[...] = (qkv[:, :nq] * (HEAD_DIM ** -0.5)).astype(BF16)
    k_ref[...] = qkv[:, nq:nq + nk].astype(BF16)
    v_ref[...] = qkv[:, nq + nk:].astype(BF16)


def _qkv(x2d, g, w_qkv, b_qkv):
    t, d = x2d.shape
    ts = min(ROW_TILE, t)
    nq = N_HEADS * HEAD_DIM
    kv = N_KV_HEADS * HEAD_DIM

    def dup_heads(m):
        parts = []
        for hd in range(N_KV_HEADS):
            sl = m[..., hd * HEAD_DIM:(hd + 1) * HEAD_DIM]
            parts += [sl, sl]
        return jnp.concatenate(parts, axis=-1)

    w = jnp.concatenate([w_qkv[:, :nq], dup_heads(w_qkv[:, nq:nq + kv]),
                         dup_heads(w_qkv[:, nq + kv:])], axis=1).astype(BF16)
    b = jnp.concatenate([b_qkv[:nq], dup_heads(b_qkv[nq:nq + kv]),
                         dup_heads(b_qkv[nq + kv:])]).reshape(1, -1)
    n_out = w.shape[1]
    row = lambda width: pl.BlockSpec((ts, width), lambda i: (i, 0))
    return pl.pallas_call(
        _qkv_kernel,
        out_shape=(
            jax.ShapeDtypeStruct((t, nq), BF16),
            jax.ShapeDtypeStruct((t, 2 * kv), BF16),
            jax.ShapeDtypeStruct((t, 2 * kv), BF16),
        ),
        grid=(t // ts,),
        in_specs=[
            row(d),
            pl.BlockSpec((1, d), lambda i: (0, 0)),
            pl.BlockSpec((d, n_out), lambda i: (0, 0)),
            pl.BlockSpec((1, n_out), lambda i: (0, 0)),
        ],
        out_specs=(row(nq), row(2 * kv), row(2 * kv)),
        compiler_params=pltpu.CompilerParams(
            dimension_semantics=("arbitrary",), vmem_limit_bytes=VMEM_LIMIT_BYTES),
        name="attn_qkv",
    )(x2d, g.reshape(1, d), w, b)


def _attn_kernel(sinks_ref, q_ref, kp_ref, kc_ref, vp_ref, vc_ref, bias_ref, o_ref):
    n = pl.program_id(1)
    rows = SLABS_PER_GROUP * ATT_BLOCK
    keys = 2 * ATT_BLOCK
    kall = jnp.concatenate([kp_ref[...], kc_ref[...]], axis=0)
    vall = jnp.concatenate([vp_ref[...], vc_ref[...]], axis=0)
    j = lax.broadcasted_iota(I32, (keys, rows), 0)
    r = lax.broadcasted_iota(I32, (keys, rows), 1) & (ATT_BLOCK - 1)
    in_window = (j > r) & (j <= r + WINDOW)
    lane_k = lax.broadcasted_iota(I32, (keys, LANES), 1)
    halves = [lane_k < HEAD_DIM, lane_k >= HEAD_DIM]
    sub_o = lax.broadcasted_iota(I32, (LANES, rows), 0)
    zero = jnp.zeros((), BF16)
    for blk in range(ATT_STEP_BLOCKS):
        q = q_ref[blk * ATT_BLOCK:(blk + 1) * ATT_BLOCK, :]
        kcat = kall[blk * ATT_BLOCK:blk * ATT_BLOCK + keys]
        vcat = vall[blk * ATT_BLOCK:blk * ATT_BLOCK + keys]
        if blk == 0:
            mask = in_window & ((n > 0) | (j >= ATT_BLOCK))
        else:
            mask = in_window
        for g in range(N_KV_HEADS):
            s0 = g * SLABS_PER_GROUP
            qg = jnp.concatenate([q[:, (s0 + s) * LANES:(s0 + s + 1) * LANES]
                                  for s in range(SLABS_PER_GROUP)], axis=0)
            kg = kcat[:, g * LANES:(g + 1) * LANES]
            vg = vcat[:, g * LANES:(g + 1) * LANES]
            pts, invs = [], []
            for p in range(HEADS_PER_SLAB):
                kp = jnp.where(halves[p], kg, zero)
                sc = lax.dot_general(kp, qg, (((1,), (1,)), ((), ())),
                                     preferred_element_type=F32)
                sc = jnp.where(mask, sc + bias_ref[g, p], NEG_INF)
                sink = jnp.concatenate(
                    [jnp.full((1, ATT_BLOCK),
                              sinks_ref[g * GQA_GROUP + s * HEADS_PER_SLAB + p], F32)
                     for s in range(SLABS_PER_GROUP)], axis=1)
                m = jnp.maximum(jnp.max(sc, axis=0, keepdims=True), sink)
                pe = jnp.exp(sc - m)
                den = jnp.sum(pe, axis=0, keepdims=True) + jnp.exp(sink - m)
                pts.append(pe.astype(BF16))
                invs.append(1.0 / den)
            pcat = jnp.concatenate(pts, axis=0)
            vblk = jnp.concatenate([jnp.where(halves[p], vg, zero)
                                    for p in range(HEADS_PER_SLAB)], axis=0)
            og = lax.dot_general(vblk, pcat, (((0,), (0,)), ((), ())),
                                 preferred_element_type=F32)
            og = og * jnp.where(sub_o < HEAD_DIM, invs[0], invs[1])
            og = og.T
            for s in range(SLABS_PER_GROUP):
                o_ref[blk * ATT_BLOCK:(blk + 1) * ATT_BLOCK,
                      (s0 + s) * LANES:(s0 + s + 1) * LANES] = (
                    og[s * ATT_BLOCK:(s + 1) * ATT_BLOCK].astype(BF16))


def _t5_bias_table(rel_bias):
    r = np.arange(ATT_BLOCK)[:, None]
    j = np.arange(2 * ATT_BLOCK)[None, :]
    rel = np.clip(ATT_BLOCK + r - j, 0, WINDOW - 1)
    max_exact = N_BUCKETS // 2
    nf = jnp.maximum(rel, max_exact).astype(F32)
    large = max_exact + (jnp.log(nf / max_exact) / math.log(MAX_DISTANCE / max_exact)
                         * (N_BUCKETS - max_exact)).astype(I32)
    large = jnp.minimum(large, N_BUCKETS - 1)
    bucket = jnp.where(rel < max_exact, rel, large)
    pick = (bucket[None] == jnp.arange(N_BUCKETS, dtype=I32)[:, None, None])
    bias = jnp.sum(jnp.where(pick[:, None], rel_bias.astype(F32)[:, :, None, None], 0.0),
                   axis=0)
    bias = bias.reshape(N_KV_HEADS, SLABS_PER_GROUP, HEADS_PER_SLAB, ATT_BLOCK, 2 * ATT_BLOCK)
    bias = jnp.transpose(bias, (0, 2, 4, 1, 3))
    return bias.reshape(N_KV_HEADS, HEADS_PER_SLAB, 2 * ATT_BLOCK, SLABS_PER_GROUP * ATT_BLOCK)


def _attention(q, kk, vv, sinks, rel_bias, bsz, seq):
    nq = N_HEADS * HEAD_DIM
    kvw = kk.shape[-1]
    step = ATT_STEP_BLOCKS * ATT_BLOCK
    q3 = q.reshape(bsz, seq, nq)
    k3 = kk.reshape(bsz, seq, kvw)
    v3 = vv.reshape(bsz, seq, kvw)
    bias = _t5_bias_table(rel_bias)
    prev = lambda bi, n: (bi, jnp.maximum(n * ATT_STEP_BLOCKS - 1, 0), 0)
    cur = lambda bi, n: (bi, n, 0)
    out = pl.pallas_call(
        _attn_kernel,
        out_shape=jax.ShapeDtypeStruct((bsz, seq, nq), BF16),
        grid=(bsz, seq // step),
        in_specs=[
            pl.BlockSpec(memory_space=pltpu.SMEM),
            pl.BlockSpec((None, step, nq), cur),
            pl.BlockSpec((None, ATT_BLOCK, kvw), prev),
            pl.BlockSpec((None, step, kvw), cur),
            pl.BlockSpec((None, ATT_BLOCK, kvw), prev),
            pl.BlockSpec((None, step, kvw), cur),
            pl.BlockSpec(bias.shape, lambda bi, n: (0, 0, 0, 0)),
        ],
        out_specs=pl.BlockSpec((None, step, nq), cur),
        compiler_params=pltpu.CompilerParams(
            dimension_semantics=("arbitrary", "arbitrary"),
            vmem_limit_bytes=VMEM_LIMIT_BYTES),
        name="attn_core",
    )(sinks.astype(F32), q3, k3, k3, v3, v3, bias)
    return out.reshape(bsz * seq, nq)


def _oproj_kernel(x_ref, o_ref, w_ref, b_ref, gr_ref, wr_ref, br_ref,
                  y_ref, h_ref, ids_ref, gates_ref, rank_ref, sizes_ref, carry_ref, *, ts):
    x_new = x_ref[...] + jnp.dot(o_ref[...], w_ref[...],
                                 preferred_element_type=F32) + b_ref[...]
    y_ref[...] = x_new
    _route_tile(x_new, pl.program_id(0) == 0, gr_ref, wr_ref, br_ref,
                h_ref, ids_ref, gates_ref, rank_ref, sizes_ref, carry_ref, ts)


def _out_proj(x2d, o, w_o, b_o, g_ffn, w_router, b_router):
    t, d = x2d.shape
    ts = min(ROW_TILE, t)
    nq = o.shape[1]
    r_in, r_shape, r_out = _route_specs(t, ts, d, lambda i: i)
    outs = pl.pallas_call(
        functools.partial(_oproj_kernel, ts=ts),
        out_shape=(jax.ShapeDtypeStruct((t, d), F32),) + r_shape,
        grid=(t // ts,),
        in_specs=[
            pl.BlockSpec((ts, d), lambda i: (i, 0)),
            pl.BlockSpec((ts, nq), lambda i: (i, 0)),
            pl.BlockSpec((nq, d), lambda i: (0, 0)),
            pl.BlockSpec((1, d), lambda i: (0, 0)),
        ] + r_in,
        out_specs=(pl.BlockSpec((ts, d), lambda i: (i, 0)),) + r_out,
        scratch_shapes=[pltpu.VMEM((N_EXPERTS, LANES), F32)],
        compiler_params=pltpu.CompilerParams(
            dimension_semantics=("arbitrary",), vmem_limit_bytes=VMEM_LIMIT_BYTES),
        name="attn_out_proj_route",
    )(x2d, o, w_o.astype(BF16), b_o.reshape(1, d),
      *_route_operands(g_ffn, w_router, b_router))
    return outs[0], outs[1:]


def kernel(x, norm_mix, norm_ffn, norm_final, pool_w, pool_b, pool_scale, w_qkv, b_qkv,
           sinks, w_o, b_o, rel_bias, w_router, b_router, w1, b1, w2, b2):
    bsz, seq, d = x.shape
    t = bsz * seq
    x, routed = _pool_layer(x, norm_mix[0], pool_w[0], pool_b[0], pool_scale[0],
                            norm_ffn[0], w_router[0], b_router[0])
    x = x.reshape(t, d)
    ys0, dest = _moe_experts(routed, 0, w1, b1, w2, b2)
    x = _combine(x, routed[2], dest, ys0, norm_final, final_norm=False)
    q, kk, vv = _qkv(x, norm_mix[1], w_qkv[0], b_qkv[0])
    o = _attention(q, kk, vv, sinks[0], rel_bias, bsz, seq)
    x, routed = _out_proj(x, o, w_o[0], b_o[0], norm_ffn[1], w_router[1], b_router[1])
    ys, dest = _moe_experts(routed, 1, w1, b1, w2, b2, reuse=ys0)
    x = _combine(x, routed[2], dest, ys, norm_final, final_norm=True)
    return x.reshape(bsz, seq, d)
```

```python
import functools
import math

import jax
import jax.numpy as jnp
import numpy as np
from jax import lax
from jax.experimental import pallas as pl
from jax.experimental.pallas import tpu as pltpu

F32 = jnp.float32
BF16 = jnp.bfloat16
I32 = jnp.int32

LANES = 128
D_MODEL = 1024
POOL_WINDOWS = (2, 4, 8, 16)
POOL_GROUP_CH = D_MODEL // len(POOL_WINDOWS)
POOL_HALO = 16
HEAD_DIM = 64
N_HEADS = 16
N_KV_HEADS = 2
GQA_GROUP = N_HEADS // N_KV_HEADS
HEADS_PER_SLAB = LANES // HEAD_DIM
SLABS_PER_GROUP = GQA_GROUP // HEADS_PER_SLAB
ATT_BLOCK = 128
ATT_STEP_BLOCKS = 4
WINDOW = 128
N_BUCKETS = 32
MAX_DISTANCE = 128
N_EXPERTS = 32
TOP_K = 4
D_FF = D_MODEL
SWIGLU_ALPHA = 1.702
SWIGLU_LIMIT = 7.0
RMS_EPS = 1e-5
NEG_INF = -1e30

ROW_TILE = 512
MOE_BLOCK = 512
EXPERT_STEP_BLOCKS = 2
DISPATCH_TILE = 4096
COMBINE_TILE = 256
ISSUE_UNROLL = 4
VMEM_LIMIT_BYTES = 56 * 1024 * 1024


def _rms(x, g):
    return x * lax.rsqrt(jnp.mean(x * x, axis=-1, keepdims=True) + RMS_EPS) * g


ROW_TILE_SUB = D_MODEL // LANES


def _load_row_tiles(ref, n, lead=(), start=0):
    return jnp.concatenate(
        [ref[lead + (pl.ds(start * ROW_TILE_SUB + j, n, stride=ROW_TILE_SUB), slice(None))]
         for j in range(ROW_TILE_SUB)], axis=1)


def _store_row_tiles(ref, val, n, start=0):
    for j in range(ROW_TILE_SUB):
        ref[pl.ds(start * ROW_TILE_SUB + j, n, stride=ROW_TILE_SUB), :] = (
            val[:, j * LANES:(j + 1) * LANES])


POOL_PAD = 8


def _pool_kernel(x_ref, xh_ref, g_ref, w_ref, b_ref, s_ref, gr_ref, wr_ref, br_ref,
                 o_ref, h_ref, ids_ref, gates_ref, rank_ref, sizes_ref,
                 buf_ref, s1_ref, s2_ref, s3_ref, carry_ref, *, ts):
    i = pl.program_id(1)
    c = POOL_GROUP_CH
    h0 = POOL_PAD
    t0 = POOL_PAD + POOL_HALO
    r1 = t0 + ts
    x = x_ref[...]
    g = g_ref[...]
    h = _rms(x, g)
    hh = _rms(xh_ref[...], g)
    hh = jnp.where(i == 0, 0.0, hh)
    buf_ref[0:h0, :] = jnp.zeros((h0, D_MODEL), F32)
    buf_ref[h0:t0, :] = hh
    buf_ref[t0:r1, :] = h
    sum0 = h[:, 0:c] + buf_ref[t0 - 1:r1 - 1, 0:c]
    l1 = buf_ref[h0:r1, c:] + buf_ref[h0 - 1:r1 - 1, c:]
    s1_ref[0:h0, :] = jnp.zeros((h0, 3 * c), F32)
    s1_ref[h0:r1, :] = l1
    sum1 = s1_ref[t0:r1, 0:c] + s1_ref[t0 - 2:r1 - 2, 0:c]
    l2 = s1_ref[h0:r1, c:] + s1_ref[h0 - 2:r1 - 2, c:]
    s2_ref[0:h0, :] = jnp.zeros((h0, 2 * c), F32)
    s2_ref[h0:r1, :] = l2
    sum2 = s2_ref[t0:r1, 0:c] + s2_ref[t0 - 4:r1 - 4, 0:c]
    s3_ref[h0:r1, :] = s2_ref[h0:r1, c:] + s2_ref[h0 - 4:r1 - 4, c:]
    sum3 = s3_ref[t0:r1, :] + s3_ref[t0 - 8:r1 - 8, :]
    t = i * ts + lax.broadcasted_iota(I32, (ts, 1), 0)
    outs = []
    for gi, (win, acc) in enumerate(zip(POOL_WINDOWS, (sum0, sum1, sum2, sum3))):
        cnt = jnp.minimum(t + 1, win).astype(F32)
        dlt = acc / cnt - h[:, gi * c:(gi + 1) * c]
        outs.append(jnp.dot(dlt.astype(BF16), w_ref[gi], preferred_element_type=F32))
    y = jnp.concatenate(outs, axis=1)
    x_new = x + (y + b_ref[...]) * s_ref[...]
    o_ref[...] = x_new
    first = (pl.program_id(0) == 0) & (i == 0)
    _route_tile(x_new, first, gr_ref, wr_ref, br_ref,
                h_ref, ids_ref, gates_ref, rank_ref, sizes_ref, carry_ref, ts)


def _pool_layer(x, g, w, b, s, g_ffn, w_router, b_router):
    bsz, seq, d = x.shape
    ts = min(ROW_TILE, seq)
    tiles = seq // ts
    kern = functools.partial(_pool_kernel, ts=ts)
    rows = POOL_PAD + POOL_HALO + ts
    vec = lambda: pl.BlockSpec((1, d), lambda bi, i: (0, 0))
    r_in, r_shape, r_out = _route_specs(bsz * seq, ts, d, lambda bi, i: bi * tiles + i)
    outs = pl.pallas_call(
        kern,
        out_shape=(jax.ShapeDtypeStruct(x.shape, F32),) + r_shape,
        grid=(bsz, tiles),
        in_specs=[
            pl.BlockSpec((None, ts, d), lambda bi, i: (bi, i, 0)),
            pl.BlockSpec((None, POOL_HALO, d),
                         lambda bi, i: (bi, jnp.maximum(i * (ts // POOL_HALO) - 1, 0), 0)),
            vec(),
            pl.BlockSpec(w.shape, lambda bi, i: (0, 0, 0)),
            vec(),
            vec(),
        ] + r_in,
        out_specs=(pl.BlockSpec((None, ts, d), lambda bi, i: (bi, i, 0)),) + r_out,
        scratch_shapes=[pltpu.VMEM((rows, d), F32),
                        pltpu.VMEM((rows, 3 * POOL_GROUP_CH), F32),
                        pltpu.VMEM((rows, 2 * POOL_GROUP_CH), F32),
                        pltpu.VMEM((rows, POOL_GROUP_CH), F32),
                        pltpu.VMEM((N_EXPERTS, LANES), F32)],
        compiler_params=pltpu.CompilerParams(
            dimension_semantics=("arbitrary", "arbitrary"),
            vmem_limit_bytes=VMEM_LIMIT_BYTES),
        name="pool_route",
    )(x, x, g.reshape(1, d), w.astype(BF16), b.reshape(1, d), s.reshape(1, d),
      *_route_operands(g_ffn, w_router, b_router))
    return outs[0], outs[1:]


def _route_tile(x, first, g_ref, wr_ref, br_ref,
                h_ref, ids_ref, gates_ref, rank_ref, sizes_ref, carry_ref, ts):
    @pl.when(first)
    def _():
        carry_ref[...] = jnp.zeros_like(carry_ref)

    h = _rms(x, g_ref[...])
    _store_row_tiles(h_ref, h, ts)
    h_hi = h.astype(BF16)
    h_lo = (h - h_hi.astype(F32)).astype(BF16)
    w = wr_ref[...]
    w_hi = w.astype(BF16)
    w_lo = (w - w_hi.astype(F32)).astype(BF16)
    dn = (((1,), (1,)), ((), ()))
    lg = (lax.dot_general(w_hi, h_hi, dn, preferred_element_type=F32)
          + lax.dot_general(w_lo, h_hi, dn, preferred_element_type=F32)
          + lax.dot_general(w_hi, h_lo, dn, preferred_element_type=F32))
    lg = lg + br_ref[:, 0:1]

    iota_e = lax.broadcasted_iota(I32, (N_EXPERTS, ts), 0)
    vals, ids = [], []
    for _ in range(TOP_K):
        m = jnp.max(lg, axis=0, keepdims=True)
        idx = jnp.min(jnp.where(lg == m, iota_e, N_EXPERTS), axis=0, keepdims=True)
        vals.append(m)
        ids.append(idx)
        lg = jnp.where(iota_e == idx, -jnp.inf, lg)
    ex = [jnp.exp(v - vals[0]) for v in vals]
    den = ex[0] + ex[1] + ex[2] + ex[3]
    gates_ref[...] = jnp.concatenate([e / den for e in ex], axis=0)
    ids_ref[...] = jnp.concatenate(ids, axis=0)

    hot = [(iota_e == idx) for idx in ids]
    multi = (hot[0] | hot[1] | hot[2] | hot[3])
    multi_f = multi.astype(F32)
    r = lax.broadcasted_iota(I32, (ts, ts), 0)
    c = lax.broadcasted_iota(I32, (ts, ts), 1)
    upper = (r < c).astype(BF16)
    before = jnp.dot(multi_f.astype(BF16), upper, preferred_element_type=F32)
    before = before + carry_ref[:, 0:1]
    ranks = [jnp.sum(jnp.where(hk, before, 0.0), axis=0, keepdims=True) for hk in hot]
    rank_ref[...] = jnp.concatenate(ranks, axis=0).astype(I32)
    carry_ref[...] = carry_ref[...] + jnp.sum(multi_f, axis=1, keepdims=True)
    sizes_ref[...] = carry_ref[...]


def _route_operands(g, w_router, b_router):
    d = g.shape[0]
    return (g.reshape(1, d), w_router.T,
            jnp.broadcast_to(b_router.reshape(N_EXPERTS, 1), (N_EXPERTS, LANES)))


def _route_specs(t, ts, d, tile_of):
    const = lambda *idx: (0, 0)
    in_specs = [pl.BlockSpec((1, d), const), pl.BlockSpec((N_EXPERTS, d), const),
                pl.BlockSpec((N_EXPERTS, LANES), const)]
    out_shape = (
        jax.ShapeDtypeStruct((t * ROW_TILE_SUB, LANES), F32),
        jax.ShapeDtypeStruct((TOP_K, t), I32),
        jax.ShapeDtypeStruct((TOP_K, t), F32),
        jax.ShapeDtypeStruct((TOP_K, t), I32),
        jax.ShapeDtypeStruct((N_EXPERTS, LANES), F32),
    )
    tok = lambda: pl.BlockSpec((TOP_K, ts), lambda *idx: (0, tile_of(*idx)))
    out_specs = (
        pl.BlockSpec((ts * ROW_TILE_SUB, LANES), lambda *idx: (tile_of(*idx), 0)),
        tok(), tok(), tok(),
        pl.BlockSpec((N_EXPERTS, LANES), const),
    )
    return in_specs, out_shape, out_specs


def _dispatch_kernel(pad_ref, n_used_ref, *refs, td, blk, nblk, zero_fill):
    dest_refs, h_ref, rest = refs[:TOP_K], refs[TOP_K], refs[TOP_K + 1:]
    xs_ref, zbuf, sem_z, sem = rest if zero_fill else rest[1:]
    i = pl.program_id(0)
    sub = ROW_TILE_SUB

    def zero_rows(row, n):
        return pltpu.make_async_copy(
            zbuf.at[pl.ds(0, n * sub), :],
            xs_ref.at[pl.ds(pl.multiple_of(row * sub, sub), n * sub), :], sem_z)

    def padding_copies(act):
        for e in range(N_EXPERTS):
            pos = pad_ref[e]
            length = pad_ref[N_EXPERTS + e]
            n = blk // 2
            while n >= 1:
                has = (length & n) != 0

                @pl.when(has)
                def _(pos=pos, n=n):
                    act(zero_rows(pos, n))
                pos = pos + jnp.where(has, n, 0)
                n //= 2

        def tail(b, carry):
            act(zero_rows(b * blk, blk))
            return carry
        lax.fori_loop(n_used_ref[0], nblk, tail, 0)

    if zero_fill:
        @pl.when(i == 0)
        def _():
            zbuf[...] = jnp.zeros_like(zbuf)
            padding_copies(lambda cp: cp.start())

    def issue(t, carry):
        src = h_ref.at[pl.ds(pl.multiple_of(t * sub, sub), sub), :]
        for k in range(TOP_K):
            d = dest_refs[k][0, 0, t]
            pltpu.make_async_copy(src, xs_ref.at[pl.ds(pl.multiple_of(d * sub, sub), sub), :],
                                  sem).start(priority=k % 2)
        return carry

    lax.fori_loop(0, td, issue, 0, unroll=ISSUE_UNROLL)
    for k in range(TOP_K):
        pltpu.make_async_copy(h_ref, xs_ref.at[pl.ds(0, td * sub), :], sem).wait()

    if zero_fill:
        @pl.when(i == 0)
        def _():
            padding_copies(lambda cp: cp.wait())


def _dispatch(h_tiles, dest, pad, n_used, n_rows, reuse=None):
    sub = ROW_TILE_SUB
    t = h_tiles.shape[0] // sub
    td = min(DISPATCH_TILE, t)
    nt = t // td
    dest_k = [dest[k].reshape(nt, 1, td) for k in range(TOP_K)]
    zero_fill = reuse is None
    kern = functools.partial(_dispatch_kernel, td=td, blk=MOE_BLOCK, nblk=n_rows // MOE_BLOCK,
                             zero_fill=zero_fill)
    extra_specs = [] if zero_fill else [pl.BlockSpec(memory_space=pl.ANY)]
    extra_args = [] if zero_fill else [reuse]
    return pl.pallas_call(
        kern,
        out_shape=jax.ShapeDtypeStruct((n_rows * sub, LANES), F32),
        grid_spec=pltpu.PrefetchScalarGridSpec(
            num_scalar_prefetch=2,
            grid=(nt,),
            in_specs=[pl.BlockSpec((1, 1, td), lambda i, z, nu: (i, 0, 0),
                                   memory_space=pltpu.SMEM)] * TOP_K + [
                pl.BlockSpec((td * sub, LANES), lambda i, z, nu: (i, 0)),
            ] + extra_specs,
            out_specs=pl.BlockSpec(memory_space=pl.ANY),
            scratch_shapes=[
                pltpu.VMEM((MOE_BLOCK * sub, LANES), F32),
                pltpu.SemaphoreType.DMA(()),
                pltpu.SemaphoreType.DMA(()),
            ],
        ),
        compiler_params=pltpu.CompilerParams(
            dimension_semantics=("arbitrary",), vmem_limit_bytes=VMEM_LIMIT_BYTES),
        input_output_aliases={} if zero_fill else {3 + TOP_K: 0},
        name="moe_dispatch" if zero_fill else "moe_dispatch_reuse",
    )(pad, n_used, *dest_k, h_tiles, *extra_args)


def _expert_kernel(blk_e_ref, n_used_ref, nxt_e_ref, n_valid_ref, x_ref, w1_hbm, b1_ref, w2_hbm,
                   b2_ref, o_ref, w1s_ref, w2s_ref, w1b_ref, w2b_ref, wsem, *, layer):
    step = pl.program_id(0)
    sub_rows = MOE_BLOCK * ROW_TILE_SUB

    def weight_copies(expert):
        return (pltpu.make_async_copy(w1_hbm.at[layer, expert], w1s_ref, wsem.at[0]),
                pltpu.make_async_copy(w2_hbm.at[layer, expert], w2s_ref, wsem.at[1]))

    @pl.when(step == 0)
    def _():
        for cp in weight_copies(blk_e_ref[0]):
            cp.start()

    for sub in range(EXPERT_STEP_BLOCKS):
        i = step * EXPERT_STEP_BLOCKS + sub
        e = blk_e_ref[i]
        prev = blk_e_ref[jnp.maximum(i - 1, 0)]
        fresh = (i == 0) | (e != prev)
        live = i < n_used_ref[0]

        @pl.when(live & fresh)
        def _(i=i, e=e):
            for cp in weight_copies(e):
                cp.wait()
            w1b_ref[...] = w1s_ref[...].astype(BF16)
            w2b_ref[...] = w2s_ref[...].astype(BF16)
            nxt = nxt_e_ref[i]

            @pl.when(nxt >= 0)
            def _():
                for cp in weight_copies(nxt):
                    cp.start()

        def mlp(rows, sub=sub, e=e):
            r0 = sub * MOE_BLOCK
            x = _load_row_tiles(x_ref, rows, start=r0).astype(BF16)
            a = jnp.dot(x, w1b_ref[...], preferred_element_type=F32) + b1_ref[e]
            glu = jnp.minimum(a[:, :D_FF], SWIGLU_LIMIT)
            lin = jnp.clip(a[:, D_FF:], -SWIGLU_LIMIT, SWIGLU_LIMIT)
            act = glu * jax.nn.sigmoid(SWIGLU_ALPHA * glu) * (lin + 1.0)
            y = jnp.dot(act.astype(BF16), w2b_ref[...], preferred_element_type=F32) + b2_ref[e]
            _store_row_tiles(o_ref, y, rows, start=r0)
            if rows < MOE_BLOCK:
                o_ref[(r0 + rows) * ROW_TILE_SUB:(r0 + MOE_BLOCK) * ROW_TILE_SUB, :] = jnp.zeros(
                    ((MOE_BLOCK - rows) * ROW_TILE_SUB, LANES), F32)

        half = MOE_BLOCK // 2
        short = n_valid_ref[i] <= half

        @pl.when(live & jnp.logical_not(short))
        def _(mlp=mlp):
            mlp(MOE_BLOCK)

        @pl.when(live & short)
        def _(mlp=mlp):
            mlp(half)

        @pl.when(jnp.logical_not(live) & (step * EXPERT_STEP_BLOCKS < n_used_ref[0]))
        def _(sub=sub):
            o_ref[sub * sub_rows:(sub + 1) * sub_rows, :] = jnp.zeros((sub_rows, LANES), F32)


def _experts(xs, blk_e, n_used, nxt_e, n_valid, layer, w1, b1, w2, b2):
    sub = ROW_TILE_SUB
    n_rows = xs.shape[0] // sub
    d = w2.shape[-1]
    nblk = n_rows // MOE_BLOCK
    depth = w1.shape[0]
    grp = EXPERT_STEP_BLOCKS
    assert nblk % grp == 0
    row = lambda i, be, nu, nx, nv: (jnp.minimum(i, (nu[0] - 1) // grp), 0)
    out_row = row
    bias = lambda i, be, nu, nx, nv: (layer, 0, 0, 0)
    return pl.pallas_call(
        functools.partial(_expert_kernel, layer=layer),
        out_shape=jax.ShapeDtypeStruct((n_rows * sub, LANES), F32),
        grid_spec=pltpu.PrefetchScalarGridSpec(
            num_scalar_prefetch=4,
            grid=(nblk // grp,),
            in_specs=[
                pl.BlockSpec((grp * MOE_BLOCK * sub, LANES), row),
                pl.BlockSpec(memory_space=pl.ANY),
                pl.BlockSpec((None, N_EXPERTS, 1, 2 * D_FF), bias),
                pl.BlockSpec(memory_space=pl.ANY),
                pl.BlockSpec((None, N_EXPERTS, 1, d), bias),
            ],
            out_specs=pl.BlockSpec((grp * MOE_BLOCK * sub, LANES), out_row),
            scratch_shapes=[
                pltpu.VMEM((d, 2 * D_FF), F32),
                pltpu.VMEM((D_FF, d), F32),
                pltpu.VMEM((d, 2 * D_FF), BF16),
                pltpu.VMEM((D_FF, d), BF16),
                pltpu.SemaphoreType.DMA((2,)),
            ],
        ),
        compiler_params=pltpu.CompilerParams(
            dimension_semantics=("arbitrary",), vmem_limit_bytes=VMEM_LIMIT_BYTES),
        input_output_aliases={4: 0},
        name="moe_experts",
    )(blk_e, n_used, nxt_e, n_valid, xs, w1, b1.reshape(depth, N_EXPERTS, 1, 2 * D_FF), w2,
      b2.reshape(depth, N_EXPERTS, 1, d))


def _combine_kernel(*refs, tc, n_tiles, final_norm):
    dcur_refs, dnext_refs = refs[:TOP_K], refs[TOP_K:2 * TOP_K]
    x_ref, g_ref, ys_ref, gn_ref, o_ref, buf, sem = refs[2 * TOP_K:]
    i = pl.program_id(0)
    sub = ROW_TILE_SUB

    def issue(drefs, slot):
        def body(t, carry):
            for k in range(TOP_K):
                d = drefs[k][0, 0, t]
                pltpu.make_async_copy(
                    ys_ref.at[pl.ds(pl.multiple_of(d * sub, sub), sub), :],
                    buf.at[slot, k, pl.ds(pl.multiple_of(t * sub, sub), sub), :],
                    sem.at[slot]).start(priority=k % 2)
            return carry
        lax.fori_loop(0, tc, body, 0, unroll=ISSUE_UNROLL)

    @pl.when(i == 0)
    def _():
        issue(dcur_refs, 0)

    @pl.when(i + 1 < n_tiles)
    def _():
        issue(dnext_refs, (i + 1) % 2)

    slot = i % 2
    for k in range(TOP_K):
        pltpu.make_async_copy(ys_ref.at[pl.ds(0, tc * sub), :], buf.at[slot, k],
                              sem.at[slot]).wait()
    acc = x_ref[...]
    gates = g_ref[...]
    for k in range(TOP_K):
        acc = acc + gates[:, k:k + 1] * _load_row_tiles(buf, tc, lead=(slot, k))
    if final_norm:
        acc = _rms(acc, gn_ref[...])
    o_ref[...] = acc


def _combine(x2d, gates_t, dest, ys, g_norm, final_norm):
    t, d = x2d.shape
    tc = min(COMBINE_TILE, t)
    nt = t // tc
    dest_k = [dest[k].reshape(nt, 1, tc) for k in range(TOP_K)]
    cur = pl.BlockSpec((1, 1, tc), lambda i: (i, 0, 0), memory_space=pltpu.SMEM)
    nxt = pl.BlockSpec((1, 1, tc), lambda i: (jnp.minimum(i + 1, nt - 1), 0, 0),
                       memory_space=pltpu.SMEM)
    kern = functools.partial(_combine_kernel, tc=tc, n_tiles=nt, final_norm=final_norm)
    return pl.pallas_call(
        kern,
        out_shape=jax.ShapeDtypeStruct((t, d), F32),
        grid=(nt,),
        in_specs=[cur] * TOP_K + [nxt] * TOP_K + [
            pl.BlockSpec((tc, d), lambda i: (i, 0)),
            pl.BlockSpec((tc, TOP_K), lambda i: (i, 0)),
            pl.BlockSpec(memory_space=pl.ANY),
            pl.BlockSpec((1, d), lambda i: (0, 0)),
        ],
        out_specs=pl.BlockSpec((tc, d), lambda i: (i, 0)),
        scratch_shapes=[
            pltpu.VMEM((2, TOP_K, tc * ROW_TILE_SUB, LANES), F32),
            pltpu.SemaphoreType.DMA((2,)),
        ],
        compiler_params=pltpu.CompilerParams(
            dimension_semantics=("arbitrary",), vmem_limit_bytes=VMEM_LIMIT_BYTES),
        name="moe_combine_final" if final_norm else "moe_combine",
    )(*dest_k, *dest_k, x2d, gates_t.T, ys, g_norm.reshape(1, d))


def _moe_experts(routed, layer, w1, b1, w2, b2, reuse=None):
    h, ids, gates, rank, sizes = routed
    t = ids.shape[1]
    sizes = sizes[:, 0].astype(I32)
    padded = ((sizes + MOE_BLOCK - 1) // MOE_BLOCK) * MOE_BLOCK
    pend = jnp.cumsum(padded)
    pstart = pend - padded
    onehot = ids[:, :, None] == jnp.arange(N_EXPERTS, dtype=I32)
    dest = rank + jnp.sum(jnp.where(onehot, pstart, 0), axis=-1)
    n_rows = t * TOP_K + N_EXPERTS * MOE_BLOCK
    nblk = n_rows // MOE_BLOCK
    blk_row = jnp.arange(nblk, dtype=I32) * MOE_BLOCK
    blk_e = jnp.minimum(jnp.sum((pend[None, :] <= blk_row[:, None]).astype(I32), axis=1),
                        N_EXPERTS - 1)
    n_used = (pend[-1:] // MOE_BLOCK).astype(I32)
    pad = jnp.concatenate([pstart + sizes, padded - sizes]).astype(I32)
    xs = _dispatch(h, dest, pad, n_used, n_rows, reuse)
    blk_id = jnp.arange(nblk, dtype=I32)
    later = ((blk_id[None, :] > blk_id[:, None]) & (blk_e[None, :] != blk_e[:, None])
             & (blk_id[None, :] < n_used[0]))
    nxt_blk = jnp.min(jnp.where(later, blk_id[None, :], nblk), axis=1)
    nxt_e = jnp.sum(jnp.where(blk_id[None, :] == nxt_blk[:, None], blk_e[None, :] + 1, 0),
                    axis=1).astype(I32) - 1
    own = blk_e[:, None] == jnp.arange(N_EXPERTS, dtype=I32)[None, :]
    row_end = jnp.sum(jnp.where(own, (pstart + sizes)[None, :], 0), axis=1)
    n_valid = jnp.clip(row_end - blk_row, 0, MOE_BLOCK).astype(I32)
    ys = _experts(xs, blk_e, n_used, nxt_e, n_valid, layer, w1, b1, w2, b2)
    return ys, dest


def _qkv_kernel(x_ref, g_ref, w_ref, b_ref, q_ref, k_ref, v_ref):
    h = _rms(x_ref[...], g_ref[...])
    qkv = jnp.dot(h.astype(BF16), w_ref[...], preferred_element_type=F32) + b_ref[...]
    nq = N_HEADS * HEAD_DIM
    nk = 2 * N_KV_HEADS * HEAD_DIM
    q_ref[...] = (qkv[:, :nq] * (HEAD_DIM ** -0.5)).astype(BF16)
    k_ref[...] = qkv[:, nq:nq + nk].astype(BF16)
    v_ref[...] = qkv[:, nq + nk:].astype(BF16)


def _qkv(x2d, g, w_qkv, b_qkv):
    t, d = x2d.shape
    ts = min(ROW_TILE, t)
    nq = N_HEADS * HEAD_DIM
    kv = N_KV_HEADS * HEAD_DIM

    def dup_heads(m):
        parts = []
        for hd in range(N_KV_HEADS):
            sl = m[..., hd * HEAD_DIM:(hd + 1) * HEAD_DIM]
            parts += [sl, sl]
        return jnp.concatenate(parts, axis=-1)

    w = jnp.concatenate([w_qkv[:, :nq], dup_heads(w_qkv[:, nq:nq + kv]),
                         dup_heads(w_qkv[:, nq + kv:])], axis=1).astype(BF16)
    b = jnp.concatenate([b_qkv[:nq], dup_heads(b_qkv[nq:nq + kv]),
                         dup_heads(b_qkv[nq + kv:])]).reshape(1, -1)
    n_out = w.shape[1]
    row = lambda width: pl.BlockSpec((ts, width), lambda i: (i, 0))
    return pl.pallas_call(
        _qkv_kernel,
        out_shape=(
            jax.ShapeDtypeStruct((t, nq), BF16),
            jax.ShapeDtypeStruct((t, 2 * kv), BF16),
            jax.ShapeDtypeStruct((t, 2 * kv), BF16),
        ),
        grid=(t // ts,),
        in_specs=[
            row(d),
            pl.BlockSpec((1, d), lambda i: (0, 0)),
            pl.BlockSpec((d, n_out), lambda i: (0, 0)),
            pl.BlockSpec((1, n_out), lambda i: (0, 0)),
        ],
        out_specs=(row(nq), row(2 * kv), row(2 * kv)),
        compiler_params=pltpu.CompilerParams(
            dimension_semantics=("arbitrary",), vmem_limit_bytes=VMEM_LIMIT_BYTES),
        name="attn_qkv",
    )(x2d, g.reshape(1, d), w, b)


def _attn_kernel(sinks_ref, q_ref, kp_ref, kc_ref, vp_ref, vc_ref, bias_ref,
                 x_ref, wo_ref, bo_ref, gr_ref, wr_ref, br_ref,
                 y_ref, h_ref, ids_ref, gates_ref, rank_ref, sizes_ref, o_ref, carry_ref, *, ts):
    n = pl.program_id(1)
    rows = SLABS_PER_GROUP * ATT_BLOCK
    keys = 2 * ATT_BLOCK
    kall = jnp.concatenate([kp_ref[...], kc_ref[...]], axis=0)
    vall = jnp.concatenate([vp_ref[...], vc_ref[...]], axis=0)
    j = lax.broadcasted_iota(I32, (keys, rows), 0)
    r = lax.broadcasted_iota(I32, (keys, rows), 1) & (ATT_BLOCK - 1)
    in_window = (j > r) & (j <= r + WINDOW)
    lane_k = lax.broadcasted_iota(I32, (keys, LANES), 1)
    halves = [lane_k < HEAD_DIM, lane_k >= HEAD_DIM]
    sub_o = lax.broadcasted_iota(I32, (LANES, rows), 0)
    zero = jnp.zeros((), BF16)
    for blk in range(ATT_STEP_BLOCKS):
        q = q_ref[blk * ATT_BLOCK:(blk + 1) * ATT_BLOCK, :]
        kcat = kall[blk * ATT_BLOCK:blk * ATT_BLOCK + keys]
        vcat = vall[blk * ATT_BLOCK:blk * ATT_BLOCK + keys]
        if blk == 0:
            mask = in_window & ((n > 0) | (j >= ATT_BLOCK))
        else:
            mask = in_window
        for g in range(N_KV_HEADS):
            s0 = g * SLABS_PER_GROUP
            qg = jnp.concatenate([q[:, (s0 + s) * LANES:(s0 + s + 1) * LANES]
                                  for s in range(SLABS_PER_GROUP)], axis=0)
            kg = kcat[:, g * LANES:(g + 1) * LANES]
            vg = vcat[:, g * LANES:(g + 1) * LANES]
            pts, invs = [], []
            for p in range(HEADS_PER_SLAB):
                kp = jnp.where(halves[p], kg, zero)
                sc = lax.dot_general(kp, qg, (((1,), (1,)), ((), ())),
                                     preferred_element_type=F32)
                sc = jnp.where(mask, sc + bias_ref[g, p], NEG_INF)
                sink = jnp.concatenate(
                    [jnp.full((1, ATT_BLOCK),
                              sinks_ref[g * GQA_GROUP + s * HEADS_PER_SLAB + p], F32)
                     for s in range(SLABS_PER_GROUP)], axis=1)
                m = jnp.maximum(jnp.max(sc, axis=0, keepdims=True), sink)
                pe = jnp.exp(sc - m)
                den = jnp.sum(pe, axis=0, keepdims=True) + jnp.exp(sink - m)
                pts.append(pe.astype(BF16))
                invs.append(1.0 / den)
            pcat = jnp.concatenate(pts, axis=0)
            vblk = jnp.concatenate([jnp.where(halves[p], vg, zero)
                                    for p in range(HEADS_PER_SLAB)], axis=0)
            og = lax.dot_general(vblk, pcat, (((0,), (0,)), ((), ())),
                                 preferred_element_type=F32)
            og = og * jnp.where(sub_o < HEAD_DIM, invs[0], invs[1])
            og = og.T
            for s in range(SLABS_PER_GROUP):
                o_ref[blk * ATT_BLOCK:(blk + 1) * ATT_BLOCK,
                      (s0 + s) * LANES:(s0 + s + 1) * LANES] = (
                    og[s * ATT_BLOCK:(s + 1) * ATT_BLOCK].astype(BF16))
    x_new = x_ref[...] + jnp.dot(o_ref[...], wo_ref[...],
                                 preferred_element_type=F32) + bo_ref[...]
    y_ref[...] = x_new
    first = (pl.program_id(0) == 0) & (n == 0)
    _route_tile(x_new, first, gr_ref, wr_ref, br_ref,
                h_ref, ids_ref, gates_ref, rank_ref, sizes_ref, carry_ref, ts)


def _t5_bias_table(rel_bias):
    r = np.arange(ATT_BLOCK)[:, None]
    j = np.arange(2 * ATT_BLOCK)[None, :]
    rel = np.clip(ATT_BLOCK + r - j, 0, WINDOW - 1)
    max_exact = N_BUCKETS // 2
    nf = jnp.maximum(rel, max_exact).astype(F32)
    large = max_exact + (jnp.log(nf / max_exact) / math.log(MAX_DISTANCE / max_exact)
                         * (N_BUCKETS - max_exact)).astype(I32)
    large = jnp.minimum(large, N_BUCKETS - 1)
    bucket = jnp.where(rel < max_exact, rel, large)
    pick = (bucket[None] == jnp.arange(N_BUCKETS, dtype=I32)[:, None, None])
    bias = jnp.sum(jnp.where(pick[:, None], rel_bias.astype(F32)[:, :, None, None], 0.0),
                   axis=0)
    bias = bias.reshape(N_KV_HEADS, SLABS_PER_GROUP, HEADS_PER_SLAB, ATT_BLOCK, 2 * ATT_BLOCK)
    bias = jnp.transpose(bias, (0, 2, 4, 1, 3))
    return bias.reshape(N_KV_HEADS, HEADS_PER_SLAB, 2 * ATT_BLOCK, SLABS_PER_GROUP * ATT_BLOCK)


def _attention(x2d, q, kk, vv, sinks, rel_bias, w_o, b_o, g_ffn, w_router, b_router, bsz, seq):
    nq = N_HEADS * HEAD_DIM
    kvw = kk.shape[-1]
    d = x2d.shape[1]
    step = ATT_STEP_BLOCKS * ATT_BLOCK
    tiles = seq // step
    q3 = q.reshape(bsz, seq, nq)
    k3 = kk.reshape(bsz, seq, kvw)
    v3 = vv.reshape(bsz, seq, kvw)
    bias = _t5_bias_table(rel_bias)
    prev = lambda bi, n: (bi, jnp.maximum(n * ATT_STEP_BLOCKS - 1, 0), 0)
    cur = lambda bi, n: (bi, n, 0)
    const = lambda bi, n: (0, 0)
    r_in, r_shape, r_out = _route_specs(bsz * seq, step, d, lambda bi, n: bi * tiles + n)
    outs = pl.pallas_call(
        functools.partial(_attn_kernel, ts=step),
        out_shape=(jax.ShapeDtypeStruct((bsz, seq, d), F32),) + r_shape,
        grid=(bsz, tiles),
        in_specs=[
            pl.BlockSpec(memory_space=pltpu.SMEM),
            pl.BlockSpec((None, step, nq), cur),
            pl.BlockSpec((None, ATT_BLOCK, kvw), prev),
            pl.BlockSpec((None, step, kvw), cur),
            pl.BlockSpec((None, ATT_BLOCK, kvw), prev),
            pl.BlockSpec((None, step, kvw), cur),
            pl.BlockSpec(bias.shape, lambda bi, n: (0, 0, 0, 0)),
            pl.BlockSpec((None, step, d), cur),
            pl.BlockSpec((nq, d), const),
            pl.BlockSpec((1, d), const),
        ] + r_in,
        out_specs=(pl.BlockSpec((None, step, d), cur),) + r_out,
        scratch_shapes=[pltpu.VMEM((step, nq), BF16),
                        pltpu.VMEM((N_EXPERTS, LANES), F32)],
        compiler_params=pltpu.CompilerParams(
            dimension_semantics=("arbitrary", "arbitrary"),
            vmem_limit_bytes=VMEM_LIMIT_BYTES),
        name="attn_oproj_route",
    )(sinks.astype(F32), q3, k3, k3, v3, v3, bias, x2d.reshape(bsz, seq, d),
      w_o.astype(BF16), b_o.reshape(1, d), *_route_operands(g_ffn, w_router, b_router))
    return outs[0].reshape(bsz * seq, d), outs[1:]


def kernel(x, norm_mix, norm_ffn, norm_final, pool_w, pool_b, pool_scale, w_qkv, b_qkv,
           sinks, w_o, b_o, rel_bias, w_router, b_router, w1, b1, w2, b2):
    bsz, seq, d = x.shape
    t = bsz * seq
    x, routed = _pool_layer(x, norm_mix[0], pool_w[0], pool_b[0], pool_scale[0],
                            norm_ffn[0], w_router[0], b_router[0])
    x = x.reshape(t, d)
    ys0, dest = _moe_experts(routed, 0, w1, b1, w2, b2)
    x = _combine(x, routed[2], dest, ys0, norm_final, final_norm=False)
    q, kk, vv = _qkv(x, norm_mix[1], w_qkv[0], b_qkv[0])
    x, routed = _attention(x, q, kk, vv, sinks[0], rel_bias, w_o[0], b_o[0],
                           norm_ffn[1], w_router[1], b_router[1], bsz, seq)
    ys, dest = _moe_experts(routed, 1, w1, b1, w2, b2, reuse=ys0)
    x = _combine(x, routed[2], dest, ys, norm_final, final_norm=True)
    return x.reshape(bsz, seq, d)
```

```python
import functools
import math

import jax
import jax.numpy as jnp
import numpy as np
from jax import lax
from jax.experimental import pallas as pl
from jax.experimental.pallas import tpu as pltpu

F32 = jnp.float32
BF16 = jnp.bfloat16
I32 = jnp.int32

LANES = 128
D_MODEL = 1024
POOL_WINDOWS = (2, 4, 8, 16)
POOL_GROUP_CH = D_MODEL // len(POOL_WINDOWS)
POOL_HALO = 16
HEAD_DIM = 64
N_HEADS = 16
N_KV_HEADS = 2
GQA_GROUP = N_HEADS // N_KV_HEADS
HEADS_PER_SLAB = LANES // HEAD_DIM
SLABS_PER_GROUP = GQA_GROUP // HEADS_PER_SLAB
ATT_BLOCK = 128
ATT_STEP_BLOCKS = 4
WINDOW = 128
N_BUCKETS = 32
MAX_DISTANCE = 128
N_EXPERTS = 32
TOP_K = 4
D_FF = D_MODEL
SWIGLU_ALPHA = 1.702
SWIGLU_LIMIT = 7.0
RMS_EPS = 1e-5
NEG_INF = -1e30

ROW_TILE = 512
MOE_BLOCK = 512
EXPERT_STEP_BLOCKS = 2
DISPATCH_TILE = 4096
COMBINE_TILE = 256
ISSUE_UNROLL = 4
VMEM_LIMIT_BYTES = 56 * 1024 * 1024


def _rms(x, g):
    return x * lax.rsqrt(jnp.mean(x * x, axis=-1, keepdims=True) + RMS_EPS) * g


ROW_TILE_SUB = D_MODEL // LANES


def _load_row_tiles(ref, n, lead=(), start=0):
    return jnp.concatenate(
        [ref[lead + (pl.ds(start * ROW_TILE_SUB + j, n, stride=ROW_TILE_SUB), slice(None))]
         for j in range(ROW_TILE_SUB)], axis=1)


def _store_row_tiles(ref, val, n, start=0):
    for j in range(ROW_TILE_SUB):
        ref[pl.ds(start * ROW_TILE_SUB + j, n, stride=ROW_TILE_SUB), :] = (
            val[:, j * LANES:(j + 1) * LANES])


POOL_PAD = 8


def _pool_kernel(x_ref, xh_ref, g_ref, w_ref, b_ref, s_ref, gr_ref, wr_ref, br_ref,
                 o_ref, h_ref, ids_ref, gates_ref, rank_ref, sizes_ref,
                 buf_ref, s1_ref, s2_ref, s3_ref, carry_ref, *, ts):
    i = pl.program_id(1)
    c = POOL_GROUP_CH
    h0 = POOL_PAD
    t0 = POOL_PAD + POOL_HALO
    r1 = t0 + ts
    x = x_ref[...]
    g = g_ref[...]
    h = _rms(x, g)
    hh = _rms(xh_ref[...], g)
    hh = jnp.where(i == 0, 0.0, hh)
    buf_ref[0:h0, :] = jnp.zeros((h0, D_MODEL), F32)
    buf_ref[h0:t0, :] = hh
    buf_ref[t0:r1, :] = h
    sum0 = h[:, 0:c] + buf_ref[t0 - 1:r1 - 1, 0:c]
    l1 = buf_ref[h0:r1, c:] + buf_ref[h0 - 1:r1 - 1, c:]
    s1_ref[0:h0, :] = jnp.zeros((h0, 3 * c), F32)
    s1_ref[h0:r1, :] = l1
    sum1 = s1_ref[t0:r1, 0:c] + s1_ref[t0 - 2:r1 - 2, 0:c]
    l2 = s1_ref[h0:r1, c:] + s1_ref[h0 - 2:r1 - 2, c:]
    s2_ref[0:h0, :] = jnp.zeros((h0, 2 * c), F32)
    s2_ref[h0:r1, :] = l2
    sum2 = s2_ref[t0:r1, 0:c] + s2_ref[t0 - 4:r1 - 4, 0:c]
    s3_ref[h0:r1, :] = s2_ref[h0:r1, c:] + s2_ref[h0 - 4:r1 - 4, c:]
    sum3 = s3_ref[t0:r1, :] + s3_ref[t0 - 8:r1 - 8, :]
    t = i * ts + lax.broadcasted_iota(I32, (ts, 1), 0)
    outs = []
    for gi, (win, acc) in enumerate(zip(POOL_WINDOWS, (sum0, sum1, sum2, sum3))):
        cnt = jnp.minimum(t + 1, win).astype(F32)
        dlt = acc / cnt - h[:, gi * c:(gi + 1) * c]
        outs.append(jnp.dot(dlt.astype(BF16), w_ref[gi], preferred_element_type=F32))
    y = jnp.concatenate(outs, axis=1)
    x_new = x + (y + b_ref[...]) * s_ref[...]
    o_ref[...] = x_new
    first = (pl.program_id(0) == 0) & (i == 0)
    _route_tile(x_new, first, gr_ref, wr_ref, br_ref,
                h_ref, ids_ref, gates_ref, rank_ref, sizes_ref, carry_ref, ts)


def _pool_layer(x, g, w, b, s, g_ffn, w_router, b_router):
    bsz, seq, d = x.shape
    ts = min(ROW_TILE, seq)
    tiles = seq // ts
    kern = functools.partial(_pool_kernel, ts=ts)
    rows = POOL_PAD + POOL_HALO + ts
    vec = lambda: pl.BlockSpec((1, d), lambda bi, i: (0, 0))
    r_in, r_shape, r_out = _route_specs(bsz * seq, ts, d, lambda bi, i: bi * tiles + i)
    outs = pl.pallas_call(
        kern,
        out_shape=(jax.ShapeDtypeStruct(x.shape, F32),) + r_shape,
        grid=(bsz, tiles),
        in_specs=[
            pl.BlockSpec((None, ts, d), lambda bi, i: (bi, i, 0)),
            pl.BlockSpec((None, POOL_HALO, d),
                         lambda bi, i: (bi, jnp.maximum(i * (ts // POOL_HALO) - 1, 0), 0)),
            vec(),
            pl.BlockSpec(w.shape, lambda bi, i: (0, 0, 0)),
            vec(),
            vec(),
        ] + r_in,
        out_specs=(pl.BlockSpec((None, ts, d), lambda bi, i: (bi, i, 0)),) + r_out,
        scratch_shapes=[pltpu.VMEM((rows, d), F32),
                        pltpu.VMEM((rows, 3 * POOL_GROUP_CH), F32),
                        pltpu.VMEM((rows, 2 * POOL_GROUP_CH), F32),
                        pltpu.VMEM((rows, POOL_GROUP_CH), F32),
                        pltpu.VMEM((N_EXPERTS, LANES), F32)],
        compiler_params=pltpu.CompilerParams(
            dimension_semantics=("arbitrary", "arbitrary"),
            vmem_limit_bytes=VMEM_LIMIT_BYTES),
        name="pool_route",
    )(x, x, g.reshape(1, d), w.astype(BF16), b.reshape(1, d), s.reshape(1, d),
      *_route_operands(g_ffn, w_router, b_router))
    return outs[0], outs[1:]


def _route_tile(x, first, g_ref, wr_ref, br_ref,
                h_ref, ids_ref, gates_ref, rank_ref, sizes_ref, carry_ref, ts):
    @pl.when(first)
    def _():
        carry_ref[...] = jnp.zeros_like(carry_ref)

    h = _rms(x, g_ref[...])
    _store_row_tiles(h_ref, h, ts)
    h_hi = h.astype(BF16)
    h_lo = (h - h_hi.astype(F32)).astype(BF16)
    w = wr_ref[...]
    w_hi = w.astype(BF16)
    w_lo = (w - w_hi.astype(F32)).astype(BF16)
    dn = (((1,), (1,)), ((), ()))
    lg = (lax.dot_general(w_hi, h_hi, dn, preferred_element_type=F32)
          + lax.dot_general(w_lo, h_hi, dn, preferred_element_type=F32)
          + lax.dot_general(w_hi, h_lo, dn, preferred_element_type=F32))
    lg = lg + br_ref[:, 0:1]

    iota_e = lax.broadcasted_iota(I32, (N_EXPERTS, ts), 0)
    vals, ids = [], []
    for _ in range(TOP_K):
        m = jnp.max(lg, axis=0, keepdims=True)
        idx = jnp.min(jnp.where(lg == m, iota_e, N_EXPERTS), axis=0, keepdims=True)
        vals.append(m)
        ids.append(idx)
        lg = jnp.where(iota_e == idx, -jnp.inf, lg)
    ex = [jnp.exp(v - vals[0]) for v in vals]
    den = ex[0] + ex[1] + ex[2] + ex[3]
    gates_ref[...] = jnp.concatenate([e / den for e in ex], axis=0)
    ids_ref[...] = jnp.concatenate(ids, axis=0)

    hot = [(iota_e == idx) for idx in ids]
    multi = (hot[0] | hot[1] | hot[2] | hot[3])
    multi_f = multi.astype(F32)
    r = lax.broadcasted_iota(I32, (ts, ts), 0)
    c = lax.broadcasted_iota(I32, (ts, ts), 1)
    upper = (r < c).astype(BF16)
    before = jnp.dot(multi_f.astype(BF16), upper, preferred_element_type=F32)
    before = before + carry_ref[:, 0:1]
    ranks = [jnp.sum(jnp.where(hk, before, 0.0), axis=0, keepdims=True) for hk in hot]
    rank_ref[...] = jnp.concatenate(ranks, axis=0).astype(I32)
    carry_ref[...] = carry_ref[...] + jnp.sum(multi_f, axis=1, keepdims=True)
    sizes_ref[...] = carry_ref[...]


def _route_operands(g, w_router, b_router):
    d = g.shape[0]
    return (g.reshape(1, d), w_router.T,
            jnp.broadcast_to(b_router.reshape(N_EXPERTS, 1), (N_EXPERTS, LANES)))


def _route_specs(t, ts, d, tile_of):
    const = lambda *idx: (0, 0)
    in_specs = [pl.BlockSpec((1, d), const), pl.BlockSpec((N_EXPERTS, d), const),
                pl.BlockSpec((N_EXPERTS, LANES), const)]
    out_shape = (
        jax.ShapeDtypeStruct((t * ROW_TILE_SUB, LANES), F32),
        jax.ShapeDtypeStruct((TOP_K, t), I32),
        jax.ShapeDtypeStruct((TOP_K, t), F32),
        jax.ShapeDtypeStruct((TOP_K, t), I32),
        jax.ShapeDtypeStruct((N_EXPERTS, LANES), F32),
    )
    tok = lambda: pl.BlockSpec((TOP_K, ts), lambda *idx: (0, tile_of(*idx)))
    out_specs = (
        pl.BlockSpec((ts * ROW_TILE_SUB, LANES), lambda *idx: (tile_of(*idx), 0)),
        tok(), tok(), tok(),
        pl.BlockSpec((N_EXPERTS, LANES), const),
    )
    return in_specs, out_shape, out_specs


def _dispatch_kernel(pad_ref, n_used_ref, *refs, td, blk, nblk, zero_fill):
    dest_refs, h_ref, rest = refs[:TOP_K], refs[TOP_K], refs[TOP_K + 1:]
    xs_ref, zbuf, sem_z, sem = rest if zero_fill else rest[1:]
    i = pl.program_id(0)
    sub = ROW_TILE_SUB

    def zero_rows(row, n):
        return pltpu.make_async_copy(
            zbuf.at[pl.ds(0, n * sub), :],
            xs_ref.at[pl.ds(pl.multiple_of(row * sub, sub), n * sub), :], sem_z)

    def padding_copies(act):
        for e in range(N_EXPERTS):
            pos = pad_ref[e]
            length = pad_ref[N_EXPERTS + e]
            n = blk // 2
            while n >= 1:
                has = (length & n) != 0

                @pl.when(has)
                def _(pos=pos, n=n):
                    act(zero_rows(pos, n))
                pos = pos + jnp.where(has, n, 0)
                n //= 2

        def tail(b, carry):
            act(zero_rows(b * blk, blk))
            return carry
        lax.fori_loop(n_used_ref[0], nblk, tail, 0)

    if zero_fill:
        @pl.when(i == 0)
        def _():
            zbuf[...] = jnp.zeros_like(zbuf)
            padding_copies(lambda cp: cp.start())

    def issue(t, carry):
        src = h_ref.at[pl.ds(pl.multiple_of(t * sub, sub), sub), :]
        for k in range(TOP_K):
            d = dest_refs[k][0, 0, t]
            pltpu.make_async_copy(src, xs_ref.at[pl.ds(pl.multiple_of(d * sub, sub), sub), :],
                                  sem).start(priority=k % 2)
        return carry

    lax.fori_loop(0, td, issue, 0, unroll=ISSUE_UNROLL)
    for k in range(TOP_K):
        pltpu.make_async_copy(h_ref, xs_ref.at[pl.ds(0, td * sub), :], sem).wait()

    if zero_fill:
        @pl.when(i == 0)
        def _():
            padding_copies(lambda cp: cp.wait())


def _dispatch(h_tiles, dest, pad, n_used, n_rows, reuse=None):
    sub = ROW_TILE_SUB
    t = h_tiles.shape[0] // sub
    td = min(DISPATCH_TILE, t)
    nt = t // td
    dest_k = [dest[k].reshape(nt, 1, td) for k in range(TOP_K)]
    zero_fill = reuse is None
    kern = functools.partial(_dispatch_kernel, td=td, blk=MOE_BLOCK, nblk=n_rows // MOE_BLOCK,
                             zero_fill=zero_fill)
    extra_specs = [] if zero_fill else [pl.BlockSpec(memory_space=pl.ANY)]
    extra_args = [] if zero_fill else [reuse]
    return pl.pallas_call(
        kern,
        out_shape=jax.ShapeDtypeStruct((n_rows * sub, LANES), F32),
        grid_spec=pltpu.PrefetchScalarGridSpec(
            num_scalar_prefetch=2,
            grid=(nt,),
            in_specs=[pl.BlockSpec((1, 1, td), lambda i, z, nu: (i, 0, 0),
                                   memory_space=pltpu.SMEM)] * TOP_K + [
                pl.BlockSpec((td * sub, LANES), lambda i, z, nu: (i, 0)),
            ] + extra_specs,
            out_specs=pl.BlockSpec(memory_space=pl.ANY),
            scratch_shapes=[
                pltpu.VMEM((MOE_BLOCK * sub, LANES), F32),
                pltpu.SemaphoreType.DMA(()),
                pltpu.SemaphoreType.DMA(()),
            ],
        ),
        compiler_params=pltpu.CompilerParams(
            dimension_semantics=("arbitrary",), vmem_limit_bytes=VMEM_LIMIT_BYTES),
        input_output_aliases={} if zero_fill else {3 + TOP_K: 0},
        name="moe_dispatch" if zero_fill else "moe_dispatch_reuse",
    )(pad, n_used, *dest_k, h_tiles, *extra_args)


def _expert_kernel(blk_e_ref, n_used_ref, nxt_e_ref, n_valid_ref, x_ref, w1_hbm, b1_ref, w2_hbm,
                   b2_ref, o_ref, w1s_ref, w2s_ref, w1b_ref, w2b_ref, wsem, *, layer):
    step = pl.program_id(0)
    sub_rows = MOE_BLOCK * ROW_TILE_SUB

    def weight_copies(expert):
        return (pltpu.make_async_copy(w1_hbm.at[layer, expert], w1s_ref, wsem.at[0]),
                pltpu.make_async_copy(w2_hbm.at[layer, expert], w2s_ref, wsem.at[1]))

    @pl.when(step == 0)
    def _():
        for cp in weight_copies(blk_e_ref[0]):
            cp.start()

    for sub in range(EXPERT_STEP_BLOCKS):
        i = step * EXPERT_STEP_BLOCKS + sub
        e = blk_e_ref[i]
        prev = blk_e_ref[jnp.maximum(i - 1, 0)]
        fresh = (i == 0) | (e != prev)
        live = i < n_used_ref[0]

        @pl.when(live & fresh)
        def _(i=i, e=e):
            for cp in weight_copies(e):
                cp.wait()
            w1b_ref[...] = w1s_ref[...].astype(BF16)
            w2b_ref[...] = w2s_ref[...].astype(BF16)
            nxt = nxt_e_ref[i]

            @pl.when(nxt >= 0)
            def _():
                for cp in weight_copies(nxt):
                    cp.start()

        def mlp(rows, sub=sub, e=e):
            r0 = sub * MOE_BLOCK
            x = _load_row_tiles(x_ref, rows, start=r0).astype(BF16)
            a = jnp.dot(x, w1b_ref[...], preferred_element_type=F32) + b1_ref[e]
            glu = jnp.minimum(a[:, :D_FF], SWIGLU_LIMIT)
            lin = jnp.clip(a[:, D_FF:], -SWIGLU_LIMIT, SWIGLU_LIMIT)
            act = glu * jax.nn.sigmoid(SWIGLU_ALPHA * glu) * (lin + 1.0)
            y = jnp.dot(act.astype(BF16), w2b_ref[...], preferred_element_type=F32) + b2_ref[e]
            _store_row_tiles(o_ref, y, rows, start=r0)
            if rows < MOE_BLOCK:
                o_ref[(r0 + rows) * ROW_TILE_SUB:(r0 + MOE_BLOCK) * ROW_TILE_SUB, :] = jnp.zeros(
                    ((MOE_BLOCK - rows) * ROW_TILE_SUB, LANES), F32)

        half = MOE_BLOCK // 2
        short = n_valid_ref[i] <= half

        @pl.when(live & jnp.logical_not(short))
        def _(mlp=mlp):
            mlp(MOE_BLOCK)

        @pl.when(live & short)
        def _(mlp=mlp):
            mlp(half)

        @pl.when(jnp.logical_not(live) & (step * EXPERT_STEP_BLOCKS < n_used_ref[0]))
        def _(sub=sub):
            o_ref[sub * sub_rows:(sub + 1) * sub_rows, :] = jnp.zeros((sub_rows, LANES), F32)


def _experts(xs, blk_e, n_used, nxt_e, n_valid, layer, w1, b1, w2, b2):
    sub = ROW_TILE_SUB
    n_rows = xs.shape[0] // sub
    d = w2.shape[-1]
    nblk = n_rows // MOE_BLOCK
    depth = w1.shape[0]
    grp = EXPERT_STEP_BLOCKS
    assert nblk % grp == 0
    row = lambda i, be, nu, nx, nv: (jnp.minimum(i, (nu[0] - 1) // grp), 0)
    out_row = row
    bias = lambda i, be, nu, nx, nv: (layer, 0, 0, 0)
    return pl.pallas_call(
        functools.partial(_expert_kernel, layer=layer),
        out_shape=jax.ShapeDtypeStruct((n_rows * sub, LANES), F32),
        grid_spec=pltpu.PrefetchScalarGridSpec(
            num_scalar_prefetch=4,
            grid=(nblk // grp,),
            in_specs=[
                pl.BlockSpec((grp * MOE_BLOCK * sub, LANES), row),
                pl.BlockSpec(memory_space=pl.ANY),
                pl.BlockSpec((None, N_EXPERTS, 1, 2 * D_FF), bias),
                pl.BlockSpec(memory_space=pl.ANY),
                pl.BlockSpec((None, N_EXPERTS, 1, d), bias),
            ],
            out_specs=pl.BlockSpec((grp * MOE_BLOCK * sub, LANES), out_row),
            scratch_shapes=[
                pltpu.VMEM((d, 2 * D_FF), F32),
                pltpu.VMEM((D_FF, d), F32),
                pltpu.VMEM((d, 2 * D_FF), BF16),
                pltpu.VMEM((D_FF, d), BF16),
                pltpu.SemaphoreType.DMA((2,)),
            ],
        ),
        compiler_params=pltpu.CompilerParams(
            dimension_semantics=("arbitrary",), vmem_limit_bytes=VMEM_LIMIT_BYTES),
        input_output_aliases={4: 0},
        name="moe_experts",
    )(blk_e, n_used, nxt_e, n_valid, xs, w1, b1.reshape(depth, N_EXPERTS, 1, 2 * D_FF), w2,
      b2.reshape(depth, N_EXPERTS, 1, d))


def _combine_kernel(*refs, tc, n_tiles, final_norm):
    dcur_refs, dnext_refs = refs[:TOP_K], refs[TOP_K:2 * TOP_K]
    x_ref, g_ref, ys_ref, gn_ref, o_ref, buf, sem = refs[2 * TOP_K:]
    i = pl.program_id(0)
    sub = ROW_TILE_SUB

    def issue(drefs, slot):
        def body(t, carry):
            for k in range(TOP_K):
                d = drefs[k][0, 0, t]
                pltpu.make_async_copy(
                    ys_ref.at[pl.ds(pl.multiple_of(d * sub, sub), sub), :],
                    buf.at[slot, k, pl.ds(pl.multiple_of(t * sub, sub), sub), :],
                    sem.at[slot]).start(priority=k % 2)
            return carry
        lax.fori_loop(0, tc, body, 0, unroll=ISSUE_UNROLL)

    @pl.when(i == 0)
    def _():
        issue(dcur_refs, 0)

    @pl.when(i + 1 < n_tiles)
    def _():
        issue(dnext_refs, (i + 1) % 2)

    slot = i % 2
    for k in range(TOP_K):
        pltpu.make_async_copy(ys_ref.at[pl.ds(0, tc * sub), :], buf.at[slot, k],
                              sem.at[slot]).wait()
    acc = x_ref[...]
    gates = g_ref[...]
    for k in range(TOP_K):
        acc = acc + gates[:, k:k + 1] * _load_row_tiles(buf, tc, lead=(slot, k))
    if final_norm:
        acc = _rms(acc, gn_ref[...])
    o_ref[...] = acc


def _combine(x2d, gates_t, dest, ys, g_norm, final_norm):
    t, d = x2d.shape
    tc = min(COMBINE_TILE, t)
    nt = t // tc
    dest_k = [dest[k].reshape(nt, 1, tc) for k in range(TOP_K)]
    cur = pl.BlockSpec((1, 1, tc), lambda i: (i, 0, 0), memory_space=pltpu.SMEM)
    nxt = pl.BlockSpec((1, 1, tc), lambda i: (jnp.minimum(i + 1, nt - 1), 0, 0),
                       memory_space=pltpu.SMEM)
    kern = functools.partial(_combine_kernel, tc=tc, n_tiles=nt, final_norm=final_norm)
    return pl.pallas_call(
        kern,
        out_shape=jax.ShapeDtypeStruct((t, d), F32),
        grid=(nt,),
        in_specs=[cur] * TOP_K + [nxt] * TOP_K + [
            pl.BlockSpec((tc, d), lambda i: (i, 0)),
            pl.BlockSpec((tc, TOP_K), lambda i: (i, 0)),
            pl.BlockSpec(memory_space=pl.ANY),
            pl.BlockSpec((1, d), lambda i: (0, 0)),
        ],
        out_specs=pl.BlockSpec((tc, d), lambda i: (i, 0)),
        scratch_shapes=[
            pltpu.VMEM((2, TOP_K, tc * ROW_TILE_SUB, LANES), F32),
            pltpu.SemaphoreType.DMA((2,)),
        ],
        compiler_params=pltpu.CompilerParams(
            dimension_semantics=("arbitrary",), vmem_limit_bytes=VMEM_LIMIT_BYTES),
        name="moe_combine_final" if final_norm else "moe_combine",
    )(*dest_k, *dest_k, x2d, gates_t.T, ys, g_norm.reshape(1, d))


def _moe_experts(routed, layer, w1, b1, w2, b2, reuse=None):
    h, ids, gates, rank, sizes = routed
    t = ids.shape[1]
    sizes = sizes[:, 0].astype(I32)
    padded = ((sizes + MOE_BLOCK - 1) // MOE_BLOCK) * MOE_BLOCK
    pend = jnp.cumsum(padded)
    pstart = pend - padded
    onehot = ids[:, :, None] == jnp.arange(N_EXPERTS, dtype=I32)
    dest = rank + jnp.sum(jnp.where(onehot, pstart, 0), axis=-1)
    n_rows = t * TOP_K + N_EXPERTS * MOE_BLOCK
    nblk = n_rows // MOE_BLOCK
    blk_row = jnp.arange(nblk, dtype=I32) * MOE_BLOCK
    blk_e = jnp.minimum(jnp.sum((pend[None, :] <= blk_row[:, None]).astype(I32), axis=1),
                        N_EXPERTS - 1)
    n_used = (pend[-1:] // MOE_BLOCK).astype(I32)
    pad = jnp.concatenate([pstart + sizes, padded - sizes]).astype(I32)
    xs = _dispatch(h, dest, pad, n_used, n_rows, reuse)
    blk_id = jnp.arange(nblk, dtype=I32)
    later = ((blk_id[None, :] > blk_id[:, None]) & (blk_e[None, :] != blk_e[:, None])
             & (blk_id[None, :] < n_used[0]))
    nxt_blk = jnp.min(jnp.where(later, blk_id[None, :], nblk), axis=1)
    nxt_e = jnp.sum(jnp.where(blk_id[None, :] == nxt_blk[:, None], blk_e[None, :] + 1, 0),
                    axis=1).astype(I32) - 1
    own = blk_e[:, None] == jnp.arange(N_EXPERTS, dtype=I32)[None, :]
    row_end = jnp.sum(jnp.where(own, (pstart + sizes)[None, :], 0), axis=1)
    n_valid = jnp.clip(row_end - blk_row, 0, MOE_BLOCK).astype(I32)
    ys = _experts(xs, blk_e, n_used, nxt_e, n_valid, layer, w1, b1, w2, b2)
    return ys, dest


def _qkv_operands(w_qkv, b_qkv):
    nq = N_HEADS * HEAD_DIM
    kv = N_KV_HEADS * HEAD_DIM

    def dup_heads(m):
        parts = []
        for hd in range(N_KV_HEADS):
            sl = m[..., hd * HEAD_DIM:(hd + 1) * HEAD_DIM]
            parts += [sl, sl]
        return jnp.concatenate(parts, axis=-1)

    w = jnp.concatenate([w_qkv[:, :nq], dup_heads(w_qkv[:, nq:nq + kv]),
                         dup_heads(w_qkv[:, nq + kv:])], axis=1).astype(BF16)
    b = jnp.concatenate([b_qkv[:nq], dup_heads(b_qkv[nq:nq + kv]),
                         dup_heads(b_qkv[nq + kv:])]).reshape(1, -1)
    return w, b


def _attn_kernel(sinks_ref, xp_ref, gq_ref, wq_ref, bq_ref, bias_ref,
                 x_ref, wo_ref, bo_ref, gr_ref, wr_ref, br_ref,
                 y_ref, h_ref, ids_ref, gates_ref, rank_ref, sizes_ref, o_ref, carry_ref, *, ts):
    n = pl.program_id(1)
    rows = SLABS_PER_GROUP * ATT_BLOCK
    keys = 2 * ATT_BLOCK
    nq = N_HEADS * HEAD_DIM
    nk = 2 * N_KV_HEADS * HEAD_DIM
    x = x_ref[...]
    qkv = jnp.dot(_rms(x, gq_ref[...]).astype(BF16), wq_ref[...],
                  preferred_element_type=F32) + bq_ref[...]
    q_all = (qkv[:, :nq] * (HEAD_DIM ** -0.5)).astype(BF16)
    kvp = jnp.dot(_rms(xp_ref[...], gq_ref[...]).astype(BF16), wq_ref[:, nq:],
                  preferred_element_type=F32) + bq_ref[:, nq:]
    kall = jnp.concatenate([kvp[:, :nk], qkv[:, nq:nq + nk]], axis=0).astype(BF16)
    vall = jnp.concatenate([kvp[:, nk:], qkv[:, nq + nk:]], axis=0).astype(BF16)
    j = lax.broadcasted_iota(I32, (keys, rows), 0)
    r = lax.broadcasted_iota(I32, (keys, rows), 1) & (ATT_BLOCK - 1)
    in_window = (j > r) & (j <= r + WINDOW)
    lane_k = lax.broadcasted_iota(I32, (keys, LANES), 1)
    halves = [lane_k < HEAD_DIM, lane_k >= HEAD_DIM]
    sub_o = lax.broadcasted_iota(I32, (LANES, rows), 0)
    zero = jnp.zeros((), BF16)
    for blk in range(ATT_STEP_BLOCKS):
        q = q_all[blk * ATT_BLOCK:(blk + 1) * ATT_BLOCK, :]
        kcat = kall[blk * ATT_BLOCK:blk * ATT_BLOCK + keys]
        vcat = vall[blk * ATT_BLOCK:blk * ATT_BLOCK + keys]
        if blk == 0:
            mask = in_window & ((n > 0) | (j >= ATT_BLOCK))
        else:
            mask = in_window
        for g in range(N_KV_HEADS):
            s0 = g * SLABS_PER_GROUP
            qg = jnp.concatenate([q[:, (s0 + s) * LANES:(s0 + s + 1) * LANES]
                                  for s in range(SLABS_PER_GROUP)], axis=0)
            kg = kcat[:, g * LANES:(g + 1) * LANES]
            vg = vcat[:, g * LANES:(g + 1) * LANES]
            pts, invs = [], []
            for p in range(HEADS_PER_SLAB):
                kp = jnp.where(halves[p], kg, zero)
                sc = lax.dot_general(kp, qg, (((1,), (1,)), ((), ())),
                                     preferred_element_type=F32)
                sc = jnp.where(mask, sc + bias_ref[g, p], NEG_INF)
                sink = jnp.concatenate(
                    [jnp.full((1, ATT_BLOCK),
                              sinks_ref[g * GQA_GROUP + s * HEADS_PER_SLAB + p], F32)
                     for s in range(SLABS_PER_GROUP)], axis=1)
                m = jnp.maximum(jnp.max(sc, axis=0, keepdims=True), sink)
                pe = jnp.exp(sc - m)
                den = jnp.sum(pe, axis=0, keepdims=True) + jnp.exp(sink - m)
                pts.append(pe.astype(BF16))
                invs.append(1.0 / den)
            pcat = jnp.concatenate(pts, axis=0)
            vblk = jnp.concatenate([jnp.where(halves[p], vg, zero)
                                    for p in range(HEADS_PER_SLAB)], axis=0)
            og = lax.dot_general(vblk, pcat, (((0,), (0,)), ((), ())),
                                 preferred_element_type=F32)
            og = og * jnp.where(sub_o < HEAD_DIM, invs[0], invs[1])
            og = og.T
            for s in range(SLABS_PER_GROUP):
                o_ref[blk * ATT_BLOCK:(blk + 1) * ATT_BLOCK,
                      (s0 + s) * LANES:(s0 + s + 1) * LANES] = (
                    og[s * ATT_BLOCK:(s + 1) * ATT_BLOCK].astype(BF16))
    x_new = x + jnp.dot(o_ref[...], wo_ref[...], preferred_element_type=F32) + bo_ref[...]
    y_ref[...] = x_new
    first = (pl.program_id(0) == 0) & (n == 0)
    _route_tile(x_new, first, gr_ref, wr_ref, br_ref,
                h_ref, ids_ref, gates_ref, rank_ref, sizes_ref, carry_ref, ts)


def _t5_bias_table(rel_bias):
    r = np.arange(ATT_BLOCK)[:, None]
    j = np.arange(2 * ATT_BLOCK)[None, :]
    rel = np.clip(ATT_BLOCK + r - j, 0, WINDOW - 1)
    max_exact = N_BUCKETS // 2
    nf = jnp.maximum(rel, max_exact).astype(F32)
    large = max_exact + (jnp.log(nf / max_exact) / math.log(MAX_DISTANCE / max_exact)
                         * (N_BUCKETS - max_exact)).astype(I32)
    large = jnp.minimum(large, N_BUCKETS - 1)
    bucket = jnp.where(rel < max_exact, rel, large)
    pick = (bucket[None] == jnp.arange(N_BUCKETS, dtype=I32)[:, None, None])
    bias = jnp.sum(jnp.where(pick[:, None], rel_bias.astype(F32)[:, :, None, None], 0.0),
                   axis=0)
    bias = bias.reshape(N_KV_HEADS, SLABS_PER_GROUP, HEADS_PER_SLAB, ATT_BLOCK, 2 * ATT_BLOCK)
    bias = jnp.transpose(bias, (0, 2, 4, 1, 3))
    return bias.reshape(N_KV_HEADS, HEADS_PER_SLAB, 2 * ATT_BLOCK, SLABS_PER_GROUP * ATT_BLOCK)


def _attention(x2d, g_mix, w_qkv, b_qkv, sinks, rel_bias, w_o, b_o, g_ffn, w_router, b_router,
               bsz, seq):
    nq = N_HEADS * HEAD_DIM
    d = x2d.shape[1]
    step = ATT_STEP_BLOCKS * ATT_BLOCK
    tiles = seq // step
    x3 = x2d.reshape(bsz, seq, d)
    wq, bq = _qkv_operands(w_qkv, b_qkv)
    n_out = wq.shape[1]
    bias = _t5_bias_table(rel_bias)
    prev = lambda bi, n: (bi, jnp.maximum(n * ATT_STEP_BLOCKS - 1, 0), 0)
    cur = lambda bi, n: (bi, n, 0)
    const = lambda bi, n: (0, 0)
    r_in, r_shape, r_out = _route_specs(bsz * seq, step, d, lambda bi, n: bi * tiles + n)
    outs = pl.pallas_call(
        functools.partial(_attn_kernel, ts=step),
        out_shape=(jax.ShapeDtypeStruct((bsz, seq, d), F32),) + r_shape,
        grid=(bsz, tiles),
        in_specs=[
            pl.BlockSpec(memory_space=pltpu.SMEM),
            pl.BlockSpec((None, ATT_BLOCK, d), prev),
            pl.BlockSpec((1, d), const),
            pl.BlockSpec((d, n_out), const),
            pl.BlockSpec((1, n_out), const),
            pl.BlockSpec(bias.shape, lambda bi, n: (0, 0, 0, 0)),
            pl.BlockSpec((None, step, d), cur),
            pl.BlockSpec((nq, d), const),
            pl.BlockSpec((1, d), const),
        ] + r_in,
        out_specs=(pl.BlockSpec((None, step, d), cur),) + r_out,
        scratch_shapes=[pltpu.VMEM((step, nq), BF16),
                        pltpu.VMEM((N_EXPERTS, LANES), F32)],
        compiler_params=pltpu.CompilerParams(
            dimension_semantics=("arbitrary", "arbitrary"),
            vmem_limit_bytes=VMEM_LIMIT_BYTES),
        name="qkv_attn_oproj_route",
    )(sinks.astype(F32), x3, g_mix.reshape(1, d), wq, bq, bias, x3,
      w_o.astype(BF16), b_o.reshape(1, d), *_route_operands(g_ffn, w_router, b_router))
    return outs[0].reshape(bsz * seq, d), outs[1:]


def kernel(x, norm_mix, norm_ffn, norm_final, pool_w, pool_b, pool_scale, w_qkv, b_qkv,
           sinks, w_o, b_o, rel_bias, w_router, b_router, w1, b1, w2, b2):
    bsz, seq, d = x.shape
    t = bsz * seq
    x, routed = _pool_layer(x, norm_mix[0], pool_w[0], pool_b[0], pool_scale[0],
                            norm_ffn[0], w_router[0], b_router[0])
    x = x.reshape(t, d)
    ys0, dest = _moe_experts(routed, 0, w1, b1, w2, b2)
    x = _combine(x, routed[2], dest, ys0, norm_final, final_norm=False)
    x, routed = _attention(x, norm_mix[1], w_qkv[0], b_qkv[0], sinks[0], rel_bias, w_o[0], b_o[0],
                           norm_ffn[1], w_router[1], b_router[1], bsz, seq)
    ys, dest = _moe_experts(routed, 1, w1, b1, w2, b2, reuse=ys0)
    x = _combine(x, routed[2], dest, ys, norm_final, final_norm=True)
    return x.reshape(bsz, seq, d)
```

```python
import functools
import math

import jax
import jax.numpy as jnp
import numpy as np
from jax import lax
from jax.experimental import pallas as pl
from jax.experimental.pallas import tpu as pltpu

F32 = jnp.float32
BF16 = jnp.bfloat16
I32 = jnp.int32

LANES = 128
D_MODEL = 1024
POOL_WINDOWS = (2, 4, 8, 16)
POOL_GROUP_CH = D_MODEL // len(POOL_WINDOWS)
POOL_HALO = 16
HEAD_DIM = 64
N_HEADS = 16
N_KV_HEADS = 2
GQA_GROUP = N_HEADS // N_KV_HEADS
HEADS_PER_SLAB = LANES // HEAD_DIM
SLABS_PER_GROUP = GQA_GROUP // HEADS_PER_SLAB
ATT_BLOCK = 128
ATT_STEP_BLOCKS = 4
WINDOW = 128
N_BUCKETS = 32
MAX_DISTANCE = 128
N_EXPERTS = 32
TOP_K = 4
D_FF = D_MODEL
SWIGLU_ALPHA = 1.702
SWIGLU_LIMIT = 7.0
RMS_EPS = 1e-5
NEG_INF = -1e30

ROW_TILE = 1024
MOE_BLOCK = 512
EXPERT_STEP_BLOCKS = 2
DISPATCH_TILE = 4096
COMBINE_TILE = 256
ISSUE_UNROLL = 4
VMEM_LIMIT_BYTES = 56 * 1024 * 1024


def _rms(x, g):
    return x * lax.rsqrt(jnp.mean(x * x, axis=-1, keepdims=True) + RMS_EPS) * g


ROW_TILE_SUB = D_MODEL // LANES


def _load_row_tiles(ref, n, lead=(), start=0):
    return jnp.concatenate(
        [ref[lead + (pl.ds(start * ROW_TILE_SUB + j, n, stride=ROW_TILE_SUB), slice(None))]
         for j in range(ROW_TILE_SUB)], axis=1)


def _store_row_tiles(ref, val, n, start=0):
    for j in range(ROW_TILE_SUB):
        ref[pl.ds(start * ROW_TILE_SUB + j, n, stride=ROW_TILE_SUB), :] = (
            val[:, j * LANES:(j + 1) * LANES])


POOL_PAD = 8


def _pool_kernel(x_ref, xh_ref, g_ref, w_ref, b_ref, s_ref, gr_ref, wr_ref, br_ref,
                 o_ref, h_ref, ids_ref, gates_ref, rank_ref, sizes_ref,
                 buf_ref, s1_ref, s2_ref, s3_ref, carry_ref, *, ts):
    i = pl.program_id(1)
    c = POOL_GROUP_CH
    h0 = POOL_PAD
    t0 = POOL_PAD + POOL_HALO
    r1 = t0 + ts
    x = x_ref[...]
    g = g_ref[...]
    h = _rms(x, g)
    hh = _rms(xh_ref[...], g)
    hh = jnp.where(i == 0, 0.0, hh)
    buf_ref[0:h0, :] = jnp.zeros((h0, D_MODEL), F32)
    buf_ref[h0:t0, :] = hh
    buf_ref[t0:r1, :] = h
    sum0 = h[:, 0:c] + buf_ref[t0 - 1:r1 - 1, 0:c]
    l1 = buf_ref[h0:r1, c:] + buf_ref[h0 - 1:r1 - 1, c:]
    s1_ref[0:h0, :] = jnp.zeros((h0, 3 * c), F32)
    s1_ref[h0:r1, :] = l1
    sum1 = s1_ref[t0:r1, 0:c] + s1_ref[t0 - 2:r1 - 2, 0:c]
    l2 = s1_ref[h0:r1, c:] + s1_ref[h0 - 2:r1 - 2, c:]
    s2_ref[0:h0, :] = jnp.zeros((h0, 2 * c), F32)
    s2_ref[h0:r1, :] = l2
    sum2 = s2_ref[t0:r1, 0:c] + s2_ref[t0 - 4:r1 - 4, 0:c]
    s3_ref[h0:r1, :] = s2_ref[h0:r1, c:] + s2_ref[h0 - 4:r1 - 4, c:]
    sum3 = s3_ref[t0:r1, :] + s3_ref[t0 - 8:r1 - 8, :]
    t = i * ts + lax.broadcasted_iota(I32, (ts, 1), 0)
    outs = []
    for gi, (win, acc) in enumerate(zip(POOL_WINDOWS, (sum0, sum1, sum2, sum3))):
        cnt = jnp.minimum(t + 1, win).astype(F32)
        dlt = acc / cnt - h[:, gi * c:(gi + 1) * c]
        outs.append(jnp.dot(dlt.astype(BF16), w_ref[gi], preferred_element_type=F32))
    y = jnp.concatenate(outs, axis=1)
    x_new = x + (y + b_ref[...]) * s_ref[...]
    o_ref[...] = x_new
    first = (pl.program_id(0) == 0) & (i == 0)
    _route_tile(x_new, first, gr_ref, wr_ref, br_ref,
                h_ref, ids_ref, gates_ref, rank_ref, sizes_ref, carry_ref, ts)


def _pool_layer(x, g, w, b, s, g_ffn, w_router, b_router):
    bsz, seq, d = x.shape
    ts = min(ROW_TILE, seq)
    tiles = seq // ts
    kern = functools.partial(_pool_kernel, ts=ts)
    rows = POOL_PAD + POOL_HALO + ts
    vec = lambda: pl.BlockSpec((1, d), lambda bi, i: (0, 0))
    r_in, r_shape, r_out = _route_specs(bsz * seq, ts, d, lambda bi, i: bi * tiles + i)
    outs = pl.pallas_call(
        kern,
        out_shape=(jax.ShapeDtypeStruct(x.shape, F32),) + r_shape,
        grid=(bsz, tiles),
        in_specs=[
            pl.BlockSpec((None, ts, d), lambda bi, i: (bi, i, 0)),
            pl.BlockSpec((None, POOL_HALO, d),
                         lambda bi, i: (bi, jnp.maximum(i * (ts // POOL_HALO) - 1, 0), 0)),
            vec(),
            pl.BlockSpec(w.shape, lambda bi, i: (0, 0, 0)),
            vec(),
            vec(),
        ] + r_in,
        out_specs=(pl.BlockSpec((None, ts, d), lambda bi, i: (bi, i, 0)),) + r_out,
        scratch_shapes=[pltpu.VMEM((rows, d), F32),
                        pltpu.VMEM((rows, 3 * POOL_GROUP_CH), F32),
                        pltpu.VMEM((rows, 2 * POOL_GROUP_CH), F32),
                        pltpu.VMEM((rows, POOL_GROUP_CH), F32),
                        pltpu.VMEM((N_EXPERTS, LANES), F32)],
        compiler_params=pltpu.CompilerParams(
            dimension_semantics=("arbitrary", "arbitrary"),
            vmem_limit_bytes=VMEM_LIMIT_BYTES),
        name="pool_route",
    )(x, x, g.reshape(1, d), w.astype(BF16), b.reshape(1, d), s.reshape(1, d),
      *_route_operands(g_ffn, w_router, b_router))
    return outs[0], outs[1:]


def _route_tile(x, first, g_ref, wr_ref, br_ref,
                h_ref, ids_ref, gates_ref, rank_ref, sizes_ref, carry_ref, ts):
    @pl.when(first)
    def _():
        carry_ref[...] = jnp.zeros_like(carry_ref)

    h = _rms(x, g_ref[...])
    _store_row_tiles(h_ref, h, ts)
    h_hi = h.astype(BF16)
    h_lo = (h - h_hi.astype(F32)).astype(BF16)
    w = wr_ref[...]
    w_hi = w.astype(BF16)
    w_lo = (w - w_hi.astype(F32)).astype(BF16)
    dn = (((1,), (1,)), ((), ()))
    lg = (lax.dot_general(w_hi, h_hi, dn, preferred_element_type=F32)
          + lax.dot_general(w_lo, h_hi, dn, preferred_element_type=F32)
          + lax.dot_general(w_hi, h_lo, dn, preferred_element_type=F32))
    lg = lg + br_ref[:, 0:1]

    iota_e = lax.broadcasted_iota(I32, (N_EXPERTS, ts), 0)
    vals, ids = [], []
    for _ in range(TOP_K):
        m = jnp.max(lg, axis=0, keepdims=True)
        idx = jnp.min(jnp.where(lg == m, iota_e, N_EXPERTS), axis=0, keepdims=True)
        vals.append(m)
        ids.append(idx)
        lg = jnp.where(iota_e == idx, -jnp.inf, lg)
    ex = [jnp.exp(v - vals[0]) for v in vals]
    den = ex[0] + ex[1] + ex[2] + ex[3]
    gates_ref[...] = jnp.concatenate([e / den for e in ex], axis=0)
    ids_ref[...] = jnp.concatenate(ids, axis=0)

    hot = [(iota_e == idx) for idx in ids]
    multi = (hot[0] | hot[1] | hot[2] | hot[3])
    multi_f = multi.astype(F32)
    r = lax.broadcasted_iota(I32, (ts, ts), 0)
    c = lax.broadcasted_iota(I32, (ts, ts), 1)
    upper = (r < c).astype(BF16)
    before = jnp.dot(multi_f.astype(BF16), upper, preferred_element_type=F32)
    before = before + carry_ref[:, 0:1]
    ranks = [jnp.sum(jnp.where(hk, before, 0.0), axis=0, keepdims=True) for hk in hot]
    rank_ref[...] = jnp.concatenate(ranks, axis=0).astype(I32)
    carry_ref[...] = carry_ref[...] + jnp.sum(multi_f, axis=1, keepdims=True)
    sizes_ref[...] = carry_ref[...]


def _route_operands(g, w_router, b_router):
    d = g.shape[0]
    return (g.reshape(1, d), w_router.T,
            jnp.broadcast_to(b_router.reshape(N_EXPERTS, 1), (N_EXPERTS, LANES)))


def _route_specs(t, ts, d, tile_of):
    const = lambda *idx: (0, 0)
    in_specs = [pl.BlockSpec((1, d), const), pl.BlockSpec((N_EXPERTS, d), const),
                pl.BlockSpec((N_EXPERTS, LANES), const)]
    out_shape = (
        jax.ShapeDtypeStruct((t * ROW_TILE_SUB, LANES), F32),
        jax.ShapeDtypeStruct((TOP_K, t), I32),
        jax.ShapeDtypeStruct((TOP_K, t), F32),
        jax.ShapeDtypeStruct((TOP_K, t), I32),
        jax.ShapeDtypeStruct((N_EXPERTS, LANES), F32),
    )
    tok = lambda: pl.BlockSpec((TOP_K, ts), lambda *idx: (0, tile_of(*idx)))
    out_specs = (
        pl.BlockSpec((ts * ROW_TILE_SUB, LANES), lambda *idx: (tile_of(*idx), 0)),
        tok(), tok(), tok(),
        pl.BlockSpec((N_EXPERTS, LANES), const),
    )
    return in_specs, out_shape, out_specs


def _dispatch_kernel(pad_ref, n_used_ref, *refs, td, blk, nblk, zero_fill):
    dest_refs, h_ref, rest = refs[:TOP_K], refs[TOP_K], refs[TOP_K + 1:]
    xs_ref, zbuf, sem_z, sem = rest if zero_fill else rest[1:]
    i = pl.program_id(0)
    sub = ROW_TILE_SUB

    def zero_rows(row, n):
        return pltpu.make_async_copy(
            zbuf.at[pl.ds(0, n * sub), :],
            xs_ref.at[pl.ds(pl.multiple_of(row * sub, sub), n * sub), :], sem_z)

    def padding_copies(act):
        for e in range(N_EXPERTS):
            pos = pad_ref[e]
            length = pad_ref[N_EXPERTS + e]
            n = blk // 2
            while n >= 1:
                has = (length & n) != 0

                @pl.when(has)
                def _(pos=pos, n=n):
                    act(zero_rows(pos, n))
                pos = pos + jnp.where(has, n, 0)
                n //= 2

        def tail(b, carry):
            act(zero_rows(b * blk, blk))
            return carry
        lax.fori_loop(n_used_ref[0], nblk, tail, 0)

    if zero_fill:
        @pl.when(i == 0)
        def _():
            zbuf[...] = jnp.zeros_like(zbuf)
            padding_copies(lambda cp: cp.start())

    def issue(t, carry):
        src = h_ref.at[pl.ds(pl.multiple_of(t * sub, sub), sub), :]
        for k in range(TOP_K):
            d = dest_refs[k][0, 0, t]
            pltpu.make_async_copy(src, xs_ref.at[pl.ds(pl.multiple_of(d * sub, sub), sub), :],
                                  sem).start(priority=k % 2)
        return carry

    lax.fori_loop(0, td, issue, 0, unroll=ISSUE_UNROLL)
    for k in range(TOP_K):
        pltpu.make_async_copy(h_ref, xs_ref.at[pl.ds(0, td * sub), :], sem).wait()

    if zero_fill:
        @pl.when(i == 0)
        def _():
            padding_copies(lambda cp: cp.wait())


def _dispatch(h_tiles, dest, pad, n_used, n_rows, reuse=None):
    sub = ROW_TILE_SUB
    t = h_tiles.shape[0] // sub
    td = min(DISPATCH_TILE, t)
    nt = t // td
    dest_k = [dest[k].reshape(nt, 1, td) for k in range(TOP_K)]
    zero_fill = reuse is None
    kern = functools.partial(_dispatch_kernel, td=td, blk=MOE_BLOCK, nblk=n_rows // MOE_BLOCK,
                             zero_fill=zero_fill)
    extra_specs = [] if zero_fill else [pl.BlockSpec(memory_space=pl.ANY)]
    extra_args = [] if zero_fill else [reuse]
    return pl.pallas_call(
        kern,
        out_shape=jax.ShapeDtypeStruct((n_rows * sub, LANES), F32),
        grid_spec=pltpu.PrefetchScalarGridSpec(
            num_scalar_prefetch=2,
            grid=(nt,),
            in_specs=[pl.BlockSpec((1, 1, td), lambda i, z, nu: (i, 0, 0),
                                   memory_space=pltpu.SMEM)] * TOP_K + [
                pl.BlockSpec((td * sub, LANES), lambda i, z, nu: (i, 0)),
            ] + extra_specs,
            out_specs=pl.BlockSpec(memory_space=pl.ANY),
            scratch_shapes=[
                pltpu.VMEM((MOE_BLOCK * sub, LANES), F32),
                pltpu.SemaphoreType.DMA(()),
                pltpu.SemaphoreType.DMA(()),
            ],
        ),
        compiler_params=pltpu.CompilerParams(
            dimension_semantics=("arbitrary",), vmem_limit_bytes=VMEM_LIMIT_BYTES),
        input_output_aliases={} if zero_fill else {3 + TOP_K: 0},
        name="moe_dispatch" if zero_fill else "moe_dispatch_reuse",
    )(pad, n_used, *dest_k, h_tiles, *extra_args)


def _expert_kernel(blk_e_ref, n_used_ref, nxt_e_ref, n_valid_ref, x_ref, w1_hbm, b1_ref, w2_hbm,
                   b2_ref, o_ref, w1s_ref, w2s_ref, w1b_ref, w2b_ref, wsem, *, layer):
    step = pl.program_id(0)
    sub_rows = MOE_BLOCK * ROW_TILE_SUB

    def weight_copies(expert):
        return (pltpu.make_async_copy(w1_hbm.at[layer, expert], w1s_ref, wsem.at[0]),
                pltpu.make_async_copy(w2_hbm.at[layer, expert], w2s_ref, wsem.at[1]))

    @pl.when(step == 0)
    def _():
        for cp in weight_copies(blk_e_ref[0]):
            cp.start()

    for sub in range(EXPERT_STEP_BLOCKS):
        i = step * EXPERT_STEP_BLOCKS + sub
        e = blk_e_ref[i]
        prev = blk_e_ref[jnp.maximum(i - 1, 0)]
        fresh = (i == 0) | (e != prev)
        live = i < n_used_ref[0]

        @pl.when(live & fresh)
        def _(i=i, e=e):
            for cp in weight_copies(e):
                cp.wait()
            w1b_ref[...] = w1s_ref[...].astype(BF16)
            w2b_ref[...] = w2s_ref[...].astype(BF16)
            nxt = nxt_e_ref[i]

            @pl.when(nxt >= 0)
            def _():
                for cp in weight_copies(nxt):
                    cp.start()

        def mlp(rows, sub=sub, e=e):
            r0 = sub * MOE_BLOCK
            x = _load_row_tiles(x_ref, rows, start=r0).astype(BF16)
            a = jnp.dot(x, w1b_ref[...], preferred_element_type=F32) + b1_ref[e]
            glu = jnp.minimum(a[:, :D_FF], SWIGLU_LIMIT)
            lin = jnp.clip(a[:, D_FF:], -SWIGLU_LIMIT, SWIGLU_LIMIT)
            act = glu * jax.nn.sigmoid(SWIGLU_ALPHA * glu) * (lin + 1.0)
            y = jnp.dot(act.astype(BF16), w2b_ref[...], preferred_element_type=F32) + b2_ref[e]
            _store_row_tiles(o_ref, y, rows, start=r0)
            if rows < MOE_BLOCK:
                o_ref[(r0 + rows) * ROW_TILE_SUB:(r0 + MOE_BLOCK) * ROW_TILE_SUB, :] = jnp.zeros(
                    ((MOE_BLOCK - rows) * ROW_TILE_SUB, LANES), F32)

        half = MOE_BLOCK // 2
        short = n_valid_ref[i] <= half

        @pl.when(live & jnp.logical_not(short))
        def _(mlp=mlp):
            mlp(MOE_BLOCK)

        @pl.when(live & short)
        def _(mlp=mlp):
            mlp(half)

        @pl.when(jnp.logical_not(live) & (step * EXPERT_STEP_BLOCKS < n_used_ref[0]))
        def _(sub=sub):
            o_ref[sub * sub_rows:(sub + 1) * sub_rows, :] = jnp.zeros((sub_rows, LANES), F32)


def _experts(xs, blk_e, n_used, nxt_e, n_valid, layer, w1, b1, w2, b2):
    sub = ROW_TILE_SUB
    n_rows = xs.shape[0] // sub
    d = w2.shape[-1]
    nblk = n_rows // MOE_BLOCK
    depth = w1.shape[0]
    grp = EXPERT_STEP_BLOCKS
    assert nblk % grp == 0
    row = lambda i, be, nu, nx, nv: (jnp.minimum(i, (nu[0] - 1) // grp), 0)
    out_row = row
    bias = lambda i, be, nu, nx, nv: (layer, 0, 0, 0)
    return pl.pallas_call(
        functools.partial(_expert_kernel, layer=layer),
        out_shape=jax.ShapeDtypeStruct((n_rows * sub, LANES), F32),
        grid_spec=pltpu.PrefetchScalarGridSpec(
            num_scalar_prefetch=4,
            grid=(nblk // grp,),
            in_specs=[
                pl.BlockSpec((grp * MOE_BLOCK * sub, LANES), row),
                pl.BlockSpec(memory_space=pl.ANY),
                pl.BlockSpec((None, N_EXPERTS, 1, 2 * D_FF), bias),
                pl.BlockSpec(memory_space=pl.ANY),
                pl.BlockSpec((None, N_EXPERTS, 1, d), bias),
            ],
            out_specs=pl.BlockSpec((grp * MOE_BLOCK * sub, LANES), out_row),
            scratch_shapes=[
                pltpu.VMEM((d, 2 * D_FF), F32),
                pltpu.VMEM((D_FF, d), F32),
                pltpu.VMEM((d, 2 * D_FF), BF16),
                pltpu.VMEM((D_FF, d), BF16),
                pltpu.SemaphoreType.DMA((2,)),
            ],
        ),
        compiler_params=pltpu.CompilerParams(
            dimension_semantics=("arbitrary",), vmem_limit_bytes=VMEM_LIMIT_BYTES),
        input_output_aliases={4: 0},
        name="moe_experts",
    )(blk_e, n_used, nxt_e, n_valid, xs, w1, b1.reshape(depth, N_EXPERTS, 1, 2 * D_FF), w2,
      b2.reshape(depth, N_EXPERTS, 1, d))


def _combine_kernel(*refs, tc, n_tiles, final_norm):
    dcur_refs, dnext_refs = refs[:TOP_K], refs[TOP_K:2 * TOP_K]
    x_ref, g_ref, ys_ref, gn_ref, o_ref, buf, sem = refs[2 * TOP_K:]
    i = pl.program_id(0)
    sub = ROW_TILE_SUB

    def issue(drefs, slot):
        def body(t, carry):
            for k in range(TOP_K):
                d = drefs[k][0, 0, t]
                pltpu.make_async_copy(
                    ys_ref.at[pl.ds(pl.multiple_of(d * sub, sub), sub), :],
                    buf.at[slot, k, pl.ds(pl.multiple_of(t * sub, sub), sub), :],
                    sem.at[slot]).start(priority=k % 2)
            return carry
        lax.fori_loop(0, tc, body, 0, unroll=ISSUE_UNROLL)

    @pl.when(i == 0)
    def _():
        issue(dcur_refs, 0)

    @pl.when(i + 1 < n_tiles)
    def _():
        issue(dnext_refs, (i + 1) % 2)

    slot = i % 2
    for k in range(TOP_K):
        pltpu.make_async_copy(ys_ref.at[pl.ds(0, tc * sub), :], buf.at[slot, k],
                              sem.at[slot]).wait()
    acc = x_ref[...]
    gates = g_ref[...]
    for k in range(TOP_K):
        acc = acc + gates[:, k:k + 1] * _load_row_tiles(buf, tc, lead=(slot, k))
    if final_norm:
        acc = _rms(acc, gn_ref[...])
    o_ref[...] = acc


def _combine(x2d, gates_t, dest, ys, g_norm, final_norm):
    t, d = x2d.shape
    tc = min(COMBINE_TILE, t)
    nt = t // tc
    dest_k = [dest[k].reshape(nt, 1, tc) for k in range(TOP_K)]
    cur = pl.BlockSpec((1, 1, tc), lambda i: (i, 0, 0), memory_space=pltpu.SMEM)
    nxt = pl.BlockSpec((1, 1, tc), lambda i: (jnp.minimum(i + 1, nt - 1), 0, 0),
                       memory_space=pltpu.SMEM)
    kern = functools.partial(_combine_kernel, tc=tc, n_tiles=nt, final_norm=final_norm)
    return pl.pallas_call(
        kern,
        out_shape=jax.ShapeDtypeStruct((t, d), F32),
        grid=(nt,),
        in_specs=[cur] * TOP_K + [nxt] * TOP_K + [
            pl.BlockSpec((tc, d), lambda i: (i, 0)),
            pl.BlockSpec((tc, TOP_K), lambda i: (i, 0)),
            pl.BlockSpec(memory_space=pl.ANY),
            pl.BlockSpec((1, d), lambda i: (0, 0)),
        ],
        out_specs=pl.BlockSpec((tc, d), lambda i: (i, 0)),
        scratch_shapes=[
            pltpu.VMEM((2, TOP_K, tc * ROW_TILE_SUB, LANES), F32),
            pltpu.SemaphoreType.DMA((2,)),
        ],
        compiler_params=pltpu.CompilerParams(
            dimension_semantics=("arbitrary",), vmem_limit_bytes=VMEM_LIMIT_BYTES),
        name="moe_combine_final" if final_norm else "moe_combine",
    )(*dest_k, *dest_k, x2d, gates_t.T, ys, g_norm.reshape(1, d))


def _moe_experts(routed, layer, w1, b1, w2, b2, reuse=None):
    h, ids, gates, rank, sizes = routed
    t = ids.shape[1]
    sizes = sizes[:, 0].astype(I32)
    padded = ((sizes + MOE_BLOCK - 1) // MOE_BLOCK) * MOE_BLOCK
    pend = jnp.cumsum(padded)
    pstart = pend - padded
    onehot = ids[:, :, None] == jnp.arange(N_EXPERTS, dtype=I32)
    dest = rank + jnp.sum(jnp.where(onehot, pstart, 0), axis=-1)
    n_rows = t * TOP_K + N_EXPERTS * MOE_BLOCK
    nblk = n_rows // MOE_BLOCK
    blk_row = jnp.arange(nblk, dtype=I32) * MOE_BLOCK
    blk_e = jnp.minimum(jnp.sum((pend[None, :] <= blk_row[:, None]).astype(I32), axis=1),
                        N_EXPERTS - 1)
    n_used = (pend[-1:] // MOE_BLOCK).astype(I32)
    pad = jnp.concatenate([pstart + sizes, padded - sizes]).astype(I32)
    xs = _dispatch(h, dest, pad, n_used, n_rows, reuse)
    blk_id = jnp.arange(nblk, dtype=I32)
    later = ((blk_id[None, :] > blk_id[:, None]) & (blk_e[None, :] != blk_e[:, None])
             & (blk_id[None, :] < n_used[0]))
    nxt_blk = jnp.min(jnp.where(later, blk_id[None, :], nblk), axis=1)
    nxt_e = jnp.sum(jnp.where(blk_id[None, :] == nxt_blk[:, None], blk_e[None, :] + 1, 0),
                    axis=1).astype(I32) - 1
    own = blk_e[:, None] == jnp.arange(N_EXPERTS, dtype=I32)[None, :]
    row_end = jnp.sum(jnp.where(own, (pstart + sizes)[None, :], 0), axis=1)
    n_valid = jnp.clip(row_end - blk_row, 0, MOE_BLOCK).astype(I32)
    ys = _experts(xs, blk_e, n_used, nxt_e, n_valid, layer, w1, b1, w2, b2)
    return ys, dest


def _qkv_kernel(x_ref, g_ref, w_ref, b_ref, q_ref, k_ref, v_ref):
    h = _rms(x_ref[...], g_ref[...])
    qkv = jnp.dot(h.astype(BF16), w_ref[...], preferred_element_type=F32) + b_ref[...]
    nq = N_HEADS * HEAD_DIM
    nk = 2 * N_KV_HEADS * HEAD_DIM
    q_ref[...] = (qkv[:, :nq] * (HEAD_DIM ** -0.5)).astype(BF16)
    k_ref[...] = qkv[:, nq:nq + nk].astype(BF16)
    v_ref[...] = qkv[:, nq + nk:].astype(BF16)


def _qkv(x2d, g, w_qkv, b_qkv):
    t, d = x2d.shape
    ts = min(ROW_TILE, t)
    nq = N_HEADS * HEAD_DIM
    kv = N_KV_HEADS * HEAD_DIM

    def dup_heads(m):
        parts = []
        for hd in range(N_KV_HEADS):
            sl = m[..., hd * HEAD_DIM:(hd + 1) * HEAD_DIM]
            parts += [sl, sl]
        return jnp.concatenate(parts, axis=-1)

    w = jnp.concatenate([w_qkv[:, :nq], dup_heads(w_qkv[:, nq:nq + kv]),
                         dup_heads(w_qkv[:, nq + kv:])], axis=1).astype(BF16)
    b = jnp.concatenate([b_qkv[:nq], dup_heads(b_qkv[nq:nq + kv]),
                         dup_heads(b_qkv[nq + kv:])]).reshape(1, -1)
    n_out = w.shape[1]
    row = lambda width: pl.BlockSpec((ts, width), lambda i: (i, 0))
    return pl.pallas_call(
        _qkv_kernel,
        out_shape=(
            jax.ShapeDtypeStruct((t, nq), BF16),
            jax.ShapeDtypeStruct((t, 2 * kv), BF16),
            jax.ShapeDtypeStruct((t, 2 * kv), BF16),
        ),
        grid=(t // ts,),
        in_specs=[
            row(d),
            pl.BlockSpec((1, d), lambda i: (0, 0)),
            pl.BlockSpec((d, n_out), lambda i: (0, 0)),
            pl.BlockSpec((1, n_out), lambda i: (0, 0)),
        ],
        out_specs=(row(nq), row(2 * kv), row(2 * kv)),
        compiler_params=pltpu.CompilerParams(
            dimension_semantics=("arbitrary",), vmem_limit_bytes=VMEM_LIMIT_BYTES),
        name="attn_qkv",
    )(x2d, g.reshape(1, d), w, b)


def _attn_kernel(sinks_ref, q_ref, kp_ref, kc_ref, vp_ref, vc_ref, bias_ref,
                 x_ref, wo_ref, bo_ref, gr_ref, wr_ref, br_ref,
                 y_ref, h_ref, ids_ref, gates_ref, rank_ref, sizes_ref, o_ref, carry_ref, *, ts):
    n = pl.program_id(1)
    rows = SLABS_PER_GROUP * ATT_BLOCK
    keys = 2 * ATT_BLOCK
    kall = jnp.concatenate([kp_ref[...], kc_ref[...]], axis=0)
    vall = jnp.concatenate([vp_ref[...], vc_ref[...]], axis=0)
    j = lax.broadcasted_iota(I32, (keys, rows), 0)
    r = lax.broadcasted_iota(I32, (keys, rows), 1) & (ATT_BLOCK - 1)
    in_window = (j > r) & (j <= r + WINDOW)
    lane_k = lax.broadcasted_iota(I32, (keys, LANES), 1)
    halves = [lane_k < HEAD_DIM, lane_k >= HEAD_DIM]
    sub_o = lax.broadcasted_iota(I32, (LANES, rows), 0)
    zero = jnp.zeros((), BF16)
    for blk in range(ATT_STEP_BLOCKS):
        q = q_ref[blk * ATT_BLOCK:(blk + 1) * ATT_BLOCK, :]
        kcat = kall[blk * ATT_BLOCK:blk * ATT_BLOCK + keys]
        vcat = vall[blk * ATT_BLOCK:blk * ATT_BLOCK + keys]
        if blk == 0:
            mask = in_window & ((n > 0) | (j >= ATT_BLOCK))
        else:
            mask = in_window
        for g in range(N_KV_HEADS):
            s0 = g * SLABS_PER_GROUP
            qg = jnp.concatenate([q[:, (s0 + s) * LANES:(s0 + s + 1) * LANES]
                                  for s in range(SLABS_PER_GROUP)], axis=0)
            kg = kcat[:, g * LANES:(g + 1) * LANES]
            vg = vcat[:, g * LANES:(g + 1) * LANES]
            pts, invs = [], []
            for p in range(HEADS_PER_SLAB):
                kp = jnp.where(halves[p], kg, zero)
                sc = lax.dot_general(kp, qg, (((1,), (1,)), ((), ())),
                                     preferred_element_type=F32)
                sc = jnp.where(mask, sc + bias_ref[g, p], NEG_INF)
                sink = jnp.concatenate(
                    [jnp.full((1, ATT_BLOCK),
                              sinks_ref[g * GQA_GROUP + s * HEADS_PER_SLAB + p], F32)
                     for s in range(SLABS_PER_GROUP)], axis=1)
                m = jnp.maximum(jnp.max(sc, axis=0, keepdims=True), sink)
                pe = jnp.exp(sc - m)
                den = jnp.sum(pe, axis=0, keepdims=True) + jnp.exp(sink - m)
                pts.append(pe.astype(BF16))
                invs.append(1.0 / den)
            pcat = jnp.concatenate(pts, axis=0)
            vblk = jnp.concatenate([jnp.where(halves[p], vg, zero)
                                    for p in range(HEADS_PER_SLAB)], axis=0)
            og = lax.dot_general(vblk, pcat, (((0,), (0,)), ((), ())),
                                 preferred_element_type=F32)
            og = og * jnp.where(sub_o < HEAD_DIM, invs[0], invs[1])
            og = og.T
            for s in range(SLABS_PER_GROUP):
                o_ref[blk * ATT_BLOCK:(blk + 1) * ATT_BLOCK,
                      (s0 + s) * LANES:(s0 + s + 1) * LANES] = (
                    og[s * ATT_BLOCK:(s + 1) * ATT_BLOCK].astype(BF16))
    x_new = x_ref[...] + jnp.dot(o_ref[...], wo_ref[...],
                                 preferred_element_type=F32) + bo_ref[...]
    y_ref[...] = x_new
    first = (pl.program_id(0) == 0) & (n == 0)
    _route_tile(x_new, first, gr_ref, wr_ref, br_ref,
                h_ref, ids_ref, gates_ref, rank_ref, sizes_ref, carry_ref, ts)


def _t5_bias_table(rel_bias):
    r = np.arange(ATT_BLOCK)[:, None]
    j = np.arange(2 * ATT_BLOCK)[None, :]
    rel = np.clip(ATT_BLOCK + r - j, 0, WINDOW - 1)
    max_exact = N_BUCKETS // 2
    nf = jnp.maximum(rel, max_exact).astype(F32)
    large = max_exact + (jnp.log(nf / max_exact) / math.log(MAX_DISTANCE / max_exact)
                         * (N_BUCKETS - max_exact)).astype(I32)
    large = jnp.minimum(large, N_BUCKETS - 1)
    bucket = jnp.where(rel < max_exact, rel, large)
    pick = (bucket[None] == jnp.arange(N_BUCKETS, dtype=I32)[:, None, None])
    bias = jnp.sum(jnp.where(pick[:, None], rel_bias.astype(F32)[:, :, None, None], 0.0),
                   axis=0)
    bias = bias.reshape(N_KV_HEADS, SLABS_PER_GROUP, HEADS_PER_SLAB, ATT_BLOCK, 2 * ATT_BLOCK)
    bias = jnp.transpose(bias, (0, 2, 4, 1, 3))
    return bias.reshape(N_KV_HEADS, HEADS_PER_SLAB, 2 * ATT_BLOCK, SLABS_PER_GROUP * ATT_BLOCK)


def _attention(x2d, q, kk, vv, sinks, rel_bias, w_o, b_o, g_ffn, w_router, b_router, bsz, seq):
    nq = N_HEADS * HEAD_DIM
    kvw = kk.shape[-1]
    d = x2d.shape[1]
    step = ATT_STEP_BLOCKS * ATT_BLOCK
    tiles = seq // step
    q3 = q.reshape(bsz, seq, nq)
    k3 = kk.reshape(bsz, seq, kvw)
    v3 = vv.reshape(bsz, seq, kvw)
    bias = _t5_bias_table(rel_bias)
    prev = lambda bi, n: (bi, jnp.maximum(n * ATT_STEP_BLOCKS - 1, 0), 0)
    cur = lambda bi, n: (bi, n, 0)
    const = lambda bi, n: (0, 0)
    r_in, r_shape, r_out = _route_specs(bsz * seq, step, d, lambda bi, n: bi * tiles + n)
    outs = pl.pallas_call(
        functools.partial(_attn_kernel, ts=step),
        out_shape=(jax.ShapeDtypeStruct((bsz, seq, d), F32),) + r_shape,
        grid=(bsz, tiles),
        in_specs=[
            pl.BlockSpec(memory_space=pltpu.SMEM),
            pl.BlockSpec((None, step, nq), cur),
            pl.BlockSpec((None, ATT_BLOCK, kvw), prev),
            pl.BlockSpec((None, step, kvw), cur),
            pl.BlockSpec((None, ATT_BLOCK, kvw), prev),
            pl.BlockSpec((None, step, kvw), cur),
            pl.BlockSpec(bias.shape, lambda bi, n: (0, 0, 0, 0)),
            pl.BlockSpec((None, step, d), cur),
            pl.BlockSpec((nq, d), const),
            pl.BlockSpec((1, d), const),
        ] + r_in,
        out_specs=(pl.BlockSpec((None, step, d), cur),) + r_out,
        scratch_shapes=[pltpu.VMEM((step, nq), BF16),
                        pltpu.VMEM((N_EXPERTS, LANES), F32)],
        compiler_params=pltpu.CompilerParams(
            dimension_semantics=("arbitrary", "arbitrary"),
            vmem_limit_bytes=VMEM_LIMIT_BYTES),
        name="attn_oproj_route",
    )(sinks.astype(F32), q3, k3, k3, v3, v3, bias, x2d.reshape(bsz, seq, d),
      w_o.astype(BF16), b_o.reshape(1, d), *_route_operands(g_ffn, w_router, b_router))
    return outs[0].reshape(bsz * seq, d), outs[1:]


def kernel(x, norm_mix, norm_ffn, norm_final, pool_w, pool_b, pool_scale, w_qkv, b_qkv,
           sinks, w_o, b_o, rel_bias, w_router, b_router, w1, b1, w2, b2):
    bsz, seq, d = x.shape
    t = bsz * seq
    x, routed = _pool_layer(x, norm_mix[0], pool_w[0], pool_b[0], pool_scale[0],
                            norm_ffn[0], w_router[0], b_router[0])
    x = x.reshape(t, d)
    ys0, dest = _moe_experts(routed, 0, w1, b1, w2, b2)
    x = _combine(x, routed[2], dest, ys0, norm_final, final_norm=False)
    q, kk, vv = _qkv(x, norm_mix[1], w_qkv[0], b_qkv[0])
    x, routed = _attention(x, q, kk, vv, sinks[0], rel_bias, w_o[0], b_o[0],
                           norm_ffn[1], w_router[1], b_router[1], bsz, seq)
    ys, dest = _moe_experts(routed, 1, w1, b1, w2, b2, reuse=ys0)
    x = _combine(x, routed[2], dest, ys, norm_final, final_norm=True)
    return x.reshape(bsz, seq, d)
```
